```python
import math
import jax, jax.numpy as jnp
from jax import lax
import numpy as np

D_MODEL = 1024
BATCH = 4
SEQ = 4096
DEPTH = 1
DEC_BATCH = 128
DEC_SEQ = 4
PAST_LEN = 8192
PAGE_SIZE = 128

N_META = 16
D_MIX = D_MODEL
HEAD_DIM = 64
ATT_WIDTH = D_MIX // 2
N_HEADS = ATT_WIDTH // HEAD_DIM
N_KV_HEADS = 2
Q_PER_KV = N_HEADS // N_KV_HEADS
WINDOW = 128
BLOCK = 128
ROPE_DIM = HEAD_DIM // 4
ROPE_THETA = 500000.0
SSM_WIDTH = D_MIX - ATT_WIDTH
SSM_GROUP = 16
SSM_GROUPS = SSM_WIDTH // SSM_GROUP
SSM_STATE = 64
KV_WIDTH = N_KV_HEADS * HEAD_DIM
D_IN = ATT_WIDTH + 2 * KV_WIDTH + SSM_WIDTH
D_FF = 4 * D_MODEL
EPS = 1e-6
NEG = -1e30

kernel_name = "hymba_s5_swa_sink_decode_step"


def rmsnorm(x, g):
    xf = x.astype(jnp.float32)
    y = xf * lax.rsqrt(jnp.mean(xf * xf, axis=-1, keepdims=True) + EPS)
    return (y * g.astype(jnp.float32)).astype(x.dtype)


def partial_rope(x, pos):
    half = ROPE_DIM // 2
    inv = ROPE_THETA ** (-jnp.arange(half, dtype=jnp.float32) / half)
    ang = pos.astype(jnp.float32)[:, None] * inv[None, :]
    cos = jnp.cos(ang)[:, None, :]
    sin = jnp.sin(ang)[:, None, :]
    xr = x[..., :ROPE_DIM].astype(jnp.float32)
    x1, x2 = xr[..., :half], xr[..., half:]
    rot = jnp.concatenate([x1 * cos - x2 * sin, x2 * cos + x1 * sin], axis=-1).astype(x.dtype)
    return jnp.concatenate([rot, x[..., ROPE_DIM:]], axis=-1)


def in_project(h, g_pre, w_in):
    hn = rmsnorm(h, g_pre)
    proj = jnp.einsum('bsd,de->bse', hn, w_in)
    B, S = h.shape[:2]
    q = proj[..., :ATT_WIDTH].reshape(B, S, N_HEADS, HEAD_DIM)
    k = proj[..., ATT_WIDTH:ATT_WIDTH + KV_WIDTH].reshape(B, S, N_KV_HEADS, HEAD_DIM)
    v = proj[..., ATT_WIDTH + KV_WIDTH:ATT_WIDTH + 2 * KV_WIDTH].reshape(B, S, N_KV_HEADS, HEAD_DIM)
    u = proj[..., ATT_WIDTH + 2 * KV_WIDTH:]
    return q, k, v, u


def sink_softmax(logits, mask, sink):
    logits = jnp.where(mask, logits, NEG)
    m = jnp.maximum(jnp.max(logits, axis=-1, keepdims=True), sink)
    e = jnp.exp(logits - m)
    return e / (jnp.sum(e, axis=-1, keepdims=True) + jnp.exp(sink - m))


def swa_prompt(q, k, v, sinks):
    B, L = q.shape[:2]
    front = (BLOCK - N_META % BLOCK) % BLOCK
    back = (-(front + L)) % BLOCK
    pad = ((0, 0), (front, back), (0, 0), (0, 0))
    qp, kp, vp = jnp.pad(q, pad), jnp.pad(k, pad), jnp.pad(v, pad)
    nb = qp.shape[1] // BLOCK
    qb = qp.reshape(B, nb, BLOCK, N_KV_HEADS, Q_PER_KV, HEAD_DIM)
    kb = kp.reshape(B, nb, BLOCK, N_KV_HEADS, HEAD_DIM)
    vb = vp.reshape(B, nb, BLOCK, N_KV_HEADS, HEAD_DIM)
    zblk = jnp.zeros_like(kb[:, :1])
    kk = jnp.concatenate([jnp.concatenate([zblk, kb[:, :-1]], axis=1), kb], axis=2)
    vv = jnp.concatenate([jnp.concatenate([zblk, vb[:, :-1]], axis=1), vb], axis=2)
    pq = (jnp.arange(nb * BLOCK, dtype=jnp.int32) - front).reshape(nb, BLOCK)
    pk = jnp.concatenate([pq - BLOCK, pq], axis=1)
    diff = pq[:, :, None] - pk[:, None, :]
    mask = (diff >= 0) & (diff < WINDOW) & (pk[:, None, :] >= 0)
    logits = jnp.einsum('bnqkgd,bnskd->bnkgqs', qb, kk).astype(jnp.float32) * (HEAD_DIM ** -0.5)
    sink = sinks.astype(jnp.float32).reshape(1, 1, N_KV_HEADS, Q_PER_KV, 1, 1)
    p = sink_softmax(logits, mask[None, :, None, None], sink)
    o = jnp.einsum('bnkgqs,bnskd->bnqkgd', p.astype(v.dtype), vv)
    o = o.reshape(B, nb * BLOCK, ATT_WIDTH)[:, front:front + L]
    return o


def swa_sample(q, k, v, cache_k, cache_v, pos_q, sinks):
    w = cache_k.shape[1]
    keys = jnp.concatenate([cache_k.astype(k.dtype), k], axis=1)
    vals = jnp.concatenate([cache_v.astype(v.dtype), v], axis=1)
    pos_k = jnp.concatenate([pos_q[0] - w + jnp.arange(w, dtype=jnp.int32), pos_q])
    diff = pos_q[:, None] - pos_k[None, :]
    mask = (diff >= 0) & (diff < WINDOW)
    B, T = q.shape[:2]
    qg = q.reshape(B, T, N_KV_HEADS, Q_PER_KV, HEAD_DIM)
    logits = jnp.einsum('btkgd,bskd->bkgts', qg, keys).astype(jnp.float32) * (HEAD_DIM ** -0.5)
    sink = sinks.astype(jnp.float32).reshape(1, N_KV_HEADS, Q_PER_KV, 1, 1)
    p = sink_softmax(logits, mask[None, None, None], sink)
    o = jnp.einsum('bkgts,bskd->btkgd', p.astype(v.dtype), vals).reshape(B, T, ATT_WIDTH)
    return o, keys[:, -WINDOW:], vals[:, -WINDOW:]


def zoh(a_re, a_im, log_dt, b_re, b_im):
    dt = jnp.exp(log_dt.astype(jnp.float32))[:, None]
    ar, ai = a_re.astype(jnp.float32), a_im.astype(jnp.float32)
    mag = jnp.exp(ar * dt)
    abar_re, abar_im = mag * jnp.cos(ai * dt), mag * jnp.sin(ai * dt)
    nr, ni = abar_re - 1.0, abar_im
    den = ar * ar + ai * ai
    coef_re = (nr * ar + ni * ai) / den
    coef_im = (ni * ar - nr * ai) / den
    br, bi = b_re.astype(jnp.float32), b_im.astype(jnp.float32)
    bbar_re = coef_re[..., None] * br - coef_im[..., None] * bi
    bbar_im = coef_re[..., None] * bi + coef_im[..., None] * br
    return abar_re, abar_im, bbar_re, bbar_im


def cplx_combine(e1, e2):
    ar1, ai1, br1, bi1 = e1
    ar2, ai2, br2, bi2 = e2
    return (ar2 * ar1 - ai2 * ai1,
            ar2 * ai1 + ai2 * ar1,
            ar2 * br1 - ai2 * bi1 + br2,
            ar2 * bi1 + ai2 * br1 + bi2)


def ssm_mixer(u, h0_re, h0_im, a_re, a_im, log_dt, b_re, b_im, c_re, c_im, d, w_glu):
    B, S = u.shape[:2]
    abar_re, abar_im, bbar_re, bbar_im = zoh(a_re, a_im, log_dt, b_re, b_im)
    ug = u.astype(jnp.float32).reshape(B, S, SSM_GROUPS, SSM_GROUP)
    bu_re = jnp.einsum('bsgh,gph->bsgp', ug, bbar_re)
    bu_im = jnp.einsum('bsgh,gph->bsgp', ug, bbar_im)
    if h0_re is not None:
        hr0, hi0 = h0_re.astype(jnp.float32), h0_im.astype(jnp.float32)
        bu_re = bu_re.at[:, 0].add(abar_re * hr0 - abar_im * hi0)
        bu_im = bu_im.at[:, 0].add(abar_re * hi0 + abar_im * hr0)
    ar = jnp.broadcast_to(abar_re, bu_re.shape)
    ai = jnp.broadcast_to(abar_im, bu_re.shape)
    _, _, hr, hi = lax.associative_scan(cplx_combine, (ar, ai, bu_re, bu_im), axis=1)
    y = (jnp.einsum('bsgp,ghp->bsgh', hr, c_re.astype(jnp.float32))
         - jnp.einsum('bsgp,ghp->bsgh', hi, c_im.astype(jnp.float32))
         + d.astype(jnp.float32) * ug).reshape(B, S, SSM_WIDTH)
    z = jax.nn.gelu(y)
    out = z * jax.nn.sigmoid(jnp.einsum('bsc,ce->bse', z, w_glu.astype(jnp.float32)))
    return out.astype(u.dtype), hr[:, -1], hi[:, -1]


def merge_out(att_o, ssm_o, g_att, g_ssm, w_out, g_post):
    cat = jnp.concatenate([rmsnorm(att_o, g_att), rmsnorm(ssm_o, g_ssm)], axis=-1)
    return rmsnorm(jnp.einsum('bse,ed->bsd', cat, w_out), g_post)


def sqrelu_mlp(h, g_pre, w_up, w_down, g_post):
    hn = rmsnorm(h, g_pre)
    a = jnp.square(jax.nn.relu(jnp.einsum('bsd,df->bsf', hn, w_up)))
    return rmsnorm(jnp.einsum('bsf,fd->bsd', a, w_down), g_post)


def setup_inputs(seed: int = 0) -> dict:
    key = jax.random.key(seed)
    ks = jax.random.split(key, 32)
    f32 = jnp.float32

    def nrm(k, shape, scale):
        return jax.random.normal(k, shape, f32) * scale

    def gain(k, shape):
        return 1.0 + 0.05 * jax.random.normal(k, shape, f32)

    w_cache = min(WINDOW, PAST_LEN)
    n_idx = jnp.arange(SSM_STATE, dtype=f32)
    return {
        'x_prompt': nrm(ks[0], (BATCH, SEQ, D_MODEL), 1.0),
        'x_sample': nrm(ks[1], (DEC_BATCH, DEC_SEQ, D_MODEL), 1.0),
        'cache_k_win': nrm(ks[2], (DEPTH, DEC_BATCH, w_cache, N_KV_HEADS, HEAD_DIM), 1.0),
        'cache_v_win': nrm(ks[3], (DEPTH, DEC_BATCH, w_cache, N_KV_HEADS, HEAD_DIM), 1.0),
        'state_ssm_re': nrm(ks[4], (DEPTH, DEC_BATCH, SSM_GROUPS, SSM_STATE), 0.5),
        'state_ssm_im': nrm(ks[5], (DEPTH, DEC_BATCH, SSM_GROUPS, SSM_STATE), 0.5),
        'meta_tokens': nrm(ks[6], (N_META, D_MODEL), 1.0),
        'norm_mix_pre': gain(ks[7], (DEPTH, D_MODEL)),
        'w_in': nrm(ks[8], (DEPTH, D_MODEL, D_IN), D_MODEL ** -0.5),
        'attn_sinks': nrm(ks[9], (DEPTH, N_HEADS), 0.5),
        'ssm_a_re': -0.5 + 0.01 * jax.random.normal(ks[10], (DEPTH, SSM_GROUPS, SSM_STATE), f32),
        'ssm_a_im': math.pi * n_idx + 0.01 * jax.random.normal(ks[11], (DEPTH, SSM_GROUPS, SSM_STATE), f32),
        'ssm_log_dt': jax.random.uniform(ks[12], (DEPTH, SSM_GROUPS), f32, math.log(0.001), math.log(0.1)),
        'ssm_b_re': nrm(ks[13], (DEPTH, SSM_GROUPS, SSM_STATE, SSM_GROUP), (2 * SSM_GROUP) ** -0.5),
        'ssm_b_im': nrm(ks[14], (DEPTH, SSM_GROUPS, SSM_STATE, SSM_GROUP), (2 * SSM_GROUP) ** -0.5),
        'ssm_c_re': nrm(ks[15], (DEPTH, SSM_GROUPS, SSM_GROUP, SSM_STATE), (2 * SSM_STATE) ** -0.5),
        'ssm_c_im': nrm(ks[16], (DEPTH, SSM_GROUPS, SSM_GROUP, SSM_STATE), (2 * SSM_STATE) ** -0.5),
        'ssm_d': nrm(ks[17], (DEPTH, SSM_GROUPS, SSM_GROUP), 1.0),
        'w_glu': nrm(ks[18], (DEPTH, SSM_WIDTH, SSM_WIDTH), SSM_WIDTH ** -0.5),
        'norm_att_out': gain(ks[19], (DEPTH, ATT_WIDTH)),
        'norm_ssm_out': gain(ks[20], (DEPTH, SSM_WIDTH)),
        'w_out': nrm(ks[21], (DEPTH, D_MIX, D_MODEL), D_MIX ** -0.5),
        'norm_mix_post': gain(ks[22], (DEPTH, D_MODEL)),
        'norm_mlp_pre': gain(ks[23], (DEPTH, D_MODEL)),
        'w_up': nrm(ks[24], (DEPTH, D_MODEL, D_FF), D_MODEL ** -0.5),
        'w_down': nrm(ks[25], (DEPTH, D_FF, D_MODEL), D_FF ** -0.5),
        'norm_mlp_post': gain(ks[26], (DEPTH, D_MODEL)),
    }


def reference(x_prompt, x_sample, cache_k_win, cache_v_win, state_ssm_re, state_ssm_im,
              meta_tokens, norm_mix_pre, w_in, attn_sinks, ssm_a_re, ssm_a_im, ssm_log_dt,
              ssm_b_re, ssm_b_im, ssm_c_re, ssm_c_im, ssm_d, w_glu, norm_att_out, norm_ssm_out,
              w_out, norm_mix_post, norm_mlp_pre, w_up, w_down, norm_mlp_post):
    B = x_prompt.shape[0]
    meta = jnp.broadcast_to(meta_tokens[None].astype(x_prompt.dtype), (B, N_META, D_MODEL))
    hp = jnp.concatenate([meta, x_prompt], axis=1)
    hs = x_sample
    pos_p = jnp.arange(hp.shape[1], dtype=jnp.int32)
    pos_s = PAST_LEN + jnp.arange(hs.shape[1], dtype=jnp.int32)

    kp_l, vp_l, srp_l, sip_l = [], [], [], []
    ks_l, vs_l, srs_l, sis_l = [], [], [], []
    for l in range(DEPTH):
        ssm_p = (ssm_a_re[l], ssm_a_im[l], ssm_log_dt[l], ssm_b_re[l], ssm_b_im[l],
                 ssm_c_re[l], ssm_c_im[l], ssm_d[l], w_glu[l])
        q, k, v, u = in_project(hp, norm_mix_pre[l], w_in[l])
        q, k = partial_rope(q, pos_p), partial_rope(k, pos_p)
        att_o = swa_prompt(q, k, v, attn_sinks[l])
        ssm_o, hr, hi = ssm_mixer(u, None, None, *ssm_p)
        hp = hp + merge_out(att_o, ssm_o, norm_att_out[l], norm_ssm_out[l], w_out[l], norm_mix_post[l])
        hp = hp + sqrelu_mlp(hp, norm_mlp_pre[l], w_up[l], w_down[l], norm_mlp_post[l])
        kp_l.append(k[:, -WINDOW:])
        vp_l.append(v[:, -WINDOW:])
        srp_l.append(hr.astype(x_prompt.dtype))
        sip_l.append(hi.astype(x_prompt.dtype))
        q, k, v, u = in_project(hs, norm_mix_pre[l], w_in[l])
        q, k = partial_rope(q, pos_s), partial_rope(k, pos_s)
        att_o, nk, nv = swa_sample(q, k, v, cache_k_win[l], cache_v_win[l], pos_s, attn_sinks[l])
        ssm_o, hr, hi = ssm_mixer(u, state_ssm_re[l], state_ssm_im[l], *ssm_p)
        hs = hs + merge_out(att_o, ssm_o, norm_att_out[l], norm_ssm_out[l], w_out[l], norm_mix_post[l])
        hs = hs + sqrelu_mlp(hs, norm_mlp_pre[l], w_up[l], w_down[l], norm_mlp_post[l])
        ks_l.append(nk)
        vs_l.append(nv)
        srs_l.append(hr.astype(state_ssm_re.dtype))
        sis_l.append(hi.astype(state_ssm_im.dtype))

    y_prompt = hp[:, N_META:]
    y_sample = hs
    return (y_prompt, y_sample,
            jnp.stack(kp_l), jnp.stack(vp_l), jnp.stack(srp_l), jnp.stack(sip_l),
            jnp.stack(ks_l), jnp.stack(vs_l), jnp.stack(srs_l), jnp.stack(sis_l))
```

```python
import functools
import math

import jax
import jax.numpy as jnp
from jax import lax
from jax.experimental import pallas as pl
from jax.experimental.pallas import tpu as pltpu

F32 = jnp.float32
BF16 = jnp.bfloat16

N_META = 16
HEAD_DIM = 64
N_HEADS = 8
N_KV_HEADS = 2
WINDOW = 128
BLOCK = 128
ROPE_DIM = 16
ROPE_THETA = 500000.0
SSM_GROUP = 16
SSM_GROUPS = 32
SSM_STATE = 64
PAST_LEN = 8192
EPS = 1e-6
NEG = -1e30

D_MODEL = 1024
ATT_WIDTH = 512
KV_WIDTH = 128
SSM_WIDTH = 512
N_STATE = SSM_GROUPS * SSM_STATE
D_FF = 4096

ROWS = 512
LANES = 128
HALF = LANES // 2
SSM_TILE = 256
STATE_TILE = (SSM_TILE // SSM_GROUP) * SSM_STATE
N_SSM_TILES = SSM_WIDTH // SSM_TILE
LB_PER_TILE = STATE_TILE // LANES
N_LB = N_STATE // LANES
FF_CHUNK = 1024
VMEM_LIMIT = 56 * 1024 * 1024


def _dot(a, b):
    return jnp.dot(a, b, preferred_element_type=F32)


def _dot_nt(a, b):
    return lax.dot_general(a, b, (((1,), (1,)), ((), ())), preferred_element_type=F32)


def _rms(x, g):
    return x * lax.rsqrt(jnp.mean(x * x, axis=-1, keepdims=True) + EPS) * g


def _rope(x, cos, sa, sb):
    return x * cos + pltpu.roll(x, 8, axis=1) * sa + pltpu.roll(x, LANES - 8, axis=1) * sb


def _lane_is_lo(shape):
    return lax.broadcasted_iota(jnp.int32, shape, 1) < HALF


def _project(x, g_ref, w_in_ref):
    hn = _rms(x, g_ref[...]).astype(BF16)
    q = _dot(hn, w_in_ref[:, 0:ATT_WIDTH])
    kv = _dot(hn, w_in_ref[:, ATT_WIDTH:ATT_WIDTH + 2 * KV_WIDTH])
    u = _dot(hn, w_in_ref[:, ATT_WIDTH + 2 * KV_WIDTH:])
    return q, kv[:, 0:KV_WIDTH], kv[:, KV_WIDTH:2 * KV_WIDTH], u


def _ssm_input(u, wbre_ref, wbim_ref, s_ref):
    ub = u.astype(BF16)
    for c in range(N_SSM_TILES):
        uc = ub[:, c * SSM_TILE:(c + 1) * SSM_TILE]
        bre, bim = _dot(uc, wbre_ref[c]), _dot(uc, wbim_ref[c])
        for l in range(LB_PER_TILE):
            s_ref[c * LB_PER_TILE + l, 0:ROWS, :] = bre[:, l * LANES:(l + 1) * LANES]
            s_ref[c * LB_PER_TILE + l, ROWS:2 * ROWS, :] = bim[:, l * LANES:(l + 1) * LANES]


def _ssm_output(s_ref, u, wcre_ref, wcim_ref, d_ref, wglu_ref):
    ys = []
    for c in range(N_SSM_TILES):
        blocks = range(c * LB_PER_TILE, (c + 1) * LB_PER_TILE)
        hr = jnp.concatenate([s_ref[l, 0:ROWS, :].astype(BF16) for l in blocks], axis=1)
        hi = jnp.concatenate([s_ref[l, ROWS:2 * ROWS, :].astype(BF16) for l in blocks], axis=1)
        ys.append(_dot(hr, wcre_ref[c]) + _dot(hi, wcim_ref[c]))
    y = jnp.concatenate(ys, axis=1) + d_ref[...] * u
    z = 0.5 * y * (1.0 + jnp.tanh(math.sqrt(2.0 / math.pi) * (y + 0.044715 * (y * y * y))))
    gate = 1.0 / (1.0 + jnp.exp(-_dot(z.astype(BF16), wglu_ref[...])))
    return z * gate


def _merge(x, att, ssm_o, gatt_ref, gssm_ref, wout_ref, gpost_ref):
    a = _rms(att, gatt_ref[...]).astype(BF16)
    s = _rms(ssm_o, gssm_ref[...]).astype(BF16)
    m = _dot(a, wout_ref[0:ATT_WIDTH, :]) + _dot(s, wout_ref[ATT_WIDTH:, :])
    return x + _rms(m, gpost_ref[...])


def _div(x, k):
    return lax.shift_right_logical(x, int(math.log2(k)))


def _mod(x, k):
    return lax.bitwise_and(x, k - 1)


def _sink_softmax(logits, mask, sink_col):
    lm = jnp.where(mask, logits, NEG)
    m = jnp.maximum(jnp.max(lm, axis=-1, keepdims=True), sink_col)
    e = jnp.exp(lm - m)
    den = jnp.sum(e, axis=-1, keepdims=True) + jnp.exp(sink_col - m)
    return e, den


def _prompt_mixer_kernel(
        sink_ref, x_ref, meta_ref, cos_ref, sa_ref, sb_ref, gpre_ref, win_ref,
        ar_ref, ai_ref, wbre_ref, wbim_ref, wcre_ref, wcim_ref, d_ref, wglu_ref,
        gatt_ref, gssm_ref, wout_ref, gpost_ref,
        h1_ref, klast_ref, vlast_ref, sre_ref, sim_ref,
        kbuf, vbuf, s_ref, hstate, att_ref):
    n = pl.program_id(0)
    nb = x_ref.shape[0]
    last = pl.num_programs(0) - 1

    @pl.when(n == 0)
    def _init():
        kbuf[...] = jnp.zeros_like(kbuf)
        vbuf[...] = jnp.zeros_like(vbuf)
        hstate[...] = jnp.zeros_like(hstate)

    meta = meta_ref[...]
    x = x_ref[...].reshape(ROWS, D_MODEL)
    x = jnp.where(n == 0, jnp.concatenate([meta] * nb, axis=0), x)

    q, k, v, u = _project(x, gpre_ref, win_ref)

    cos, sa, sb = cos_ref[...], sa_ref[...], sb_ref[...]
    lo = _lane_is_lo((BLOCK, LANES))

    _ssm_input(u, wbre_ref, wbim_ref, s_ref)
    is_re = lax.broadcasted_iota(jnp.int32, (2 * nb, LANES), 0) < nb
    a1, a2 = [], []
    for l in range(N_LB):
        cols = slice(l * LANES, (l + 1) * LANES)
        ai = jnp.broadcast_to(ai_ref[:, cols], (2 * nb, LANES))
        a1.append(jnp.broadcast_to(ar_ref[:, cols], (2 * nb, LANES)))
        a2.append(jnp.where(is_re, -ai, ai))

    def step(t, h):
        idx = pl.ds(t, 2 * nb, stride=BLOCK)
        out = []
        for l in range(N_LB):
            hn = a1[l] * h[l] + a2[l] * pltpu.roll(h[l], nb, axis=0) + s_ref[l, idx, :]
            s_ref[l, idx, :] = hn
            out.append(hn)
        return tuple(out)

    hfin = lax.fori_loop(0, BLOCK, step, tuple(hstate[l] for l in range(N_LB)))
    for l in range(N_LB):
        hstate[l] = hfin[l]
    ssm_o = _ssm_output(s_ref, u, wcre_ref, wcim_ref, d_ref, wglu_ref)

    r_i = _mod(lax.broadcasted_iota(jnp.int32, (2 * BLOCK, 2 * BLOCK), 0), BLOCK)
    c_i = lax.broadcasted_iota(jnp.int32, (2 * BLOCK, 2 * BLOCK), 1)
    c_min = jnp.where(n == 0, 2 * BLOCK - N_META, jnp.where(n == 1, BLOCK - N_META, 0))
    mask = (c_i > r_i) & (c_i <= r_i + WINDOW) & (c_i >= c_min)
    top = lax.broadcasted_iota(jnp.int32, (2 * BLOCK, 1), 0) < BLOCK

    def variants(t):
        tr = pltpu.roll(t, HALF, axis=1)
        z = jnp.zeros_like(t)
        return (jnp.where(lo, t, z), jnp.where(lo, z, tr),
                jnp.where(lo, tr, z), jnp.where(lo, z, t))

    for b in range(nb):
        rows = slice(b * BLOCK, (b + 1) * BLOCK)
        kb = _rope(k[rows], cos, sa, sb)
        vb = v[rows]
        for i, (kv_, vv_) in enumerate(zip(variants(kb), variants(vb))):
            kbuf[b, i, 0:BLOCK, :] = kbuf[b, i, BLOCK:2 * BLOCK, :]
            vbuf[b, i, 0:BLOCK, :] = vbuf[b, i, BLOCK:2 * BLOCK, :]
            kbuf[b, i, BLOCK:2 * BLOCK, :] = kv_.astype(BF16)
            vbuf[b, i, BLOCK:2 * BLOCK, :] = vv_.astype(BF16)

        @pl.when(n == last)
        def _emit_window(kb=kb, vb=vb, b=b):
            klast_ref[b] = kb
            vlast_ref[b] = vb

        qs = []
        for j in range(ATT_WIDTH // LANES):
            qj = _rope(q[rows, j * LANES:(j + 1) * LANES], cos, sa, sb) * (HEAD_DIM ** -0.5)
            qs.append(qj.astype(BF16))
        for g in range(N_KV_HEADS):
            qst = jnp.concatenate([qs[2 * g], qs[2 * g + 1]], axis=0)
            o = None
            for half in range(2):
                var = 2 * g + half
                h_top, h_bot = 4 * g + half, 4 * g + 2 + half
                sink_col = jnp.where(top, sink_ref[h_top], sink_ref[h_bot])
                e, den = _sink_softmax(_dot_nt(qst, kbuf[b, var]), mask, sink_col)
                part = _dot(e.astype(BF16), vbuf[b, var]) / den
                o = part if o is None else o + part
            att_ref[rows, (2 * g) * LANES:(2 * g + 1) * LANES] = o[0:BLOCK]
            att_ref[rows, (2 * g + 1) * LANES:(2 * g + 2) * LANES] = o[BLOCK:2 * BLOCK]

    h1 = _merge(x, att_ref[...], ssm_o, gatt_ref, gssm_ref, wout_ref, gpost_ref)
    h1_ref[...] = h1.reshape(h1_ref.shape)

    @pl.when(n == last)
    def _emit_state():
        for l in range(N_LB):
            sre_ref[:, l * LANES:(l + 1) * LANES] = hfin[l][0:nb]
            sim_ref[:, l * LANES:(l + 1) * LANES] = hfin[l][nb:2 * nb]


def _const_spec(shape):
    zeros = (0,) * len(shape)
    return pl.BlockSpec(shape, lambda *_: zeros)


def _prompt_mixer(x_prompt, meta_blk, cos, sa, sb, sinks, p):
    nb, seq, _ = x_prompt.shape
    n_blocks = seq // BLOCK + 1
    xmap = lambda n: (0, jnp.maximum(n - 1, 0), 0)
    tmap = lambda n: (n, 0)
    in_specs = [
        pl.BlockSpec(memory_space=pltpu.SMEM),
        pl.BlockSpec((nb, BLOCK, D_MODEL), xmap),
        _const_spec((BLOCK, D_MODEL)),
        pl.BlockSpec((BLOCK, LANES), tmap),
        pl.BlockSpec((BLOCK, LANES), tmap),
        pl.BlockSpec((BLOCK, LANES), tmap),
    ] + [_const_spec(a.shape) for a in p]
    out_shape = (
        jax.ShapeDtypeStruct((nb, seq, D_MODEL), F32),
        jax.ShapeDtypeStruct((nb, BLOCK, KV_WIDTH), F32),
        jax.ShapeDtypeStruct((nb, BLOCK, KV_WIDTH), F32),
        jax.ShapeDtypeStruct((nb, N_STATE), F32),
        jax.ShapeDtypeStruct((nb, N_STATE), F32),
    )
    out_specs = (
        pl.BlockSpec((nb, BLOCK, D_MODEL), xmap),
        _const_spec((nb, BLOCK, KV_WIDTH)),
        _const_spec((nb, BLOCK, KV_WIDTH)),
        _const_spec((nb, N_STATE)),
        _const_spec((nb, N_STATE)),
    )
    scratch = [
        pltpu.VMEM((nb, 4, 2 * BLOCK, LANES), BF16),
        pltpu.VMEM((nb, 4, 2 * BLOCK, LANES), BF16),
        pltpu.VMEM((N_LB, 2 * ROWS, LANES), F32),
        pltpu.VMEM((N_LB, 2 * nb, LANES), F32),
        pltpu.VMEM((ROWS, ATT_WIDTH), F32),
    ]
    return pl.pallas_call(
        _prompt_mixer_kernel,
        grid=(n_blocks,),
        in_specs=in_specs,
        out_specs=out_specs,
        out_shape=out_shape,
        scratch_shapes=scratch,
        compiler_params=pltpu.CompilerParams(
            dimension_semantics=("arbitrary",), vmem_limit_bytes=VMEM_LIMIT),
        name="prompt_mixer",
    )(sinks, x_prompt, meta_blk, cos, sa, sb, *p)


GROUP = 8


def _decode_mixer_kernel(
        x_ref, cos_ref, sa_ref, sb_ref, sink_ref, ck_ref, cv_ref, stre_ref, stim_ref,
        gpre_ref, win_ref, ar_ref, ai_ref, wbre_ref, wbim_ref, wcre_ref, wcim_ref,
        d_ref, wglu_ref, gatt_ref, gssm_ref, wout_ref, gpost_ref,
        h1_ref, kwin_ref, vwin_ref, sre_ref, sim_ref,
        qp, knew, vnew, ssm_s, att_ref, s_ref):
    g = pl.program_id(0)
    n_seq = stre_ref.shape[0]
    n_tok = ROWS // n_seq
    last = pl.num_programs(0) - 1

    @pl.when(g == 0)
    def _project_and_ssm():
        q, k, v, u = _project(x_ref[...], gpre_ref, win_ref)
        cos, sa, sb = cos_ref[...], sa_ref[...], sb_ref[...]
        knew[...] = _rope(k, cos, sa, sb)
        vnew[...] = v
        lo = _lane_is_lo((ROWS, LANES))
        for j in range(ATT_WIDTH // LANES):
            qj = _rope(q[:, j * LANES:(j + 1) * LANES], cos, sa, sb) * (HEAD_DIM ** -0.5)
            qr = pltpu.roll(qj, HALF, axis=1)
            z = jnp.zeros_like(qj)
            if j < 2:
                qp[2 * j] = jnp.where(lo, qj, z)
                qp[2 * j + 1] = jnp.where(lo, qr, z)
            else:
                qp[2 * j] = jnp.where(lo, z, qr)
                qp[2 * j + 1] = jnp.where(lo, z, qj)

        _ssm_input(u, wbre_ref, wbim_ref, s_ref)
        for l in range(N_LB):
            cols = slice(l * LANES, (l + 1) * LANES)
            ar, ai = ar_ref[:, cols], ai_ref[:, cols]
            hr, hi = stre_ref[:, cols], stim_ref[:, cols]
            for t in range(n_tok):
                r_re = slice(t * n_seq, (t + 1) * n_seq)
                r_im = slice(ROWS + t * n_seq, ROWS + (t + 1) * n_seq)
                hr, hi = (ar * hr - ai * hi + s_ref[l, r_re, :],
                          ar * hi + ai * hr + s_ref[l, r_im, :])
                s_ref[l, r_re, :] = hr
                s_ref[l, r_im, :] = hi
            sre_ref[:, cols] = hr
            sim_ref[:, cols] = hi
        ssm_s[...] = _ssm_output(s_ref, u, wcre_ref, wcim_ref, d_ref, wglu_ref)

    def rows_of(t):
        return pl.ds(pl.multiple_of(t * n_seq + g * GROUP, GROUP), GROUP)

    qb = jnp.concatenate(
        [qp[h, rows_of(t), :] for h in range(N_HEADS) for t in range(n_tok)], axis=0).astype(BF16)
    kn_f = [knew[rows_of(t), :] for t in range(n_tok)]
    vn_f = [vnew[rows_of(t), :] for t in range(n_tok)]
    kn = jnp.concatenate(kn_f, axis=0).astype(BF16)
    vn = jnp.concatenate(vn_f, axis=0).astype(BF16)
    kc_f = ck_ref[...]
    vc_f = cv_ref[...]
    n_q = N_HEADS * n_tok * GROUP
    n_c = GROUP * WINDOW
    n_n = n_tok * GROUP

    lc = _dot_nt(qb, kc_f.astype(BF16))
    ln = _dot_nt(qb, kn)
    r_c = lax.broadcasted_iota(jnp.int32, (n_q, n_c), 0)
    c_c = lax.broadcasted_iota(jnp.int32, (n_q, n_c), 1)
    mask_c = ((_div(c_c, WINDOW) == _mod(r_c, GROUP))
              & (_mod(c_c, WINDOW) > _mod(_div(r_c, GROUP), n_tok)))
    r_n = lax.broadcasted_iota(jnp.int32, (n_q, n_n), 0)
    c_n = lax.broadcasted_iota(jnp.int32, (n_q, n_n), 1)
    mask_n = ((_mod(c_n, GROUP) == _mod(r_n, GROUP))
              & (_div(c_n, GROUP) <= _mod(_div(r_n, GROUP), n_tok)))

    sink_col = sink_ref[...]
    lc = jnp.where(mask_c, lc, NEG)
    ln = jnp.where(mask_n, ln, NEG)
    m = jnp.maximum(jnp.maximum(jnp.max(lc, axis=-1, keepdims=True),
                                jnp.max(ln, axis=-1, keepdims=True)), sink_col)
    ec = jnp.exp(lc - m)
    en = jnp.exp(ln - m)
    den = (jnp.sum(ec, axis=-1, keepdims=True) + jnp.sum(en, axis=-1, keepdims=True)
           + jnp.exp(sink_col - m))
    o = (_dot(ec.astype(BF16), vc_f.astype(BF16)) + _dot(en.astype(BF16), vn)) / den

    per_head = n_tok * GROUP
    lo = _lane_is_lo((per_head, LANES))
    for j in range(ATT_WIDTH // LANES):
        o_even = o[(2 * j) * per_head:(2 * j + 1) * per_head]
        o_odd = o[(2 * j + 1) * per_head:(2 * j + 2) * per_head]
        if j < 2:
            pair = jnp.where(lo, o_even, pltpu.roll(o_odd, HALF, axis=1))
        else:
            pair = jnp.where(lo, pltpu.roll(o_even, HALF, axis=1), o_odd)
        for t in range(n_tok):
            att_ref[rows_of(t), j * LANES:(j + 1) * LANES] = pair[t * GROUP:(t + 1) * GROUP]

    for b in range(GROUP):
        rows = slice(b * WINDOW, (b + 1) * WINDOW)
        kwin_ref[rows, :] = pltpu.roll(kc_f[rows], WINDOW - n_tok, axis=0)
        vwin_ref[rows, :] = pltpu.roll(vc_f[rows], WINDOW - n_tok, axis=0)
    for t in range(n_tok):
        dst = pl.ds(WINDOW - n_tok + t, GROUP, stride=WINDOW)
        kwin_ref[dst, :] = kn_f[t]
        vwin_ref[dst, :] = vn_f[t]

    @pl.when(g == last)
    def _merge_out():
        h1_ref[...] = _merge(x_ref[...], att_ref[...], ssm_s[...],
                             gatt_ref, gssm_ref, wout_ref, gpost_ref)


def _decode_mixer(x2d, cos, sa, sb, sink_col, ck2d, cv2d, st_re, st_im, p):
    n_seq = st_re.shape[0]
    n_groups = n_seq // GROUP
    cmap = lambda g: (g, 0)
    in_specs = [
        _const_spec(x2d.shape), _const_spec(cos.shape), _const_spec(sa.shape),
        _const_spec(sb.shape), _const_spec(sink_col.shape),
        pl.BlockSpec((GROUP * WINDOW, KV_WIDTH), cmap),
        pl.BlockSpec((GROUP * WINDOW, KV_WIDTH), cmap),
        _const_spec(st_re.shape), _const_spec(st_im.shape),
    ] + [_const_spec(a.shape) for a in p]
    out_shape = (
        jax.ShapeDtypeStruct(x2d.shape, F32),
        jax.ShapeDtypeStruct(ck2d.shape, F32),
        jax.ShapeDtypeStruct(cv2d.shape, F32),
        jax.ShapeDtypeStruct(st_re.shape, F32),
        jax.ShapeDtypeStruct(st_im.shape, F32),
    )
    out_specs = (
        _const_spec(x2d.shape),
        pl.BlockSpec((GROUP * WINDOW, KV_WIDTH), cmap),
        pl.BlockSpec((GROUP * WINDOW, KV_WIDTH), cmap),
        _const_spec(st_re.shape), _const_spec(st_im.shape),
    )
    scratch = [
        pltpu.VMEM((N_HEADS, ROWS, LANES), F32),
        pltpu.VMEM((ROWS, KV_WIDTH), F32),
        pltpu.VMEM((ROWS, KV_WIDTH), F32),
        pltpu.VMEM((ROWS, SSM_WIDTH), F32),
        pltpu.VMEM((ROWS, ATT_WIDTH), F32),
        pltpu.VMEM((N_LB, 2 * ROWS, LANES), F32),
    ]
    return pl.pallas_call(
        _decode_mixer_kernel,
        grid=(n_groups,),
        in_specs=in_specs,
        out_specs=out_specs,
        out_shape=out_shape,
        scratch_shapes=scratch,
        compiler_params=pltpu.CompilerParams(
            dimension_semantics=("arbitrary",), vmem_limit_bytes=VMEM_LIMIT),
        name="decode_mixer",
    )(x2d, cos, sa, sb, sink_col, ck2d, cv2d, st_re, st_im, *p)


def _mlp_kernel(x_ref, gpre_ref, wup_ref, wdn_ref, gpost_ref, o_ref):
    x = x_ref[...]
    hn = _rms(x, gpre_ref[...]).astype(BF16)
    acc = None
    for c in range(D_FF // FF_CHUNK):
        cols = slice(c * FF_CHUNK, (c + 1) * FF_CHUNK)
        a = jnp.maximum(_dot(hn, wup_ref[:, cols]), 0.0)
        part = _dot((a * a).astype(BF16), wdn_ref[cols, :])
        acc = part if acc is None else acc + part
    o_ref[...] = x + _rms(acc, gpost_ref[...])


def _mlp(x2d, g_pre, w_up, w_down, g_post):
    n = x2d.shape[0]
    rmap = lambda i: (i, 0)
    return pl.pallas_call(
        _mlp_kernel,
        grid=(n // ROWS,),
        in_specs=[pl.BlockSpec((ROWS, D_MODEL), rmap), _const_spec(g_pre.shape),
                  _const_spec(w_up.shape), _const_spec(w_down.shape), _const_spec(g_post.shape)],
        out_specs=pl.BlockSpec((ROWS, D_MODEL), rmap),
        out_shape=jax.ShapeDtypeStruct(x2d.shape, F32),
        compiler_params=pltpu.CompilerParams(
            dimension_semantics=("arbitrary",), vmem_limit_bytes=VMEM_LIMIT),
        name="mlp",
    )(x2d, g_pre, w_up, w_down, g_post)


def _zoh(a_re, a_im, log_dt, b_re, b_im):
    dt = jnp.exp(log_dt)[:, None]
    mag = jnp.exp(a_re * dt)
    abar_re, abar_im = mag * jnp.cos(a_im * dt), mag * jnp.sin(a_im * dt)
    nr, ni = abar_re - 1.0, abar_im
    den = a_re * a_re + a_im * a_im
    coef_re = (nr * a_re + ni * a_im) / den
    coef_im = (ni * a_re - nr * a_im) / den
    bbar_re = coef_re[..., None] * b_re - coef_im[..., None] * b_im
    bbar_im = coef_re[..., None] * b_im + coef_im[..., None] * b_re
    return abar_re, abar_im, bbar_re, bbar_im


def _block_diag_in(bbar):
    gt = SSM_TILE // SSM_GROUP
    w = bbar.reshape(N_SSM_TILES, gt, SSM_STATE, SSM_GROUP)
    eye = jnp.eye(gt, dtype=F32)
    w = jnp.einsum('cgph,gk->cghkp', w, eye)
    return w.reshape(N_SSM_TILES, SSM_TILE, STATE_TILE).astype(BF16)


def _block_diag_out(c):
    gt = SSM_TILE // SSM_GROUP
    w = c.reshape(N_SSM_TILES, gt, SSM_GROUP, SSM_STATE)
    eye = jnp.eye(gt, dtype=F32)
    w = jnp.einsum('cghp,gk->ckpgh', w, eye)
    return w.reshape(N_SSM_TILES, STATE_TILE, SSM_TILE).astype(BF16)


def _rope_tables(pos):
    half = ROPE_DIM // 2
    inv = ROPE_THETA ** (-jnp.arange(half, dtype=F32) / half)
    ang = pos.astype(F32)[:, None] * inv[None, :]
    cos, sin = jnp.cos(ang), jnp.sin(ang)
    d = jnp.arange(LANES) % HEAD_DIM
    f = d % half
    cos_t = jnp.where(d[None, :] < ROPE_DIM, cos[:, f], 1.0)
    sa_t = jnp.where((d[None, :] >= half) & (d[None, :] < ROPE_DIM), sin[:, f], 0.0)
    sb_t = jnp.where(d[None, :] < half, -sin[:, f], 0.0)
    return cos_t, sa_t, sb_t


def kernel(x_prompt, x_sample, cache_k_win, cache_v_win, state_ssm_re, state_ssm_im, meta_tokens, norm_mix_pre, w_in, attn_sinks, ssm_a_re, ssm_a_im, ssm_log_dt, ssm_b_re, ssm_b_im, ssm_c_re, ssm_c_im, ssm_d, w_glu, norm_att_out, norm_ssm_out, w_out, norm_mix_post, norm_mlp_pre, w_up, w_down, norm_mlp_post):
    depth = w_in.shape[0]
    assert depth == 1
    l = 0
    nb, seq, _ = x_prompt.shape
    n_seq, n_tok, _ = x_sample.shape
    assert n_seq * n_tok == ROWS and nb * BLOCK == ROWS and seq % BLOCK == 0

    abar_re, abar_im, bbar_re, bbar_im = _zoh(
        ssm_a_re[l], ssm_a_im[l], ssm_log_dt[l], ssm_b_re[l], ssm_b_im[l])
    row = lambda a: a.reshape(1, -1)
    params = (
        row(norm_mix_pre[l]), w_in[l].astype(BF16),
        row(abar_re), row(abar_im),
        _block_diag_in(bbar_re), _block_diag_in(bbar_im),
        _block_diag_out(ssm_c_re[l]), _block_diag_out(-ssm_c_im[l]),
        row(ssm_d[l]), w_glu[l].astype(BF16),
        row(norm_att_out[l]), row(norm_ssm_out[l]), w_out[l].astype(BF16), row(norm_mix_post[l]),
    )
    mlp_params = (row(norm_mlp_pre[l]), w_up[l].astype(BF16), w_down[l].astype(BF16),
                  row(norm_mlp_post[l]))

    front = BLOCK - N_META
    meta_blk = jnp.concatenate([jnp.zeros((front, D_MODEL), F32), meta_tokens], axis=0)
    cos, sa, sb = _rope_tables(jnp.arange(seq + BLOCK, dtype=jnp.int32) - front)
    h1, k_last, v_last, p_re, p_im = _prompt_mixer(
        x_prompt, meta_blk, cos, sa, sb, attn_sinks[l], params)
    y_prompt = _mlp(h1.reshape(nb * seq, D_MODEL), *mlp_params).reshape(nb, seq, D_MODEL)

    pos_s = PAST_LEN + jnp.arange(n_tok, dtype=jnp.int32)
    cos_s, sa_s, sb_s = (jnp.repeat(t, n_seq, axis=0) for t in _rope_tables(pos_s))
    sink_col = jnp.repeat(attn_sinks[l], n_tok * GROUP).reshape(-1, 1)
    ck2d = cache_k_win[l].reshape(n_seq * WINDOW, KV_WIDTH)
    cv2d = cache_v_win[l].reshape(n_seq * WINDOW, KV_WIDTH)
    xs_tm = jnp.swapaxes(x_sample, 0, 1).reshape(ROWS, D_MODEL)
    h1s, kwin, vwin, s_re, s_im = _decode_mixer(
        xs_tm, cos_s, sa_s, sb_s, sink_col, ck2d, cv2d,
        state_ssm_re[l].reshape(n_seq, N_STATE), state_ssm_im[l].reshape(n_seq, N_STATE), params)
    y_sample = jnp.swapaxes(_mlp(h1s, *mlp_params).reshape(n_tok, n_seq, D_MODEL), 0, 1)

    win = lambda a, n: a.reshape(1, n, WINDOW, N_KV_HEADS, HEAD_DIM)
    st = lambda a, n: a.reshape(1, n, SSM_GROUPS, SSM_STATE)
    return (y_prompt, y_sample,
            win(k_last, nb), win(v_last, nb), st(p_re, nb), st(p_im, nb),
            win(kwin, n_seq), win(vwin, n_seq), st(s_re, n_seq), st(s_im, n_seq))
```

```python
import functools
import math

import jax
import jax.numpy as jnp
from jax import lax
from jax.experimental import pallas as pl
from jax.experimental.pallas import tpu as pltpu

F32 = jnp.float32
BF16 = jnp.bfloat16

N_META = 16
HEAD_DIM = 64
N_HEADS = 8
N_KV_HEADS = 2
WINDOW = 128
BLOCK = 128
ROPE_DIM = 16
ROPE_THETA = 500000.0
SSM_GROUP = 16
SSM_GROUPS = 32
SSM_STATE = 64
PAST_LEN = 8192
EPS = 1e-6
NEG = -1e30

D_MODEL = 1024
ATT_WIDTH = 512
KV_WIDTH = 128
SSM_WIDTH = 512
N_STATE = SSM_GROUPS * SSM_STATE
D_FF = 4096

ROWS = 512
LANES = 128
HALF = LANES // 2
SSM_TILE = 256
STATE_TILE = (SSM_TILE // SSM_GROUP) * SSM_STATE
N_SSM_TILES = SSM_WIDTH // SSM_TILE
LB_PER_TILE = STATE_TILE // LANES
N_LB = N_STATE // LANES
FF_CHUNK = 1024
VMEM_LIMIT = 56 * 1024 * 1024


def _dot(a, b):
    return jnp.dot(a, b, preferred_element_type=F32)


def _dot_nt(a, b):
    return lax.dot_general(a, b, (((1,), (1,)), ((), ())), preferred_element_type=F32)


def _rms(x, g):
    return x * lax.rsqrt(jnp.mean(x * x, axis=-1, keepdims=True) + EPS) * g


def _rope(x, cos, sa, sb):
    return x * cos + pltpu.roll(x, 8, axis=1) * sa + pltpu.roll(x, LANES - 8, axis=1) * sb


def _lane_is_lo(shape):
    return lax.broadcasted_iota(jnp.int32, shape, 1) < HALF


def _project_qkv(hn, w_in_ref):
    q = _dot(hn, w_in_ref[:, 0:ATT_WIDTH])
    kv = _dot(hn, w_in_ref[:, ATT_WIDTH:ATT_WIDTH + 2 * KV_WIDTH])
    return q, kv[:, 0:KV_WIDTH], kv[:, KV_WIDTH:2 * KV_WIDTH]


def _project_u(hn, w_in_ref):
    return _dot(hn, w_in_ref[:, ATT_WIDTH + 2 * KV_WIDTH:])


def _ssm_input(u, wbre_ref, wbim_ref, s_ref):
    ub = u.astype(BF16)
    for c in range(N_SSM_TILES):
        uc = ub[:, c * SSM_TILE:(c + 1) * SSM_TILE]
        bre, bim = _dot(uc, wbre_ref[c]), _dot(uc, wbim_ref[c])
        for l in range(LB_PER_TILE):
            s_ref[c * LB_PER_TILE + l, 0:ROWS, :] = bre[:, l * LANES:(l + 1) * LANES]
            s_ref[c * LB_PER_TILE + l, ROWS:2 * ROWS, :] = bim[:, l * LANES:(l + 1) * LANES]


def _ssm_output(s_ref, u, wcre_ref, wcim_ref, d_ref, wglu_ref):
    ys = []
    for c in range(N_SSM_TILES):
        blocks = range(c * LB_PER_TILE, (c + 1) * LB_PER_TILE)
        hr = jnp.concatenate([s_ref[l, 0:ROWS, :].astype(BF16) for l in blocks], axis=1)
        hi = jnp.concatenate([s_ref[l, ROWS:2 * ROWS, :].astype(BF16) for l in blocks], axis=1)
        ys.append(_dot(hr, wcre_ref[c]) + _dot(hi, wcim_ref[c]))
    y = jnp.concatenate(ys, axis=1) + d_ref[...] * u
    z = 0.5 * y * (1.0 + jnp.tanh(math.sqrt(2.0 / math.pi) * (y + 0.044715 * (y * y * y))))
    gate = 1.0 / (1.0 + jnp.exp(-_dot(z.astype(BF16), wglu_ref[...])))
    return z * gate


def _merge(x, att, s, gatt_ref, wout_ref, gpost_ref):
    a = _rms(att, gatt_ref[...]).astype(BF16)
    m = _dot(a, wout_ref[0:ATT_WIDTH, :]) + _dot(s, wout_ref[ATT_WIDTH:, :])
    return x + _rms(m, gpost_ref[...])


def _div(x, k):
    return lax.shift_right_logical(x, int(math.log2(k)))


def _mod(x, k):
    return lax.bitwise_and(x, k - 1)


def _sink_softmax(logits, mask, sink_col):
    lm = jnp.where(mask, logits, NEG)
    m = jnp.maximum(jnp.max(lm, axis=-1, keepdims=True), sink_col)
    e = jnp.exp(lm - m)
    den = jnp.sum(e, axis=-1, keepdims=True) + jnp.exp(sink_col - m)
    return e, den


def _prompt_mixer_kernel(
        sink_ref, x_ref, meta_ref, cos_ref, sa_ref, sb_ref, perm_ref, permt_ref,
        gpre_ref, win_ref,
        ar_ref, ai_ref, wbre_ref, wbim_ref, wcre_ref, wcim_ref, d_ref, wglu_ref,
        gatt_ref, gssm_ref, wout_ref, gpost_ref,
        h1_ref, klast_ref, vlast_ref, sre_ref, sim_ref,
        kbuf, vbuf, s_ref, hstate, att_ref):
    n = pl.program_id(0)
    nb = x_ref.shape[0]
    assert 2 * nb == 8
    last = pl.num_programs(0) - 1

    @pl.when(n == 0)
    def _init():
        kbuf[...] = jnp.zeros_like(kbuf)
        vbuf[...] = jnp.zeros_like(vbuf)
        hstate[...] = jnp.zeros_like(hstate)

    meta = meta_ref[...]
    x = x_ref[...].reshape(ROWS, D_MODEL)
    x = jnp.where(n == 0, jnp.concatenate([meta] * nb, axis=0), x)

    hn = _rms(x, gpre_ref[...]).astype(BF16)
    q, k, v = _project_qkv(hn, win_ref)

    cos, sa, sb = cos_ref[...], sa_ref[...], sb_ref[...]
    lo = _lane_is_lo((BLOCK, LANES))

    u = _project_u(_dot(perm_ref[...], hn).astype(BF16), win_ref)
    _ssm_input(u, wbre_ref, wbim_ref, s_ref)
    is_re = lax.broadcasted_iota(jnp.int32, (2 * nb, LANES), 0) < nb
    a1, a2 = [], []
    for l in range(N_LB):
        cols = slice(l * LANES, (l + 1) * LANES)
        ai = jnp.broadcast_to(ai_ref[:, cols], (2 * nb, LANES))
        a1.append(jnp.broadcast_to(ar_ref[:, cols], (2 * nb, LANES)))
        a2.append(jnp.where(is_re, -ai, ai))

    def swap(t):
        return pltpu.roll(t, nb, axis=0)

    def two_steps(i, h):
        r_re = pl.ds(pl.multiple_of(i * 2 * nb, 2 * nb), 2 * nb)
        r_im = pl.ds(pl.multiple_of(ROWS + i * 2 * nb, 2 * nb), 2 * nb)
        out = []
        for l in range(N_LB):
            re, im = s_ref[l, r_re, :], s_ref[l, r_im, :]
            h0 = a1[l] * h[l] + a2[l] * swap(h[l]) + jnp.where(is_re, re, swap(im))
            h1 = a1[l] * h0 + a2[l] * swap(h0) + jnp.where(is_re, swap(re), im)
            s_ref[l, r_re, :] = jnp.where(is_re, h0, swap(h1))
            s_ref[l, r_im, :] = jnp.where(is_re, swap(h0), h1)
            out.append(h1)
        return tuple(out)

    hfin = lax.fori_loop(0, BLOCK // 2, two_steps, tuple(hstate[l] for l in range(N_LB)))
    for l in range(N_LB):
        hstate[l] = hfin[l]
    ssm_o = _ssm_output(s_ref, u, wcre_ref, wcim_ref, d_ref, wglu_ref)
    ssm_n = _dot(permt_ref[...], _rms(ssm_o, gssm_ref[...]).astype(BF16)).astype(BF16)

    r_i = _mod(lax.broadcasted_iota(jnp.int32, (2 * BLOCK, 2 * BLOCK), 0), BLOCK)
    c_i = lax.broadcasted_iota(jnp.int32, (2 * BLOCK, 2 * BLOCK), 1)
    c_min = jnp.where(n == 0, 2 * BLOCK - N_META, jnp.where(n == 1, BLOCK - N_META, 0))
    mask = (c_i > r_i) & (c_i <= r_i + WINDOW) & (c_i >= c_min)
    top = lax.broadcasted_iota(jnp.int32, (2 * BLOCK, 1), 0) < BLOCK

    def variants(t):
        tr = pltpu.roll(t, HALF, axis=1)
        z = jnp.zeros_like(t)
        return (jnp.where(lo, t, z), jnp.where(lo, z, tr),
                jnp.where(lo, tr, z), jnp.where(lo, z, t))

    for b in range(nb):
        rows = slice(b * BLOCK, (b + 1) * BLOCK)
        kb = _rope(k[rows], cos, sa, sb)
        vb = v[rows]
        for i, (kv_, vv_) in enumerate(zip(variants(kb), variants(vb))):
            kbuf[b, i, 0:BLOCK, :] = kbuf[b, i, BLOCK:2 * BLOCK, :]
            vbuf[b, i, 0:BLOCK, :] = vbuf[b, i, BLOCK:2 * BLOCK, :]
            kbuf[b, i, BLOCK:2 * BLOCK, :] = kv_.astype(BF16)
            vbuf[b, i, BLOCK:2 * BLOCK, :] = vv_.astype(BF16)

        @pl.when(n == last)
        def _emit_window(kb=kb, vb=vb, b=b):
            klast_ref[b] = kb
            vlast_ref[b] = vb

        qs = []
        for j in range(ATT_WIDTH // LANES):
            qj = _rope(q[rows, j * LANES:(j + 1) * LANES], cos, sa, sb) * (HEAD_DIM ** -0.5)
            qs.append(qj.astype(BF16))
        for g in range(N_KV_HEADS):
            qst = jnp.concatenate([qs[2 * g], qs[2 * g + 1]], axis=0)
            o = None
            for half in range(2):
                var = 2 * g + half
                h_top, h_bot = 4 * g + half, 4 * g + 2 + half
                sink_col = jnp.where(top, sink_ref[h_top], sink_ref[h_bot])
                e, den = _sink_softmax(_dot_nt(qst, kbuf[b, var]), mask, sink_col)
                part = _dot(e.astype(BF16), vbuf[b, var]) / den
                o = part if o is None else o + part
            att_ref[rows, (2 * g) * LANES:(2 * g + 1) * LANES] = o[0:BLOCK]
            att_ref[rows, (2 * g + 1) * LANES:(2 * g + 2) * LANES] = o[BLOCK:2 * BLOCK]

    h1 = _merge(x, att_ref[...], ssm_n, gatt_ref, wout_ref, gpost_ref)
    h1_ref[...] = h1.reshape(h1_ref.shape)

    @pl.when(n == last)
    def _emit_state():
        for l in range(N_LB):
            sre_ref[:, l * LANES:(l + 1) * LANES] = hfin[l][0:nb]
            sim_ref[:, l * LANES:(l + 1) * LANES] = hfin[l][nb:2 * nb]


def _const_spec(shape):
    zeros = (0,) * len(shape)
    return pl.BlockSpec(shape, lambda *_: zeros)


def _prompt_mixer(x_prompt, meta_blk, cos, sa, sb, sinks, p):
    nb, seq, _ = x_prompt.shape
    n_blocks = seq // BLOCK + 1
    r = jnp.arange(ROWS)
    perm = jax.nn.one_hot((r % nb) * BLOCK + r // nb, ROWS, dtype=BF16)
    xmap = lambda n: (0, jnp.maximum(n - 1, 0), 0)
    tmap = lambda n: (n, 0)
    in_specs = [
        pl.BlockSpec(memory_space=pltpu.SMEM),
        pl.BlockSpec((nb, BLOCK, D_MODEL), xmap),
        _const_spec((BLOCK, D_MODEL)),
        pl.BlockSpec((BLOCK, LANES), tmap),
        pl.BlockSpec((BLOCK, LANES), tmap),
        pl.BlockSpec((BLOCK, LANES), tmap),
        _const_spec((ROWS, ROWS)),
        _const_spec((ROWS, ROWS)),
    ] + [_const_spec(a.shape) for a in p]
    out_shape = (
        jax.ShapeDtypeStruct((nb, seq, D_MODEL), F32),
        jax.ShapeDtypeStruct((nb, BLOCK, KV_WIDTH), F32),
        jax.ShapeDtypeStruct((nb, BLOCK, KV_WIDTH), F32),
        jax.ShapeDtypeStruct((nb, N_STATE), F32),
        jax.ShapeDtypeStruct((nb, N_STATE), F32),
    )
    out_specs = (
        pl.BlockSpec((nb, BLOCK, D_MODEL), xmap),
        _const_spec((nb, BLOCK, KV_WIDTH)),
        _const_spec((nb, BLOCK, KV_WIDTH)),
        _const_spec((nb, N_STATE)),
        _const_spec((nb, N_STATE)),
    )
    scratch = [
        pltpu.VMEM((nb, 4, 2 * BLOCK, LANES), BF16),
        pltpu.VMEM((nb, 4, 2 * BLOCK, LANES), BF16),
        pltpu.VMEM((N_LB, 2 * ROWS, LANES), F32),
        pltpu.VMEM((N_LB, 2 * nb, LANES), F32),
        pltpu.VMEM((ROWS, ATT_WIDTH), F32),
    ]
    return pl.pallas_call(
        _prompt_mixer_kernel,
        grid=(n_blocks,),
        in_specs=in_specs,
        out_specs=out_specs,
        out_shape=out_shape,
        scratch_shapes=scratch,
        compiler_params=pltpu.CompilerParams(
            dimension_semantics=("arbitrary",), vmem_limit_bytes=VMEM_LIMIT),
        name="prompt_mixer",
    )(sinks, x_prompt, meta_blk, cos, sa, sb, perm, perm.T, *p)


GROUP = 8


def _decode_mixer_kernel(
        x_ref, cos_ref, sa_ref, sb_ref, sink_ref, ck_ref, cv_ref, stre_ref, stim_ref,
        gpre_ref, win_ref, ar_ref, ai_ref, wbre_ref, wbim_ref, wcre_ref, wcim_ref,
        d_ref, wglu_ref, gatt_ref, gssm_ref, wout_ref, gpost_ref,
        h1_ref, kwin_ref, vwin_ref, sre_ref, sim_ref,
        qp, knew, vnew, ssm_s, att_ref, s_ref):
    g = pl.program_id(0)
    n_seq = stre_ref.shape[0]
    n_tok = ROWS // n_seq
    last = pl.num_programs(0) - 1

    @pl.when(g == 0)
    def _project_and_ssm():
        hn = _rms(x_ref[...], gpre_ref[...]).astype(BF16)
        q, k, v = _project_qkv(hn, win_ref)
        u = _project_u(hn, win_ref)
        cos, sa, sb = cos_ref[...], sa_ref[...], sb_ref[...]
        knew[...] = _rope(k, cos, sa, sb)
        vnew[...] = v
        lo = _lane_is_lo((ROWS, LANES))
        for j in range(ATT_WIDTH // LANES):
            qj = _rope(q[:, j * LANES:(j + 1) * LANES], cos, sa, sb) * (HEAD_DIM ** -0.5)
            qr = pltpu.roll(qj, HALF, axis=1)
            z = jnp.zeros_like(qj)
            if j < 2:
                qp[2 * j] = jnp.where(lo, qj, z)
                qp[2 * j + 1] = jnp.where(lo, qr, z)
            else:
                qp[2 * j] = jnp.where(lo, z, qr)
                qp[2 * j + 1] = jnp.where(lo, z, qj)

        _ssm_input(u, wbre_ref, wbim_ref, s_ref)
        for l in range(N_LB):
            cols = slice(l * LANES, (l + 1) * LANES)
            ar, ai = ar_ref[:, cols], ai_ref[:, cols]
            hr, hi = stre_ref[:, cols], stim_ref[:, cols]
            for t in range(n_tok):
                r_re = slice(t * n_seq, (t + 1) * n_seq)
                r_im = slice(ROWS + t * n_seq, ROWS + (t + 1) * n_seq)
                hr, hi = (ar * hr - ai * hi + s_ref[l, r_re, :],
                          ar * hi + ai * hr + s_ref[l, r_im, :])
                s_ref[l, r_re, :] = hr
                s_ref[l, r_im, :] = hi
            sre_ref[:, cols] = hr
            sim_ref[:, cols] = hi
        ssm_s[...] = _ssm_output(s_ref, u, wcre_ref, wcim_ref, d_ref, wglu_ref)

    def rows_of(t):
        return pl.ds(pl.multiple_of(t * n_seq + g * GROUP, GROUP), GROUP)

    qb = jnp.concatenate(
        [qp[h, rows_of(t), :] for h in range(N_HEADS) for t in range(n_tok)], axis=0).astype(BF16)
    kn_f = [knew[rows_of(t), :] for t in range(n_tok)]
    vn_f = [vnew[rows_of(t), :] for t in range(n_tok)]
    kn = jnp.concatenate(kn_f, axis=0).astype(BF16)
    vn = jnp.concatenate(vn_f, axis=0).astype(BF16)
    kc_f = ck_ref[...]
    vc_f = cv_ref[...]
    n_q = N_HEADS * n_tok * GROUP
    n_c = GROUP * WINDOW
    n_n = n_tok * GROUP

    lc = _dot_nt(qb, kc_f.astype(BF16))
    ln = _dot_nt(qb, kn)
    r_c = lax.broadcasted_iota(jnp.int32, (n_q, n_c), 0)
    c_c = lax.broadcasted_iota(jnp.int32, (n_q, n_c), 1)
    mask_c = ((_div(c_c, WINDOW) == _mod(r_c, GROUP))
              & (_mod(c_c, WINDOW) > _mod(_div(r_c, GROUP), n_tok)))
    r_n = lax.broadcasted_iota(jnp.int32, (n_q, n_n), 0)
    c_n = lax.broadcasted_iota(jnp.int32, (n_q, n_n), 1)
    mask_n = ((_mod(c_n, GROUP) == _mod(r_n, GROUP))
              & (_div(c_n, GROUP) <= _mod(_div(r_n, GROUP), n_tok)))

    sink_col = sink_ref[...]
    lc = jnp.where(mask_c, lc, NEG)
    ln = jnp.where(mask_n, ln, NEG)
    m = jnp.maximum(jnp.maximum(jnp.max(lc, axis=-1, keepdims=True),
                                jnp.max(ln, axis=-1, keepdims=True)), sink_col)
    ec = jnp.exp(lc - m)
    en = jnp.exp(ln - m)
    den = (jnp.sum(ec, axis=-1, keepdims=True) + jnp.sum(en, axis=-1, keepdims=True)
           + jnp.exp(sink_col - m))
    o = (_dot(ec.astype(BF16), vc_f.astype(BF16)) + _dot(en.astype(BF16), vn)) / den

    per_head = n_tok * GROUP
    lo = _lane_is_lo((per_head, LANES))
    for j in range(ATT_WIDTH // LANES):
        o_even = o[(2 * j) * per_head:(2 * j + 1) * per_head]
        o_odd = o[(2 * j + 1) * per_head:(2 * j + 2) * per_head]
        if j < 2:
            pair = jnp.where(lo, o_even, pltpu.roll(o_odd, HALF, axis=1))
        else:
            pair = jnp.where(lo, pltpu.roll(o_even, HALF, axis=1), o_odd)
        for t in range(n_tok):
            att_ref[rows_of(t), j * LANES:(j + 1) * LANES] = pair[t * GROUP:(t + 1) * GROUP]

    for b in range(GROUP):
        rows = slice(b * WINDOW, (b + 1) * WINDOW)
        kwin_ref[rows, :] = pltpu.roll(kc_f[rows], WINDOW - n_tok, axis=0)
        vwin_ref[rows, :] = pltpu.roll(vc_f[rows], WINDOW - n_tok, axis=0)
    for t in range(n_tok):
        dst = pl.ds(WINDOW - n_tok + t, GROUP, stride=WINDOW)
        kwin_ref[dst, :] = kn_f[t]
        vwin_ref[dst, :] = vn_f[t]

    @pl.when(g == last)
    def _merge_out():
        h1_ref[...] = _merge(x_ref[...], att_ref[...],
                             _rms(ssm_s[...], gssm_ref[...]).astype(BF16),
                             gatt_ref, wout_ref, gpost_ref)


def _decode_mixer(x2d, cos, sa, sb, sink_col, ck2d, cv2d, st_re, st_im, p):
    n_seq = st_re.shape[0]
    n_groups = n_seq // GROUP
    cmap = lambda g: (g, 0)
    in_specs = [
        _const_spec(x2d.shape), _const_spec(cos.shape), _const_spec(sa.shape),
        _const_spec(sb.shape), _const_spec(sink_col.shape),
        pl.BlockSpec((GROUP * WINDOW, KV_WIDTH), cmap),
        pl.BlockSpec((GROUP * WINDOW, KV_WIDTH), cmap),
        _const_spec(st_re.shape), _const_spec(st_im.shape),
    ] + [_const_spec(a.shape) for a in p]
    out_shape = (
        jax.ShapeDtypeStruct(x2d.shape, F32),
        jax.ShapeDtypeStruct(ck2d.shape, F32),
        jax.ShapeDtypeStruct(cv2d.shape, F32),
        jax.ShapeDtypeStruct(st_re.shape, F32),
        jax.ShapeDtypeStruct(st_im.shape, F32),
    )
    out_specs = (
        _const_spec(x2d.shape),
        pl.BlockSpec((GROUP * WINDOW, KV_WIDTH), cmap),
        pl.BlockSpec((GROUP * WINDOW, KV_WIDTH), cmap),
        _const_spec(st_re.shape), _const_spec(st_im.shape),
    )
    scratch = [
        pltpu.VMEM((N_HEADS, ROWS, LANES), F32),
        pltpu.VMEM((ROWS, KV_WIDTH), F32),
        pltpu.VMEM((ROWS, KV_WIDTH), F32),
        pltpu.VMEM((ROWS, SSM_WIDTH), F32),
        pltpu.VMEM((ROWS, ATT_WIDTH), F32),
        pltpu.VMEM((N_LB, 2 * ROWS, LANES), F32),
    ]
    return pl.pallas_call(
        _decode_mixer_kernel,
        grid=(n_groups,),
        in_specs=in_specs,
        out_specs=out_specs,
        out_shape=out_shape,
        scratch_shapes=scratch,
        compiler_params=pltpu.CompilerParams(
            dimension_semantics=("arbitrary",), vmem_limit_bytes=VMEM_LIMIT),
        name="decode_mixer",
    )(x2d, cos, sa, sb, sink_col, ck2d, cv2d, st_re, st_im, *p)


def _mlp_kernel(x_ref, gpre_ref, wup_ref, wdn_ref, gpost_ref, o_ref):
    x = x_ref[...]
    hn = _rms(x, gpre_ref[...]).astype(BF16)
    acc = None
    for c in range(D_FF // FF_CHUNK):
        cols = slice(c * FF_CHUNK, (c + 1) * FF_CHUNK)
        a = jnp.maximum(_dot(hn, wup_ref[:, cols]), 0.0)
        part = _dot((a * a).astype(BF16), wdn_ref[cols, :])
        acc = part if acc is None else acc + part
    o_ref[...] = x + _rms(acc, gpost_ref[...])


def _mlp(x2d, g_pre, w_up, w_down, g_post):
    n = x2d.shape[0]
    rmap = lambda i: (i, 0)
    return pl.pallas_call(
        _mlp_kernel,
        grid=(n // ROWS,),
        in_specs=[pl.BlockSpec((ROWS, D_MODEL), rmap), _const_spec(g_pre.shape),
                  _const_spec(w_up.shape), _const_spec(w_down.shape), _const_spec(g_post.shape)],
        out_specs=pl.BlockSpec((ROWS, D_MODEL), rmap),
        out_shape=jax.ShapeDtypeStruct(x2d.shape, F32),
        compiler_params=pltpu.CompilerParams(
            dimension_semantics=("arbitrary",), vmem_limit_bytes=VMEM_LIMIT),
        name="mlp",
    )(x2d, g_pre, w_up, w_down, g_post)


def _zoh(a_re, a_im, log_dt, b_re, b_im):
    dt = jnp.exp(log_dt)[:, None]
    mag = jnp.exp(a_re * dt)
    abar_re, abar_im = mag * jnp.cos(a_im * dt), mag * jnp.sin(a_im * dt)
    nr, ni = abar_re - 1.0, abar_im
    den = a_re * a_re + a_im * a_im
    coef_re = (nr * a_re + ni * a_im) / den
    coef_im = (ni * a_re - nr * a_im) / den
    bbar_re = coef_re[..., None] * b_re - coef_im[..., None] * b_im
    bbar_im = coef_re[..., None] * b_im + coef_im[..., None] * b_re
    return abar_re, abar_im, bbar_re, bbar_im


def _block_diag_in(bbar):
    gt = SSM_TILE // SSM_GROUP
    w = bbar.reshape(N_SSM_TILES, gt, SSM_STATE, SSM_GROUP)
    eye = jnp.eye(gt, dtype=F32)
    w = jnp.einsum('cgph,gk->cghkp', w, eye)
    return w.reshape(N_SSM_TILES, SSM_TILE, STATE_TILE).astype(BF16)


def _block_diag_out(c):
    gt = SSM_TILE // SSM_GROUP
    w = c.reshape(N_SSM_TILES, gt, SSM_GROUP, SSM_STATE)
    eye = jnp.eye(gt, dtype=F32)
    w = jnp.einsum('cghp,gk->ckpgh', w, eye)
    return w.reshape(N_SSM_TILES, STATE_TILE, SSM_TILE).astype(BF16)


def _rope_tables(pos):
    half = ROPE_DIM // 2
    inv = ROPE_THETA ** (-jnp.arange(half, dtype=F32) / half)
    ang = pos.astype(F32)[:, None] * inv[None, :]
    cos, sin = jnp.cos(ang), jnp.sin(ang)
    d = jnp.arange(LANES) % HEAD_DIM
    f = d % half
    cos_t = jnp.where(d[None, :] < ROPE_DIM, cos[:, f], 1.0)
    sa_t = jnp.where((d[None, :] >= half) & (d[None, :] < ROPE_DIM), sin[:, f], 0.0)
    sb_t = jnp.where(d[None, :] < half, -sin[:, f], 0.0)
    return cos_t, sa_t, sb_t


def kernel(x_prompt, x_sample, cache_k_win, cache_v_win, state_ssm_re, state_ssm_im, meta_tokens, norm_mix_pre, w_in, attn_sinks, ssm_a_re, ssm_a_im, ssm_log_dt, ssm_b_re, ssm_b_im, ssm_c_re, ssm_c_im, ssm_d, w_glu, norm_att_out, norm_ssm_out, w_out, norm_mix_post, norm_mlp_pre, w_up, w_down, norm_mlp_post):
    depth = w_in.shape[0]
    assert depth == 1
    l = 0
    nb, seq, _ = x_prompt.shape
    n_seq, n_tok, _ = x_sample.shape
    assert n_seq * n_tok == ROWS and nb * BLOCK == ROWS and seq % BLOCK == 0

    abar_re, abar_im, bbar_re, bbar_im = _zoh(
        ssm_a_re[l], ssm_a_im[l], ssm_log_dt[l], ssm_b_re[l], ssm_b_im[l])
    row = lambda a: a.reshape(1, -1)
    params = (
        row(norm_mix_pre[l]), w_in[l].astype(BF16),
        row(abar_re), row(abar_im),
        _block_diag_in(bbar_re), _block_diag_in(bbar_im),
        _block_diag_out(ssm_c_re[l]), _block_diag_out(-ssm_c_im[l]),
        row(ssm_d[l]), w_glu[l].astype(BF16),
        row(norm_att_out[l]), row(norm_ssm_out[l]), w_out[l].astype(BF16), row(norm_mix_post[l]),
    )
    mlp_params = (row(norm_mlp_pre[l]), w_up[l].astype(BF16), w_down[l].astype(BF16),
                  row(norm_mlp_post[l]))

    front = BLOCK - N_META
    meta_blk = jnp.concatenate([jnp.zeros((front, D_MODEL), F32), meta_tokens], axis=0)
    cos, sa, sb = _rope_tables(jnp.arange(seq + BLOCK, dtype=jnp.int32) - front)
    h1, k_last, v_last, p_re, p_im = _prompt_mixer(
        x_prompt, meta_blk, cos, sa, sb, attn_sinks[l], params)
    y_prompt = _mlp(h1.reshape(nb * seq, D_MODEL), *mlp_params).reshape(nb, seq, D_MODEL)

    pos_s = PAST_LEN + jnp.arange(n_tok, dtype=jnp.int32)
    cos_s, sa_s, sb_s = (jnp.repeat(t, n_seq, axis=0) for t in _rope_tables(pos_s))
    sink_col = jnp.repeat(attn_sinks[l], n_tok * GROUP).reshape(-1, 1)
    ck2d = cache_k_win[l].reshape(n_seq * WINDOW, KV_WIDTH)
    cv2d = cache_v_win[l].reshape(n_seq * WINDOW, KV_WIDTH)
    xs_tm = jnp.swapaxes(x_sample, 0, 1).reshape(ROWS, D_MODEL)
    h1s, kwin, vwin, s_re, s_im = _decode_mixer(
        xs_tm, cos_s, sa_s, sb_s, sink_col, ck2d, cv2d,
        state_ssm_re[l].reshape(n_seq, N_STATE), state_ssm_im[l].reshape(n_seq, N_STATE), params)
    y_sample = jnp.swapaxes(_mlp(h1s, *mlp_params).reshape(n_tok, n_seq, D_MODEL), 0, 1)

    win = lambda a, n: a.reshape(1, n, WINDOW, N_KV_HEADS, HEAD_DIM)
    st = lambda a, n: a.reshape(1, n, SSM_GROUPS, SSM_STATE)
    return (y_prompt, y_sample,
            win(k_last, nb), win(v_last, nb), st(p_re, nb), st(p_im, nb),
            win(kwin, n_seq), win(vwin, n_seq), st(s_re, n_seq), st(s_im, n_seq))
```

```python
import functools
import math

import jax
import jax.numpy as jnp
from jax import lax
from jax.experimental import pallas as pl
from jax.experimental.pallas import tpu as pltpu

F32 = jnp.float32
BF16 = jnp.bfloat16

N_META = 16
HEAD_DIM = 64
N_HEADS = 8
N_KV_HEADS = 2
WINDOW = 128
BLOCK = 128
ROPE_DIM = 16
ROPE_THETA = 500000.0
SSM_GROUP = 16
SSM_GROUPS = 32
SSM_STATE = 64
PAST_LEN = 8192
EPS = 1e-6
NEG = -1e30

D_MODEL = 1024
ATT_WIDTH = 512
KV_WIDTH = 128
SSM_WIDTH = 512
N_STATE = SSM_GROUPS * SSM_STATE
D_FF = 4096

ROWS = 512
LANES = 128
HALF = LANES // 2
SSM_TILE = 256
STATE_TILE = (SSM_TILE // SSM_GROUP) * SSM_STATE
N_SSM_TILES = SSM_WIDTH // SSM_TILE
LB_PER_TILE = STATE_TILE // LANES
N_LB = N_STATE // LANES
FF_CHUNK = 1024
VMEM_LIMIT = 56 * 1024 * 1024


def _dot(a, b):
    return jnp.dot(a, b, preferred_element_type=F32)


def _dot_nt(a, b):
    return lax.dot_general(a, b, (((1,), (1,)), ((), ())), preferred_element_type=F32)


def _rms(x, g):
    return x * lax.rsqrt(jnp.mean(x * x, axis=-1, keepdims=True) + EPS) * g


def _rope(x, cos, sa, sb):
    return x * cos + pltpu.roll(x, 8, axis=1) * sa + pltpu.roll(x, LANES - 8, axis=1) * sb


def _lane_is_lo(shape):
    return lax.broadcasted_iota(jnp.int32, shape, 1) < HALF


def _project_qkv(hn, w_in_ref):
    q = _dot(hn, w_in_ref[:, 0:ATT_WIDTH])
    kv = _dot(hn, w_in_ref[:, ATT_WIDTH:ATT_WIDTH + 2 * KV_WIDTH])
    return q, kv[:, 0:KV_WIDTH], kv[:, KV_WIDTH:2 * KV_WIDTH]


def _project_u(hn, w_in_ref):
    return _dot(hn, w_in_ref[:, ATT_WIDTH + 2 * KV_WIDTH:])


def _ssm_input(u, wbre_ref, wbim_ref, s_ref):
    ub = u.astype(BF16)
    for c in range(N_SSM_TILES):
        uc = ub[:, c * SSM_TILE:(c + 1) * SSM_TILE]
        bre, bim = _dot(uc, wbre_ref[c]), _dot(uc, wbim_ref[c])
        for l in range(LB_PER_TILE):
            s_ref[c * LB_PER_TILE + l, 0:ROWS, :] = bre[:, l * LANES:(l + 1) * LANES]
            s_ref[c * LB_PER_TILE + l, ROWS:2 * ROWS, :] = bim[:, l * LANES:(l + 1) * LANES]


def _ssm_output(s_ref, u, wcre_ref, wcim_ref, d_ref, wglu_ref):
    ys = []
    for c in range(N_SSM_TILES):
        blocks = range(c * LB_PER_TILE, (c + 1) * LB_PER_TILE)
        hr = jnp.concatenate([s_ref[l, 0:ROWS, :].astype(BF16) for l in blocks], axis=1)
        hi = jnp.concatenate([s_ref[l, ROWS:2 * ROWS, :].astype(BF16) for l in blocks], axis=1)
        ys.append(_dot(hr, wcre_ref[c]) + _dot(hi, wcim_ref[c]))
    y = jnp.concatenate(ys, axis=1) + d_ref[...] * u
    z = 0.5 * y * (1.0 + jnp.tanh(math.sqrt(2.0 / math.pi) * (y + 0.044715 * (y * y * y))))
    gate = 1.0 / (1.0 + jnp.exp(-_dot(z.astype(BF16), wglu_ref[...])))
    return z * gate


def _merge(x, att, s, gatt_ref, wout_ref, gpost_ref):
    a = _rms(att, gatt_ref[...]).astype(BF16)
    m = _dot(a, wout_ref[0:ATT_WIDTH, :]) + _dot(s, wout_ref[ATT_WIDTH:, :])
    return x + _rms(m, gpost_ref[...])


def _div(x, k):
    return lax.shift_right_logical(x, int(math.log2(k)))


def _mod(x, k):
    return lax.bitwise_and(x, k - 1)


def _sink_softmax(logits, mask, sink_col):
    lm = jnp.where(mask, logits, NEG)
    m = jnp.maximum(jnp.max(lm, axis=-1, keepdims=True), sink_col)
    e = jnp.exp(lm - m)
    den = jnp.sum(e, axis=-1, keepdims=True) + jnp.exp(sink_col - m)
    return e, den


def _prompt_mixer_kernel(
        sink_ref, x_ref, meta_ref, cos_ref, sa_ref, sb_ref, perm_ref, permt_ref,
        gpre_ref, win_ref,
        ar_ref, ai_ref, wbre_ref, wbim_ref, wcre_ref, wcim_ref, d_ref, wglu_ref,
        gatt_ref, gssm_ref, wout_ref, gpost_ref,
        h1_ref, klast_ref, vlast_ref, sre_ref, sim_ref,
        kbuf, vbuf, s_ref, hstate, att_ref):
    n = pl.program_id(0)
    nb = x_ref.shape[0]
    assert 2 * nb == 8
    last = pl.num_programs(0) - 1

    @pl.when(n == 0)
    def _init():
        kbuf[...] = jnp.zeros_like(kbuf)
        vbuf[...] = jnp.zeros_like(vbuf)
        hstate[...] = jnp.zeros_like(hstate)

    meta = meta_ref[...]
    x = x_ref[...].reshape(ROWS, D_MODEL)
    x = jnp.where(n == 0, jnp.concatenate([meta] * nb, axis=0), x)

    hn = _rms(x, gpre_ref[...]).astype(BF16)
    q, k, v = _project_qkv(hn, win_ref)

    cos, sa, sb = cos_ref[...], sa_ref[...], sb_ref[...]
    lo = _lane_is_lo((BLOCK, LANES))

    u = _project_u(_dot(perm_ref[...], hn).astype(BF16), win_ref)
    _ssm_input(u, wbre_ref, wbim_ref, s_ref)
    is_re = lax.broadcasted_iota(jnp.int32, (2 * nb, LANES), 0) < nb
    a1, a2 = [], []
    for l in range(N_LB):
        cols = slice(l * LANES, (l + 1) * LANES)
        ai = jnp.broadcast_to(ai_ref[:, cols], (2 * nb, LANES))
        a1.append(jnp.broadcast_to(ar_ref[:, cols], (2 * nb, LANES)))
        a2.append(jnp.where(is_re, -ai, ai))

    def swap(t):
        return pltpu.roll(t, nb, axis=0)

    def two_steps(i, h):
        r_re = pl.ds(pl.multiple_of(i * 2 * nb, 2 * nb), 2 * nb)
        r_im = pl.ds(pl.multiple_of(ROWS + i * 2 * nb, 2 * nb), 2 * nb)
        out = []
        for l in range(N_LB):
            re, im = s_ref[l, r_re, :], s_ref[l, r_im, :]
            h0 = a1[l] * h[l] + a2[l] * swap(h[l]) + jnp.where(is_re, re, swap(im))
            h1 = a1[l] * h0 + a2[l] * swap(h0) + jnp.where(is_re, swap(re), im)
            s_ref[l, r_re, :] = jnp.where(is_re, h0, swap(h1))
            s_ref[l, r_im, :] = jnp.where(is_re, swap(h0), h1)
            out.append(h1)
        return tuple(out)

    hfin = lax.fori_loop(0, BLOCK // 2, two_steps, tuple(hstate[l] for l in range(N_LB)))
    for l in range(N_LB):
        hstate[l] = hfin[l]
    ssm_o = _ssm_output(s_ref, u, wcre_ref, wcim_ref, d_ref, wglu_ref)
    ssm_n = _dot(permt_ref[...], _rms(ssm_o, gssm_ref[...]).astype(BF16)).astype(BF16)

    r_i = _mod(lax.broadcasted_iota(jnp.int32, (2 * BLOCK, 2 * BLOCK), 0), BLOCK)
    c_i = lax.broadcasted_iota(jnp.int32, (2 * BLOCK, 2 * BLOCK), 1)
    c_min = jnp.where(n == 0, 2 * BLOCK - N_META, jnp.where(n == 1, BLOCK - N_META, 0))
    mask = (c_i > r_i) & (c_i <= r_i + WINDOW) & (c_i >= c_min)
    top = lax.broadcasted_iota(jnp.int32, (2 * BLOCK, 1), 0) < BLOCK

    def variants(t):
        tr = pltpu.roll(t, HALF, axis=1)
        z = jnp.zeros_like(t)
        return (jnp.where(lo, t, z), jnp.where(lo, z, tr),
                jnp.where(lo, tr, z), jnp.where(lo, z, t))

    for b in range(nb):
        rows = slice(b * BLOCK, (b + 1) * BLOCK)
        kb = _rope(k[rows], cos, sa, sb)
        vb = v[rows]
        for i, (kv_, vv_) in enumerate(zip(variants(kb), variants(vb))):
            kbuf[b, i, 0:BLOCK, :] = kbuf[b, i, BLOCK:2 * BLOCK, :]
            vbuf[b, i, 0:BLOCK, :] = vbuf[b, i, BLOCK:2 * BLOCK, :]
            kbuf[b, i, BLOCK:2 * BLOCK, :] = kv_.astype(BF16)
            vbuf[b, i, BLOCK:2 * BLOCK, :] = vv_.astype(BF16)

        @pl.when(n == last)
        def _emit_window(kb=kb, vb=vb, b=b):
            klast_ref[b] = kb.T
            vlast_ref[b] = vb.T

        qs = []
        for j in range(ATT_WIDTH // LANES):
            qj = _rope(q[rows, j * LANES:(j + 1) * LANES], cos, sa, sb) * (HEAD_DIM ** -0.5)
            qs.append(qj.astype(BF16))
        for g in range(N_KV_HEADS):
            qst = jnp.concatenate([qs[2 * g], qs[2 * g + 1]], axis=0)
            o = None
            for half in range(2):
                var = 2 * g + half
                h_top, h_bot = 4 * g + half, 4 * g + 2 + half
                sink_col = jnp.where(top, sink_ref[h_top], sink_ref[h_bot])
                e, den = _sink_softmax(_dot_nt(qst, kbuf[b, var]), mask, sink_col)
                part = _dot(e.astype(BF16), vbuf[b, var]) / den
                o = part if o is None else o + part
            att_ref[rows, (2 * g) * LANES:(2 * g + 1) * LANES] = o[0:BLOCK]
            att_ref[rows, (2 * g + 1) * LANES:(2 * g + 2) * LANES] = o[BLOCK:2 * BLOCK]

    h1 = _merge(x, att_ref[...], ssm_n, gatt_ref, wout_ref, gpost_ref)
    h1_ref[...] = h1.reshape(h1_ref.shape)

    @pl.when(n == last)
    def _emit_state():
        for l in range(N_LB):
            sre_ref[:, l * LANES:(l + 1) * LANES] = hfin[l][0:nb]
            sim_ref[:, l * LANES:(l + 1) * LANES] = hfin[l][nb:2 * nb]


def _const_spec(shape):
    zeros = (0,) * len(shape)
    return pl.BlockSpec(shape, lambda *_: zeros)


def _prompt_mixer(x_prompt, meta_blk, cos, sa, sb, sinks, p):
    nb, seq, _ = x_prompt.shape
    n_blocks = seq // BLOCK + 1
    r = jnp.arange(ROWS)
    perm = jax.nn.one_hot((r % nb) * BLOCK + r // nb, ROWS, dtype=BF16)
    xmap = lambda n: (0, jnp.maximum(n - 1, 0), 0)
    tmap = lambda n: (n, 0)
    in_specs = [
        pl.BlockSpec(memory_space=pltpu.SMEM),
        pl.BlockSpec((nb, BLOCK, D_MODEL), xmap),
        _const_spec((BLOCK, D_MODEL)),
        pl.BlockSpec((BLOCK, LANES), tmap),
        pl.BlockSpec((BLOCK, LANES), tmap),
        pl.BlockSpec((BLOCK, LANES), tmap),
        _const_spec((ROWS, ROWS)),
        _const_spec((ROWS, ROWS)),
    ] + [_const_spec(a.shape) for a in p]
    out_shape = (
        jax.ShapeDtypeStruct((nb, seq, D_MODEL), F32),
        jax.ShapeDtypeStruct((nb, BLOCK, KV_WIDTH), F32),
        jax.ShapeDtypeStruct((nb, BLOCK, KV_WIDTH), F32),
        jax.ShapeDtypeStruct((nb, N_STATE), F32),
        jax.ShapeDtypeStruct((nb, N_STATE), F32),
    )
    out_specs = (
        pl.BlockSpec((nb, BLOCK, D_MODEL), xmap),
        _const_spec((nb, BLOCK, KV_WIDTH)),
        _const_spec((nb, BLOCK, KV_WIDTH)),
        _const_spec((nb, N_STATE)),
        _const_spec((nb, N_STATE)),
    )
    scratch = [
        pltpu.VMEM((nb, 4, 2 * BLOCK, LANES), BF16),
        pltpu.VMEM((nb, 4, 2 * BLOCK, LANES), BF16),
        pltpu.VMEM((N_LB, 2 * ROWS, LANES), F32),
        pltpu.VMEM((N_LB, 2 * nb, LANES), F32),
        pltpu.VMEM((ROWS, ATT_WIDTH), F32),
    ]
    return pl.pallas_call(
        _prompt_mixer_kernel,
        grid=(n_blocks,),
        in_specs=in_specs,
        out_specs=out_specs,
        out_shape=out_shape,
        scratch_shapes=scratch,
        compiler_params=pltpu.CompilerParams(
            dimension_semantics=("arbitrary",), vmem_limit_bytes=VMEM_LIMIT),
        name="prompt_mixer",
    )(sinks, x_prompt, meta_blk, cos, sa, sb, perm, perm.T, *p)


GROUP = 8


def _decode_mixer_kernel(
        x_ref, cos_ref, sa_ref, sb_ref, cost_ref, sat_ref, sbt_ref, sink_ref,
        ck_ref, cv_ref, stre_ref, stim_ref, wkt_ref, wvt_ref,
        gpre_ref, win_ref, ar_ref, ai_ref, wbre_ref, wbim_ref, wcre_ref, wcim_ref,
        d_ref, wglu_ref, gatt_ref, gssm_ref, wout_ref, gpost_ref,
        h1_ref, kwin_ref, vwin_ref, sre_ref, sim_ref,
        qp, knew, vnew, knew_b, vnew_b, ssm_s, att_ref, s_ref):
    g = pl.program_id(0)
    n_seq = stre_ref.shape[0]
    n_tok = ROWS // n_seq
    last = pl.num_programs(0) - 1

    @pl.when(g == 0)
    def _project_and_ssm():
        hn = _rms(x_ref[...], gpre_ref[...]).astype(BF16)
        q = _dot(hn, win_ref[:, 0:ATT_WIDTH])
        u = _project_u(hn, win_ref)
        cos, sa, sb = cos_ref[...], sa_ref[...], sb_ref[...]
        kt = _dot_nt(wkt_ref[...], hn)
        kt = (kt * cost_ref[...] + pltpu.roll(kt, 8, axis=0) * sat_ref[...]
              + pltpu.roll(kt, KV_WIDTH - 8, axis=0) * sbt_ref[...])
        vt = _dot_nt(wvt_ref[...], hn)
        knew[...] = kt
        vnew[...] = vt
        knew_b[...] = kt.astype(BF16)
        vnew_b[...] = vt.astype(BF16)
        lo = _lane_is_lo((ROWS, LANES))
        for j in range(ATT_WIDTH // LANES):
            qj = _rope(q[:, j * LANES:(j + 1) * LANES], cos, sa, sb) * (HEAD_DIM ** -0.5)
            qr = pltpu.roll(qj, HALF, axis=1)
            z = jnp.zeros_like(qj)
            if j < 2:
                qp[2 * j] = jnp.where(lo, qj, z)
                qp[2 * j + 1] = jnp.where(lo, qr, z)
            else:
                qp[2 * j] = jnp.where(lo, z, qr)
                qp[2 * j + 1] = jnp.where(lo, z, qj)

        _ssm_input(u, wbre_ref, wbim_ref, s_ref)
        for l in range(N_LB):
            cols = slice(l * LANES, (l + 1) * LANES)
            ar, ai = ar_ref[:, cols], ai_ref[:, cols]
            hr, hi = stre_ref[:, cols], stim_ref[:, cols]
            for t in range(n_tok):
                r_re = slice(t * n_seq, (t + 1) * n_seq)
                r_im = slice(ROWS + t * n_seq, ROWS + (t + 1) * n_seq)
                hr, hi = (ar * hr - ai * hi + s_ref[l, r_re, :],
                          ar * hi + ai * hr + s_ref[l, r_im, :])
                s_ref[l, r_re, :] = hr
                s_ref[l, r_im, :] = hi
            sre_ref[:, cols] = hr
            sim_ref[:, cols] = hi
        ssm_s[...] = _ssm_output(s_ref, u, wcre_ref, wcim_ref, d_ref, wglu_ref)

    def rows_of(t):
        return pl.ds(pl.multiple_of(t * n_seq + g * GROUP, GROUP), GROUP)

    qb = jnp.concatenate(
        [qp[h, rows_of(t), :] for h in range(N_HEADS) for t in range(n_tok)], axis=0).astype(BF16)
    kc_f = ck_ref[...]
    vc_f = cv_ref[...]
    seq_rows = [slice(b * KV_WIDTH, (b + 1) * KV_WIDTH) for b in range(GROUP)]
    kcat = jnp.concatenate([kc_f[r] for r in seq_rows], axis=1).astype(BF16)
    vcat = jnp.concatenate([vc_f[r] for r in seq_rows], axis=1).astype(BF16)
    n_q = N_HEADS * n_tok * GROUP
    n_c = GROUP * WINDOW

    r_c = lax.broadcasted_iota(jnp.int32, (n_q, n_c), 0)
    c_c = lax.broadcasted_iota(jnp.int32, (n_q, n_c), 1)
    mask_c = ((_div(c_c, WINDOW) == _mod(r_c, GROUP))
              & (_mod(c_c, WINDOW) > _mod(_div(r_c, GROUP), n_tok)))
    lc = jnp.where(mask_c, _dot(qb, kcat), NEG)
    r_n = lax.broadcasted_iota(jnp.int32, (n_q, n_seq), 0)
    c_n = lax.broadcasted_iota(jnp.int32, (n_q, n_seq), 1)
    own = c_n == g * GROUP + _mod(r_n, GROUP)
    t_n = _mod(_div(r_n, GROUP), n_tok)
    tok_cols = [slice(t * n_seq, (t + 1) * n_seq) for t in range(n_tok)]
    lns = [jnp.where(own & (t_n >= t), _dot(qb, knew_b[:, tok_cols[t]]), NEG)
           for t in range(n_tok)]

    sink_col = sink_ref[...]
    m = jnp.maximum(jnp.max(lc, axis=-1, keepdims=True), sink_col)
    for ln in lns:
        m = jnp.maximum(m, jnp.max(ln, axis=-1, keepdims=True))
    ec = jnp.exp(lc - m)
    den = jnp.sum(ec, axis=-1, keepdims=True) + jnp.exp(sink_col - m)
    o = _dot_nt(ec.astype(BF16), vcat)
    for t, ln in enumerate(lns):
        en = jnp.exp(ln - m)
        den = den + jnp.sum(en, axis=-1, keepdims=True)
        o = o + _dot_nt(en.astype(BF16), vnew_b[:, tok_cols[t]])
    o = o / den

    per_head = n_tok * GROUP
    lo = _lane_is_lo((per_head, LANES))
    for j in range(ATT_WIDTH // LANES):
        o_even = o[(2 * j) * per_head:(2 * j + 1) * per_head]
        o_odd = o[(2 * j + 1) * per_head:(2 * j + 2) * per_head]
        if j < 2:
            pair = jnp.where(lo, o_even, pltpu.roll(o_odd, HALF, axis=1))
        else:
            pair = jnp.where(lo, pltpu.roll(o_even, HALF, axis=1), o_odd)
        for t in range(n_tok):
            att_ref[rows_of(t), j * LANES:(j + 1) * LANES] = pair[t * GROUP:(t + 1) * GROUP]

    to_front = _mod(LANES - g * GROUP, LANES)
    kt_g = [pltpu.roll(knew[:, c], to_front, axis=1) for c in tok_cols]
    vt_g = [pltpu.roll(vnew[:, c], to_front, axis=1) for c in tok_cols]
    lane = lax.broadcasted_iota(jnp.int32, (KV_WIDTH, WINDOW), 1)
    for b, rows in enumerate(seq_rows):
        kw = pltpu.roll(kc_f[rows], WINDOW - n_tok, axis=1)
        vw = pltpu.roll(vc_f[rows], WINDOW - n_tok, axis=1)
        for t in range(n_tok):
            dst = WINDOW - n_tok + t
            kw = jnp.where(lane == dst, pltpu.roll(kt_g[t], (dst - b) % LANES, axis=1), kw)
            vw = jnp.where(lane == dst, pltpu.roll(vt_g[t], (dst - b) % LANES, axis=1), vw)
        kwin_ref[rows, :] = kw
        vwin_ref[rows, :] = vw

    @pl.when(g == last)
    def _merge_out():
        h1_ref[...] = _merge(x_ref[...], att_ref[...],
                             _rms(ssm_s[...], gssm_ref[...]).astype(BF16),
                             gatt_ref, wout_ref, gpost_ref)


def _decode_mixer(x2d, tables, tables_t, sink_col, ck2d, cv2d, st_re, st_im, wkt, wvt, p):
    n_seq = st_re.shape[0]
    n_groups = n_seq // GROUP
    cmap = lambda g: (g, 0)
    head = (x2d, *tables, *tables_t, sink_col)
    in_specs = [_const_spec(a.shape) for a in head] + [
        pl.BlockSpec((GROUP * KV_WIDTH, WINDOW), cmap),
        pl.BlockSpec((GROUP * KV_WIDTH, WINDOW), cmap),
    ] + [_const_spec(a.shape) for a in (st_re, st_im, wkt, wvt, *p)]
    out_shape = (
        jax.ShapeDtypeStruct(x2d.shape, F32),
        jax.ShapeDtypeStruct(ck2d.shape, F32),
        jax.ShapeDtypeStruct(cv2d.shape, F32),
        jax.ShapeDtypeStruct(st_re.shape, F32),
        jax.ShapeDtypeStruct(st_im.shape, F32),
    )
    out_specs = (
        _const_spec(x2d.shape),
        pl.BlockSpec((GROUP * KV_WIDTH, WINDOW), cmap),
        pl.BlockSpec((GROUP * KV_WIDTH, WINDOW), cmap),
        _const_spec(st_re.shape), _const_spec(st_im.shape),
    )
    scratch = [
        pltpu.VMEM((N_HEADS, ROWS, LANES), F32),
        pltpu.VMEM((KV_WIDTH, ROWS), F32),
        pltpu.VMEM((KV_WIDTH, ROWS), F32),
        pltpu.VMEM((KV_WIDTH, ROWS), BF16),
        pltpu.VMEM((KV_WIDTH, ROWS), BF16),
        pltpu.VMEM((ROWS, SSM_WIDTH), F32),
        pltpu.VMEM((ROWS, ATT_WIDTH), F32),
        pltpu.VMEM((N_LB, 2 * ROWS, LANES), F32),
    ]
    return pl.pallas_call(
        _decode_mixer_kernel,
        grid=(n_groups,),
        in_specs=in_specs,
        out_specs=out_specs,
        out_shape=out_shape,
        scratch_shapes=scratch,
        compiler_params=pltpu.CompilerParams(
            dimension_semantics=("arbitrary",), vmem_limit_bytes=VMEM_LIMIT),
        name="decode_mixer",
    )(*head, ck2d, cv2d, st_re, st_im, wkt, wvt, *p)


def _mlp_kernel(x_ref, gpre_ref, wup_ref, wdn_ref, gpost_ref, o_ref):
    x = x_ref[...]
    hn = _rms(x, gpre_ref[...]).astype(BF16)
    acc = None
    for c in range(D_FF // FF_CHUNK):
        cols = slice(c * FF_CHUNK, (c + 1) * FF_CHUNK)
        a = jnp.maximum(_dot(hn, wup_ref[:, cols]), 0.0)
        part = _dot((a * a).astype(BF16), wdn_ref[cols, :])
        acc = part if acc is None else acc + part
    o_ref[...] = x + _rms(acc, gpost_ref[...])


def _mlp(x2d, g_pre, w_up, w_down, g_post):
    n = x2d.shape[0]
    rmap = lambda i: (i, 0)
    return pl.pallas_call(
        _mlp_kernel,
        grid=(n // ROWS,),
        in_specs=[pl.BlockSpec((ROWS, D_MODEL), rmap), _const_spec(g_pre.shape),
                  _const_spec(w_up.shape), _const_spec(w_down.shape), _const_spec(g_post.shape)],
        out_specs=pl.BlockSpec((ROWS, D_MODEL), rmap),
        out_shape=jax.ShapeDtypeStruct(x2d.shape, F32),
        compiler_params=pltpu.CompilerParams(
            dimension_semantics=("arbitrary",), vmem_limit_bytes=VMEM_LIMIT),
        name="mlp",
    )(x2d, g_pre, w_up, w_down, g_post)


def _zoh(a_re, a_im, log_dt, b_re, b_im):
    dt = jnp.exp(log_dt)[:, None]
    mag = jnp.exp(a_re * dt)
    abar_re, abar_im = mag * jnp.cos(a_im * dt), mag * jnp.sin(a_im * dt)
    nr, ni = abar_re - 1.0, abar_im
    den = a_re * a_re + a_im * a_im
    coef_re = (nr * a_re + ni * a_im) / den
    coef_im = (ni * a_re - nr * a_im) / den
    bbar_re = coef_re[..., None] * b_re - coef_im[..., None] * b_im
    bbar_im = coef_re[..., None] * b_im + coef_im[..., None] * b_re
    return abar_re, abar_im, bbar_re, bbar_im


def _block_diag_in(bbar):
    gt = SSM_TILE // SSM_GROUP
    w = bbar.reshape(N_SSM_TILES, gt, SSM_STATE, SSM_GROUP)
    eye = jnp.eye(gt, dtype=F32)
    w = jnp.einsum('cgph,gk->cghkp', w, eye)
    return w.reshape(N_SSM_TILES, SSM_TILE, STATE_TILE).astype(BF16)


def _block_diag_out(c):
    gt = SSM_TILE // SSM_GROUP
    w = c.reshape(N_SSM_TILES, gt, SSM_GROUP, SSM_STATE)
    eye = jnp.eye(gt, dtype=F32)
    w = jnp.einsum('cghp,gk->ckpgh', w, eye)
    return w.reshape(N_SSM_TILES, STATE_TILE, SSM_TILE).astype(BF16)


def _rope_tables(pos):
    half = ROPE_DIM // 2
    inv = ROPE_THETA ** (-jnp.arange(half, dtype=F32) / half)
    ang = pos.astype(F32)[:, None] * inv[None, :]
    cos, sin = jnp.cos(ang), jnp.sin(ang)
    d = jnp.arange(LANES) % HEAD_DIM
    f = d % half
    cos_t = jnp.where(d[None, :] < ROPE_DIM, cos[:, f], 1.0)
    sa_t = jnp.where((d[None, :] >= half) & (d[None, :] < ROPE_DIM), sin[:, f], 0.0)
    sb_t = jnp.where(d[None, :] < half, -sin[:, f], 0.0)
    return cos_t, sa_t, sb_t


def kernel(x_prompt, x_sample, cache_k_win, cache_v_win, state_ssm_re, state_ssm_im, meta_tokens, norm_mix_pre, w_in, attn_sinks, ssm_a_re, ssm_a_im, ssm_log_dt, ssm_b_re, ssm_b_im, ssm_c_re, ssm_c_im, ssm_d, w_glu, norm_att_out, norm_ssm_out, w_out, norm_mix_post, norm_mlp_pre, w_up, w_down, norm_mlp_post):
    depth = w_in.shape[0]
    assert depth == 1
    l = 0
    nb, seq, _ = x_prompt.shape
    n_seq, n_tok, _ = x_sample.shape
    assert n_seq * n_tok == ROWS and nb * BLOCK == ROWS and seq % BLOCK == 0

    abar_re, abar_im, bbar_re, bbar_im = _zoh(
        ssm_a_re[l], ssm_a_im[l], ssm_log_dt[l], ssm_b_re[l], ssm_b_im[l])
    row = lambda a: a.reshape(1, -1)
    params = (
        row(norm_mix_pre[l]), w_in[l].astype(BF16),
        row(abar_re), row(abar_im),
        _block_diag_in(bbar_re), _block_diag_in(bbar_im),
        _block_diag_out(ssm_c_re[l]), _block_diag_out(-ssm_c_im[l]),
        row(ssm_d[l]), w_glu[l].astype(BF16),
        row(norm_att_out[l]), row(norm_ssm_out[l]), w_out[l].astype(BF16), row(norm_mix_post[l]),
    )
    mlp_params = (row(norm_mlp_pre[l]), w_up[l].astype(BF16), w_down[l].astype(BF16),
                  row(norm_mlp_post[l]))

    front = BLOCK - N_META
    meta_blk = jnp.concatenate([jnp.zeros((front, D_MODEL), F32), meta_tokens], axis=0)
    cos, sa, sb = _rope_tables(jnp.arange(seq + BLOCK, dtype=jnp.int32) - front)
    h1, k_last, v_last, p_re, p_im = _prompt_mixer(
        x_prompt, meta_blk, cos, sa, sb, attn_sinks[l], params)
    y_prompt = _mlp(h1.reshape(nb * seq, D_MODEL), *mlp_params).reshape(nb, seq, D_MODEL)

    pos_s = PAST_LEN + jnp.arange(n_tok, dtype=jnp.int32)
    tabs = _rope_tables(pos_s)
    tables = tuple(jnp.repeat(t, n_seq, axis=0) for t in tabs)
    tables_t = tuple(jnp.repeat(t.T, n_seq, axis=1) for t in tabs)
    sink_col = jnp.repeat(attn_sinks[l], n_tok * GROUP).reshape(-1, 1)
    to_t = lambda a: jnp.transpose(a, (0, 2, 3, 1)).reshape(-1, WINDOW)
    from_t = lambda a, n: jnp.transpose(
        a.reshape(n, N_KV_HEADS, HEAD_DIM, WINDOW), (0, 3, 1, 2))[None]
    w_kv = w_in[l][:, ATT_WIDTH:ATT_WIDTH + 2 * KV_WIDTH].astype(BF16)
    xs_tm = jnp.swapaxes(x_sample, 0, 1).reshape(ROWS, D_MODEL)
    h1s, kwin, vwin, s_re, s_im = _decode_mixer(
        xs_tm, tables, tables_t, sink_col, to_t(cache_k_win[l]), to_t(cache_v_win[l]),
        state_ssm_re[l].reshape(n_seq, N_STATE), state_ssm_im[l].reshape(n_seq, N_STATE),
        w_kv[:, 0:KV_WIDTH].T, w_kv[:, KV_WIDTH:].T, params)
    y_sample = jnp.swapaxes(_mlp(h1s, *mlp_params).reshape(n_tok, n_seq, D_MODEL), 0, 1)

    win = from_t
    st = lambda a, n: a.reshape(1, n, SSM_GROUPS, SSM_STATE)
    return (y_prompt, y_sample,
            win(k_last, nb), win(v_last, nb), st(p_re, nb), st(p_im, nb),
            win(kwin, n_seq), win(vwin, n_seq), st(s_re, n_seq), st(s_im, n_seq))
```

```python
import functools
import math

import jax
import jax.numpy as jnp
from jax import lax
from jax.experimental import pallas as pl
from jax.experimental.pallas import tpu as pltpu

F32 = jnp.float32
BF16 = jnp.bfloat16

N_META = 16
HEAD_DIM = 64
N_HEADS = 8
N_KV_HEADS = 2
WINDOW = 128
BLOCK = 128
ROPE_DIM = 16
ROPE_THETA = 500000.0
SSM_GROUP = 16
SSM_GROUPS = 32
SSM_STATE = 64
PAST_LEN = 8192
EPS = 1e-6
NEG = -1e30

D_MODEL = 1024
ATT_WIDTH = 512
KV_WIDTH = 128
SSM_WIDTH = 512
N_STATE = SSM_GROUPS * SSM_STATE
D_FF = 4096

ROWS = 512
LANES = 128
HALF = LANES // 2
SSM_TILE = 256
STATE_TILE = (SSM_TILE // SSM_GROUP) * SSM_STATE
N_SSM_TILES = SSM_WIDTH // SSM_TILE
LB_PER_TILE = STATE_TILE // LANES
N_LB = N_STATE // LANES
FF_CHUNK = 1024
VMEM_LIMIT = 56 * 1024 * 1024


def _dot(a, b):
    return jnp.dot(a, b, preferred_element_type=F32)


def _dot_nt(a, b):
    return lax.dot_general(a, b, (((1,), (1,)), ((), ())), preferred_element_type=F32)


def _rms(x, g):
    return x * lax.rsqrt(jnp.mean(x * x, axis=-1, keepdims=True) + EPS) * g


def _rope(x, cos, sa, sb):
    return x * cos + pltpu.roll(x, 8, axis=1) * sa + pltpu.roll(x, LANES - 8, axis=1) * sb


def _lane_is_lo(shape):
    return lax.broadcasted_iota(jnp.int32, shape, 1) < HALF


def _project_qkv(hn, w_in_ref):
    q = _dot(hn, w_in_ref[:, 0:ATT_WIDTH])
    kv = _dot(hn, w_in_ref[:, ATT_WIDTH:ATT_WIDTH + 2 * KV_WIDTH])
    return q, kv[:, 0:KV_WIDTH], kv[:, KV_WIDTH:2 * KV_WIDTH]


def _project_u(hn, w_in_ref):
    return _dot(hn, w_in_ref[:, ATT_WIDTH + 2 * KV_WIDTH:])


def _ssm_input(u, wbre_ref, wbim_ref, s_ref):
    ub = u.astype(BF16)
    for c in range(N_SSM_TILES):
        uc = ub[:, c * SSM_TILE:(c + 1) * SSM_TILE]
        bre, bim = _dot(uc, wbre_ref[c]), _dot(uc, wbim_ref[c])
        for l in range(LB_PER_TILE):
            s_ref[c * LB_PER_TILE + l, 0:ROWS, :] = bre[:, l * LANES:(l + 1) * LANES]
            s_ref[c * LB_PER_TILE + l, ROWS:2 * ROWS, :] = bim[:, l * LANES:(l + 1) * LANES]


def _ssm_output(s_ref, u, wcre_ref, wcim_ref, d_ref, wglu_ref):
    ys = []
    for c in range(N_SSM_TILES):
        blocks = range(c * LB_PER_TILE, (c + 1) * LB_PER_TILE)
        hr = jnp.concatenate([s_ref[l, 0:ROWS, :].astype(BF16) for l in blocks], axis=1)
        hi = jnp.concatenate([s_ref[l, ROWS:2 * ROWS, :].astype(BF16) for l in blocks], axis=1)
        ys.append(_dot(hr, wcre_ref[c]) + _dot(hi, wcim_ref[c]))
    y = jnp.concatenate(ys, axis=1) + d_ref[...] * u
    z = 0.5 * y * (1.0 + jnp.tanh(math.sqrt(2.0 / math.pi) * (y + 0.044715 * (y * y * y))))
    gate = 1.0 / (1.0 + jnp.exp(-_dot(z.astype(BF16), wglu_ref[...])))
    return z * gate


def _merge(x, att, s, gatt_ref, wout_ref, gpost_ref):
    a = _rms(att, gatt_ref[...]).astype(BF16)
    m = _dot(a, wout_ref[0:ATT_WIDTH, :]) + _dot(s, wout_ref[ATT_WIDTH:, :])
    return x + _rms(m, gpost_ref[...])


def _div(x, k):
    return lax.shift_right_logical(x, int(math.log2(k)))


def _mod(x, k):
    return lax.bitwise_and(x, k - 1)


def _sink_softmax(logits, mask, sink_col):
    lm = jnp.where(mask, logits, NEG)
    m = jnp.maximum(jnp.max(lm, axis=-1, keepdims=True), sink_col)
    e = jnp.exp(lm - m)
    den = jnp.sum(e, axis=-1, keepdims=True) + jnp.exp(sink_col - m)
    return e, den


def _prompt_mixer_kernel(
        sink_ref, x_ref, meta_ref, cos_ref, sa_ref, sb_ref, perm_ref, permt_ref,
        gpre_ref, win_ref,
        ar_ref, ai_ref, wbre_ref, wbim_ref, wcre_ref, wcim_ref, d_ref, wglu_ref,
        gatt_ref, gssm_ref, wout_ref, gpost_ref,
        h1_ref, klast_ref, vlast_ref, sre_ref, sim_ref,
        kbuf, vbuf, kcur, vcur, s_ref, hstate, att_ref):
    n = pl.program_id(0)
    nb = x_ref.shape[0]
    assert 2 * nb == 8
    last = pl.num_programs(0) - 1

    @pl.when(n == 0)
    def _init():
        kbuf[...] = jnp.zeros_like(kbuf)
        vbuf[...] = jnp.zeros_like(vbuf)
        hstate[...] = jnp.zeros_like(hstate)

    meta = meta_ref[...]
    x = x_ref[...].reshape(ROWS, D_MODEL)
    x = jnp.where(n == 0, jnp.concatenate([meta] * nb, axis=0), x)

    hn = _rms(x, gpre_ref[...]).astype(BF16)
    q, k, v = _project_qkv(hn, win_ref)

    cos, sa, sb = cos_ref[...], sa_ref[...], sb_ref[...]
    lo = _lane_is_lo((BLOCK, LANES))

    u = _project_u(_dot(perm_ref[...], hn).astype(BF16), win_ref)
    _ssm_input(u, wbre_ref, wbim_ref, s_ref)
    is_re = lax.broadcasted_iota(jnp.int32, (2 * nb, LANES), 0) < nb
    a1, a2 = [], []
    for l in range(N_LB):
        cols = slice(l * LANES, (l + 1) * LANES)
        ai = jnp.broadcast_to(ai_ref[:, cols], (2 * nb, LANES))
        a1.append(jnp.broadcast_to(ar_ref[:, cols], (2 * nb, LANES)))
        a2.append(jnp.where(is_re, -ai, ai))

    def swap(t):
        return pltpu.roll(t, nb, axis=0)

    hfin = []
    for l in range(N_LB):
        h = hstate[l]
        for i in range(BLOCK // 2):
            r_re = slice(i * 2 * nb, (i + 1) * 2 * nb)
            r_im = slice(ROWS + i * 2 * nb, ROWS + (i + 1) * 2 * nb)
            re, im = s_ref[l, r_re, :], s_ref[l, r_im, :]
            h0 = a1[l] * h + a2[l] * swap(h) + jnp.where(is_re, re, swap(im))
            h = a1[l] * h0 + a2[l] * swap(h0) + jnp.where(is_re, swap(re), im)
            s_ref[l, r_re, :] = jnp.where(is_re, h0, swap(h))
            s_ref[l, r_im, :] = jnp.where(is_re, swap(h0), h)
        hstate[l] = h
        hfin.append(h)
    ssm_o = _ssm_output(s_ref, u, wcre_ref, wcim_ref, d_ref, wglu_ref)
    ssm_n = _dot(permt_ref[...], _rms(ssm_o, gssm_ref[...]).astype(BF16)).astype(BF16)

    r_i = _mod(lax.broadcasted_iota(jnp.int32, (2 * BLOCK, 2 * BLOCK), 0), BLOCK)
    c_i = lax.broadcasted_iota(jnp.int32, (2 * BLOCK, 2 * BLOCK), 1)
    c_min = jnp.where(n == 0, 2 * BLOCK - N_META, jnp.where(n == 1, BLOCK - N_META, 0))
    mask = (c_i > r_i) & (c_i <= r_i + WINDOW) & (c_i >= c_min)
    top = lax.broadcasted_iota(jnp.int32, (2 * BLOCK, 1), 0) < BLOCK

    def variants(t):
        tr = pltpu.roll(t, HALF, axis=1)
        z = jnp.zeros_like(t)
        return (jnp.where(lo, t, z), jnp.where(lo, z, tr),
                jnp.where(lo, tr, z), jnp.where(lo, z, t))

    for b in range(nb):
        rows = slice(b * BLOCK, (b + 1) * BLOCK)
        kb = _rope(k[rows], cos, sa, sb)
        vb = v[rows]
        for i, (kv_, vv_) in enumerate(zip(variants(kb), variants(vb))):
            kbuf[b, i, 0:BLOCK, :] = kbuf[b, i, BLOCK:2 * BLOCK, :]
            vbuf[b, i, 0:BLOCK, :] = vbuf[b, i, BLOCK:2 * BLOCK, :]
            kbuf[b, i, BLOCK:2 * BLOCK, :] = kv_.astype(BF16)
            vbuf[b, i, BLOCK:2 * BLOCK, :] = vv_.astype(BF16)

        kcur[b] = kb
        vcur[b] = vb

        qs = []
        for j in range(ATT_WIDTH // LANES):
            qj = _rope(q[rows, j * LANES:(j + 1) * LANES], cos, sa, sb) * (HEAD_DIM ** -0.5)
            qs.append(qj.astype(BF16))
        for g in range(N_KV_HEADS):
            qst = jnp.concatenate([qs[2 * g], qs[2 * g + 1]], axis=0)
            o = None
            for half in range(2):
                var = 2 * g + half
                h_top, h_bot = 4 * g + half, 4 * g + 2 + half
                sink_col = jnp.where(top, sink_ref[h_top], sink_ref[h_bot])
                e, den = _sink_softmax(_dot_nt(qst, kbuf[b, var]), mask, sink_col)
                part = _dot(e.astype(BF16), vbuf[b, var]) / den
                o = part if o is None else o + part
            att_ref[rows, (2 * g) * LANES:(2 * g + 1) * LANES] = o[0:BLOCK]
            att_ref[rows, (2 * g + 1) * LANES:(2 * g + 2) * LANES] = o[BLOCK:2 * BLOCK]

    h1 = _merge(x, att_ref[...], ssm_n, gatt_ref, wout_ref, gpost_ref)
    h1_ref[...] = h1.reshape(h1_ref.shape)

    @pl.when(n == last)
    def _emit_state():
        for b in range(nb):
            klast_ref[b] = kcur[b].T
            vlast_ref[b] = vcur[b].T
        for l in range(N_LB):
            sre_ref[:, l * LANES:(l + 1) * LANES] = hfin[l][0:nb]
            sim_ref[:, l * LANES:(l + 1) * LANES] = hfin[l][nb:2 * nb]


def _const_spec(shape):
    zeros = (0,) * len(shape)
    return pl.BlockSpec(shape, lambda *_: zeros)


def _prompt_mixer(x_prompt, meta_blk, cos, sa, sb, sinks, p):
    nb, seq, _ = x_prompt.shape
    n_blocks = seq // BLOCK + 1
    r = jnp.arange(ROWS)
    perm = jax.nn.one_hot((r % nb) * BLOCK + r // nb, ROWS, dtype=BF16)
    xmap = lambda n: (0, jnp.maximum(n - 1, 0), 0)
    tmap = lambda n: (n, 0)
    in_specs = [
        pl.BlockSpec(memory_space=pltpu.SMEM),
        pl.BlockSpec((nb, BLOCK, D_MODEL), xmap),
        _const_spec((BLOCK, D_MODEL)),
        pl.BlockSpec((BLOCK, LANES), tmap),
        pl.BlockSpec((BLOCK, LANES), tmap),
        pl.BlockSpec((BLOCK, LANES), tmap),
        _const_spec((ROWS, ROWS)),
        _const_spec((ROWS, ROWS)),
    ] + [_const_spec(a.shape) for a in p]
    out_shape = (
        jax.ShapeDtypeStruct((nb, seq, D_MODEL), F32),
        jax.ShapeDtypeStruct((nb, BLOCK, KV_WIDTH), F32),
        jax.ShapeDtypeStruct((nb, BLOCK, KV_WIDTH), F32),
        jax.ShapeDtypeStruct((nb, N_STATE), F32),
        jax.ShapeDtypeStruct((nb, N_STATE), F32),
    )
    out_specs = (
        pl.BlockSpec((nb, BLOCK, D_MODEL), xmap),
        _const_spec((nb, BLOCK, KV_WIDTH)),
        _const_spec((nb, BLOCK, KV_WIDTH)),
        _const_spec((nb, N_STATE)),
        _const_spec((nb, N_STATE)),
    )
    scratch = [
        pltpu.VMEM((nb, 4, 2 * BLOCK, LANES), BF16),
        pltpu.VMEM((nb, 4, 2 * BLOCK, LANES), BF16),
        pltpu.VMEM((nb, BLOCK, KV_WIDTH), F32),
        pltpu.VMEM((nb, BLOCK, KV_WIDTH), F32),
        pltpu.VMEM((N_LB, 2 * ROWS, LANES), F32),
        pltpu.VMEM((N_LB, 2 * nb, LANES), F32),
        pltpu.VMEM((ROWS, ATT_WIDTH), F32),
    ]
    return pl.pallas_call(
        _prompt_mixer_kernel,
        grid=(n_blocks,),
        in_specs=in_specs,
        out_specs=out_specs,
        out_shape=out_shape,
        scratch_shapes=scratch,
        compiler_params=pltpu.CompilerParams(
            dimension_semantics=("arbitrary",), vmem_limit_bytes=VMEM_LIMIT),
        name="prompt_mixer",
    )(sinks, x_prompt, meta_blk, cos, sa, sb, perm, perm.T, *p)


GROUP = 8


def _decode_mixer_kernel(
        x_ref, cos_ref, sa_ref, sb_ref, cost_ref, sat_ref, sbt_ref, sink_ref,
        ck_ref, cv_ref, stre_ref, stim_ref, wkt_ref, wvt_ref,
        gpre_ref, win_ref, ar_ref, ai_ref, wbre_ref, wbim_ref, wcre_ref, wcim_ref,
        d_ref, wglu_ref, gatt_ref, gssm_ref, wout_ref, gpost_ref,
        h1_ref, kwin_ref, vwin_ref, sre_ref, sim_ref,
        qp, knew, vnew, knew_b, vnew_b, ssm_s, att_ref, s_ref):
    g = pl.program_id(0)
    n_seq = stre_ref.shape[0]
    n_tok = ROWS // n_seq
    last = pl.num_programs(0) - 1

    @pl.when(g == 0)
    def _project_and_ssm():
        hn = _rms(x_ref[...], gpre_ref[...]).astype(BF16)
        q = _dot(hn, win_ref[:, 0:ATT_WIDTH])
        u = _project_u(hn, win_ref)
        cos, sa, sb = cos_ref[...], sa_ref[...], sb_ref[...]
        kt = _dot_nt(wkt_ref[...], hn)
        kt = (kt * cost_ref[...] + pltpu.roll(kt, 8, axis=0) * sat_ref[...]
              + pltpu.roll(kt, KV_WIDTH - 8, axis=0) * sbt_ref[...])
        vt = _dot_nt(wvt_ref[...], hn)
        knew[...] = kt
        vnew[...] = vt
        knew_b[...] = kt.astype(BF16)
        vnew_b[...] = vt.astype(BF16)
        lo = _lane_is_lo((ROWS, LANES))
        for j in range(ATT_WIDTH // LANES):
            qj = _rope(q[:, j * LANES:(j + 1) * LANES], cos, sa, sb) * (HEAD_DIM ** -0.5)
            qr = pltpu.roll(qj, HALF, axis=1)
            z = jnp.zeros_like(qj)
            if j < 2:
                qp[2 * j] = jnp.where(lo, qj, z)
                qp[2 * j + 1] = jnp.where(lo, qr, z)
            else:
                qp[2 * j] = jnp.where(lo, z, qr)
                qp[2 * j + 1] = jnp.where(lo, z, qj)

        _ssm_input(u, wbre_ref, wbim_ref, s_ref)
        for l in range(N_LB):
            cols = slice(l * LANES, (l + 1) * LANES)
            ar, ai = ar_ref[:, cols], ai_ref[:, cols]
            hr, hi = stre_ref[:, cols], stim_ref[:, cols]
            for t in range(n_tok):
                r_re = slice(t * n_seq, (t + 1) * n_seq)
                r_im = slice(ROWS + t * n_seq, ROWS + (t + 1) * n_seq)
                hr, hi = (ar * hr - ai * hi + s_ref[l, r_re, :],
                          ar * hi + ai * hr + s_ref[l, r_im, :])
                s_ref[l, r_re, :] = hr
                s_ref[l, r_im, :] = hi
            sre_ref[:, cols] = hr
            sim_ref[:, cols] = hi
        ssm_s[...] = _ssm_output(s_ref, u, wcre_ref, wcim_ref, d_ref, wglu_ref)

    def rows_of(t):
        return pl.ds(pl.multiple_of(t * n_seq + g * GROUP, GROUP), GROUP)

    qb = jnp.concatenate(
        [qp[h, rows_of(t), :] for h in range(N_HEADS) for t in range(n_tok)], axis=0).astype(BF16)
    kc_f = ck_ref[...]
    vc_f = cv_ref[...]
    seq_rows = [slice(b * KV_WIDTH, (b + 1) * KV_WIDTH) for b in range(GROUP)]
    kcat = jnp.concatenate([kc_f[r] for r in seq_rows], axis=1).astype(BF16)
    vcat = jnp.concatenate([vc_f[r] for r in seq_rows], axis=1).astype(BF16)
    n_q = N_HEADS * n_tok * GROUP
    n_c = GROUP * WINDOW

    r_c = lax.broadcasted_iota(jnp.int32, (n_q, n_c), 0)
    c_c = lax.broadcasted_iota(jnp.int32, (n_q, n_c), 1)
    mask_c = ((_div(c_c, WINDOW) == _mod(r_c, GROUP))
              & (_mod(c_c, WINDOW) > _mod(_div(r_c, GROUP), n_tok)))
    lc = jnp.where(mask_c, _dot(qb, kcat), NEG)
    r_n = lax.broadcasted_iota(jnp.int32, (n_q, n_seq), 0)
    c_n = lax.broadcasted_iota(jnp.int32, (n_q, n_seq), 1)
    own = c_n == g * GROUP + _mod(r_n, GROUP)
    t_n = _mod(_div(r_n, GROUP), n_tok)
    tok_cols = [slice(t * n_seq, (t + 1) * n_seq) for t in range(n_tok)]
    lns = [jnp.where(own & (t_n >= t), _dot(qb, knew_b[:, tok_cols[t]]), NEG)
           for t in range(n_tok)]

    sink_col = sink_ref[...]
    m = jnp.maximum(jnp.max(lc, axis=-1, keepdims=True), sink_col)
    for ln in lns:
        m = jnp.maximum(m, jnp.max(ln, axis=-1, keepdims=True))
    ec = jnp.exp(lc - m)
    den = jnp.sum(ec, axis=-1, keepdims=True) + jnp.exp(sink_col - m)
    o = _dot_nt(ec.astype(BF16), vcat)
    for t, ln in enumerate(lns):
        en = jnp.exp(ln - m)
        den = den + jnp.sum(en, axis=-1, keepdims=True)
        o = o + _dot_nt(en.astype(BF16), vnew_b[:, tok_cols[t]])
    o = o / den

    per_head = n_tok * GROUP
    lo = _lane_is_lo((per_head, LANES))
    for j in range(ATT_WIDTH // LANES):
        o_even = o[(2 * j) * per_head:(2 * j + 1) * per_head]
        o_odd = o[(2 * j + 1) * per_head:(2 * j + 2) * per_head]
        if j < 2:
            pair = jnp.where(lo, o_even, pltpu.roll(o_odd, HALF, axis=1))
        else:
            pair = jnp.where(lo, pltpu.roll(o_even, HALF, axis=1), o_odd)
        for t in range(n_tok):
            att_ref[rows_of(t), j * LANES:(j + 1) * LANES] = pair[t * GROUP:(t + 1) * GROUP]

    to_front = _mod(LANES - g * GROUP, LANES)
    kt_g = [pltpu.roll(knew[:, c], to_front, axis=1) for c in tok_cols]
    vt_g = [pltpu.roll(vnew[:, c], to_front, axis=1) for c in tok_cols]
    lane = lax.broadcasted_iota(jnp.int32, (KV_WIDTH, WINDOW), 1)
    for b, rows in enumerate(seq_rows):
        kw = pltpu.roll(kc_f[rows], WINDOW - n_tok, axis=1)
        vw = pltpu.roll(vc_f[rows], WINDOW - n_tok, axis=1)
        for t in range(n_tok):
            dst = WINDOW - n_tok + t
            kw = jnp.where(lane == dst, pltpu.roll(kt_g[t], (dst - b) % LANES, axis=1), kw)
            vw = jnp.where(lane == dst, pltpu.roll(vt_g[t], (dst - b) % LANES, axis=1), vw)
        kwin_ref[rows, :] = kw
        vwin_ref[rows, :] = vw

    @pl.when(g == last)
    def _merge_out():
        h1_ref[...] = _merge(x_ref[...], att_ref[...],
                             _rms(ssm_s[...], gssm_ref[...]).astype(BF16),
                             gatt_ref, wout_ref, gpost_ref)


def _decode_mixer(x2d, tables, tables_t, sink_col, ck2d, cv2d, st_re, st_im, wkt, wvt, p):
    n_seq = st_re.shape[0]
    n_groups = n_seq // GROUP
    cmap = lambda g: (g, 0)
    head = (x2d, *tables, *tables_t, sink_col)
    in_specs = [_const_spec(a.shape) for a in head] + [
        pl.BlockSpec((GROUP * KV_WIDTH, WINDOW), cmap),
        pl.BlockSpec((GROUP * KV_WIDTH, WINDOW), cmap),
    ] + [_const_spec(a.shape) for a in (st_re, st_im, wkt, wvt, *p)]
    out_shape = (
        jax.ShapeDtypeStruct(x2d.shape, F32),
        jax.ShapeDtypeStruct(ck2d.shape, F32),
        jax.ShapeDtypeStruct(cv2d.shape, F32),
        jax.ShapeDtypeStruct(st_re.shape, F32),
        jax.ShapeDtypeStruct(st_im.shape, F32),
    )
    out_specs = (
        _const_spec(x2d.shape),
        pl.BlockSpec((GROUP * KV_WIDTH, WINDOW), cmap),
        pl.BlockSpec((GROUP * KV_WIDTH, WINDOW), cmap),
        _const_spec(st_re.shape), _const_spec(st_im.shape),
    )
    scratch = [
        pltpu.VMEM((N_HEADS, ROWS, LANES), F32),
        pltpu.VMEM((KV_WIDTH, ROWS), F32),
        pltpu.VMEM((KV_WIDTH, ROWS), F32),
        pltpu.VMEM((KV_WIDTH, ROWS), BF16),
        pltpu.VMEM((KV_WIDTH, ROWS), BF16),
        pltpu.VMEM((ROWS, SSM_WIDTH), F32),
        pltpu.VMEM((ROWS, ATT_WIDTH), F32),
        pltpu.VMEM((N_LB, 2 * ROWS, LANES), F32),
    ]
    return pl.pallas_call(
        _decode_mixer_kernel,
        grid=(n_groups,),
        in_specs=in_specs,
        out_specs=out_specs,
        out_shape=out_shape,
        scratch_shapes=scratch,
        compiler_params=pltpu.CompilerParams(
            dimension_semantics=("arbitrary",), vmem_limit_bytes=VMEM_LIMIT),
        name="decode_mixer",
    )(*head, ck2d, cv2d, st_re, st_im, wkt, wvt, *p)


def _mlp_kernel(x_ref, gpre_ref, wup_ref, wdn_ref, gpost_ref, o_ref):
    x = x_ref[...]
    hn = _rms(x, gpre_ref[...]).astype(BF16)
    acc = None
    for c in range(D_FF // FF_CHUNK):
        cols = slice(c * FF_CHUNK, (c + 1) * FF_CHUNK)
        a = jnp.maximum(_dot(hn, wup_ref[:, cols]), 0.0)
        part = _dot((a * a).astype(BF16), wdn_ref[cols, :])
        acc = part if acc is None else acc + part
    o_ref[...] = x + _rms(acc, gpost_ref[...])


def _mlp(x2d, g_pre, w_up, w_down, g_post):
    n = x2d.shape[0]
    rmap = lambda i: (i, 0)
    return pl.pallas_call(
        _mlp_kernel,
        grid=(n // ROWS,),
        in_specs=[pl.BlockSpec((ROWS, D_MODEL), rmap), _const_spec(g_pre.shape),
                  _const_spec(w_up.shape), _const_spec(w_down.shape), _const_spec(g_post.shape)],
        out_specs=pl.BlockSpec((ROWS, D_MODEL), rmap),
        out_shape=jax.ShapeDtypeStruct(x2d.shape, F32),
        compiler_params=pltpu.CompilerParams(
            dimension_semantics=("arbitrary",), vmem_limit_bytes=VMEM_LIMIT),
        name="mlp",
    )(x2d, g_pre, w_up, w_down, g_post)


def _zoh(a_re, a_im, log_dt, b_re, b_im):
    dt = jnp.exp(log_dt)[:, None]
    mag = jnp.exp(a_re * dt)
    abar_re, abar_im = mag * jnp.cos(a_im * dt), mag * jnp.sin(a_im * dt)
    nr, ni = abar_re - 1.0, abar_im
    den = a_re * a_re + a_im * a_im
    coef_re = (nr * a_re + ni * a_im) / den
    coef_im = (ni * a_re - nr * a_im) / den
    bbar_re = coef_re[..., None] * b_re - coef_im[..., None] * b_im
    bbar_im = coef_re[..., None] * b_im + coef_im[..., None] * b_re
    return abar_re, abar_im, bbar_re, bbar_im


def _block_diag_in(bbar):
    gt = SSM_TILE // SSM_GROUP
    w = bbar.reshape(N_SSM_TILES, gt, SSM_STATE, SSM_GROUP)
    eye = jnp.eye(gt, dtype=F32)
    w = jnp.einsum('cgph,gk->cghkp', w, eye)
    return w.reshape(N_SSM_TILES, SSM_TILE, STATE_TILE).astype(BF16)


def _block_diag_out(c):
    gt = SSM_TILE // SSM_GROUP
    w = c.reshape(N_SSM_TILES, gt, SSM_GROUP, SSM_STATE)
    eye = jnp.eye(gt, dtype=F32)
    w = jnp.einsum('cghp,gk->ckpgh', w, eye)
    return w.reshape(N_SSM_TILES, STATE_TILE, SSM_TILE).astype(BF16)


def _rope_tables(pos):
    half = ROPE_DIM // 2
    inv = ROPE_THETA ** (-jnp.arange(half, dtype=F32) / half)
    ang = pos.astype(F32)[:, None] * inv[None, :]
    cos, sin = jnp.cos(ang), jnp.sin(ang)
    d = jnp.arange(LANES) % HEAD_DIM
    f = d % half
    cos_t = jnp.where(d[None, :] < ROPE_DIM, cos[:, f], 1.0)
    sa_t = jnp.where((d[None, :] >= half) & (d[None, :] < ROPE_DIM), sin[:, f], 0.0)
    sb_t = jnp.where(d[None, :] < half, -sin[:, f], 0.0)
    return cos_t, sa_t, sb_t


def kernel(x_prompt, x_sample, cache_k_win, cache_v_win, state_ssm_re, state_ssm_im, meta_tokens, norm_mix_pre, w_in, attn_sinks, ssm_a_re, ssm_a_im, ssm_log_dt, ssm_b_re, ssm_b_im, ssm_c_re, ssm_c_im, ssm_d, w_glu, norm_att_out, norm_ssm_out, w_out, norm_mix_post, norm_mlp_pre, w_up, w_down, norm_mlp_post):
    depth = w_in.shape[0]
    assert depth == 1
    l = 0
    nb, seq, _ = x_prompt.shape
    n_seq, n_tok, _ = x_sample.shape
    assert n_seq * n_tok == ROWS and nb * BLOCK == ROWS and seq % BLOCK == 0

    abar_re, abar_im, bbar_re, bbar_im = _zoh(
        ssm_a_re[l], ssm_a_im[l], ssm_log_dt[l], ssm_b_re[l], ssm_b_im[l])
    row = lambda a: a.reshape(1, -1)
    params = (
        row(norm_mix_pre[l]), w_in[l].astype(BF16),
        row(abar_re), row(abar_im),
        _block_diag_in(bbar_re), _block_diag_in(bbar_im),
        _block_diag_out(ssm_c_re[l]), _block_diag_out(-ssm_c_im[l]),
        row(ssm_d[l]), w_glu[l].astype(BF16),
        row(norm_att_out[l]), row(norm_ssm_out[l]), w_out[l].astype(BF16), row(norm_mix_post[l]),
    )
    mlp_params = (row(norm_mlp_pre[l]), w_up[l].astype(BF16), w_down[l].astype(BF16),
                  row(norm_mlp_post[l]))

    front = BLOCK - N_META
    meta_blk = jnp.concatenate([jnp.zeros((front, D_MODEL), F32), meta_tokens], axis=0)
    cos, sa, sb = _rope_tables(jnp.arange(seq + BLOCK, dtype=jnp.int32) - front)
    h1, k_last, v_last, p_re, p_im = _prompt_mixer(
        x_prompt, meta_blk, cos, sa, sb, attn_sinks[l], params)
    y_prompt = _mlp(h1.reshape(nb * seq, D_MODEL), *mlp_params).reshape(nb, seq, D_MODEL)

    pos_s = PAST_LEN + jnp.arange(n_tok, dtype=jnp.int32)
    tabs = _rope_tables(pos_s)
    tables = tuple(jnp.repeat(t, n_seq, axis=0) for t in tabs)
    tables_t = tuple(jnp.repeat(t.T, n_seq, axis=1) for t in tabs)
    sink_col = jnp.repeat(attn_sinks[l], n_tok * GROUP).reshape(-1, 1)
    to_t = lambda a: jnp.transpose(a, (0, 2, 3, 1)).reshape(-1, WINDOW)
    from_t = lambda a, n: jnp.transpose(
        a.reshape(n, N_KV_HEADS, HEAD_DIM, WINDOW), (0, 3, 1, 2))[None]
    w_kv = w_in[l][:, ATT_WIDTH:ATT_WIDTH + 2 * KV_WIDTH].astype(BF16)
    xs_tm = jnp.swapaxes(x_sample, 0, 1).reshape(ROWS, D_MODEL)
    h1s, kwin, vwin, s_re, s_im = _decode_mixer(
        xs_tm, tables, tables_t, sink_col, to_t(cache_k_win[l]), to_t(cache_v_win[l]),
        state_ssm_re[l].reshape(n_seq, N_STATE), state_ssm_im[l].reshape(n_seq, N_STATE),
        w_kv[:, 0:KV_WIDTH].T, w_kv[:, KV_WIDTH:].T, params)
    y_sample = jnp.swapaxes(_mlp(h1s, *mlp_params).reshape(n_tok, n_seq, D_MODEL), 0, 1)

    win = from_t
    st = lambda a, n: a.reshape(1, n, SSM_GROUPS, SSM_STATE)
    return (y_prompt, y_sample,
            win(k_last, nb), win(v_last, nb), st(p_re, nb), st(p_im, nb),
            win(kwin, n_seq), win(vwin, n_seq), st(s_re, n_seq), st(s_im, n_seq))
```

```python
import math

import jax
import jax.numpy as jnp
from jax import lax
from jax.experimental import pallas as pl
from jax.experimental.pallas import tpu as pltpu

F32 = jnp.float32
BF16 = jnp.bfloat16

N_META = 16
HEAD_DIM = 64
N_HEADS = 8
N_KV_HEADS = 2
WINDOW = 128
BLOCK = 128
ROPE_DIM = 16
ROPE_THETA = 500000.0
SSM_GROUP = 16
SSM_GROUPS = 32
SSM_STATE = 64
PAST_LEN = 8192
EPS = 1e-6
NEG = -1e30

D_MODEL = 1024
ATT_WIDTH = 512
KV_WIDTH = 128
SSM_WIDTH = 512
N_STATE = SSM_GROUPS * SSM_STATE
D_FF = 4096

ROWS = 512
LANES = 128
HALF = LANES // 2
SSM_TILE = 256
STATE_TILE = (SSM_TILE // SSM_GROUP) * SSM_STATE
N_SSM_TILES = SSM_WIDTH // SSM_TILE
LB_PER_TILE = STATE_TILE // LANES
N_LB = N_STATE // LANES
FF_CHUNK = 1024
VMEM_LIMIT = 56 * 1024 * 1024


def _dot(a, b):
    return jnp.dot(a, b, preferred_element_type=F32)


def _dot_nt(a, b):
    return lax.dot_general(a, b, (((1,), (1,)), ((), ())), preferred_element_type=F32)


def _rms(x, g):
    return x * lax.rsqrt(jnp.mean(x * x, axis=-1, keepdims=True) + EPS) * g


def _rope(x, cos, sa, sb):
    return x * cos + pltpu.roll(x, 8, axis=1) * sa + pltpu.roll(x, LANES - 8, axis=1) * sb


def _lane_is_lo(shape):
    return lax.broadcasted_iota(jnp.int32, shape, 1) < HALF


def _project_qkv(hn, w_in_ref):
    q = _dot(hn, w_in_ref[:, 0:ATT_WIDTH])
    kv = _dot(hn, w_in_ref[:, ATT_WIDTH:ATT_WIDTH + 2 * KV_WIDTH])
    return q, kv[:, 0:KV_WIDTH], kv[:, KV_WIDTH:2 * KV_WIDTH]


def _project_u(hn, w_in_ref):
    return _dot(hn, w_in_ref[:, ATT_WIDTH + 2 * KV_WIDTH:])


def _ssm_input(u, wbre_ref, wbim_ref, s_ref):
    ub = u.astype(BF16)
    for c in range(N_SSM_TILES):
        uc = ub[:, c * SSM_TILE:(c + 1) * SSM_TILE]
        bre, bim = _dot(uc, wbre_ref[c]), _dot(uc, wbim_ref[c])
        for l in range(LB_PER_TILE):
            s_ref[c * LB_PER_TILE + l, 0:ROWS, :] = bre[:, l * LANES:(l + 1) * LANES]
            s_ref[c * LB_PER_TILE + l, ROWS:2 * ROWS, :] = bim[:, l * LANES:(l + 1) * LANES]


def _ssm_output(s_ref, u, wcre_ref, wcim_ref, d_ref, wglu_ref):
    ys = []
    for c in range(N_SSM_TILES):
        blocks = range(c * LB_PER_TILE, (c + 1) * LB_PER_TILE)
        hr = jnp.concatenate([s_ref[l, 0:ROWS, :].astype(BF16) for l in blocks], axis=1)
        hi = jnp.concatenate([s_ref[l, ROWS:2 * ROWS, :].astype(BF16) for l in blocks], axis=1)
        ys.append(_dot(hr, wcre_ref[c]) + _dot(hi, wcim_ref[c]))
    y = jnp.concatenate(ys, axis=1) + d_ref[...] * u
    z = 0.5 * y * (1.0 + jnp.tanh(math.sqrt(2.0 / math.pi) * (y + 0.044715 * (y * y * y))))
    gate = 1.0 / (1.0 + jnp.exp(-_dot(z.astype(BF16), wglu_ref[...])))
    return z * gate


def _merge(x, att, s, gatt_ref, wout_ref, gpost_ref):
    a = _rms(att, gatt_ref[...]).astype(BF16)
    m = _dot(a, wout_ref[0:ATT_WIDTH, :]) + _dot(s, wout_ref[ATT_WIDTH:, :])
    return x + _rms(m, gpost_ref[...])


def _div(x, k):
    return lax.shift_right_logical(x, int(math.log2(k)))


def _mod(x, k):
    return lax.bitwise_and(x, k - 1)


def _sink_softmax(logits, mask, sink_col):
    lm = jnp.where(mask, logits, NEG)
    m = jnp.maximum(jnp.max(lm, axis=-1, keepdims=True), sink_col)
    e = jnp.exp(lm - m)
    den = jnp.sum(e, axis=-1, keepdims=True) + jnp.exp(sink_col - m)
    return e, den


def _prompt_mixer_kernel(
        sink_ref, x_ref, meta_ref, cos_ref, sa_ref, sb_ref, perm_ref, permt_ref,
        gpre_ref, win_ref,
        ar_ref, ai_ref, wbre_ref, wbim_ref, wcre_ref, wcim_ref, d_ref, wglu_ref,
        gatt_ref, gssm_ref, wout_ref, gpost_ref,
        h1_ref, klast_ref, vlast_ref, sre_ref, sim_ref,
        kbuf, vbuf, kcur, vcur, s_ref, hstate, att_ref):
    n = pl.program_id(0)
    nb = x_ref.shape[0]
    assert 2 * nb == 8
    last = pl.num_programs(0) - 1

    @pl.when(n == 0)
    def _init():
        kbuf[...] = jnp.zeros_like(kbuf)
        vbuf[...] = jnp.zeros_like(vbuf)
        hstate[...] = jnp.zeros_like(hstate)

    meta = meta_ref[...]
    x = x_ref[...].reshape(ROWS, D_MODEL)
    x = jnp.where(n == 0, jnp.concatenate([meta] * nb, axis=0), x)

    hn = _rms(x, gpre_ref[...]).astype(BF16)
    q, k, v = _project_qkv(hn, win_ref)

    cos, sa, sb = cos_ref[...], sa_ref[...], sb_ref[...]
    lo = _lane_is_lo((BLOCK, LANES))

    u = _project_u(_dot(perm_ref[...], hn).astype(BF16), win_ref)
    _ssm_input(u, wbre_ref, wbim_ref, s_ref)
    is_re = lax.broadcasted_iota(jnp.int32, (2 * nb, LANES), 0) < nb
    a1, a2 = [], []
    for l in range(N_LB):
        cols = slice(l * LANES, (l + 1) * LANES)
        ai = jnp.broadcast_to(ai_ref[:, cols], (2 * nb, LANES))
        a1.append(jnp.broadcast_to(ar_ref[:, cols], (2 * nb, LANES)))
        a2.append(jnp.where(is_re, -ai, ai))

    def swap(t):
        return pltpu.roll(t, nb, axis=0)

    hfin = []
    for l in range(N_LB):
        h = hstate[l]
        for i in range(BLOCK // 2):
            r_re = slice(i * 2 * nb, (i + 1) * 2 * nb)
            r_im = slice(ROWS + i * 2 * nb, ROWS + (i + 1) * 2 * nb)
            re, im = s_ref[l, r_re, :], s_ref[l, r_im, :]
            h0 = a1[l] * h + a2[l] * swap(h) + jnp.where(is_re, re, swap(im))
            h = a1[l] * h0 + a2[l] * swap(h0) + jnp.where(is_re, swap(re), im)
            s_ref[l, r_re, :] = jnp.where(is_re, h0, swap(h))
            s_ref[l, r_im, :] = jnp.where(is_re, swap(h0), h)
        hstate[l] = h
        hfin.append(h)
    ssm_o = _ssm_output(s_ref, u, wcre_ref, wcim_ref, d_ref, wglu_ref)
    ssm_n = _dot(permt_ref[...], _rms(ssm_o, gssm_ref[...]).astype(BF16)).astype(BF16)

    r_i = _mod(lax.broadcasted_iota(jnp.int32, (2 * BLOCK, 2 * BLOCK), 0), BLOCK)
    c_i = lax.broadcasted_iota(jnp.int32, (2 * BLOCK, 2 * BLOCK), 1)
    c_min = jnp.where(n == 0, 2 * BLOCK - N_META, jnp.where(n == 1, BLOCK - N_META, 0))
    mask = (c_i > r_i) & (c_i <= r_i + WINDOW) & (c_i >= c_min)
    top = lax.broadcasted_iota(jnp.int32, (2 * BLOCK, 1), 0) < BLOCK

    def variants(t):
        tr = pltpu.roll(t, HALF, axis=1)
        z = jnp.zeros_like(t)
        return (jnp.where(lo, t, z), jnp.where(lo, z, tr),
                jnp.where(lo, tr, z), jnp.where(lo, z, t))

    for b in range(nb):
        rows = slice(b * BLOCK, (b + 1) * BLOCK)
        kb = _rope(k[rows], cos, sa, sb)
        vb = v[rows]
        for i, (kv_, vv_) in enumerate(zip(variants(kb), variants(vb))):
            kbuf[b, i, 0:BLOCK, :] = kbuf[b, i, BLOCK:2 * BLOCK, :]
            vbuf[b, i, 0:BLOCK, :] = vbuf[b, i, BLOCK:2 * BLOCK, :]
            kbuf[b, i, BLOCK:2 * BLOCK, :] = kv_.astype(BF16)
            vbuf[b, i, BLOCK:2 * BLOCK, :] = vv_.astype(BF16)

        kcur[b] = kb
        vcur[b] = vb

        qs = []
        for j in range(ATT_WIDTH // LANES):
            qj = _rope(q[rows, j * LANES:(j + 1) * LANES], cos, sa, sb) * (HEAD_DIM ** -0.5)
            qs.append(qj.astype(BF16))
        for g in range(N_KV_HEADS):
            qst = jnp.concatenate([qs[2 * g], qs[2 * g + 1]], axis=0)
            o = None
            for half in range(2):
                var = 2 * g + half
                h_top, h_bot = 4 * g + half, 4 * g + 2 + half
                sink_col = jnp.where(top, sink_ref[h_top], sink_ref[h_bot])
                e, den = _sink_softmax(_dot_nt(qst, kbuf[b, var]), mask, sink_col)
                part = _dot(e.astype(BF16), vbuf[b, var]) / den
                o = part if o is None else o + part
            att_ref[rows, (2 * g) * LANES:(2 * g + 1) * LANES] = o[0:BLOCK]
            att_ref[rows, (2 * g + 1) * LANES:(2 * g + 2) * LANES] = o[BLOCK:2 * BLOCK]

    h1 = _merge(x, att_ref[...], ssm_n, gatt_ref, wout_ref, gpost_ref)
    h1_ref[...] = h1.reshape(h1_ref.shape)

    @pl.when(n == last)
    def _emit_state():
        for b in range(nb):
            klast_ref[b] = kcur[b].T
            vlast_ref[b] = vcur[b].T
        for l in range(N_LB):
            sre_ref[:, l * LANES:(l + 1) * LANES] = hfin[l][0:nb]
            sim_ref[:, l * LANES:(l + 1) * LANES] = hfin[l][nb:2 * nb]


def _const_spec(shape):
    zeros = (0,) * len(shape)
    return pl.BlockSpec(shape, lambda *_: zeros)


def _prompt_mixer(x_prompt, meta_blk, cos, sa, sb, sinks, p):
    nb, seq, _ = x_prompt.shape
    n_blocks = seq // BLOCK + 1
    r = jnp.arange(ROWS)
    perm = jax.nn.one_hot((r % nb) * BLOCK + r // nb, ROWS, dtype=BF16)
    xmap = lambda n: (0, jnp.maximum(n - 1, 0), 0)
    tmap = lambda n: (n, 0)
    in_specs = [
        pl.BlockSpec(memory_space=pltpu.SMEM),
        pl.BlockSpec((nb, BLOCK, D_MODEL), xmap),
        _const_spec((BLOCK, D_MODEL)),
        pl.BlockSpec((BLOCK, LANES), tmap),
        pl.BlockSpec((BLOCK, LANES), tmap),
        pl.BlockSpec((BLOCK, LANES), tmap),
        _const_spec((ROWS, ROWS)),
        _const_spec((ROWS, ROWS)),
    ] + [_const_spec(a.shape) for a in p]
    out_shape = (
        jax.ShapeDtypeStruct((nb, seq, D_MODEL), F32),
        jax.ShapeDtypeStruct((nb, BLOCK, KV_WIDTH), F32),
        jax.ShapeDtypeStruct((nb, BLOCK, KV_WIDTH), F32),
        jax.ShapeDtypeStruct((nb, N_STATE), F32),
        jax.ShapeDtypeStruct((nb, N_STATE), F32),
    )
    out_specs = (
        pl.BlockSpec((nb, BLOCK, D_MODEL), xmap),
        _const_spec((nb, BLOCK, KV_WIDTH)),
        _const_spec((nb, BLOCK, KV_WIDTH)),
        _const_spec((nb, N_STATE)),
        _const_spec((nb, N_STATE)),
    )
    scratch = [
        pltpu.VMEM((nb, 4, 2 * BLOCK, LANES), BF16),
        pltpu.VMEM((nb, 4, 2 * BLOCK, LANES), BF16),
        pltpu.VMEM((nb, BLOCK, KV_WIDTH), F32),
        pltpu.VMEM((nb, BLOCK, KV_WIDTH), F32),
        pltpu.VMEM((N_LB, 2 * ROWS, LANES), F32),
        pltpu.VMEM((N_LB, 2 * nb, LANES), F32),
        pltpu.VMEM((ROWS, ATT_WIDTH), F32),
    ]
    return pl.pallas_call(
        _prompt_mixer_kernel,
        grid=(n_blocks,),
        in_specs=in_specs,
        out_specs=out_specs,
        out_shape=out_shape,
        scratch_shapes=scratch,
        compiler_params=pltpu.CompilerParams(
            dimension_semantics=("arbitrary",), vmem_limit_bytes=VMEM_LIMIT),
        name="prompt_mixer",
    )(sinks, x_prompt, meta_blk, cos, sa, sb, perm, perm.T, *p)


GROUP = 8


def _decode_mixer_kernel(
        x_ref, cos_ref, sa_ref, sb_ref, cost_ref, sat_ref, sbt_ref, sink_ref,
        ck_ref, cv_ref, stre_ref, stim_ref, wkt_ref, wvt_ref,
        gpre_ref, win_ref, ar_ref, ai_ref, wbre_ref, wbim_ref, wcre_ref, wcim_ref,
        d_ref, wglu_ref, gatt_ref, gssm_ref, wout_ref, gpost_ref,
        h1_ref, kwin_ref, vwin_ref, sre_ref, sim_ref,
        qp, knew, vnew, knew_b, vnew_b, ssm_s, att_ref, s_ref):
    g = pl.program_id(0)
    n_seq = stre_ref.shape[0]
    n_tok = ROWS // n_seq
    last = pl.num_programs(0) - 1

    @pl.when(g == 0)
    def _project_and_ssm():
        hn = _rms(x_ref[...], gpre_ref[...]).astype(BF16)
        q = _dot(hn, win_ref[:, 0:ATT_WIDTH])
        u = _project_u(hn, win_ref)
        cos, sa, sb = cos_ref[...], sa_ref[...], sb_ref[...]
        kt = _dot_nt(wkt_ref[...], hn)
        kt = (kt * cost_ref[...] + pltpu.roll(kt, 8, axis=0) * sat_ref[...]
              + pltpu.roll(kt, KV_WIDTH - 8, axis=0) * sbt_ref[...])
        vt = _dot_nt(wvt_ref[...], hn)
        knew[...] = kt
        vnew[...] = vt
        knew_b[...] = kt.astype(BF16)
        vnew_b[...] = vt.astype(BF16)
        lo = _lane_is_lo((ROWS, LANES))
        for j in range(ATT_WIDTH // LANES):
            qj = _rope(q[:, j * LANES:(j + 1) * LANES], cos, sa, sb) * (HEAD_DIM ** -0.5)
            qr = pltpu.roll(qj, HALF, axis=1)
            z = jnp.zeros_like(qj)
            if j < 2:
                qp[2 * j] = jnp.where(lo, qj, z)
                qp[2 * j + 1] = jnp.where(lo, qr, z)
            else:
                qp[2 * j] = jnp.where(lo, z, qr)
                qp[2 * j + 1] = jnp.where(lo, z, qj)

        _ssm_input(u, wbre_ref, wbim_ref, s_ref)
        for l in range(N_LB):
            cols = slice(l * LANES, (l + 1) * LANES)
            ar, ai = ar_ref[:, cols], ai_ref[:, cols]
            hr, hi = stre_ref[:, cols], stim_ref[:, cols]
            for t in range(n_tok):
                r_re = slice(t * n_seq, (t + 1) * n_seq)
                r_im = slice(ROWS + t * n_seq, ROWS + (t + 1) * n_seq)
                hr, hi = (ar * hr - ai * hi + s_ref[l, r_re, :],
                          ar * hi + ai * hr + s_ref[l, r_im, :])
                s_ref[l, r_re, :] = hr
                s_ref[l, r_im, :] = hi
            sre_ref[:, cols] = hr
            sim_ref[:, cols] = hi
        ssm_s[...] = _ssm_output(s_ref, u, wcre_ref, wcim_ref, d_ref, wglu_ref)

    def rows_of(t):
        return pl.ds(pl.multiple_of(t * n_seq + g * GROUP, GROUP), GROUP)

    qb = jnp.concatenate(
        [qp[h, rows_of(t), :] for h in range(N_HEADS) for t in range(n_tok)], axis=0).astype(BF16)
    kc_f = ck_ref[...]
    vc_f = cv_ref[...]
    seq_rows = [slice(b * KV_WIDTH, (b + 1) * KV_WIDTH) for b in range(GROUP)]
    kcat = jnp.concatenate([kc_f[r] for r in seq_rows], axis=1).astype(BF16)
    vcat = jnp.concatenate([vc_f[r] for r in seq_rows], axis=1).astype(BF16)
    n_q = N_HEADS * n_tok * GROUP
    n_c = GROUP * WINDOW

    r_c = lax.broadcasted_iota(jnp.int32, (n_q, n_c), 0)
    c_c = lax.broadcasted_iota(jnp.int32, (n_q, n_c), 1)
    mask_c = ((_div(c_c, WINDOW) == _mod(r_c, GROUP))
              & (_mod(c_c, WINDOW) > _mod(_div(r_c, GROUP), n_tok)))
    lc = jnp.where(mask_c, _dot(qb, kcat), NEG)
    r_n = lax.broadcasted_iota(jnp.int32, (n_q, n_seq), 0)
    c_n = lax.broadcasted_iota(jnp.int32, (n_q, n_seq), 1)
    own = c_n == g * GROUP + _mod(r_n, GROUP)
    t_n = _mod(_div(r_n, GROUP), n_tok)
    tok_cols = [slice(t * n_seq, (t + 1) * n_seq) for t in range(n_tok)]
    lns = [jnp.where(own & (t_n >= t), _dot(qb, knew_b[:, tok_cols[t]]), NEG)
           for t in range(n_tok)]

    sink_col = sink_ref[...]
    m = jnp.maximum(jnp.max(lc, axis=-1, keepdims=True), sink_col)
    for ln in lns:
        m = jnp.maximum(m, jnp.max(ln, axis=-1, keepdims=True))
    ec = jnp.exp(lc - m)
    den = jnp.sum(ec, axis=-1, keepdims=True) + jnp.exp(sink_col - m)
    o = _dot_nt(ec.astype(BF16), vcat)
    for t, ln in enumerate(lns):
        en = jnp.exp(ln - m)
        den = den + jnp.sum(en, axis=-1, keepdims=True)
        o = o + _dot_nt(en.astype(BF16), vnew_b[:, tok_cols[t]])
    o = o / den

    per_head = n_tok * GROUP
    lo = _lane_is_lo((per_head, LANES))
    for j in range(ATT_WIDTH // LANES):
        o_even = o[(2 * j) * per_head:(2 * j + 1) * per_head]
        o_odd = o[(2 * j + 1) * per_head:(2 * j + 2) * per_head]
        if j < 2:
            pair = jnp.where(lo, o_even, pltpu.roll(o_odd, HALF, axis=1))
        else:
            pair = jnp.where(lo, pltpu.roll(o_even, HALF, axis=1), o_odd)
        for t in range(n_tok):
            att_ref[rows_of(t), j * LANES:(j + 1) * LANES] = pair[t * GROUP:(t + 1) * GROUP]

    to_front = _mod(LANES - g * GROUP, LANES)
    kt_g = [pltpu.roll(knew[:, c], to_front, axis=1) for c in tok_cols]
    vt_g = [pltpu.roll(vnew[:, c], to_front, axis=1) for c in tok_cols]
    lane = lax.broadcasted_iota(jnp.int32, (KV_WIDTH, WINDOW), 1)
    for b, rows in enumerate(seq_rows):
        kw = pltpu.roll(kc_f[rows], WINDOW - n_tok, axis=1)
        vw = pltpu.roll(vc_f[rows], WINDOW - n_tok, axis=1)
        for t in range(n_tok):
            dst = WINDOW - n_tok + t
            kw = jnp.where(lane == dst, pltpu.roll(kt_g[t], (dst - b) % LANES, axis=1), kw)
            vw = jnp.where(lane == dst, pltpu.roll(vt_g[t], (dst - b) % LANES, axis=1), vw)
        kwin_ref[rows, :] = kw
        vwin_ref[rows, :] = vw

    @pl.when(g == last)
    def _merge_out():
        h1_ref[...] = _merge(x_ref[...], att_ref[...],
                             _rms(ssm_s[...], gssm_ref[...]).astype(BF16),
                             gatt_ref, wout_ref, gpost_ref)


def _decode_mixer(x2d, tables, tables_t, sink_col, ck2d, cv2d, st_re, st_im, wkt, wvt, p):
    n_seq = st_re.shape[0]
    n_groups = n_seq // GROUP
    cmap = lambda g: (g, 0)
    head = (x2d, *tables, *tables_t, sink_col)
    in_specs = [_const_spec(a.shape) for a in head] + [
        pl.BlockSpec((GROUP * KV_WIDTH, WINDOW), cmap),
        pl.BlockSpec((GROUP * KV_WIDTH, WINDOW), cmap),
    ] + [_const_spec(a.shape) for a in (st_re, st_im, wkt, wvt, *p)]
    out_shape = (
        jax.ShapeDtypeStruct(x2d.shape, F32),
        jax.ShapeDtypeStruct(ck2d.shape, F32),
        jax.ShapeDtypeStruct(cv2d.shape, F32),
        jax.ShapeDtypeStruct(st_re.shape, F32),
        jax.ShapeDtypeStruct(st_im.shape, F32),
    )
    out_specs = (
        _const_spec(x2d.shape),
        pl.BlockSpec((GROUP * KV_WIDTH, WINDOW), cmap),
        pl.BlockSpec((GROUP * KV_WIDTH, WINDOW), cmap),
        _const_spec(st_re.shape), _const_spec(st_im.shape),
    )
    scratch = [
        pltpu.VMEM((N_HEADS, ROWS, LANES), F32),
        pltpu.VMEM((KV_WIDTH, ROWS), F32),
        pltpu.VMEM((KV_WIDTH, ROWS), F32),
        pltpu.VMEM((KV_WIDTH, ROWS), BF16),
        pltpu.VMEM((KV_WIDTH, ROWS), BF16),
        pltpu.VMEM((ROWS, SSM_WIDTH), F32),
        pltpu.VMEM((ROWS, ATT_WIDTH), F32),
        pltpu.VMEM((N_LB, 2 * ROWS, LANES), F32),
    ]
    return pl.pallas_call(
        _decode_mixer_kernel,
        grid=(n_groups,),
        in_specs=in_specs,
        out_specs=out_specs,
        out_shape=out_shape,
        scratch_shapes=scratch,
        compiler_params=pltpu.CompilerParams(
            dimension_semantics=("arbitrary",), vmem_limit_bytes=VMEM_LIMIT),
        name="decode_mixer",
    )(*head, ck2d, cv2d, st_re, st_im, wkt, wvt, *p)


def _mlp_kernel(xp_ref, xd_ref, gpre_ref, wup_ref, wdn_ref, gpost_ref, op_ref, od_ref):
    i = pl.program_id(0)
    last = pl.num_programs(0) - 1
    x = jnp.where(i == last, xd_ref[...], xp_ref[...])
    hn = _rms(x, gpre_ref[...]).astype(BF16)
    acc = None
    for c in range(D_FF // FF_CHUNK):
        cols = slice(c * FF_CHUNK, (c + 1) * FF_CHUNK)
        a = jnp.maximum(_dot(hn, wup_ref[:, cols].astype(BF16)), 0.0)
        part = _dot((a * a).astype(BF16), wdn_ref[cols, :].astype(BF16))
        acc = part if acc is None else acc + part
    res = x + _rms(acc, gpost_ref[...])

    @pl.when(i < last)
    def _store_prompt():
        op_ref[...] = res

    @pl.when(i == last)
    def _store_decode():
        od_ref[...] = res


def _mlp(xp2d, xd2d, g_pre, w_up, w_down, g_post):
    n_tiles = xp2d.shape[0] // ROWS
    assert xd2d.shape[0] == ROWS
    pmap = lambda i: (jnp.minimum(i, n_tiles - 1), 0)
    return pl.pallas_call(
        _mlp_kernel,
        grid=(n_tiles + 1,),
        in_specs=[pl.BlockSpec((ROWS, D_MODEL), pmap), _const_spec(xd2d.shape),
                  _const_spec(g_pre.shape), _const_spec(w_up.shape), _const_spec(w_down.shape),
                  _const_spec(g_post.shape)],
        out_specs=(pl.BlockSpec((ROWS, D_MODEL), pmap), _const_spec(xd2d.shape)),
        out_shape=(jax.ShapeDtypeStruct(xp2d.shape, F32), jax.ShapeDtypeStruct(xd2d.shape, F32)),
        compiler_params=pltpu.CompilerParams(
            dimension_semantics=("arbitrary",), vmem_limit_bytes=VMEM_LIMIT),
        name="mlp",
    )(xp2d, xd2d, g_pre, w_up, w_down, g_post)


def _zoh(a_re, a_im, log_dt, b_re, b_im):
    dt = jnp.exp(log_dt)[:, None]
    mag = jnp.exp(a_re * dt)
    abar_re, abar_im = mag * jnp.cos(a_im * dt), mag * jnp.sin(a_im * dt)
    nr, ni = abar_re - 1.0, abar_im
    den = a_re * a_re + a_im * a_im
    coef_re = (nr * a_re + ni * a_im) / den
    coef_im = (ni * a_re - nr * a_im) / den
    bbar_re = coef_re[..., None] * b_re - coef_im[..., None] * b_im
    bbar_im = coef_re[..., None] * b_im + coef_im[..., None] * b_re
    return abar_re, abar_im, bbar_re, bbar_im


def _block_diag_in(bbar):
    gt = SSM_TILE // SSM_GROUP
    w = bbar.reshape(N_SSM_TILES, gt, SSM_STATE, SSM_GROUP)
    eye = jnp.eye(gt, dtype=F32)
    w = jnp.einsum('cgph,gk->cghkp', w, eye)
    return w.reshape(N_SSM_TILES, SSM_TILE, STATE_TILE).astype(BF16)


def _block_diag_out(c):
    gt = SSM_TILE // SSM_GROUP
    w = c.reshape(N_SSM_TILES, gt, SSM_GROUP, SSM_STATE)
    eye = jnp.eye(gt, dtype=F32)
    w = jnp.einsum('cghp,gk->ckpgh', w, eye)
    return w.reshape(N_SSM_TILES, STATE_TILE, SSM_TILE).astype(BF16)


def _rope_tables(pos):
    half = ROPE_DIM // 2
    inv = ROPE_THETA ** (-jnp.arange(half, dtype=F32) / half)
    ang = pos.astype(F32)[:, None] * inv[None, :]
    cos, sin = jnp.cos(ang), jnp.sin(ang)
    d = jnp.arange(LANES) % HEAD_DIM
    f = d % half
    cos_t = jnp.where(d[None, :] < ROPE_DIM, cos[:, f], 1.0)
    sa_t = jnp.where((d[None, :] >= half) & (d[None, :] < ROPE_DIM), sin[:, f], 0.0)
    sb_t = jnp.where(d[None, :] < half, -sin[:, f], 0.0)
    return cos_t, sa_t, sb_t


def kernel(x_prompt, x_sample, cache_k_win, cache_v_win, state_ssm_re, state_ssm_im, meta_tokens, norm_mix_pre, w_in, attn_sinks, ssm_a_re, ssm_a_im, ssm_log_dt, ssm_b_re, ssm_b_im, ssm_c_re, ssm_c_im, ssm_d, w_glu, norm_att_out, norm_ssm_out, w_out, norm_mix_post, norm_mlp_pre, w_up, w_down, norm_mlp_post):
    depth = w_in.shape[0]
    assert depth == 1
    l = 0
    nb, seq, _ = x_prompt.shape
    n_seq, n_tok, _ = x_sample.shape
    assert n_seq * n_tok == ROWS and nb * BLOCK == ROWS and seq % BLOCK == 0

    abar_re, abar_im, bbar_re, bbar_im = _zoh(
        ssm_a_re[l], ssm_a_im[l], ssm_log_dt[l], ssm_b_re[l], ssm_b_im[l])
    row = lambda a: a.reshape(1, -1)
    params = (
        row(norm_mix_pre[l]), w_in[l].astype(BF16),
        row(abar_re), row(abar_im),
        _block_diag_in(bbar_re), _block_diag_in(bbar_im),
        _block_diag_out(ssm_c_re[l]), _block_diag_out(-ssm_c_im[l]),
        row(ssm_d[l]), w_glu[l].astype(BF16),
        row(norm_att_out[l]), row(norm_ssm_out[l]), w_out[l].astype(BF16), row(norm_mix_post[l]),
    )
    mlp_params = (row(norm_mlp_pre[l]), w_up[l], w_down[l], row(norm_mlp_post[l]))

    front = BLOCK - N_META
    meta_blk = jnp.concatenate([jnp.zeros((front, D_MODEL), F32), meta_tokens], axis=0)
    cos, sa, sb = _rope_tables(jnp.arange(seq + BLOCK, dtype=jnp.int32) - front)
    h1, k_last, v_last, p_re, p_im = _prompt_mixer(
        x_prompt, meta_blk, cos, sa, sb, attn_sinks[l], params)

    pos_s = PAST_LEN + jnp.arange(n_tok, dtype=jnp.int32)
    tabs = _rope_tables(pos_s)
    tables = tuple(jnp.repeat(t, n_seq, axis=0) for t in tabs)
    tables_t = tuple(jnp.repeat(t.T, n_seq, axis=1) for t in tabs)
    sink_col = jnp.repeat(attn_sinks[l], n_tok * GROUP).reshape(-1, 1)
    to_t = lambda a: jnp.transpose(a, (0, 2, 3, 1)).reshape(-1, WINDOW)
    from_t = lambda a, n: jnp.transpose(
        a.reshape(n, N_KV_HEADS, HEAD_DIM, WINDOW), (0, 3, 1, 2))[None]
    w_kv = w_in[l][:, ATT_WIDTH:ATT_WIDTH + 2 * KV_WIDTH].astype(BF16)
    xs_tm = jnp.swapaxes(x_sample, 0, 1).reshape(ROWS, D_MODEL)
    h1s, kwin, vwin, s_re, s_im = _decode_mixer(
        xs_tm, tables, tables_t, sink_col, to_t(cache_k_win[l]), to_t(cache_v_win[l]),
        state_ssm_re[l].reshape(n_seq, N_STATE), state_ssm_im[l].reshape(n_seq, N_STATE),
        w_kv[:, 0:KV_WIDTH].T, w_kv[:, KV_WIDTH:].T, params)
    y_prompt, ys = _mlp(h1.reshape(nb * seq, D_MODEL), h1s, *mlp_params)
    y_prompt = y_prompt.reshape(nb, seq, D_MODEL)
    y_sample = jnp.swapaxes(ys.reshape(n_tok, n_seq, D_MODEL), 0, 1)

    win = from_t
    st = lambda a, n: a.reshape(1, n, SSM_GROUPS, SSM_STATE)
    return (y_prompt, y_sample,
            win(k_last, nb), win(v_last, nb), st(p_re, nb), st(p_im, nb),
            win(kwin, n_seq), win(vwin, n_seq), st(s_re, n_seq), st(s_im, n_seq))
```

```python
import math

import jax
import jax.numpy as jnp
from jax import lax
from jax.experimental import pallas as pl
from jax.experimental.pallas import tpu as pltpu

F32 = jnp.float32
BF16 = jnp.bfloat16

N_META = 16
HEAD_DIM = 64
N_HEADS = 8
N_KV_HEADS = 2
WINDOW = 128
BLOCK = 128
ROPE_DIM = 16
ROPE_THETA = 500000.0
SSM_GROUP = 16
SSM_GROUPS = 32
SSM_STATE = 64
PAST_LEN = 8192
EPS = 1e-6
NEG = -1e30

D_MODEL = 1024
ATT_WIDTH = 512
KV_WIDTH = 128
SSM_WIDTH = 512
N_STATE = SSM_GROUPS * SSM_STATE
D_FF = 4096

ROWS = 512
LANES = 128
HALF = LANES // 2
SSM_TILE = 256
STATE_TILE = (SSM_TILE // SSM_GROUP) * SSM_STATE
N_SSM_TILES = SSM_WIDTH // SSM_TILE
LB_PER_TILE = STATE_TILE // LANES
N_LB = N_STATE // LANES
FF_CHUNK = 1024
VMEM_LIMIT = 56 * 1024 * 1024


def _dot(a, b):
    return jnp.dot(a, b, preferred_element_type=F32)


def _dot_nt(a, b):
    return lax.dot_general(a, b, (((1,), (1,)), ((), ())), preferred_element_type=F32)


def _rms(x, g):
    return x * lax.rsqrt(jnp.mean(x * x, axis=-1, keepdims=True) + EPS) * g


def _rope(x, cos, sa, sb):
    return x * cos + pltpu.roll(x, 8, axis=1) * sa + pltpu.roll(x, LANES - 8, axis=1) * sb


def _lane_is_lo(shape):
    return lax.broadcasted_iota(jnp.int32, shape, 1) < HALF


def _project_qkv(hn, w_in_ref):
    q = _dot(hn, w_in_ref[:, 0:ATT_WIDTH])
    kv = _dot(hn, w_in_ref[:, ATT_WIDTH:ATT_WIDTH + 2 * KV_WIDTH])
    return q, kv[:, 0:KV_WIDTH], kv[:, KV_WIDTH:2 * KV_WIDTH]


def _project_u(hn, w_in_ref):
    return _dot(hn, w_in_ref[:, ATT_WIDTH + 2 * KV_WIDTH:])


def _ssm_input(u, wbre_ref, wbim_ref, s_ref):
    ub = u.astype(BF16)
    for c in range(N_SSM_TILES):
        uc = ub[:, c * SSM_TILE:(c + 1) * SSM_TILE]
        bre, bim = _dot(uc, wbre_ref[c]), _dot(uc, wbim_ref[c])
        for l in range(LB_PER_TILE):
            s_ref[c * LB_PER_TILE + l, 0:ROWS, :] = bre[:, l * LANES:(l + 1) * LANES]
            s_ref[c * LB_PER_TILE + l, ROWS:2 * ROWS, :] = bim[:, l * LANES:(l + 1) * LANES]


def _ssm_output(s_ref, u, wcre_ref, wcim_ref, d_ref, wglu_ref):
    ys = []
    for c in range(N_SSM_TILES):
        blocks = range(c * LB_PER_TILE, (c + 1) * LB_PER_TILE)
        hr = jnp.concatenate([s_ref[l, 0:ROWS, :].astype(BF16) for l in blocks], axis=1)
        hi = jnp.concatenate([s_ref[l, ROWS:2 * ROWS, :].astype(BF16) for l in blocks], axis=1)
        ys.append(_dot(hr, wcre_ref[c]) + _dot(hi, wcim_ref[c]))
    y = jnp.concatenate(ys, axis=1) + d_ref[...] * u
    z = 0.5 * y * (1.0 + jnp.tanh(math.sqrt(2.0 / math.pi) * (y + 0.044715 * (y * y * y))))
    gate = 1.0 / (1.0 + jnp.exp(-_dot(z.astype(BF16), wglu_ref[...])))
    return z * gate


def _merge(x, att, s, gatt_ref, wout_ref, gpost_ref):
    a = _rms(att, gatt_ref[...]).astype(BF16)
    m = _dot(a, wout_ref[0:ATT_WIDTH, :]) + _dot(s, wout_ref[ATT_WIDTH:, :])
    return x + _rms(m, gpost_ref[...])


def _div(x, k):
    return lax.shift_right_logical(x, int(math.log2(k)))


def _mod(x, k):
    return lax.bitwise_and(x, k - 1)


def _sink_softmax(logits, mask, sink_col):
    lm = jnp.where(mask, logits, NEG)
    m = jnp.maximum(jnp.max(lm, axis=-1, keepdims=True), sink_col)
    e = jnp.exp(lm - m)
    den = jnp.sum(e, axis=-1, keepdims=True) + jnp.exp(sink_col - m)
    return e, den


def _prompt_mixer_kernel(
        sink_ref, x_ref, meta_ref, cos_ref, sa_ref, sb_ref, perm_ref, permt_ref,
        gpre_ref, win_ref,
        ar_ref, ai_ref, wbre_ref, wbim_ref, wcre_ref, wcim_ref, d_ref, wglu_ref,
        gatt_ref, gssm_ref, wout_ref, gpost_ref,
        h1_ref, klast_ref, vlast_ref, sre_ref, sim_ref,
        kbuf, vbuf, kcur, vcur, s_ref, hstate, att_ref):
    n = pl.program_id(0)
    nb = x_ref.shape[0]
    assert 2 * nb == 8
    last = pl.num_programs(0) - 1

    @pl.when(n == 0)
    def _init():
        kbuf[...] = jnp.zeros_like(kbuf)
        vbuf[...] = jnp.zeros_like(vbuf)
        hstate[...] = jnp.zeros_like(hstate)

    meta = meta_ref[...]
    x = x_ref[...].reshape(ROWS, D_MODEL)
    x = jnp.where(n == 0, jnp.concatenate([meta] * nb, axis=0), x)

    hn = _rms(x, gpre_ref[...]).astype(BF16)
    q, k, v = _project_qkv(hn, win_ref)

    cos, sa, sb = cos_ref[...], sa_ref[...], sb_ref[...]
    lo = _lane_is_lo((BLOCK, LANES))

    u = _project_u(_dot(perm_ref[...], hn).astype(BF16), win_ref)
    _ssm_input(u, wbre_ref, wbim_ref, s_ref)
    is_re = lax.broadcasted_iota(jnp.int32, (2 * nb, LANES), 0) < nb
    a1, a2 = [], []
    for l in range(N_LB):
        cols = slice(l * LANES, (l + 1) * LANES)
        ai = jnp.broadcast_to(ai_ref[:, cols], (2 * nb, LANES))
        a1.append(jnp.broadcast_to(ar_ref[:, cols], (2 * nb, LANES)))
        a2.append(jnp.where(is_re, -ai, ai))

    def swap(t):
        return pltpu.roll(t, nb, axis=0)

    hfin = []
    for l in range(N_LB):
        h = hstate[l]
        for i in range(BLOCK // 2):
            r_re = slice(i * 2 * nb, (i + 1) * 2 * nb)
            r_im = slice(ROWS + i * 2 * nb, ROWS + (i + 1) * 2 * nb)
            re, im = s_ref[l, r_re, :], s_ref[l, r_im, :]
            h0 = a1[l] * h + a2[l] * swap(h) + jnp.where(is_re, re, swap(im))
            h = a1[l] * h0 + a2[l] * swap(h0) + jnp.where(is_re, swap(re), im)
            s_ref[l, r_re, :] = jnp.where(is_re, h0, swap(h))
            s_ref[l, r_im, :] = jnp.where(is_re, swap(h0), h)
        hstate[l] = h
        hfin.append(h)
    ssm_o = _ssm_output(s_ref, u, wcre_ref, wcim_ref, d_ref, wglu_ref)
    ssm_n = _dot(permt_ref[...], _rms(ssm_o, gssm_ref[...]).astype(BF16)).astype(BF16)

    r_i = _mod(lax.broadcasted_iota(jnp.int32, (2 * BLOCK, 2 * BLOCK), 0), BLOCK)
    c_i = lax.broadcasted_iota(jnp.int32, (2 * BLOCK, 2 * BLOCK), 1)
    c_min = jnp.where(n == 0, 2 * BLOCK - N_META, jnp.where(n == 1, BLOCK - N_META, 0))
    mask = (c_i > r_i) & (c_i <= r_i + WINDOW) & (c_i >= c_min)
    top = lax.broadcasted_iota(jnp.int32, (2 * BLOCK, 1), 0) < BLOCK

    def variants(t):
        tr = pltpu.roll(t, HALF, axis=1)
        z = jnp.zeros_like(t)
        return (jnp.where(lo, t, z), jnp.where(lo, z, tr),
                jnp.where(lo, tr, z), jnp.where(lo, z, t))

    for b in range(nb):
        rows = slice(b * BLOCK, (b + 1) * BLOCK)
        kb = _rope(k[rows], cos, sa, sb)
        vb = v[rows]
        for i, (kv_, vv_) in enumerate(zip(variants(kb), variants(vb))):
            kbuf[b, i, 0:BLOCK, :] = kbuf[b, i, BLOCK:2 * BLOCK, :]
            vbuf[b, i, 0:BLOCK, :] = vbuf[b, i, BLOCK:2 * BLOCK, :]
            kbuf[b, i, BLOCK:2 * BLOCK, :] = kv_.astype(BF16)
            vbuf[b, i, BLOCK:2 * BLOCK, :] = vv_.astype(BF16)

        kcur[b] = kb
        vcur[b] = vb

        qs = []
        for j in range(ATT_WIDTH // LANES):
            qj = _rope(q[rows, j * LANES:(j + 1) * LANES], cos, sa, sb) * (HEAD_DIM ** -0.5)
            qs.append(qj.astype(BF16))
        for g in range(N_KV_HEADS):
            qst = jnp.concatenate([qs[2 * g], qs[2 * g + 1]], axis=0)
            o = None
            for half in range(2):
                var = 2 * g + half
                h_top, h_bot = 4 * g + half, 4 * g + 2 + half
                sink_col = jnp.where(top, sink_ref[h_top], sink_ref[h_bot])
                e, den = _sink_softmax(_dot_nt(qst, kbuf[b, var]), mask, sink_col)
                part = _dot(e.astype(BF16), vbuf[b, var]) / den
                o = part if o is None else o + part
            att_ref[rows, (2 * g) * LANES:(2 * g + 1) * LANES] = o[0:BLOCK]
            att_ref[rows, (2 * g + 1) * LANES:(2 * g + 2) * LANES] = o[BLOCK:2 * BLOCK]

    h1 = _merge(x, att_ref[...], ssm_n, gatt_ref, wout_ref, gpost_ref)
    h1_ref[...] = h1.reshape(h1_ref.shape)

    @pl.when(n == last)
    def _emit_state():
        for b in range(nb):
            klast_ref[b] = kcur[b].T
            vlast_ref[b] = vcur[b].T
        for l in range(N_LB):
            sre_ref[:, l * LANES:(l + 1) * LANES] = hfin[l][0:nb]
            sim_ref[:, l * LANES:(l + 1) * LANES] = hfin[l][nb:2 * nb]


def _const_spec(shape):
    zeros = (0,) * len(shape)
    return pl.BlockSpec(shape, lambda *_: zeros)


def _prompt_mixer(x_prompt, meta_blk, cos, sa, sb, sinks, p):
    nb, seq, _ = x_prompt.shape
    n_blocks = seq // BLOCK + 1
    r = jnp.arange(ROWS)
    perm = jax.nn.one_hot((r % nb) * BLOCK + r // nb, ROWS, dtype=BF16)
    xmap = lambda n: (0, jnp.maximum(n - 1, 0), 0)
    tmap = lambda n: (n, 0)
    in_specs = [
        pl.BlockSpec(memory_space=pltpu.SMEM),
        pl.BlockSpec((nb, BLOCK, D_MODEL), xmap),
        _const_spec((BLOCK, D_MODEL)),
        pl.BlockSpec((BLOCK, LANES), tmap),
        pl.BlockSpec((BLOCK, LANES), tmap),
        pl.BlockSpec((BLOCK, LANES), tmap),
        _const_spec((ROWS, ROWS)),
        _const_spec((ROWS, ROWS)),
    ] + [_const_spec(a.shape) for a in p]
    out_shape = (
        jax.ShapeDtypeStruct((nb, seq, D_MODEL), F32),
        jax.ShapeDtypeStruct((nb, BLOCK, KV_WIDTH), F32),
        jax.ShapeDtypeStruct((nb, BLOCK, KV_WIDTH), F32),
        jax.ShapeDtypeStruct((nb, N_STATE), F32),
        jax.ShapeDtypeStruct((nb, N_STATE), F32),
    )
    out_specs = (
        pl.BlockSpec((nb, BLOCK, D_MODEL), xmap),
        _const_spec((nb, BLOCK, KV_WIDTH)),
        _const_spec((nb, BLOCK, KV_WIDTH)),
        _const_spec((nb, N_STATE)),
        _const_spec((nb, N_STATE)),
    )
    scratch = [
        pltpu.VMEM((nb, 4, 2 * BLOCK, LANES), BF16),
        pltpu.VMEM((nb, 4, 2 * BLOCK, LANES), BF16),
        pltpu.VMEM((nb, BLOCK, KV_WIDTH), F32),
        pltpu.VMEM((nb, BLOCK, KV_WIDTH), F32),
        pltpu.VMEM((N_LB, 2 * ROWS, LANES), F32),
        pltpu.VMEM((N_LB, 2 * nb, LANES), F32),
        pltpu.VMEM((ROWS, ATT_WIDTH), F32),
    ]
    return pl.pallas_call(
        _prompt_mixer_kernel,
        grid=(n_blocks,),
        in_specs=in_specs,
        out_specs=out_specs,
        out_shape=out_shape,
        scratch_shapes=scratch,
        compiler_params=pltpu.CompilerParams(
            dimension_semantics=("arbitrary",), vmem_limit_bytes=VMEM_LIMIT),
        name="prompt_mixer",
    )(sinks, x_prompt, meta_blk, cos, sa, sb, perm, perm.T, *p)


GROUP = 8


def _decode_mixer_kernel(
        x_ref, cos_ref, sa_ref, sb_ref, cost_ref, sat_ref, sbt_ref, sink_ref, perm_ref, permt_ref,
        ck_ref, cv_ref, stre_ref, stim_ref, wkt_ref, wvt_ref,
        gpre_ref, win_ref, ar_ref, ai_ref, wbre_ref, wbim_ref, wcre_ref, wcim_ref,
        d_ref, wglu_ref, gatt_ref, gssm_ref, wout_ref, gpost_ref,
        h1_ref, kwin_ref, vwin_ref, sre_ref, sim_ref,
        qp, knew, vnew, knew_b, vnew_b, ssm_n, att_ref, s_ref):
    g = pl.program_id(0)
    n_seq = stre_ref.shape[0]
    n_tok = ROWS // n_seq
    grp_rows = GROUP * n_tok
    last = pl.num_programs(0) - 1

    @pl.when(g == 0)
    def _project_and_ssm():
        hn = _rms(x_ref[...], gpre_ref[...]).astype(BF16)
        q = _dot(hn, win_ref[:, 0:ATT_WIDTH])
        u = _project_u(_dot(perm_ref[...], hn).astype(BF16), win_ref)
        cos, sa, sb = cos_ref[...], sa_ref[...], sb_ref[...]
        kt = _dot_nt(wkt_ref[...], hn)
        kt = (kt * cost_ref[...] + pltpu.roll(kt, 8, axis=0) * sat_ref[...]
              + pltpu.roll(kt, KV_WIDTH - 8, axis=0) * sbt_ref[...])
        vt = _dot_nt(wvt_ref[...], hn)
        knew[...] = kt
        vnew[...] = vt
        knew_b[...] = kt.astype(BF16)
        vnew_b[...] = vt.astype(BF16)
        lo = _lane_is_lo((ROWS, LANES))
        for j in range(ATT_WIDTH // LANES):
            qj = _rope(q[:, j * LANES:(j + 1) * LANES], cos, sa, sb) * (HEAD_DIM ** -0.5)
            qr = pltpu.roll(qj, HALF, axis=1)
            z = jnp.zeros_like(qj)
            if j < 2:
                qp[2 * j] = jnp.where(lo, qj, z)
                qp[2 * j + 1] = jnp.where(lo, qr, z)
            else:
                qp[2 * j] = jnp.where(lo, z, qr)
                qp[2 * j + 1] = jnp.where(lo, z, qj)

        _ssm_input(u, wbre_ref, wbim_ref, s_ref)
        for l in range(N_LB):
            cols = slice(l * LANES, (l + 1) * LANES)
            ar, ai = ar_ref[:, cols], ai_ref[:, cols]
            hr, hi = stre_ref[:, cols], stim_ref[:, cols]
            for t in range(n_tok):
                r_re = slice(t * n_seq, (t + 1) * n_seq)
                r_im = slice(ROWS + t * n_seq, ROWS + (t + 1) * n_seq)
                hr, hi = (ar * hr - ai * hi + s_ref[l, r_re, :],
                          ar * hi + ai * hr + s_ref[l, r_im, :])
                s_ref[l, r_re, :] = hr
                s_ref[l, r_im, :] = hi
            sre_ref[:, cols] = hr
            sim_ref[:, cols] = hi
        ssm_o = _ssm_output(s_ref, u, wcre_ref, wcim_ref, d_ref, wglu_ref)
        ssm_n[...] = _dot(permt_ref[...], _rms(ssm_o, gssm_ref[...]).astype(BF16)).astype(BF16)

    grp = pl.ds(pl.multiple_of(g * grp_rows, grp_rows), grp_rows)
    qb = jnp.concatenate([qp[h, grp, :] for h in range(N_HEADS)], axis=0).astype(BF16)
    kc_f = ck_ref[...]
    vc_f = cv_ref[...]
    seq_rows = [slice(b * KV_WIDTH, (b + 1) * KV_WIDTH) for b in range(GROUP)]
    kcat = jnp.concatenate([kc_f[r] for r in seq_rows], axis=1).astype(BF16)
    vcat = jnp.concatenate([vc_f[r] for r in seq_rows], axis=1).astype(BF16)
    n_q = N_HEADS * grp_rows
    n_c = GROUP * WINDOW

    r_c = lax.broadcasted_iota(jnp.int32, (n_q, n_c), 0)
    c_c = lax.broadcasted_iota(jnp.int32, (n_q, n_c), 1)
    mask_c = ((_div(c_c, WINDOW) == _mod(_div(r_c, n_tok), GROUP))
              & (_mod(c_c, WINDOW) > _mod(r_c, n_tok)))
    lc = jnp.where(mask_c, _dot(qb, kcat), NEG)
    tile = pl.ds(pl.multiple_of(_div(g * grp_rows, LANES) * LANES, LANES), LANES)
    first = _mod(g * grp_rows, LANES)
    r_n = lax.broadcasted_iota(jnp.int32, (n_q, LANES), 0)
    c_n = lax.broadcasted_iota(jnp.int32, (n_q, LANES), 1) - first
    mask_n = ((c_n >= 0) & (c_n < grp_rows)
              & (_div(c_n, n_tok) == _mod(_div(r_n, n_tok), GROUP))
              & (_mod(c_n, n_tok) <= _mod(r_n, n_tok)))
    ln = jnp.where(mask_n, _dot(qb, knew_b[:, tile]), NEG)

    sink_col = sink_ref[...]
    m = jnp.maximum(jnp.maximum(jnp.max(lc, axis=-1, keepdims=True),
                                jnp.max(ln, axis=-1, keepdims=True)), sink_col)
    ec = jnp.exp(lc - m)
    en = jnp.exp(ln - m)
    den = (jnp.sum(ec, axis=-1, keepdims=True) + jnp.sum(en, axis=-1, keepdims=True)
           + jnp.exp(sink_col - m))
    o = (_dot_nt(ec.astype(BF16), vcat) + _dot_nt(en.astype(BF16), vnew_b[:, tile])) / den

    lo = _lane_is_lo((grp_rows, LANES))
    for j in range(ATT_WIDTH // LANES):
        o_even = o[(2 * j) * grp_rows:(2 * j + 1) * grp_rows]
        o_odd = o[(2 * j + 1) * grp_rows:(2 * j + 2) * grp_rows]
        if j < 2:
            pair = jnp.where(lo, o_even, pltpu.roll(o_odd, HALF, axis=1))
        else:
            pair = jnp.where(lo, pltpu.roll(o_even, HALF, axis=1), o_odd)
        att_ref[grp, j * LANES:(j + 1) * LANES] = pair

    keep = WINDOW - n_tok
    to_tail = _mod(keep - first + LANES, LANES)
    kt_g = pltpu.roll(knew[:, tile], to_tail, axis=1)
    vt_g = pltpu.roll(vnew[:, tile], to_tail, axis=1)
    is_new = lax.broadcasted_iota(jnp.int32, (KV_WIDTH, WINDOW), 1) >= keep
    for b, rows in enumerate(seq_rows):
        k_b = kt_g if b == 0 else pltpu.roll(kt_g, LANES - n_tok * b, axis=1)
        v_b = vt_g if b == 0 else pltpu.roll(vt_g, LANES - n_tok * b, axis=1)
        kwin_ref[rows, :] = jnp.where(is_new, k_b, pltpu.roll(kc_f[rows], keep, axis=1))
        vwin_ref[rows, :] = jnp.where(is_new, v_b, pltpu.roll(vc_f[rows], keep, axis=1))

    @pl.when(g == last)
    def _merge_out():
        h1_ref[...] = _merge(x_ref[...], att_ref[...], ssm_n[...], gatt_ref, wout_ref, gpost_ref)


def _decode_mixer(x2d, tables, tables_t, sink_col, ck2d, cv2d, st_re, st_im, wkt, wvt, p):
    n_seq = st_re.shape[0]
    n_groups = n_seq // GROUP
    cmap = lambda g: (g, 0)
    n_tok = ROWS // n_seq
    r = jnp.arange(ROWS)
    perm = jax.nn.one_hot((r % n_seq) * n_tok + r // n_seq, ROWS, dtype=BF16)
    head = (x2d, *tables, *tables_t, sink_col, perm, perm.T)
    in_specs = [_const_spec(a.shape) for a in head] + [
        pl.BlockSpec((GROUP * KV_WIDTH, WINDOW), cmap),
        pl.BlockSpec((GROUP * KV_WIDTH, WINDOW), cmap),
    ] + [_const_spec(a.shape) for a in (st_re, st_im, wkt, wvt, *p)]
    out_shape = (
        jax.ShapeDtypeStruct(x2d.shape, F32),
        jax.ShapeDtypeStruct(ck2d.shape, F32),
        jax.ShapeDtypeStruct(cv2d.shape, F32),
        jax.ShapeDtypeStruct(st_re.shape, F32),
        jax.ShapeDtypeStruct(st_im.shape, F32),
    )
    out_specs = (
        _const_spec(x2d.shape),
        pl.BlockSpec((GROUP * KV_WIDTH, WINDOW), cmap),
        pl.BlockSpec((GROUP * KV_WIDTH, WINDOW), cmap),
        _const_spec(st_re.shape), _const_spec(st_im.shape),
    )
    scratch = [
        pltpu.VMEM((N_HEADS, ROWS, LANES), F32),
        pltpu.VMEM((KV_WIDTH, ROWS), F32),
        pltpu.VMEM((KV_WIDTH, ROWS), F32),
        pltpu.VMEM((KV_WIDTH, ROWS), BF16),
        pltpu.VMEM((KV_WIDTH, ROWS), BF16),
        pltpu.VMEM((ROWS, SSM_WIDTH), BF16),
        pltpu.VMEM((ROWS, ATT_WIDTH), F32),
        pltpu.VMEM((N_LB, 2 * ROWS, LANES), F32),
    ]
    return pl.pallas_call(
        _decode_mixer_kernel,
        grid=(n_groups,),
        in_specs=in_specs,
        out_specs=out_specs,
        out_shape=out_shape,
        scratch_shapes=scratch,
        compiler_params=pltpu.CompilerParams(
            dimension_semantics=("arbitrary",), vmem_limit_bytes=VMEM_LIMIT),
        name="decode_mixer",
    )(*head, ck2d, cv2d, st_re, st_im, wkt, wvt, *p)


def _mlp_kernel(xp_ref, xd_ref, gpre_ref, wup_ref, wdn_ref, gpost_ref, op_ref, od_ref):
    i = pl.program_id(0)
    last = pl.num_programs(0) - 1
    x = jnp.where(i == last, xd_ref[...], xp_ref[...])
    hn = _rms(x, gpre_ref[...]).astype(BF16)
    acc = None
    for c in range(D_FF // FF_CHUNK):
        cols = slice(c * FF_CHUNK, (c + 1) * FF_CHUNK)
        a = jnp.maximum(_dot(hn, wup_ref[:, cols].astype(BF16)), 0.0)
        part = _dot((a * a).astype(BF16), wdn_ref[cols, :].astype(BF16))
        acc = part if acc is None else acc + part
    res = x + _rms(acc, gpost_ref[...])

    @pl.when(i < last)
    def _store_prompt():
        op_ref[...] = res

    @pl.when(i == last)
    def _store_decode():
        od_ref[...] = res


def _mlp(xp2d, xd2d, g_pre, w_up, w_down, g_post):
    n_tiles = xp2d.shape[0] // ROWS
    assert xd2d.shape[0] == ROWS
    pmap = lambda i: (jnp.minimum(i, n_tiles - 1), 0)
    return pl.pallas_call(
        _mlp_kernel,
        grid=(n_tiles + 1,),
        in_specs=[pl.BlockSpec((ROWS, D_MODEL), pmap), _const_spec(xd2d.shape),
                  _const_spec(g_pre.shape), _const_spec(w_up.shape), _const_spec(w_down.shape),
                  _const_spec(g_post.shape)],
        out_specs=(pl.BlockSpec((ROWS, D_MODEL), pmap), _const_spec(xd2d.shape)),
        out_shape=(jax.ShapeDtypeStruct(xp2d.shape, F32), jax.ShapeDtypeStruct(xd2d.shape, F32)),
        compiler_params=pltpu.CompilerParams(
            dimension_semantics=("arbitrary",), vmem_limit_bytes=VMEM_LIMIT),
        name="mlp",
    )(xp2d, xd2d, g_pre, w_up, w_down, g_post)


def _zoh(a_re, a_im, log_dt, b_re, b_im):
    dt = jnp.exp(log_dt)[:, None]
    mag = jnp.exp(a_re * dt)
    abar_re, abar_im = mag * jnp.cos(a_im * dt), mag * jnp.sin(a_im * dt)
    nr, ni = abar_re - 1.0, abar_im
    den = a_re * a_re + a_im * a_im
    coef_re = (nr * a_re + ni * a_im) / den
    coef_im = (ni * a_re - nr * a_im) / den
    bbar_re = coef_re[..., None] * b_re - coef_im[..., None] * b_im
    bbar_im = coef_re[..., None] * b_im + coef_im[..., None] * b_re
    return abar_re, abar_im, bbar_re, bbar_im


def _block_diag_in(bbar):
    gt = SSM_TILE // SSM_GROUP
    w = bbar.reshape(N_SSM_TILES, gt, SSM_STATE, SSM_GROUP)
    eye = jnp.eye(gt, dtype=F32)
    w = jnp.einsum('cgph,gk->cghkp', w, eye)
    return w.reshape(N_SSM_TILES, SSM_TILE, STATE_TILE).astype(BF16)


def _block_diag_out(c):
    gt = SSM_TILE // SSM_GROUP
    w = c.reshape(N_SSM_TILES, gt, SSM_GROUP, SSM_STATE)
    eye = jnp.eye(gt, dtype=F32)
    w = jnp.einsum('cghp,gk->ckpgh', w, eye)
    return w.reshape(N_SSM_TILES, STATE_TILE, SSM_TILE).astype(BF16)


def _rope_tables(pos):
    half = ROPE_DIM // 2
    inv = ROPE_THETA ** (-jnp.arange(half, dtype=F32) / half)
    ang = pos.astype(F32)[:, None] * inv[None, :]
    cos, sin = jnp.cos(ang), jnp.sin(ang)
    d = jnp.arange(LANES) % HEAD_DIM
    f = d % half
    cos_t = jnp.where(d[None, :] < ROPE_DIM, cos[:, f], 1.0)
    sa_t = jnp.where((d[None, :] >= half) & (d[None, :] < ROPE_DIM), sin[:, f], 0.0)
    sb_t = jnp.where(d[None, :] < half, -sin[:, f], 0.0)
    return cos_t, sa_t, sb_t


def kernel(x_prompt, x_sample, cache_k_win, cache_v_win, state_ssm_re, state_ssm_im, meta_tokens, norm_mix_pre, w_in, attn_sinks, ssm_a_re, ssm_a_im, ssm_log_dt, ssm_b_re, ssm_b_im, ssm_c_re, ssm_c_im, ssm_d, w_glu, norm_att_out, norm_ssm_out, w_out, norm_mix_post, norm_mlp_pre, w_up, w_down, norm_mlp_post):
    depth = w_in.shape[0]
    assert depth == 1
    l = 0
    nb, seq, _ = x_prompt.shape
    n_seq, n_tok, _ = x_sample.shape
    assert n_seq * n_tok == ROWS and nb * BLOCK == ROWS and seq % BLOCK == 0

    abar_re, abar_im, bbar_re, bbar_im = _zoh(
        ssm_a_re[l], ssm_a_im[l], ssm_log_dt[l], ssm_b_re[l], ssm_b_im[l])
    row = lambda a: a.reshape(1, -1)
    params = (
        row(norm_mix_pre[l]), w_in[l].astype(BF16),
        row(abar_re), row(abar_im),
        _block_diag_in(bbar_re), _block_diag_in(bbar_im),
        _block_diag_out(ssm_c_re[l]), _block_diag_out(-ssm_c_im[l]),
        row(ssm_d[l]), w_glu[l].astype(BF16),
        row(norm_att_out[l]), row(norm_ssm_out[l]), w_out[l].astype(BF16), row(norm_mix_post[l]),
    )
    mlp_params = (row(norm_mlp_pre[l]), w_up[l], w_down[l], row(norm_mlp_post[l]))

    front = BLOCK - N_META
    meta_blk = jnp.concatenate([jnp.zeros((front, D_MODEL), F32), meta_tokens], axis=0)
    cos, sa, sb = _rope_tables(jnp.arange(seq + BLOCK, dtype=jnp.int32) - front)
    h1, k_last, v_last, p_re, p_im = _prompt_mixer(
        x_prompt, meta_blk, cos, sa, sb, attn_sinks[l], params)

    pos_s = PAST_LEN + jnp.arange(n_tok, dtype=jnp.int32)
    tabs = _rope_tables(pos_s)
    tables = tuple(jnp.tile(t, (n_seq, 1)) for t in tabs)
    tables_t = tuple(jnp.tile(t.T, (1, n_seq)) for t in tabs)
    sink_col = jnp.repeat(attn_sinks[l], n_tok * GROUP).reshape(-1, 1)
    to_t = lambda a: jnp.transpose(a, (0, 2, 3, 1)).reshape(-1, WINDOW)
    from_t = lambda a, n: jnp.transpose(
        a.reshape(n, N_KV_HEADS, HEAD_DIM, WINDOW), (0, 3, 1, 2))[None]
    w_kv = w_in[l][:, ATT_WIDTH:ATT_WIDTH + 2 * KV_WIDTH].astype(BF16)
    h1s, kwin, vwin, s_re, s_im = _decode_mixer(
        x_sample.reshape(ROWS, D_MODEL), tables, tables_t, sink_col, to_t(cache_k_win[l]), to_t(cache_v_win[l]),
        state_ssm_re[l].reshape(n_seq, N_STATE), state_ssm_im[l].reshape(n_seq, N_STATE),
        w_kv[:, 0:KV_WIDTH].T, w_kv[:, KV_WIDTH:].T, params)
    y_prompt, ys = _mlp(h1.reshape(nb * seq, D_MODEL), h1s, *mlp_params)
    y_prompt = y_prompt.reshape(nb, seq, D_MODEL)
    y_sample = ys.reshape(n_seq, n_tok, D_MODEL)

    win = from_t
    st = lambda a, n: a.reshape(1, n, SSM_GROUPS, SSM_STATE)
    return (y_prompt, y_sample,
            win(k_last, nb), win(v_last, nb), st(p_re, nb), st(p_im, nb),
            win(kwin, n_seq), win(vwin, n_seq), st(s_re, n_seq), st(s_im, n_seq))
```

```python
import math

import jax
import jax.numpy as jnp
from jax import lax
from jax.experimental import pallas as pl
from jax.experimental.pallas import tpu as pltpu

F32 = jnp.float32
BF16 = jnp.bfloat16

N_META = 16
HEAD_DIM = 64
N_HEADS = 8
N_KV_HEADS = 2
WINDOW = 128
BLOCK = 128
ROPE_DIM = 16
ROPE_THETA = 500000.0
SSM_GROUP = 16
SSM_GROUPS = 32
SSM_STATE = 64
PAST_LEN = 8192
EPS = 1e-6
NEG = -1e30

D_MODEL = 1024
ATT_WIDTH = 512
KV_WIDTH = 128
SSM_WIDTH = 512
N_STATE = SSM_GROUPS * SSM_STATE
D_FF = 4096

ROWS = 512
LANES = 128
HALF = LANES // 2
SSM_TILE = 256
STATE_TILE = (SSM_TILE // SSM_GROUP) * SSM_STATE
N_SSM_TILES = SSM_WIDTH // SSM_TILE
LB_PER_TILE = STATE_TILE // LANES
N_LB = N_STATE // LANES
FF_CHUNK = 1024
VMEM_LIMIT = 56 * 1024 * 1024


def _dot(a, b):
    return jnp.dot(a, b, preferred_element_type=F32)


def _dot_nt(a, b):
    return lax.dot_general(a, b, (((1,), (1,)), ((), ())), preferred_element_type=F32)


def _rms(x, g):
    return x * lax.rsqrt(jnp.mean(x * x, axis=-1, keepdims=True) + EPS) * g


def _rope(x, cos, sa, sb):
    return x * cos + pltpu.roll(x, 8, axis=1) * sa + pltpu.roll(x, LANES - 8, axis=1) * sb


def _lane_is_lo(shape):
    return lax.broadcasted_iota(jnp.int32, shape, 1) < HALF


def _project_qkv(hn, w_in_ref):
    q = _dot(hn, w_in_ref[:, 0:ATT_WIDTH])
    kv = _dot(hn, w_in_ref[:, ATT_WIDTH:ATT_WIDTH + 2 * KV_WIDTH])
    return q, kv[:, 0:KV_WIDTH], kv[:, KV_WIDTH:2 * KV_WIDTH]


def _project_u(hn, w_in_ref):
    return _dot(hn, w_in_ref[:, ATT_WIDTH + 2 * KV_WIDTH:])


def _ssm_input(u, wbre_ref, wbim_ref, s_ref):
    ub = u.astype(BF16)
    for c in range(N_SSM_TILES):
        uc = ub[:, c * SSM_TILE:(c + 1) * SSM_TILE]
        bre, bim = _dot(uc, wbre_ref[c]), _dot(uc, wbim_ref[c])
        for l in range(LB_PER_TILE):
            s_ref[c * LB_PER_TILE + l, 0:ROWS, :] = bre[:, l * LANES:(l + 1) * LANES]
            s_ref[c * LB_PER_TILE + l, ROWS:2 * ROWS, :] = bim[:, l * LANES:(l + 1) * LANES]


def _ssm_output(s_ref, u, wcre_ref, wcim_ref, d_ref, wglu_ref):
    ys = []
    for c in range(N_SSM_TILES):
        blocks = range(c * LB_PER_TILE, (c + 1) * LB_PER_TILE)
        hr = jnp.concatenate([s_ref[l, 0:ROWS, :].astype(BF16) for l in blocks], axis=1)
        hi = jnp.concatenate([s_ref[l, ROWS:2 * ROWS, :].astype(BF16) for l in blocks], axis=1)
        ys.append(_dot(hr, wcre_ref[c]) + _dot(hi, wcim_ref[c]))
    y = jnp.concatenate(ys, axis=1) + d_ref[...] * u
    z = 0.5 * y * (1.0 + jnp.tanh(math.sqrt(2.0 / math.pi) * (y + 0.044715 * (y * y * y))))
    gate = 1.0 / (1.0 + jnp.exp(-_dot(z.astype(BF16), wglu_ref[...])))
    return z * gate


def _merge(x, att, s, gatt_ref, wout_ref, gpost_ref):
    a = _rms(att, gatt_ref[...]).astype(BF16)
    m = _dot(a, wout_ref[0:ATT_WIDTH, :]) + _dot(s, wout_ref[ATT_WIDTH:, :])
    return x + _rms(m, gpost_ref[...])


def _div(x, k):
    return lax.shift_right_logical(x, int(math.log2(k)))


def _mod(x, k):
    return lax.bitwise_and(x, k - 1)


def _sink_softmax(logits, mask, sink_col):
    lm = jnp.where(mask, logits, NEG)
    m = jnp.maximum(jnp.max(lm, axis=-1, keepdims=True), sink_col)
    e = jnp.exp(lm - m)
    den = jnp.sum(e, axis=-1, keepdims=True) + jnp.exp(sink_col - m)
    return e, den


def _prompt_mixer_kernel(
        sink_ref, x_ref, meta_ref, cos_ref, sa_ref, sb_ref, perm_ref, permt_ref,
        gpre_ref, win_ref,
        ar_ref, ai_ref, wbre_ref, wbim_ref, wcre_ref, wcim_ref, d_ref, wglu_ref,
        gatt_ref, gssm_ref, wout_ref, gpost_ref, wupf_ref, wdnf_ref,
        h1_ref, klast_ref, vlast_ref, sre_ref, sim_ref, wupb_ref, wdnb_ref,
        kbuf, vbuf, kcur, vcur, s_ref, hstate, att_ref):
    n = pl.program_id(0)
    wupb_ref[...] = wupf_ref[...].astype(BF16)
    wdnb_ref[...] = wdnf_ref[...].astype(BF16)
    nb = x_ref.shape[0]
    assert 2 * nb == 8
    last = pl.num_programs(0) - 1

    @pl.when(n == 0)
    def _init():
        kbuf[...] = jnp.zeros_like(kbuf)
        vbuf[...] = jnp.zeros_like(vbuf)
        hstate[...] = jnp.zeros_like(hstate)

    meta = meta_ref[...]
    x = x_ref[...].reshape(ROWS, D_MODEL)
    x = jnp.where(n == 0, jnp.concatenate([meta] * nb, axis=0), x)

    hn = _rms(x, gpre_ref[...]).astype(BF16)
    q, k, v = _project_qkv(hn, win_ref)

    cos, sa, sb = cos_ref[...], sa_ref[...], sb_ref[...]
    lo = _lane_is_lo((BLOCK, LANES))

    u = _project_u(_dot(perm_ref[...], hn).astype(BF16), win_ref)
    _ssm_input(u, wbre_ref, wbim_ref, s_ref)
    is_re = lax.broadcasted_iota(jnp.int32, (2 * nb, LANES), 0) < nb
    a1, a2 = [], []
    for l in range(N_LB):
        cols = slice(l * LANES, (l + 1) * LANES)
        ai = jnp.broadcast_to(ai_ref[:, cols], (2 * nb, LANES))
        a1.append(jnp.broadcast_to(ar_ref[:, cols], (2 * nb, LANES)))
        a2.append(jnp.where(is_re, -ai, ai))

    def swap(t):
        return pltpu.roll(t, nb, axis=0)

    hfin = []
    for l in range(N_LB):
        h = hstate[l]
        for i in range(BLOCK // 2):
            r_re = slice(i * 2 * nb, (i + 1) * 2 * nb)
            r_im = slice(ROWS + i * 2 * nb, ROWS + (i + 1) * 2 * nb)
            re, im = s_ref[l, r_re, :], s_ref[l, r_im, :]
            h0 = a1[l] * h + a2[l] * swap(h) + jnp.where(is_re, re, swap(im))
            h = a1[l] * h0 + a2[l] * swap(h0) + jnp.where(is_re, swap(re), im)
            s_ref[l, r_re, :] = jnp.where(is_re, h0, swap(h))
            s_ref[l, r_im, :] = jnp.where(is_re, swap(h0), h)
        hstate[l] = h
        hfin.append(h)
    ssm_o = _ssm_output(s_ref, u, wcre_ref, wcim_ref, d_ref, wglu_ref)
    ssm_n = _dot(permt_ref[...], _rms(ssm_o, gssm_ref[...]).astype(BF16)).astype(BF16)

    r_i = _mod(lax.broadcasted_iota(jnp.int32, (2 * BLOCK, 2 * BLOCK), 0), BLOCK)
    c_i = lax.broadcasted_iota(jnp.int32, (2 * BLOCK, 2 * BLOCK), 1)
    c_min = jnp.where(n == 0, 2 * BLOCK - N_META, jnp.where(n == 1, BLOCK - N_META, 0))
    mask = (c_i > r_i) & (c_i <= r_i + WINDOW) & (c_i >= c_min)
    top = lax.broadcasted_iota(jnp.int32, (2 * BLOCK, 1), 0) < BLOCK

    def variants(t):
        tr = pltpu.roll(t, HALF, axis=1)
        z = jnp.zeros_like(t)
        return (jnp.where(lo, t, z), jnp.where(lo, z, tr),
                jnp.where(lo, tr, z), jnp.where(lo, z, t))

    for b in range(nb):
        rows = slice(b * BLOCK, (b + 1) * BLOCK)
        kb = _rope(k[rows], cos, sa, sb)
        vb = v[rows]
        for i, (kv_, vv_) in enumerate(zip(variants(kb), variants(vb))):
            kbuf[b, i, 0:BLOCK, :] = kbuf[b, i, BLOCK:2 * BLOCK, :]
            vbuf[b, i, 0:BLOCK, :] = vbuf[b, i, BLOCK:2 * BLOCK, :]
            kbuf[b, i, BLOCK:2 * BLOCK, :] = kv_.astype(BF16)
            vbuf[b, i, BLOCK:2 * BLOCK, :] = vv_.astype(BF16)

        kcur[b] = kb
        vcur[b] = vb

        qs = []
        for j in range(ATT_WIDTH // LANES):
            qj = _rope(q[rows, j * LANES:(j + 1) * LANES], cos, sa, sb) * (HEAD_DIM ** -0.5)
            qs.append(qj.astype(BF16))
        for g in range(N_KV_HEADS):
            qst = jnp.concatenate([qs[2 * g], qs[2 * g + 1]], axis=0)
            o = None
            for half in range(2):
                var = 2 * g + half
                h_top, h_bot = 4 * g + half, 4 * g + 2 + half
                sink_col = jnp.where(top, sink_ref[h_top], sink_ref[h_bot])
                e, den = _sink_softmax(_dot_nt(qst, kbuf[b, var]), mask, sink_col)
                part = _dot(e.astype(BF16), vbuf[b, var]) / den
                o = part if o is None else o + part
            att_ref[rows, (2 * g) * LANES:(2 * g + 1) * LANES] = o[0:BLOCK]
            att_ref[rows, (2 * g + 1) * LANES:(2 * g + 2) * LANES] = o[BLOCK:2 * BLOCK]

    h1 = _merge(x, att_ref[...], ssm_n, gatt_ref, wout_ref, gpost_ref)
    h1_ref[...] = h1.reshape(h1_ref.shape)

    @pl.when(n == last)
    def _emit_state():
        for b in range(nb):
            klast_ref[b] = kcur[b].T
            vlast_ref[b] = vcur[b].T
        for l in range(N_LB):
            sre_ref[:, l * LANES:(l + 1) * LANES] = hfin[l][0:nb]
            sim_ref[:, l * LANES:(l + 1) * LANES] = hfin[l][nb:2 * nb]


def _const_spec(shape):
    zeros = (0,) * len(shape)
    return pl.BlockSpec(shape, lambda *_: zeros)


def _prompt_mixer(x_prompt, meta_blk, cos, sa, sb, sinks, p, w_up, w_down):
    nb, seq, _ = x_prompt.shape
    n_blocks = seq // BLOCK + 1
    r = jnp.arange(ROWS)
    perm = jax.nn.one_hot((r % nb) * BLOCK + r // nb, ROWS, dtype=BF16)
    xmap = lambda n: (0, jnp.maximum(n - 1, 0), 0)
    tmap = lambda n: (n, 0)
    n_slabs = D_FF // LANES
    assert n_slabs <= n_blocks
    upmap = lambda n: (0, jnp.minimum(n, n_slabs - 1))
    dnmap = lambda n: (jnp.minimum(n, n_slabs - 1), 0)
    in_specs = [
        pl.BlockSpec(memory_space=pltpu.SMEM),
        pl.BlockSpec((nb, BLOCK, D_MODEL), xmap),
        _const_spec((BLOCK, D_MODEL)),
        pl.BlockSpec((BLOCK, LANES), tmap),
        pl.BlockSpec((BLOCK, LANES), tmap),
        pl.BlockSpec((BLOCK, LANES), tmap),
        _const_spec((ROWS, ROWS)),
        _const_spec((ROWS, ROWS)),
    ] + [_const_spec(a.shape) for a in p] + [
        pl.BlockSpec((D_MODEL, LANES), upmap),
        pl.BlockSpec((LANES, D_MODEL), dnmap),
    ]
    out_shape = (
        jax.ShapeDtypeStruct((nb, seq, D_MODEL), F32),
        jax.ShapeDtypeStruct((nb, BLOCK, KV_WIDTH), F32),
        jax.ShapeDtypeStruct((nb, BLOCK, KV_WIDTH), F32),
        jax.ShapeDtypeStruct((nb, N_STATE), F32),
        jax.ShapeDtypeStruct((nb, N_STATE), F32),
        jax.ShapeDtypeStruct(w_up.shape, BF16),
        jax.ShapeDtypeStruct(w_down.shape, BF16),
    )
    out_specs = (
        pl.BlockSpec((nb, BLOCK, D_MODEL), xmap),
        _const_spec((nb, BLOCK, KV_WIDTH)),
        _const_spec((nb, BLOCK, KV_WIDTH)),
        _const_spec((nb, N_STATE)),
        _const_spec((nb, N_STATE)),
        pl.BlockSpec((D_MODEL, LANES), upmap),
        pl.BlockSpec((LANES, D_MODEL), dnmap),
    )
    scratch = [
        pltpu.VMEM((nb, 4, 2 * BLOCK, LANES), BF16),
        pltpu.VMEM((nb, 4, 2 * BLOCK, LANES), BF16),
        pltpu.VMEM((nb, BLOCK, KV_WIDTH), F32),
        pltpu.VMEM((nb, BLOCK, KV_WIDTH), F32),
        pltpu.VMEM((N_LB, 2 * ROWS, LANES), F32),
        pltpu.VMEM((N_LB, 2 * nb, LANES), F32),
        pltpu.VMEM((ROWS, ATT_WIDTH), F32),
    ]
    return pl.pallas_call(
        _prompt_mixer_kernel,
        grid=(n_blocks,),
        in_specs=in_specs,
        out_specs=out_specs,
        out_shape=out_shape,
        scratch_shapes=scratch,
        compiler_params=pltpu.CompilerParams(
            dimension_semantics=("arbitrary",), vmem_limit_bytes=VMEM_LIMIT),
        name="prompt_mixer",
    )(sinks, x_prompt, meta_blk, cos, sa, sb, perm, perm.T, *p, w_up, w_down)


GROUP = 8


def _decode_mixer_kernel(
        x_ref, cos_ref, sa_ref, sb_ref, cost_ref, sat_ref, sbt_ref, sink_ref, perm_ref, permt_ref,
        ck_ref, cv_ref, stre_ref, stim_ref, wkt_ref, wvt_ref,
        gpre_ref, win_ref, ar_ref, ai_ref, wbre_ref, wbim_ref, wcre_ref, wcim_ref,
        d_ref, wglu_ref, gatt_ref, gssm_ref, wout_ref, gpost_ref,
        h1_ref, kwin_ref, vwin_ref, sre_ref, sim_ref,
        qp, knew, vnew, knew_b, vnew_b, ssm_n, att_ref, s_ref):
    g = pl.program_id(0)
    n_seq = stre_ref.shape[0]
    n_tok = ROWS // n_seq
    grp_rows = GROUP * n_tok
    last = pl.num_programs(0) - 1

    @pl.when(g == 0)
    def _project_and_ssm():
        hn = _rms(x_ref[...], gpre_ref[...]).astype(BF16)
        q = _dot(hn, win_ref[:, 0:ATT_WIDTH])
        u = _project_u(_dot(perm_ref[...], hn).astype(BF16), win_ref)
        cos, sa, sb = cos_ref[...], sa_ref[...], sb_ref[...]
        kt = _dot_nt(wkt_ref[...], hn)
        kt = (kt * cost_ref[...] + pltpu.roll(kt, 8, axis=0) * sat_ref[...]
              + pltpu.roll(kt, KV_WIDTH - 8, axis=0) * sbt_ref[...])
        vt = _dot_nt(wvt_ref[...], hn)
        knew[...] = kt
        vnew[...] = vt
        knew_b[...] = kt.astype(BF16)
        vnew_b[...] = vt.astype(BF16)
        lo = _lane_is_lo((ROWS, LANES))
        for j in range(ATT_WIDTH // LANES):
            qj = _rope(q[:, j * LANES:(j + 1) * LANES], cos, sa, sb) * (HEAD_DIM ** -0.5)
            qr = pltpu.roll(qj, HALF, axis=1)
            z = jnp.zeros_like(qj)
            if j < 2:
                qp[2 * j] = jnp.where(lo, qj, z)
                qp[2 * j + 1] = jnp.where(lo, qr, z)
            else:
                qp[2 * j] = jnp.where(lo, z, qr)
                qp[2 * j + 1] = jnp.where(lo, z, qj)

        _ssm_input(u, wbre_ref, wbim_ref, s_ref)
        for l in range(N_LB):
            cols = slice(l * LANES, (l + 1) * LANES)
            ar, ai = ar_ref[:, cols], ai_ref[:, cols]
            hr, hi = stre_ref[:, cols], stim_ref[:, cols]
            for t in range(n_tok):
                r_re = slice(t * n_seq, (t + 1) * n_seq)
                r_im = slice(ROWS + t * n_seq, ROWS + (t + 1) * n_seq)
                hr, hi = (ar * hr - ai * hi + s_ref[l, r_re, :],
                          ar * hi + ai * hr + s_ref[l, r_im, :])
                s_ref[l, r_re, :] = hr
                s_ref[l, r_im, :] = hi
            sre_ref[:, cols] = hr
            sim_ref[:, cols] = hi
        ssm_o = _ssm_output(s_ref, u, wcre_ref, wcim_ref, d_ref, wglu_ref)
        ssm_n[...] = _dot(permt_ref[...], _rms(ssm_o, gssm_ref[...]).astype(BF16)).astype(BF16)

    grp = pl.ds(pl.multiple_of(g * grp_rows, grp_rows), grp_rows)
    qb = jnp.concatenate([qp[h, grp, :] for h in range(N_HEADS)], axis=0).astype(BF16)
    kc_f = ck_ref[...]
    vc_f = cv_ref[...]
    seq_rows = [slice(b * KV_WIDTH, (b + 1) * KV_WIDTH) for b in range(GROUP)]
    kcat = jnp.concatenate([kc_f[r] for r in seq_rows], axis=1).astype(BF16)
    vcat = jnp.concatenate([vc_f[r] for r in seq_rows], axis=1).astype(BF16)
    n_q = N_HEADS * grp_rows
    n_c = GROUP * WINDOW

    r_c = lax.broadcasted_iota(jnp.int32, (n_q, n_c), 0)
    c_c = lax.broadcasted_iota(jnp.int32, (n_q, n_c), 1)
    mask_c = ((_div(c_c, WINDOW) == _mod(_div(r_c, n_tok), GROUP))
              & (_mod(c_c, WINDOW) > _mod(r_c, n_tok)))
    lc = jnp.where(mask_c, _dot(qb, kcat), NEG)
    tile = pl.ds(pl.multiple_of(_div(g * grp_rows, LANES) * LANES, LANES), LANES)
    first = _mod(g * grp_rows, LANES)
    r_n = lax.broadcasted_iota(jnp.int32, (n_q, LANES), 0)
    c_n = lax.broadcasted_iota(jnp.int32, (n_q, LANES), 1) - first
    mask_n = ((c_n >= 0) & (c_n < grp_rows)
              & (_div(c_n, n_tok) == _mod(_div(r_n, n_tok), GROUP))
              & (_mod(c_n, n_tok) <= _mod(r_n, n_tok)))
    ln = jnp.where(mask_n, _dot(qb, knew_b[:, tile]), NEG)

    sink_col = sink_ref[...]
    m = jnp.maximum(jnp.maximum(jnp.max(lc, axis=-1, keepdims=True),
                                jnp.max(ln, axis=-1, keepdims=True)), sink_col)
    ec = jnp.exp(lc - m)
    en = jnp.exp(ln - m)
    den = (jnp.sum(ec, axis=-1, keepdims=True) + jnp.sum(en, axis=-1, keepdims=True)
           + jnp.exp(sink_col - m))
    o = (_dot_nt(ec.astype(BF16), vcat) + _dot_nt(en.astype(BF16), vnew_b[:, tile])) / den

    lo = _lane_is_lo((grp_rows, LANES))
    for j in range(ATT_WIDTH // LANES):
        o_even = o[(2 * j) * grp_rows:(2 * j + 1) * grp_rows]
        o_odd = o[(2 * j + 1) * grp_rows:(2 * j + 2) * grp_rows]
        if j < 2:
            pair = jnp.where(lo, o_even, pltpu.roll(o_odd, HALF, axis=1))
        else:
            pair = jnp.where(lo, pltpu.roll(o_even, HALF, axis=1), o_odd)
        att_ref[grp, j * LANES:(j + 1) * LANES] = pair

    keep = WINDOW - n_tok
    to_tail = _mod(keep - first + LANES, LANES)
    kt_g = pltpu.roll(knew[:, tile], to_tail, axis=1)
    vt_g = pltpu.roll(vnew[:, tile], to_tail, axis=1)
    is_new = lax.broadcasted_iota(jnp.int32, (KV_WIDTH, WINDOW), 1) >= keep
    for b, rows in enumerate(seq_rows):
        k_b = kt_g if b == 0 else pltpu.roll(kt_g, LANES - n_tok * b, axis=1)
        v_b = vt_g if b == 0 else pltpu.roll(vt_g, LANES - n_tok * b, axis=1)
        kwin_ref[rows, :] = jnp.where(is_new, k_b, pltpu.roll(kc_f[rows], keep, axis=1))
        vwin_ref[rows, :] = jnp.where(is_new, v_b, pltpu.roll(vc_f[rows], keep, axis=1))

    @pl.when(g == last)
    def _merge_out():
        h1_ref[...] = _merge(x_ref[...], att_ref[...], ssm_n[...], gatt_ref, wout_ref, gpost_ref)


def _decode_mixer(x2d, tables, tables_t, sink_col, ck2d, cv2d, st_re, st_im, wkt, wvt, p):
    n_seq = st_re.shape[0]
    n_groups = n_seq // GROUP
    cmap = lambda g: (g, 0)
    n_tok = ROWS // n_seq
    r = jnp.arange(ROWS)
    perm = jax.nn.one_hot((r % n_seq) * n_tok + r // n_seq, ROWS, dtype=BF16)
    head = (x2d, *tables, *tables_t, sink_col, perm, perm.T)
    in_specs = [_const_spec(a.shape) for a in head] + [
        pl.BlockSpec((GROUP * KV_WIDTH, WINDOW), cmap),
        pl.BlockSpec((GROUP * KV_WIDTH, WINDOW), cmap),
    ] + [_const_spec(a.shape) for a in (st_re, st_im, wkt, wvt, *p)]
    out_shape = (
        jax.ShapeDtypeStruct(x2d.shape, F32),
        jax.ShapeDtypeStruct(ck2d.shape, F32),
        jax.ShapeDtypeStruct(cv2d.shape, F32),
        jax.ShapeDtypeStruct(st_re.shape, F32),
        jax.ShapeDtypeStruct(st_im.shape, F32),
    )
    out_specs = (
        _const_spec(x2d.shape),
        pl.BlockSpec((GROUP * KV_WIDTH, WINDOW), cmap),
        pl.BlockSpec((GROUP * KV_WIDTH, WINDOW), cmap),
        _const_spec(st_re.shape), _const_spec(st_im.shape),
    )
    scratch = [
        pltpu.VMEM((N_HEADS, ROWS, LANES), F32),
        pltpu.VMEM((KV_WIDTH, ROWS), F32),
        pltpu.VMEM((KV_WIDTH, ROWS), F32),
        pltpu.VMEM((KV_WIDTH, ROWS), BF16),
        pltpu.VMEM((KV_WIDTH, ROWS), BF16),
        pltpu.VMEM((ROWS, SSM_WIDTH), BF16),
        pltpu.VMEM((ROWS, ATT_WIDTH), F32),
        pltpu.VMEM((N_LB, 2 * ROWS, LANES), F32),
    ]
    return pl.pallas_call(
        _decode_mixer_kernel,
        grid=(n_groups,),
        in_specs=in_specs,
        out_specs=out_specs,
        out_shape=out_shape,
        scratch_shapes=scratch,
        compiler_params=pltpu.CompilerParams(
            dimension_semantics=("arbitrary",), vmem_limit_bytes=VMEM_LIMIT),
        name="decode_mixer",
    )(*head, ck2d, cv2d, st_re, st_im, wkt, wvt, *p)


def _mlp_kernel(xp_ref, xd_ref, gpre_ref, wup_ref, wdn_ref, gpost_ref, op_ref, od_ref):
    i = pl.program_id(0)
    x = jnp.where(i == 0, xd_ref[...], xp_ref[...])
    hn = _rms(x, gpre_ref[...]).astype(BF16)
    acc = None
    for c in range(D_FF // FF_CHUNK):
        cols = slice(c * FF_CHUNK, (c + 1) * FF_CHUNK)
        a = jnp.maximum(_dot(hn, wup_ref[:, cols]), 0.0)
        part = _dot((a * a).astype(BF16), wdn_ref[cols, :])
        acc = part if acc is None else acc + part
    op_ref[...] = x + _rms(acc, gpost_ref[...])

    @pl.when(i == 0)
    def _store_decode():
        od_ref[...] = op_ref[...]


def _mlp(xp2d, xd2d, g_pre, w_up, w_down, g_post):
    n_tiles = xp2d.shape[0] // ROWS
    assert xd2d.shape[0] == ROWS
    pmap = lambda i: (jnp.maximum(i - 1, 0), 0)
    return pl.pallas_call(
        _mlp_kernel,
        grid=(n_tiles + 1,),
        in_specs=[pl.BlockSpec((ROWS, D_MODEL), pmap), _const_spec(xd2d.shape),
                  _const_spec(g_pre.shape), _const_spec(w_up.shape), _const_spec(w_down.shape),
                  _const_spec(g_post.shape)],
        out_specs=(pl.BlockSpec((ROWS, D_MODEL), pmap), _const_spec(xd2d.shape)),
        out_shape=(jax.ShapeDtypeStruct(xp2d.shape, F32), jax.ShapeDtypeStruct(xd2d.shape, F32)),
        compiler_params=pltpu.CompilerParams(
            dimension_semantics=("arbitrary",), vmem_limit_bytes=VMEM_LIMIT),
        name="mlp",
    )(xp2d, xd2d, g_pre, w_up, w_down, g_post)


def _zoh(a_re, a_im, log_dt, b_re, b_im):
    dt = jnp.exp(log_dt)[:, None]
    mag = jnp.exp(a_re * dt)
    abar_re, abar_im = mag * jnp.cos(a_im * dt), mag * jnp.sin(a_im * dt)
    nr, ni = abar_re - 1.0, abar_im
    den = a_re * a_re + a_im * a_im
    coef_re = (nr * a_re + ni * a_im) / den
    coef_im = (ni * a_re - nr * a_im) / den
    bbar_re = coef_re[..., None] * b_re - coef_im[..., None] * b_im
    bbar_im = coef_re[..., None] * b_im + coef_im[..., None] * b_re
    return abar_re, abar_im, bbar_re, bbar_im


def _same_group_mask():
    row_g = jnp.arange(SSM_TILE)[:, None] // SSM_GROUP
    col_g = jnp.arange(STATE_TILE)[None, :] // SSM_STATE
    return row_g == col_g


def _block_diag_in(bbar):
    w = jnp.swapaxes(bbar, 1, 2).reshape(N_SSM_TILES, SSM_TILE, SSM_STATE)
    w = jnp.tile(w, (1, 1, SSM_TILE // SSM_GROUP))
    return jnp.where(_same_group_mask()[None], w, 0.0).astype(BF16)


def _block_diag_out(c):
    w = jnp.swapaxes(c, 1, 2).reshape(N_SSM_TILES, STATE_TILE, SSM_GROUP)
    w = jnp.tile(w, (1, 1, SSM_TILE // SSM_GROUP))
    return jnp.where(_same_group_mask().T[None], w, 0.0).astype(BF16)


def _rope_tables(pos):
    half = ROPE_DIM // 2
    inv = ROPE_THETA ** (-jnp.arange(half, dtype=F32) / half)
    ang = pos.astype(F32)[:, None] * inv[None, :]
    cos, sin = jnp.cos(ang), jnp.sin(ang)
    d = jnp.arange(LANES) % HEAD_DIM
    f = d % half
    cos_t = jnp.where(d[None, :] < ROPE_DIM, cos[:, f], 1.0)
    sa_t = jnp.where((d[None, :] >= half) & (d[None, :] < ROPE_DIM), sin[:, f], 0.0)
    sb_t = jnp.where(d[None, :] < half, -sin[:, f], 0.0)
    return cos_t, sa_t, sb_t


def kernel(x_prompt, x_sample, cache_k_win, cache_v_win, state_ssm_re, state_ssm_im, meta_tokens, norm_mix_pre, w_in, attn_sinks, ssm_a_re, ssm_a_im, ssm_log_dt, ssm_b_re, ssm_b_im, ssm_c_re, ssm_c_im, ssm_d, w_glu, norm_att_out, norm_ssm_out, w_out, norm_mix_post, norm_mlp_pre, w_up, w_down, norm_mlp_post):
    depth = w_in.shape[0]
    assert depth == 1
    l = 0
    nb, seq, _ = x_prompt.shape
    n_seq, n_tok, _ = x_sample.shape
    assert n_seq * n_tok == ROWS and nb * BLOCK == ROWS and seq % BLOCK == 0

    abar_re, abar_im, bbar_re, bbar_im = _zoh(
        ssm_a_re[l], ssm_a_im[l], ssm_log_dt[l], ssm_b_re[l], ssm_b_im[l])
    row = lambda a: a.reshape(1, -1)
    params = (
        row(norm_mix_pre[l]), w_in[l].astype(BF16),
        row(abar_re), row(abar_im),
        _block_diag_in(bbar_re), _block_diag_in(bbar_im),
        _block_diag_out(ssm_c_re[l]), _block_diag_out(-ssm_c_im[l]),
        row(ssm_d[l]), w_glu[l].astype(BF16),
        row(norm_att_out[l]), row(norm_ssm_out[l]), w_out[l].astype(BF16), row(norm_mix_post[l]),
    )

    front = BLOCK - N_META
    meta_blk = jnp.concatenate([jnp.zeros((front, D_MODEL), F32), meta_tokens], axis=0)
    cos, sa, sb = _rope_tables(jnp.arange(seq + BLOCK, dtype=jnp.int32) - front)
    h1, k_last, v_last, p_re, p_im, w_up_b, w_down_b = _prompt_mixer(
        x_prompt, meta_blk, cos, sa, sb, attn_sinks[l], params, w_up[l], w_down[l])
    mlp_params = (row(norm_mlp_pre[l]), w_up_b, w_down_b, row(norm_mlp_post[l]))

    pos_s = PAST_LEN + jnp.arange(n_tok, dtype=jnp.int32)
    tabs = _rope_tables(pos_s)
    tables = tuple(jnp.tile(t, (n_seq, 1)) for t in tabs)
    tables_t = tuple(jnp.tile(t.T, (1, n_seq)) for t in tabs)
    sink_col = jnp.repeat(attn_sinks[l], n_tok * GROUP).reshape(-1, 1)
    to_t = lambda a: jnp.transpose(a, (0, 2, 3, 1)).reshape(-1, WINDOW)
    from_t = lambda a, n: jnp.transpose(
        a.reshape(n, N_KV_HEADS, HEAD_DIM, WINDOW), (0, 3, 1, 2))[None]
    w_kv = w_in[l][:, ATT_WIDTH:ATT_WIDTH + 2 * KV_WIDTH].astype(BF16)
    h1s, kwin, vwin, s_re, s_im = _decode_mixer(
        x_sample.reshape(ROWS, D_MODEL), tables, tables_t, sink_col, to_t(cache_k_win[l]), to_t(cache_v_win[l]),
        state_ssm_re[l].reshape(n_seq, N_STATE), state_ssm_im[l].reshape(n_seq, N_STATE),
        w_kv[:, 0:KV_WIDTH].T, w_kv[:, KV_WIDTH:].T, params)
    y_prompt, ys = _mlp(h1.reshape(nb * seq, D_MODEL), h1s, *mlp_params)
    y_prompt = y_prompt.reshape(nb, seq, D_MODEL)
    y_sample = ys.reshape(n_seq, n_tok, D_MODEL)

    win = from_t
    st = lambda a, n: a.reshape(1, n, SSM_GROUPS, SSM_STATE)
    return (y_prompt, y_sample,
            win(k_last, nb), win(v_last, nb), st(p_re, nb), st(p_im, nb),
            win(kwin, n_seq), win(vwin, n_seq), st(s_re, n_seq), st(s_im, n_seq))
```

```python
import math

import jax
import jax.numpy as jnp
import numpy as np
from jax import lax
from jax.experimental import pallas as pl
from jax.experimental.pallas import tpu as pltpu

F32 = jnp.float32
BF16 = jnp.bfloat16

N_META = 16
HEAD_DIM = 64
N_HEADS = 8
N_KV_HEADS = 2
WINDOW = 128
BLOCK = 128
ROPE_DIM = 16
ROPE_THETA = 500000.0
SSM_GROUP = 16
SSM_GROUPS = 32
SSM_STATE = 64
PAST_LEN = 8192
EPS = 1e-6
NEG = -1e30

D_MODEL = 1024
ATT_WIDTH = 512
KV_WIDTH = 128
SSM_WIDTH = 512
N_STATE = SSM_GROUPS * SSM_STATE
D_FF = 4096

ROWS = 512
MLP_ROWS = 1024
LANES = 128
HALF = LANES // 2
SSM_TILE = 256
STATE_TILE = (SSM_TILE // SSM_GROUP) * SSM_STATE
N_SSM_TILES = SSM_WIDTH // SSM_TILE
LB_PER_TILE = STATE_TILE // LANES
N_LB = N_STATE // LANES
FF_CHUNK = 1024
VMEM_LIMIT = 56 * 1024 * 1024


def _dot(a, b):
    return jnp.dot(a, b, preferred_element_type=F32)


def _dot_nt(a, b):
    return lax.dot_general(a, b, (((1,), (1,)), ((), ())), preferred_element_type=F32)


def _rms(x, g):
    return x * lax.rsqrt(jnp.mean(x * x, axis=-1, keepdims=True) + EPS) * g


def _rope(x, cos, sa, sb):
    return x * cos + pltpu.roll(x, 8, axis=1) * sa + pltpu.roll(x, LANES - 8, axis=1) * sb


def _lane_is_lo(shape):
    return lax.broadcasted_iota(jnp.int32, shape, 1) < HALF


def _project_qkv(hn, w_in_ref):
    q = _dot(hn, w_in_ref[:, 0:ATT_WIDTH])
    kv = _dot(hn, w_in_ref[:, ATT_WIDTH:ATT_WIDTH + 2 * KV_WIDTH])
    return q, kv[:, 0:KV_WIDTH], kv[:, KV_WIDTH:2 * KV_WIDTH]


def _project_u(hn, w_in_ref):
    return _dot(hn, w_in_ref[:, ATT_WIDTH + 2 * KV_WIDTH:])


def _ssm_input(u, wbre_ref, wbim_ref, s_ref):
    ub = u.astype(BF16)
    for c in range(N_SSM_TILES):
        uc = ub[:, c * SSM_TILE:(c + 1) * SSM_TILE]
        bre, bim = _dot(uc, wbre_ref[c]), _dot(uc, wbim_ref[c])
        for l in range(LB_PER_TILE):
            s_ref[c * LB_PER_TILE + l, 0:ROWS, :] = bre[:, l * LANES:(l + 1) * LANES]
            s_ref[c * LB_PER_TILE + l, ROWS:2 * ROWS, :] = bim[:, l * LANES:(l + 1) * LANES]


def _ssm_output(s_ref, u, wcre_ref, wcim_ref, d_ref, wglu_ref):
    ys = []
    for c in range(N_SSM_TILES):
        blocks = range(c * LB_PER_TILE, (c + 1) * LB_PER_TILE)
        hr = jnp.concatenate([s_ref[l, 0:ROWS, :].astype(BF16) for l in blocks], axis=1)
        hi = jnp.concatenate([s_ref[l, ROWS:2 * ROWS, :].astype(BF16) for l in blocks], axis=1)
        ys.append(_dot(hr, wcre_ref[c]) + _dot(hi, wcim_ref[c]))
    y = jnp.concatenate(ys, axis=1) + d_ref[...] * u
    z = 0.5 * y * (1.0 + jnp.tanh(math.sqrt(2.0 / math.pi) * (y + 0.044715 * (y * y * y))))
    gate = 1.0 / (1.0 + jnp.exp(-_dot(z.astype(BF16), wglu_ref[...])))
    return z * gate


def _merge(x, att, s, gatt_ref, wout_ref, gpost_ref):
    a = _rms(att, gatt_ref[...]).astype(BF16)
    m = _dot(a, wout_ref[0:ATT_WIDTH, :]) + _dot(s, wout_ref[ATT_WIDTH:, :])
    return x + _rms(m, gpost_ref[...])


def _div(x, k):
    return lax.shift_right_logical(x, int(math.log2(k)))


def _mod(x, k):
    return lax.bitwise_and(x, k - 1)


def _sink_softmax(logits, bias, sink_col):
    lm = logits + bias
    m =jnp.maximum(jnp.max(lm, axis=-1, keepdims=True), sink_col)
    e = jnp.exp(lm - m)
    den = jnp.sum(e, axis=-1, keepdims=True) + jnp.exp(sink_col - m)
    return e, den


def _prompt_mixer_kernel(
        sink_ref, x_ref, meta_ref, cos_ref, sa_ref, sb_ref, bias_ref, perm_ref, permt_ref,
        gpre_ref, win_ref,
        ar_ref, ai_ref, wbre_ref, wbim_ref, wcre_ref, wcim_ref, d_ref, wglu_ref,
        gatt_ref, gssm_ref, wout_ref, gpost_ref, wupf_ref, wdnf_ref,
        h1_ref, klast_ref, vlast_ref, sre_ref, sim_ref, wupb_ref, wdnb_ref,
        kbuf, vbuf, kcur, vcur, s_ref, hstate, att_ref):
    n = pl.program_id(0)
    wupb_ref[...] = wupf_ref[...].astype(BF16)
    wdnb_ref[...] = wdnf_ref[...].astype(BF16)
    nb = x_ref.shape[0]
    assert 2 * nb == 8
    last = pl.num_programs(0) - 1

    @pl.when(n == 0)
    def _init():
        kbuf[...] = jnp.zeros_like(kbuf)
        vbuf[...] = jnp.zeros_like(vbuf)
        hstate[...] = jnp.zeros_like(hstate)

    meta = meta_ref[...]
    x = x_ref[...].reshape(ROWS, D_MODEL)
    x = jnp.where(n == 0, jnp.concatenate([meta] * nb, axis=0), x)

    hn = _rms(x, gpre_ref[...]).astype(BF16)
    q, k, v = _project_qkv(hn, win_ref)

    cos, sa, sb = cos_ref[...], sa_ref[...], sb_ref[...]
    lo = _lane_is_lo((BLOCK, LANES))

    u = _project_u(_dot(perm_ref[...], hn).astype(BF16), win_ref)
    _ssm_input(u, wbre_ref, wbim_ref, s_ref)
    is_re = lax.broadcasted_iota(jnp.int32, (2 * nb, LANES), 0) < nb
    a1, a2 = [], []
    for l in range(N_LB):
        cols = slice(l * LANES, (l + 1) * LANES)
        ai = jnp.broadcast_to(ai_ref[:, cols], (2 * nb, LANES))
        a1.append(jnp.broadcast_to(ar_ref[:, cols], (2 * nb, LANES)))
        a2.append(jnp.where(is_re, -ai, ai))

    def swap(t):
        return pltpu.roll(t, nb, axis=0)

    hfin = []
    for l in range(N_LB):
        h = hstate[l]
        g = swap(h)
        a2n = -a2[l]
        for i in range(BLOCK // 2):
            r_re = slice(i * 2 * nb, (i + 1) * 2 * nb)
            r_im = slice(ROWS + i * 2 * nb, ROWS + (i + 1) * 2 * nb)
            re, im_s = s_ref[l, r_re, :], swap(s_ref[l, r_im, :])
            h0 = a1[l] * h + a2[l] * g + jnp.where(is_re, re, im_s)
            g0 = swap(h0)
            g = a1[l] * g0 + a2n * h0 + jnp.where(is_re, im_s, re)
            h = swap(g)
            s_ref[l, r_re, :] = jnp.where(is_re, h0, g)
            s_ref[l, r_im, :] = jnp.where(is_re, g0, h)
        hstate[l] = h
        hfin.append(h)
    ssm_o = _ssm_output(s_ref, u, wcre_ref, wcim_ref, d_ref, wglu_ref)
    ssm_n = _dot(permt_ref[...], _rms(ssm_o, gssm_ref[...]).astype(BF16)).astype(BF16)

    mask = bias_ref[0]
    cur = pl.ds(pl.multiple_of(_mod(n, 2) * BLOCK, BLOCK), BLOCK)
    top = lax.broadcasted_iota(jnp.int32, (2 * BLOCK, 1), 0) < BLOCK

    def variants(t):
        tr = pltpu.roll(t, HALF, axis=1)
        z = jnp.zeros_like(t)
        return (jnp.where(lo, t, z), jnp.where(lo, z, tr),
                jnp.where(lo, tr, z), jnp.where(lo, z, t))

    for b in range(nb):
        rows = slice(b * BLOCK, (b + 1) * BLOCK)
        kb = _rope(k[rows], cos, sa, sb)
        vb = v[rows]
        for i, (kv_, vv_) in enumerate(zip(variants(kb), variants(vb))):
            kbuf[b, i, cur, :] = kv_.astype(BF16)
            vbuf[b, i, cur, :] = vv_.astype(BF16)

        kcur[b] = kb
        vcur[b] = vb

        qs = []
        for j in range(ATT_WIDTH // LANES):
            qs.append(_rope(q[rows, j * LANES:(j + 1) * LANES], cos, sa, sb).astype(BF16))
        for g in range(N_KV_HEADS):
            qst = jnp.concatenate([qs[2 * g], qs[2 * g + 1]], axis=0)
            o = None
            for half in range(2):
                var = 2 * g + half
                h_top, h_bot = 4 * g + half, 4 * g + 2 + half
                sink_col = jnp.where(top, sink_ref[h_top], sink_ref[h_bot])
                e, den = _sink_softmax(_dot_nt(qst, kbuf[b, var]), mask, sink_col)
                part = _dot(e.astype(BF16), vbuf[b, var]) / den
                o = part if o is None else o + part
            att_ref[rows, (2 * g) * LANES:(2 * g + 1) * LANES] = o[0:BLOCK]
            att_ref[rows, (2 * g + 1) * LANES:(2 * g + 2) * LANES] = o[BLOCK:2 * BLOCK]

    h1 = _merge(x, att_ref[...], ssm_n, gatt_ref, wout_ref, gpost_ref)
    h1_ref[...] = h1.reshape(h1_ref.shape)

    @pl.when(n == last)
    def _emit_state():
        for b in range(nb):
            klast_ref[b] = kcur[b].T
            vlast_ref[b] = vcur[b].T
        for l in range(N_LB):
            sre_ref[:, l * LANES:(l + 1) * LANES] = hfin[l][0:nb]
            sim_ref[:, l * LANES:(l + 1) * LANES] = hfin[l][nb:2 * nb]


def _window_bias():
    r = (np.arange(2 * BLOCK) % BLOCK)[:, None]
    phys = np.arange(2 * BLOCK)[None, :]
    tables = []
    for c_min in (2 * BLOCK - N_META, BLOCK - N_META, 0):
        for parity in (0, 1):
            c = phys if parity == 1 else (phys + BLOCK) % (2 * BLOCK)
            ok = (c > r) & (c <= r + WINDOW) & (c >= c_min)
            tables.append(np.where(ok, 0.0, NEG))
    return jnp.asarray(np.stack(tables), dtype=F32)


def _const_spec(shape):
    zeros = (0,) * len(shape)
    return pl.BlockSpec(shape, lambda *_: zeros)


def _prompt_mixer(x_prompt, meta_blk, cos, sa, sb, sinks, p, w_up, w_down):
    nb, seq, _ = x_prompt.shape
    n_blocks = seq // BLOCK + 1
    r = jnp.arange(ROWS)
    perm = jax.nn.one_hot((r % nb) * BLOCK + r // nb, ROWS, dtype=BF16)
    xmap = lambda n: (0, jnp.maximum(n - 1, 0), 0)
    tmap = lambda n: (n, 0)
    bmap = lambda n: (2 * jnp.minimum(n, 2) + n % 2, 0, 0)
    n_slabs = D_FF // LANES
    assert n_slabs <= n_blocks
    upmap = lambda n: (0, jnp.minimum(n, n_slabs - 1))
    dnmap = lambda n: (jnp.minimum(n, n_slabs - 1), 0)
    in_specs = [
        pl.BlockSpec(memory_space=pltpu.SMEM),
        pl.BlockSpec((nb, BLOCK, D_MODEL), xmap),
        _const_spec((BLOCK, D_MODEL)),
        pl.BlockSpec((BLOCK, LANES), tmap),
        pl.BlockSpec((BLOCK, LANES), tmap),
        pl.BlockSpec((BLOCK, LANES), tmap),
        pl.BlockSpec((1, 2 * BLOCK, 2 * BLOCK), bmap),
        _const_spec((ROWS, ROWS)),
        _const_spec((ROWS, ROWS)),
    ] + [_const_spec(a.shape) for a in p] + [
        pl.BlockSpec((D_MODEL, LANES), upmap),
        pl.BlockSpec((LANES, D_MODEL), dnmap),
    ]
    out_shape = (
        jax.ShapeDtypeStruct((nb, seq, D_MODEL), F32),
        jax.ShapeDtypeStruct((nb, BLOCK, KV_WIDTH), F32),
        jax.ShapeDtypeStruct((nb, BLOCK, KV_WIDTH), F32),
        jax.ShapeDtypeStruct((nb, N_STATE), F32),
        jax.ShapeDtypeStruct((nb, N_STATE), F32),
        jax.ShapeDtypeStruct(w_up.shape, BF16),
        jax.ShapeDtypeStruct(w_down.shape, BF16),
    )
    out_specs = (
        pl.BlockSpec((nb, BLOCK, D_MODEL), xmap),
        _const_spec((nb, BLOCK, KV_WIDTH)),
        _const_spec((nb, BLOCK, KV_WIDTH)),
        _const_spec((nb, N_STATE)),
        _const_spec((nb, N_STATE)),
        pl.BlockSpec((D_MODEL, LANES), upmap),
        pl.BlockSpec((LANES, D_MODEL), dnmap),
    )
    scratch = [
        pltpu.VMEM((nb, 4, 2 * BLOCK, LANES), BF16),
        pltpu.VMEM((nb, 4, 2 * BLOCK, LANES), BF16),
        pltpu.VMEM((nb, BLOCK, KV_WIDTH), F32),
        pltpu.VMEM((nb, BLOCK, KV_WIDTH), F32),
        pltpu.VMEM((N_LB, 2 * ROWS, LANES), F32),
        pltpu.VMEM((N_LB, 2 * nb, LANES), F32),
        pltpu.VMEM((ROWS, ATT_WIDTH), F32),
    ]
    return pl.pallas_call(
        _prompt_mixer_kernel,
        grid=(n_blocks,),
        in_specs=in_specs,
        out_specs=out_specs,
        out_shape=out_shape,
        scratch_shapes=scratch,
        compiler_params=pltpu.CompilerParams(
            dimension_semantics=("arbitrary",), vmem_limit_bytes=VMEM_LIMIT),
        name="prompt_mixer",
    )(sinks, x_prompt, meta_blk, cos, sa, sb, _window_bias(), perm, perm.T, *p, w_up, w_down)


GROUP = 8


def _decode_mixer_kernel(
        x_ref, cos_ref, sa_ref, sb_ref, cost_ref, sat_ref, sbt_ref, sink_ref, perm_ref, permt_ref,
        ck_ref, cv_ref, stre_ref, stim_ref, wkt_ref, wvt_ref,
        gpre_ref, win_ref, ar_ref, ai_ref, wbre_ref, wbim_ref, wcre_ref, wcim_ref,
        d_ref, wglu_ref, gatt_ref, gssm_ref, wout_ref, gpost_ref,
        h1_ref, kwin_ref, vwin_ref, sre_ref, sim_ref,
        qp, knew, vnew, knew_b, vnew_b, ssm_n, att_ref, s_ref):
    g = pl.program_id(0)
    n_seq = stre_ref.shape[0]
    n_tok = ROWS // n_seq
    grp_rows = GROUP * n_tok
    last = pl.num_programs(0) - 1

    @pl.when(g == 0)
    def _project_and_ssm():
        hn = _rms(x_ref[...], gpre_ref[...]).astype(BF16)
        q = _dot(hn, win_ref[:, 0:ATT_WIDTH])
        u = _project_u(_dot(perm_ref[...], hn).astype(BF16), win_ref)
        cos, sa, sb = cos_ref[...], sa_ref[...], sb_ref[...]
        kt = _dot_nt(wkt_ref[...], hn)
        kt = (kt * cost_ref[...] + pltpu.roll(kt, 8, axis=0) * sat_ref[...]
              + pltpu.roll(kt, KV_WIDTH - 8, axis=0) * sbt_ref[...])
        vt = _dot_nt(wvt_ref[...], hn)
        knew[...] = kt
        vnew[...] = vt
        knew_b[...] = kt.astype(BF16)
        vnew_b[...] = vt.astype(BF16)
        lo = _lane_is_lo((ROWS, LANES))
        for j in range(ATT_WIDTH // LANES):
            qj = _rope(q[:, j * LANES:(j + 1) * LANES], cos, sa, sb)
            qr = pltpu.roll(qj, HALF, axis=1)
            z = jnp.zeros_like(qj)
            if j < 2:
                qp[2 * j] = jnp.where(lo, qj, z)
                qp[2 * j + 1] = jnp.where(lo, qr, z)
            else:
                qp[2 * j] = jnp.where(lo, z, qr)
                qp[2 * j + 1] = jnp.where(lo, z, qj)

        _ssm_input(u, wbre_ref, wbim_ref, s_ref)
        for l in range(N_LB):
            cols = slice(l * LANES, (l + 1) * LANES)
            ar, ai = ar_ref[:, cols], ai_ref[:, cols]
            hr, hi = stre_ref[:, cols], stim_ref[:, cols]
            for t in range(n_tok):
                r_re = slice(t * n_seq, (t + 1) * n_seq)
                r_im = slice(ROWS + t * n_seq, ROWS + (t + 1) * n_seq)
                hr, hi = (ar * hr - ai * hi + s_ref[l, r_re, :],
                          ar * hi + ai * hr + s_ref[l, r_im, :])
                s_ref[l, r_re, :] = hr
                s_ref[l, r_im, :] = hi
            sre_ref[:, cols] = hr
            sim_ref[:, cols] = hi
        ssm_o = _ssm_output(s_ref, u, wcre_ref, wcim_ref, d_ref, wglu_ref)
        ssm_n[...] = _dot(permt_ref[...], _rms(ssm_o, gssm_ref[...]).astype(BF16)).astype(BF16)

    grp = pl.ds(pl.multiple_of(g * grp_rows, grp_rows), grp_rows)
    qb = jnp.concatenate([qp[h, grp, :] for h in range(N_HEADS)], axis=0).astype(BF16)
    kc_f = ck_ref[...]
    vc_f = cv_ref[...]
    seq_rows = [slice(b * KV_WIDTH, (b + 1) * KV_WIDTH) for b in range(GROUP)]
    kcat = jnp.concatenate([kc_f[r] for r in seq_rows], axis=1).astype(BF16)
    vcat = jnp.concatenate([vc_f[r] for r in seq_rows], axis=1).astype(BF16)
    n_q = N_HEADS * grp_rows
    n_c = GROUP * WINDOW

    r_c = lax.broadcasted_iota(jnp.int32, (n_q, n_c), 0)
    c_c = lax.broadcasted_iota(jnp.int32, (n_q, n_c), 1)
    mask_c = ((_div(c_c, WINDOW) == _mod(_div(r_c, n_tok), GROUP))
              & (_mod(c_c, WINDOW) > _mod(r_c, n_tok)))
    lc = jnp.where(mask_c, _dot(qb, kcat), NEG)
    tile = pl.ds(pl.multiple_of(_div(g * grp_rows, LANES) * LANES, LANES), LANES)
    first = _mod(g * grp_rows, LANES)
    r_n = lax.broadcasted_iota(jnp.int32, (n_q, LANES), 0)
    c_n = lax.broadcasted_iota(jnp.int32, (n_q, LANES), 1) - first
    mask_n = ((c_n >= 0) & (c_n < grp_rows)
              & (_div(c_n, n_tok) == _mod(_div(r_n, n_tok), GROUP))
              & (_mod(c_n, n_tok) <= _mod(r_n, n_tok)))
    ln = jnp.where(mask_n, _dot(qb, knew_b[:, tile]), NEG)

    sink_col = sink_ref[...]
    m = jnp.maximum(jnp.maximum(jnp.max(lc, axis=-1, keepdims=True),
                                jnp.max(ln, axis=-1, keepdims=True)), sink_col)
    ec = jnp.exp(lc - m)
    en = jnp.exp(ln - m)
    den = (jnp.sum(ec, axis=-1, keepdims=True) + jnp.sum(en, axis=-1, keepdims=True)
           + jnp.exp(sink_col - m))
    o = (_dot_nt(ec.astype(BF16), vcat) + _dot_nt(en.astype(BF16), vnew_b[:, tile])) / den

    lo = _lane_is_lo((grp_rows, LANES))
    for j in range(ATT_WIDTH // LANES):
        o_even = o[(2 * j) * grp_rows:(2 * j + 1) * grp_rows]
        o_odd = o[(2 * j + 1) * grp_rows:(2 * j + 2) * grp_rows]
        if j < 2:
            pair = jnp.where(lo, o_even, pltpu.roll(o_odd, HALF, axis=1))
        else:
            pair = jnp.where(lo, pltpu.roll(o_even, HALF, axis=1), o_odd)
        att_ref[grp, j * LANES:(j + 1) * LANES] = pair

    keep = WINDOW - n_tok
    to_tail = _mod(keep - first + LANES, LANES)
    kt_g = pltpu.roll(knew[:, tile], to_tail, axis=1)
    vt_g = pltpu.roll(vnew[:, tile], to_tail, axis=1)
    is_new = lax.broadcasted_iota(jnp.int32, (KV_WIDTH, WINDOW), 1) >= keep
    for b, rows in enumerate(seq_rows):
        k_b = kt_g if b == 0 else pltpu.roll(kt_g, LANES - n_tok * b, axis=1)
        v_b = vt_g if b == 0 else pltpu.roll(vt_g, LANES - n_tok * b, axis=1)
        kwin_ref[rows, :] = jnp.where(is_new, k_b, pltpu.roll(kc_f[rows], keep, axis=1))
        vwin_ref[rows, :] = jnp.where(is_new, v_b, pltpu.roll(vc_f[rows], keep, axis=1))

    @pl.when(g == last)
    def _merge_out():
        h1_ref[...] = _merge(x_ref[...], att_ref[...], ssm_n[...], gatt_ref, wout_ref, gpost_ref)


def _decode_mixer(x2d, tables, tables_t, sink_col, ck2d, cv2d, st_re, st_im, wkt, wvt, p):
    n_seq = st_re.shape[0]
    n_groups = n_seq // GROUP
    cmap = lambda g: (g, 0)
    n_tok = ROWS // n_seq
    r = jnp.arange(ROWS)
    perm = jax.nn.one_hot((r % n_seq) * n_tok + r // n_seq, ROWS, dtype=BF16)
    head = (x2d, *tables, *tables_t, sink_col, perm, perm.T)
    in_specs = [_const_spec(a.shape) for a in head] + [
        pl.BlockSpec((GROUP * KV_WIDTH, WINDOW), cmap),
        pl.BlockSpec((GROUP * KV_WIDTH, WINDOW), cmap),
    ] + [_const_spec(a.shape) for a in (st_re, st_im, wkt, wvt, *p)]
    out_shape = (
        jax.ShapeDtypeStruct(x2d.shape, F32),
        jax.ShapeDtypeStruct(ck2d.shape, F32),
        jax.ShapeDtypeStruct(cv2d.shape, F32),
        jax.ShapeDtypeStruct(st_re.shape, F32),
        jax.ShapeDtypeStruct(st_im.shape, F32),
    )
    out_specs = (
        _const_spec(x2d.shape),
        pl.BlockSpec((GROUP * KV_WIDTH, WINDOW), cmap),
        pl.BlockSpec((GROUP * KV_WIDTH, WINDOW), cmap),
        _const_spec(st_re.shape), _const_spec(st_im.shape),
    )
    scratch = [
        pltpu.VMEM((N_HEADS, ROWS, LANES), F32),
        pltpu.VMEM((KV_WIDTH, ROWS), F32),
        pltpu.VMEM((KV_WIDTH, ROWS), F32),
        pltpu.VMEM((KV_WIDTH, ROWS), BF16),
        pltpu.VMEM((KV_WIDTH, ROWS), BF16),
        pltpu.VMEM((ROWS, SSM_WIDTH), BF16),
        pltpu.VMEM((ROWS, ATT_WIDTH), F32),
        pltpu.VMEM((N_LB, 2 * ROWS, LANES), F32),
    ]
    return pl.pallas_call(
        _decode_mixer_kernel,
        grid=(n_groups,),
        in_specs=in_specs,
        out_specs=out_specs,
        out_shape=out_shape,
        scratch_shapes=scratch,
        compiler_params=pltpu.CompilerParams(
            dimension_semantics=("arbitrary",), vmem_limit_bytes=VMEM_LIMIT),
        name="decode_mixer",
    )(*head, ck2d, cv2d, st_re, st_im, wkt, wvt, *p)


def _mlp_kernel(xp_ref, xd_ref, gpre_ref, wup_ref, wdn_ref, gpost_ref, op_ref, od_ref):
    i = pl.program_id(0)

    def mlp(x):
        hn = _rms(x, gpre_ref[...]).astype(BF16)
        acc = None
        for c in range(D_FF // FF_CHUNK):
            cols = slice(c * FF_CHUNK, (c + 1) * FF_CHUNK)
            a = jnp.maximum(_dot(hn, wup_ref[:, cols]), 0.0)
            part = _dot((a * a).astype(BF16), wdn_ref[cols, :])
            acc = part if acc is None else acc + part
        return x + _rms(acc, gpost_ref[...])

    @pl.when(i == 0)
    def _decode_rows():
        od_ref[...] = mlp(xd_ref[...])

    @pl.when(i > 0)
    def _prompt_tile():
        op_ref[...] = mlp(xp_ref[...])


def _mlp(xp2d, xd2d, g_pre, w_up, w_down, g_post):
    n_tiles = xp2d.shape[0] // MLP_ROWS
    pmap = lambda i: (jnp.maximum(i - 1, 0), 0)
    return pl.pallas_call(
        _mlp_kernel,
        grid=(n_tiles + 1,),
        in_specs=[pl.BlockSpec((MLP_ROWS, D_MODEL), pmap), _const_spec(xd2d.shape),
                  _const_spec(g_pre.shape), _const_spec(w_up.shape), _const_spec(w_down.shape),
                  _const_spec(g_post.shape)],
        out_specs=(pl.BlockSpec((MLP_ROWS, D_MODEL), pmap), _const_spec(xd2d.shape)),
        out_shape=(jax.ShapeDtypeStruct(xp2d.shape, F32), jax.ShapeDtypeStruct(xd2d.shape, F32)),
        compiler_params=pltpu.CompilerParams(
            dimension_semantics=("arbitrary",), vmem_limit_bytes=VMEM_LIMIT),
        name="mlp",
    )(xp2d, xd2d, g_pre, w_up, w_down, g_post)


def _zoh(a_re, a_im, log_dt, b_re, b_im):
    dt = jnp.exp(log_dt)[:, None]
    mag = jnp.exp(a_re * dt)
    abar_re, abar_im = mag * jnp.cos(a_im * dt), mag * jnp.sin(a_im * dt)
    nr, ni = abar_re - 1.0, abar_im
    den = a_re * a_re + a_im * a_im
    coef_re = (nr * a_re + ni * a_im) / den
    coef_im = (ni * a_re - nr * a_im) / den
    bbar_re = coef_re[..., None] * b_re - coef_im[..., None] * b_im
    bbar_im = coef_re[..., None] * b_im + coef_im[..., None] * b_re
    return abar_re, abar_im, bbar_re, bbar_im


def _same_group_mask():
    row_g = jnp.arange(SSM_TILE)[:, None] // SSM_GROUP
    col_g = jnp.arange(STATE_TILE)[None, :] // SSM_STATE
    return row_g == col_g


def _block_diag_in(bbar):
    w = jnp.swapaxes(bbar, 1, 2).reshape(N_SSM_TILES, SSM_TILE, SSM_STATE)
    w = jnp.tile(w, (1, 1, SSM_TILE // SSM_GROUP))
    return jnp.where(_same_group_mask()[None], w, 0.0).astype(BF16)


def _block_diag_out(c):
    w = jnp.swapaxes(c, 1, 2).reshape(N_SSM_TILES, STATE_TILE, SSM_GROUP)
    w = jnp.tile(w, (1, 1, SSM_TILE // SSM_GROUP))
    return jnp.where(_same_group_mask().T[None], w, 0.0).astype(BF16)


def _rope_tables(pos):
    half = ROPE_DIM // 2
    inv = ROPE_THETA ** (-jnp.arange(half, dtype=F32) / half)
    ang = pos.astype(F32)[:, None] * inv[None, :]
    cos, sin = jnp.cos(ang), jnp.sin(ang)
    d = jnp.arange(LANES) % HEAD_DIM
    f = d % half
    cos_t = jnp.where(d[None, :] < ROPE_DIM, cos[:, f], 1.0)
    sa_t = jnp.where((d[None, :] >= half) & (d[None, :] < ROPE_DIM), sin[:, f], 0.0)
    sb_t = jnp.where(d[None, :] < half, -sin[:, f], 0.0)
    return cos_t, sa_t, sb_t


def kernel(x_prompt, x_sample, cache_k_win, cache_v_win, state_ssm_re, state_ssm_im, meta_tokens, norm_mix_pre, w_in, attn_sinks, ssm_a_re, ssm_a_im, ssm_log_dt, ssm_b_re, ssm_b_im, ssm_c_re, ssm_c_im, ssm_d, w_glu, norm_att_out, norm_ssm_out, w_out, norm_mix_post, norm_mlp_pre, w_up, w_down, norm_mlp_post):
    depth = w_in.shape[0]
    assert depth == 1
    l = 0
    nb, seq, _ = x_prompt.shape
    n_seq, n_tok, _ = x_sample.shape
    assert n_seq * n_tok == ROWS and nb * BLOCK == ROWS and seq % BLOCK == 0

    abar_re, abar_im, bbar_re, bbar_im = _zoh(
        ssm_a_re[l], ssm_a_im[l], ssm_log_dt[l], ssm_b_re[l], ssm_b_im[l])
    row = lambda a: a.reshape(1, -1)
    q_scale = jnp.where(jnp.arange(w_in.shape[-1]) < ATT_WIDTH, HEAD_DIM ** -0.5, 1.0)
    params = (
        row(norm_mix_pre[l]), (w_in[l] * q_scale).astype(BF16),
        row(abar_re), row(abar_im),
        _block_diag_in(bbar_re), _block_diag_in(bbar_im),
        _block_diag_out(ssm_c_re[l]), _block_diag_out(-ssm_c_im[l]),
        row(ssm_d[l]), w_glu[l].astype(BF16),
        row(norm_att_out[l]), row(norm_ssm_out[l]), w_out[l].astype(BF16), row(norm_mix_post[l]),
    )

    front = BLOCK - N_META
    meta_blk = jnp.concatenate([jnp.zeros((front, D_MODEL), F32), meta_tokens], axis=0)
    cos, sa, sb = _rope_tables(jnp.arange(seq + BLOCK, dtype=jnp.int32) - front)
    h1, k_last, v_last, p_re, p_im, w_up_b, w_down_b = _prompt_mixer(
        x_prompt, meta_blk, cos, sa, sb, attn_sinks[l], params, w_up[l], w_down[l])
    mlp_params = (row(norm_mlp_pre[l]), w_up_b, w_down_b, row(norm_mlp_post[l]))

    pos_s = PAST_LEN + jnp.arange(n_tok, dtype=jnp.int32)
    tabs = _rope_tables(pos_s)
    tables = tuple(jnp.tile(t, (n_seq, 1)) for t in tabs)
    tables_t = tuple(jnp.tile(t.T, (1, n_seq)) for t in tabs)
    sink_col = jnp.repeat(attn_sinks[l], n_tok * GROUP).reshape(-1, 1)
    to_t = lambda a: jnp.transpose(a, (0, 2, 3, 1)).reshape(-1, WINDOW)
    from_t = lambda a, n: jnp.transpose(
        a.reshape(n, N_KV_HEADS, HEAD_DIM, WINDOW), (0, 3, 1, 2))[None]
    w_kv = w_in[l][:, ATT_WIDTH:ATT_WIDTH + 2 * KV_WIDTH].astype(BF16)
    h1s, kwin, vwin, s_re, s_im = _decode_mixer(
        x_sample.reshape(ROWS, D_MODEL), tables, tables_t, sink_col, to_t(cache_k_win[l]), to_t(cache_v_win[l]),
        state_ssm_re[l].reshape(n_seq, N_STATE), state_ssm_im[l].reshape(n_seq, N_STATE),
        w_kv[:, 0:KV_WIDTH].T, w_kv[:, KV_WIDTH:].T, params)
    y_prompt, ys = _mlp(h1.reshape(nb * seq, D_MODEL), h1s, *mlp_params)
    y_prompt = y_prompt.reshape(nb, seq, D_MODEL)
    y_sample = ys.reshape(n_seq, n_tok, D_MODEL)

    win = from_t
    st = lambda a, n: a.reshape(1, n, SSM_GROUPS, SSM_STATE)
    return (y_prompt, y_sample,
            win(k_last, nb), win(v_last, nb), st(p_re, nb), st(p_im, nb),
            win(kwin, n_seq), win(vwin, n_seq), st(s_re, n_seq), st(s_im, n_seq))
```

```python
import math

import jax
import jax.numpy as jnp
import numpy as np
from jax import lax
from jax.experimental import pallas as pl
from jax.experimental.pallas import tpu as pltpu

F32 = jnp.float32
BF16 = jnp.bfloat16

N_META = 16
HEAD_DIM = 64
N_HEADS = 8
N_KV_HEADS = 2
WINDOW = 128
BLOCK = 128
ROPE_DIM = 16
ROPE_THETA = 500000.0
SSM_GROUP = 16
SSM_GROUPS = 32
SSM_STATE = 64
PAST_LEN = 8192
EPS = 1e-6
NEG = -1e30

D_MODEL = 1024
ATT_WIDTH = 512
KV_WIDTH = 128
SSM_WIDTH = 512
N_STATE = SSM_GROUPS * SSM_STATE
D_FF = 4096

ROWS = 512
MLP_ROWS = 1024
LANES = 128
HALF = LANES // 2
SSM_TILE = 256
STATE_TILE = (SSM_TILE // SSM_GROUP) * SSM_STATE
N_SSM_TILES = SSM_WIDTH // SSM_TILE
LB_PER_TILE = STATE_TILE // LANES
N_LB = N_STATE // LANES
FF_CHUNK = 1024
VMEM_LIMIT = 56 * 1024 * 1024


def _dot(a, b):
    return jnp.dot(a, b, preferred_element_type=F32)


def _dot_nt(a, b):
    return lax.dot_general(a, b, (((1,), (1,)), ((), ())), preferred_element_type=F32)


def _rms(x, g):
    return x * lax.rsqrt(jnp.mean(x * x, axis=-1, keepdims=True) + EPS) * g


def _rope(x, cos, sa, sb):
    return x * cos + pltpu.roll(x, 8, axis=1) * sa + pltpu.roll(x, LANES - 8, axis=1) * sb


def _lane_is_lo(shape):
    return lax.broadcasted_iota(jnp.int32, shape, 1) < HALF


def _project_qkv(hn, w_in_ref):
    q = _dot(hn, w_in_ref[:, 0:ATT_WIDTH])
    kv = _dot(hn, w_in_ref[:, ATT_WIDTH:ATT_WIDTH + 2 * KV_WIDTH])
    return q, kv[:, 0:KV_WIDTH], kv[:, KV_WIDTH:2 * KV_WIDTH]


def _project_u(hn, w_in_ref):
    return _dot(hn, w_in_ref[:, ATT_WIDTH + 2 * KV_WIDTH:])


def _ssm_input_tile(ub, c, wbre_ref, wbim_ref, s_ref):
    uc = ub[:, c * SSM_TILE:(c + 1) * SSM_TILE]
    bre, bim = _dot(uc, wbre_ref[c]), _dot(uc, wbim_ref[c])
    for l in range(LB_PER_TILE):
        s_ref[c * LB_PER_TILE + l, 0:ROWS, :] = bre[:, l * LANES:(l + 1) * LANES]
        s_ref[c * LB_PER_TILE + l, ROWS:2 * ROWS, :] = bim[:, l * LANES:(l + 1) * LANES]


def _ssm_input(u, wbre_ref, wbim_ref, s_ref):
    ub = u.astype(BF16)
    for c in range(N_SSM_TILES):
        _ssm_input_tile(ub, c, wbre_ref, wbim_ref, s_ref)


def _ssm_readout_tile(s_ref, c, wcre_ref, wcim_ref):
    blocks = range(c * LB_PER_TILE, (c + 1) * LB_PER_TILE)
    hr = jnp.concatenate([s_ref[l, 0:ROWS, :].astype(BF16) for l in blocks], axis=1)
    hi = jnp.concatenate([s_ref[l, ROWS:2 * ROWS, :].astype(BF16) for l in blocks], axis=1)
    return _dot(hr, wcre_ref[c]) + _dot(hi, wcim_ref[c])


def _ssm_output(s_ref, u, wcre_ref, wcim_ref, d_ref, wglu_ref):
    ys = [_ssm_readout_tile(s_ref, c, wcre_ref, wcim_ref) for c in range(N_SSM_TILES)]
    return _ssm_gate(ys, u, d_ref, wglu_ref)


def _ssm_gate(ys, u, d_ref, wglu_ref):
    y = jnp.concatenate(ys, axis=1) + d_ref[...] * u
    z = 0.5 * y * (1.0 + jnp.tanh(math.sqrt(2.0 / math.pi) * (y + 0.044715 * (y * y * y))))
    gate = 1.0 / (1.0 + jnp.exp(-_dot(z.astype(BF16), wglu_ref[...])))
    return z * gate


def _merge(x, att, s, gatt_ref, wout_ref, gpost_ref):
    a = _rms(att, gatt_ref[...]).astype(BF16)
    m = _dot(a, wout_ref[0:ATT_WIDTH, :]) + _dot(s, wout_ref[ATT_WIDTH:, :])
    return x + _rms(m, gpost_ref[...])


def _div(x, k):
    return lax.shift_right_logical(x, int(math.log2(k)))


def _mod(x, k):
    return lax.bitwise_and(x, k - 1)


def _sink_softmax(logits, bias, sink_col):
    lm = logits + bias
    m =jnp.maximum(jnp.max(lm, axis=-1, keepdims=True), sink_col)
    e = jnp.exp(lm - m)
    den = jnp.sum(e, axis=-1, keepdims=True) + jnp.exp(sink_col - m)
    return e, den


def _prompt_mixer_kernel(
        sink_ref, x_ref, meta_ref, cos_ref, sa_ref, sb_ref, bias_ref, perm_ref, permt_ref,
        gpre_ref, win_ref,
        ar_ref, ai_ref, wbre_ref, wbim_ref, wcre_ref, wcim_ref, d_ref, wglu_ref,
        gatt_ref, gssm_ref, wout_ref, gpost_ref, wupf_ref, wdnf_ref,
        h1_ref, klast_ref, vlast_ref, sre_ref, sim_ref, wupb_ref, wdnb_ref,
        kbuf, vbuf, kcur, vcur, s_ref, hstate, att_ref):
    n = pl.program_id(0)
    wupb_ref[...] = wupf_ref[...].astype(BF16)
    wdnb_ref[...] = wdnf_ref[...].astype(BF16)
    nb = x_ref.shape[0]
    assert 2 * nb == 8
    last = pl.num_programs(0) - 1

    @pl.when(n == 0)
    def _init():
        kbuf[...] = jnp.zeros_like(kbuf)
        vbuf[...] = jnp.zeros_like(vbuf)
        hstate[...] = jnp.zeros_like(hstate)

    meta = meta_ref[...]
    x = x_ref[...].reshape(ROWS, D_MODEL)
    x = jnp.where(n == 0, jnp.concatenate([meta] * nb, axis=0), x)

    hn = _rms(x, gpre_ref[...]).astype(BF16)

    cos, sa, sb = cos_ref[...], sa_ref[...], sb_ref[...]
    lo = _lane_is_lo((BLOCK, LANES))

    u = _project_u(_dot(perm_ref[...], hn).astype(BF16), win_ref)
    is_re =lax.broadcasted_iota(jnp.int32, (2 * nb, LANES), 0) < nb
    a1, a2 = [], []
    for l in range(N_LB):
        cols = slice(l * LANES, (l + 1) * LANES)
        ai = jnp.broadcast_to(ai_ref[:, cols], (2 * nb, LANES))
        a1.append(jnp.broadcast_to(ar_ref[:, cols], (2 * nb, LANES)))
        a2.append(jnp.where(is_re, -ai, ai))

    def swap(t):
        return pltpu.roll(t, nb, axis=0)

    hfin = []

    def chains(blocks):
        for l in blocks:
            h = hstate[l]
            g = swap(h)
            a2n = -a2[l]
            for i in range(BLOCK // 2):
                r_re = slice(i * 2 * nb, (i + 1) * 2 * nb)
                r_im = slice(ROWS + i * 2 * nb, ROWS + (i + 1) * 2 * nb)
                re, im_s = s_ref[l, r_re, :], swap(s_ref[l, r_im, :])
                h0 = a1[l] * h + a2[l] * g + jnp.where(is_re, re, im_s)
                g0 = swap(h0)
                g = a1[l] * g0 + a2n * h0 + jnp.where(is_re, im_s, re)
                h = swap(g)
                s_ref[l, r_re, :] = jnp.where(is_re, h0, g)
                s_ref[l, r_im, :] = jnp.where(is_re, g0, h)
            hstate[l] = h
            hfin.append(h)

    mask = bias_ref[0]
    cur = pl.ds(pl.multiple_of(_mod(n, 2) * BLOCK, BLOCK), BLOCK)
    top = lax.broadcasted_iota(jnp.int32, (2 * BLOCK, 1), 0) < BLOCK

    def variants(t):
        tr = pltpu.roll(t, HALF, axis=1)
        z = jnp.zeros_like(t)
        return (jnp.where(lo, t, z), jnp.where(lo, z, tr),
                jnp.where(lo, tr, z), jnp.where(lo, z, t))

    def attend(b):
        rows = slice(b * BLOCK, (b + 1) * BLOCK)
        kb = _rope(k[rows], cos, sa, sb)
        vb = v[rows]
        for i, (kv_, vv_) in enumerate(zip(variants(kb), variants(vb))):
            kbuf[b, i, cur, :] = kv_.astype(BF16)
            vbuf[b, i, cur, :] = vv_.astype(BF16)

        kcur[b] = kb
        vcur[b] = vb

        qs = []
        for j in range(ATT_WIDTH // LANES):
            qs.append(_rope(q[rows, j * LANES:(j + 1) * LANES], cos, sa, sb).astype(BF16))
        for g in range(N_KV_HEADS):
            qst = jnp.concatenate([qs[2 * g], qs[2 * g + 1]], axis=0)
            o = None
            for half in range(2):
                var = 2 * g + half
                h_top, h_bot = 4 * g + half, 4 * g + 2 + half
                sink_col = jnp.where(top, sink_ref[h_top], sink_ref[h_bot])
                e, den = _sink_softmax(_dot_nt(qst, kbuf[b, var]), mask, sink_col)
                part = _dot(e.astype(BF16), vbuf[b, var]) / den
                o = part if o is None else o + part
            att_ref[rows, (2 * g) * LANES:(2 * g + 1) * LANES] = o[0:BLOCK]
            att_ref[rows, (2 * g + 1) * LANES:(2 * g + 2) * LANES] = o[BLOCK:2 * BLOCK]

    _ssm_input(u, wbre_ref, wbim_ref, s_ref)
    chains(range(N_LB))
    q, k, v = _project_qkv(hn, win_ref)
    ssm_o = _ssm_output(s_ref, u, wcre_ref, wcim_ref, d_ref, wglu_ref)
    ssm_n = _dot(permt_ref[...], _rms(ssm_o, gssm_ref[...]).astype(BF16)).astype(BF16)
    for b in range(nb):
        attend(b)

    h1 = _merge(x, att_ref[...], ssm_n, gatt_ref, wout_ref, gpost_ref)
    h1_ref[...] = h1.reshape(h1_ref.shape)

    @pl.when(n == last)
    def _emit_state():
        for b in range(nb):
            klast_ref[b] = kcur[b].T
            vlast_ref[b] = vcur[b].T
        for l in range(N_LB):
            sre_ref[:, l * LANES:(l + 1) * LANES] = hfin[l][0:nb]
            sim_ref[:, l * LANES:(l + 1) * LANES] = hfin[l][nb:2 * nb]


def _window_bias():
    r = (np.arange(2 * BLOCK) % BLOCK)[:, None]
    phys = np.arange(2 * BLOCK)[None, :]
    tables = []
    for c_min in (2 * BLOCK - N_META, BLOCK - N_META, 0):
        for parity in (0, 1):
            c = phys if parity == 1 else (phys + BLOCK) % (2 * BLOCK)
            ok = (c > r) & (c <= r + WINDOW) & (c >= c_min)
            tables.append(np.where(ok, 0.0, NEG))
    return jnp.asarray(np.stack(tables), dtype=F32)


def _const_spec(shape):
    zeros = (0,) * len(shape)
    return pl.BlockSpec(shape, lambda *_: zeros)


def _prompt_mixer(x_prompt, meta_blk, cos, sa, sb, sinks, p, w_up, w_down):
    nb, seq, _ = x_prompt.shape
    n_blocks = seq // BLOCK + 1
    r = jnp.arange(ROWS)
    perm = jax.nn.one_hot((r % nb) * BLOCK + r // nb, ROWS, dtype=BF16)
    xmap = lambda n: (0, jnp.maximum(n - 1, 0), 0)
    tmap = lambda n: (n, 0)
    bmap = lambda n: (2 * jnp.minimum(n, 2) + n % 2, 0, 0)
    n_slabs = D_FF // LANES
    assert n_slabs <= n_blocks
    upmap = lambda n: (0, jnp.minimum(n, n_slabs - 1))
    dnmap = lambda n: (jnp.minimum(n, n_slabs - 1), 0)
    in_specs = [
        pl.BlockSpec(memory_space=pltpu.SMEM),
        pl.BlockSpec((nb, BLOCK, D_MODEL), xmap),
        _const_spec((BLOCK, D_MODEL)),
        pl.BlockSpec((BLOCK, LANES), tmap),
        pl.BlockSpec((BLOCK, LANES), tmap),
        pl.BlockSpec((BLOCK, LANES), tmap),
        pl.BlockSpec((1, 2 * BLOCK, 2 * BLOCK), bmap),
        _const_spec((ROWS, ROWS)),
        _const_spec((ROWS, ROWS)),
    ] + [_const_spec(a.shape) for a in p] + [
        pl.BlockSpec((D_MODEL, LANES), upmap),
        pl.BlockSpec((LANES, D_MODEL), dnmap),
    ]
    out_shape = (
        jax.ShapeDtypeStruct((nb, seq, D_MODEL), F32),
        jax.ShapeDtypeStruct((nb, BLOCK, KV_WIDTH), F32),
        jax.ShapeDtypeStruct((nb, BLOCK, KV_WIDTH), F32),
        jax.ShapeDtypeStruct((nb, N_STATE), F32),
        jax.ShapeDtypeStruct((nb, N_STATE), F32),
        jax.ShapeDtypeStruct(w_up.shape, BF16),
        jax.ShapeDtypeStruct(w_down.shape, BF16),
    )
    out_specs = (
        pl.BlockSpec((nb, BLOCK, D_MODEL), xmap),
        _const_spec((nb, BLOCK, KV_WIDTH)),
        _const_spec((nb, BLOCK, KV_WIDTH)),
        _const_spec((nb, N_STATE)),
        _const_spec((nb, N_STATE)),
        pl.BlockSpec((D_MODEL, LANES), upmap),
        pl.BlockSpec((LANES, D_MODEL), dnmap),
    )
    scratch = [
        pltpu.VMEM((nb, 4, 2 * BLOCK, LANES), BF16),
        pltpu.VMEM((nb, 4, 2 * BLOCK, LANES), BF16),
        pltpu.VMEM((nb, BLOCK, KV_WIDTH), F32),
        pltpu.VMEM((nb, BLOCK, KV_WIDTH), F32),
        pltpu.VMEM((N_LB, 2 * ROWS, LANES), F32),
        pltpu.VMEM((N_LB, 2 * nb, LANES), F32),
        pltpu.VMEM((ROWS, ATT_WIDTH), F32),
    ]
    return pl.pallas_call(
        _prompt_mixer_kernel,
        grid=(n_blocks,),
        in_specs=in_specs,
        out_specs=out_specs,
        out_shape=out_shape,
        scratch_shapes=scratch,
        compiler_params=pltpu.CompilerParams(
            dimension_semantics=("arbitrary",), vmem_limit_bytes=VMEM_LIMIT),
        name="prompt_mixer",
    )(sinks, x_prompt, meta_blk, cos, sa, sb, _window_bias(), perm, perm.T, *p, w_up, w_down)


GROUP = 8


def _decode_mixer_kernel(
        x_ref, cos_ref, sa_ref, sb_ref, cost_ref, sat_ref, sbt_ref, sink_ref, perm_ref, permt_ref,
        ck_ref, cv_ref, stre_ref, stim_ref, wkt_ref, wvt_ref,
        gpre_ref, win_ref, ar_ref, ai_ref, wbre_ref, wbim_ref, wcre_ref, wcim_ref,
        d_ref, wglu_ref, gatt_ref, gssm_ref, wout_ref, gpost_ref,
        h1_ref, kwin_ref, vwin_ref, sre_ref, sim_ref,
        qp, knew, vnew, knew_b, vnew_b, ssm_n, att_ref, s_ref):
    g = pl.program_id(0)
    n_seq = stre_ref.shape[0]
    n_tok = ROWS // n_seq
    grp_rows = GROUP * n_tok
    last = pl.num_programs(0) - 1

    @pl.when(g == 0)
    def _project_and_ssm():
        hn = _rms(x_ref[...], gpre_ref[...]).astype(BF16)
        q = _dot(hn, win_ref[:, 0:ATT_WIDTH])
        u = _project_u(_dot(perm_ref[...], hn).astype(BF16), win_ref)
        cos, sa, sb = cos_ref[...], sa_ref[...], sb_ref[...]
        kt = _dot_nt(wkt_ref[...], hn)
        kt = (kt * cost_ref[...] + pltpu.roll(kt, 8, axis=0) * sat_ref[...]
              + pltpu.roll(kt, KV_WIDTH - 8, axis=0) * sbt_ref[...])
        vt = _dot_nt(wvt_ref[...], hn)
        knew[...] = kt
        vnew[...] = vt
        knew_b[...] = kt.astype(BF16)
        vnew_b[...] = vt.astype(BF16)
        lo = _lane_is_lo((ROWS, LANES))
        for j in range(ATT_WIDTH // LANES):
            qj = _rope(q[:, j * LANES:(j + 1) * LANES], cos, sa, sb)
            qr = pltpu.roll(qj, HALF, axis=1)
            z = jnp.zeros_like(qj)
            if j < 2:
                qp[2 * j] = jnp.where(lo, qj, z)
                qp[2 * j + 1] = jnp.where(lo, qr, z)
            else:
                qp[2 * j] = jnp.where(lo, z, qr)
                qp[2 * j + 1] = jnp.where(lo, z, qj)

        _ssm_input(u, wbre_ref, wbim_ref, s_ref)
        for l in range(N_LB):
            cols = slice(l * LANES, (l + 1) * LANES)
            ar, ai = ar_ref[:, cols], ai_ref[:, cols]
            hr, hi = stre_ref[:, cols], stim_ref[:, cols]
            for t in range(n_tok):
                r_re = slice(t * n_seq, (t + 1) * n_seq)
                r_im = slice(ROWS + t * n_seq, ROWS + (t + 1) * n_seq)
                hr, hi = (ar * hr - ai * hi + s_ref[l, r_re, :],
                          ar * hi + ai * hr + s_ref[l, r_im, :])
                s_ref[l, r_re, :] = hr
                s_ref[l, r_im, :] = hi
            sre_ref[:, cols] = hr
            sim_ref[:, cols] = hi
        ssm_o = _ssm_output(s_ref, u, wcre_ref, wcim_ref, d_ref, wglu_ref)
        ssm_n[...] = _dot(permt_ref[...], _rms(ssm_o, gssm_ref[...]).astype(BF16)).astype(BF16)

    grp = pl.ds(pl.multiple_of(g * grp_rows, grp_rows), grp_rows)
    qb = jnp.concatenate([qp[h, grp, :] for h in range(N_HEADS)], axis=0).astype(BF16)
    kc_f = ck_ref[...]
    vc_f = cv_ref[...]
    seq_rows = [slice(b * KV_WIDTH, (b + 1) * KV_WIDTH) for b in range(GROUP)]
    kcat = jnp.concatenate([kc_f[r] for r in seq_rows], axis=1).astype(BF16)
    vcat = jnp.concatenate([vc_f[r] for r in seq_rows], axis=1).astype(BF16)
    n_q = N_HEADS * grp_rows
    n_c = GROUP * WINDOW

    r_c = lax.broadcasted_iota(jnp.int32, (n_q, n_c), 0)
    c_c = lax.broadcasted_iota(jnp.int32, (n_q, n_c), 1)
    mask_c = ((_div(c_c, WINDOW) == _mod(_div(r_c, n_tok), GROUP))
              & (_mod(c_c, WINDOW) > _mod(r_c, n_tok)))
    lc = jnp.where(mask_c, _dot(qb, kcat), NEG)
    tile = pl.ds(pl.multiple_of(_div(g * grp_rows, LANES) * LANES, LANES), LANES)
    first = _mod(g * grp_rows, LANES)
    r_n = lax.broadcasted_iota(jnp.int32, (n_q, LANES), 0)
    c_n = lax.broadcasted_iota(jnp.int32, (n_q, LANES), 1) - first
    mask_n = ((c_n >= 0) & (c_n < grp_rows)
              & (_div(c_n, n_tok) == _mod(_div(r_n, n_tok), GROUP))
              & (_mod(c_n, n_tok) <= _mod(r_n, n_tok)))
    ln = jnp.where(mask_n, _dot(qb, knew_b[:, tile]), NEG)

    sink_col = sink_ref[...]
    m = jnp.maximum(jnp.maximum(jnp.max(lc, axis=-1, keepdims=True),
                                jnp.max(ln, axis=-1, keepdims=True)), sink_col)
    ec = jnp.exp(lc - m)
    en = jnp.exp(ln - m)
    den = (jnp.sum(ec, axis=-1, keepdims=True) + jnp.sum(en, axis=-1, keepdims=True)
           + jnp.exp(sink_col - m))
    o = (_dot_nt(ec.astype(BF16), vcat) + _dot_nt(en.astype(BF16), vnew_b[:, tile])) / den

    lo = _lane_is_lo((grp_rows, LANES))
    for j in range(ATT_WIDTH // LANES):
        o_even = o[(2 * j) * grp_rows:(2 * j + 1) * grp_rows]
        o_odd = o[(2 * j + 1) * grp_rows:(2 * j + 2) * grp_rows]
        if j < 2:
            pair = jnp.where(lo, o_even, pltpu.roll(o_odd, HALF, axis=1))
        else:
            pair = jnp.where(lo, pltpu.roll(o_even, HALF, axis=1), o_odd)
        att_ref[grp, j * LANES:(j + 1) * LANES] = pair

    keep = WINDOW - n_tok
    to_tail = _mod(keep - first + LANES, LANES)
    kt_g = pltpu.roll(knew[:, tile], to_tail, axis=1)
    vt_g = pltpu.roll(vnew[:, tile], to_tail, axis=1)
    is_new = lax.broadcasted_iota(jnp.int32, (KV_WIDTH, WINDOW), 1) >= keep
    for b, rows in enumerate(seq_rows):
        k_b = kt_g if b == 0 else pltpu.roll(kt_g, LANES - n_tok * b, axis=1)
        v_b = vt_g if b == 0 else pltpu.roll(vt_g, LANES - n_tok * b, axis=1)
        kwin_ref[rows, :] = jnp.where(is_new, k_b, pltpu.roll(kc_f[rows], keep, axis=1))
        vwin_ref[rows, :] = jnp.where(is_new, v_b, pltpu.roll(vc_f[rows], keep, axis=1))

    @pl.when(g == last)
    def _merge_out():
        h1_ref[...] = _merge(x_ref[...], att_ref[...], ssm_n[...], gatt_ref, wout_ref, gpost_ref)


def _decode_mixer(x2d, tables, tables_t, sink_col, ck2d, cv2d, st_re, st_im, wkt, wvt, p):
    n_seq = st_re.shape[0]
    n_groups = n_seq // GROUP
    cmap = lambda g: (g, 0)
    n_tok = ROWS // n_seq
    r = jnp.arange(ROWS)
    perm = jax.nn.one_hot((r % n_seq) * n_tok + r // n_seq, ROWS, dtype=BF16)
    head = (x2d, *tables, *tables_t, sink_col, perm, perm.T)
    in_specs = [_const_spec(a.shape) for a in head] + [
        pl.BlockSpec((GROUP * KV_WIDTH, WINDOW), cmap),
        pl.BlockSpec((GROUP * KV_WIDTH, WINDOW), cmap),
    ] + [_const_spec(a.shape) for a in (st_re, st_im, wkt, wvt, *p)]
    out_shape = (
        jax.ShapeDtypeStruct(x2d.shape, F32),
        jax.ShapeDtypeStruct(ck2d.shape, F32),
        jax.ShapeDtypeStruct(cv2d.shape, F32),
        jax.ShapeDtypeStruct(st_re.shape, F32),
        jax.ShapeDtypeStruct(st_im.shape, F32),
    )
    out_specs = (
        _const_spec(x2d.shape),
        pl.BlockSpec((GROUP * KV_WIDTH, WINDOW), cmap),
        pl.BlockSpec((GROUP * KV_WIDTH, WINDOW), cmap),
        _const_spec(st_re.shape), _const_spec(st_im.shape),
    )
    scratch = [
        pltpu.VMEM((N_HEADS, ROWS, LANES), F32),
        pltpu.VMEM((KV_WIDTH, ROWS), F32),
        pltpu.VMEM((KV_WIDTH, ROWS), F32),
        pltpu.VMEM((KV_WIDTH, ROWS), BF16),
        pltpu.VMEM((KV_WIDTH, ROWS), BF16),
        pltpu.VMEM((ROWS, SSM_WIDTH), BF16),
        pltpu.VMEM((ROWS, ATT_WIDTH), F32),
        pltpu.VMEM((N_LB, 2 * ROWS, LANES), F32),
    ]
    return pl.pallas_call(
        _decode_mixer_kernel,
        grid=(n_groups,),
        in_specs=in_specs,
        out_specs=out_specs,
        out_shape=out_shape,
        scratch_shapes=scratch,
        compiler_params=pltpu.CompilerParams(
            dimension_semantics=("arbitrary",), vmem_limit_bytes=VMEM_LIMIT),
        name="decode_mixer",
    )(*head, ck2d, cv2d, st_re, st_im, wkt, wvt, *p)


def _mlp_kernel(xp_ref, xd_ref, gpre_ref, wup_ref, wdn_ref, gpost_ref, op_ref, od_ref):
    i = pl.program_id(0)

    def mlp(x):
        hn = _rms(x, gpre_ref[...]).astype(BF16)
        acc = None
        for c in range(D_FF // FF_CHUNK):
            cols = slice(c * FF_CHUNK, (c + 1) * FF_CHUNK)
            a = jnp.maximum(_dot(hn, wup_ref[:, cols]), 0.0)
            part = _dot((a * a).astype(BF16), wdn_ref[cols, :])
            acc = part if acc is None else acc + part
        return x + _rms(acc, gpost_ref[...])

    @pl.when(i == 0)
    def _decode_rows():
        od_ref[...] = mlp(xd_ref[...])

    @pl.when(i > 0)
    def _prompt_tile():
        op_ref[...] = mlp(xp_ref[...])


def _mlp(xp2d, xd2d, g_pre, w_up, w_down, g_post):
    n_tiles = xp2d.shape[0] // MLP_ROWS
    pmap = lambda i: (jnp.maximum(i - 1, 0), 0)
    return pl.pallas_call(
        _mlp_kernel,
        grid=(n_tiles + 1,),
        in_specs=[pl.BlockSpec((MLP_ROWS, D_MODEL), pmap), _const_spec(xd2d.shape),
                  _const_spec(g_pre.shape), _const_spec(w_up.shape), _const_spec(w_down.shape),
                  _const_spec(g_post.shape)],
        out_specs=(pl.BlockSpec((MLP_ROWS, D_MODEL), pmap), _const_spec(xd2d.shape)),
        out_shape=(jax.ShapeDtypeStruct(xp2d.shape, F32), jax.ShapeDtypeStruct(xd2d.shape, F32)),
        compiler_params=pltpu.CompilerParams(
            dimension_semantics=("arbitrary",), vmem_limit_bytes=VMEM_LIMIT),
        name="mlp",
    )(xp2d, xd2d, g_pre, w_up, w_down, g_post)


def _zoh(a_re, a_im, log_dt, b_re, b_im):
    dt = jnp.exp(log_dt)[:, None]
    mag = jnp.exp(a_re * dt)
    abar_re, abar_im = mag * jnp.cos(a_im * dt), mag * jnp.sin(a_im * dt)
    nr, ni = abar_re - 1.0, abar_im
    den = a_re * a_re + a_im * a_im
    coef_re = (nr * a_re + ni * a_im) / den
    coef_im = (ni * a_re - nr * a_im) / den
    bbar_re = coef_re[..., None] * b_re - coef_im[..., None] * b_im
    bbar_im = coef_re[..., None] * b_im + coef_im[..., None] * b_re
    return abar_re, abar_im, bbar_re, bbar_im


def _same_group_mask():
    row_g = jnp.arange(SSM_TILE)[:, None] // SSM_GROUP
    col_g = jnp.arange(STATE_TILE)[None, :] // SSM_STATE
    return row_g == col_g


def _block_diag_in(bbar):
    w = jnp.swapaxes(bbar, 1, 2).reshape(N_SSM_TILES, SSM_TILE, SSM_STATE)
    w = jnp.tile(w, (1, 1, SSM_TILE // SSM_GROUP))
    return jnp.where(_same_group_mask()[None], w, 0.0).astype(BF16)


def _block_diag_out(c):
    w = jnp.swapaxes(c, 1, 2).reshape(N_SSM_TILES, STATE_TILE, SSM_GROUP)
    w = jnp.tile(w, (1, 1, SSM_TILE // SSM_GROUP))
    return jnp.where(_same_group_mask().T[None], w, 0.0).astype(BF16)


def _rope_tables(pos):
    half = ROPE_DIM // 2
    inv = ROPE_THETA ** (-jnp.arange(half, dtype=F32) / half)
    ang = pos.astype(F32)[:, None] * inv[None, :]
    cos, sin = jnp.cos(ang), jnp.sin(ang)
    d = jnp.arange(LANES) % HEAD_DIM
    f = d % half
    cos_t = jnp.where(d[None, :] < ROPE_DIM, cos[:, f], 1.0)
    sa_t = jnp.where((d[None, :] >= half) & (d[None, :] < ROPE_DIM), sin[:, f], 0.0)
    sb_t = jnp.where(d[None, :] < half, -sin[:, f], 0.0)
    return cos_t, sa_t, sb_t


def kernel(x_prompt, x_sample, cache_k_win, cache_v_win, state_ssm_re, state_ssm_im, meta_tokens, norm_mix_pre, w_in, attn_sinks, ssm_a_re, ssm_a_im, ssm_log_dt, ssm_b_re, ssm_b_im, ssm_c_re, ssm_c_im, ssm_d, w_glu, norm_att_out, norm_ssm_out, w_out, norm_mix_post, norm_mlp_pre, w_up, w_down, norm_mlp_post):
    depth = w_in.shape[0]
    assert depth == 1
    l = 0
    nb, seq, _ = x_prompt.shape
    n_seq, n_tok, _ = x_sample.shape
    assert n_seq * n_tok == ROWS and nb * BLOCK == ROWS and seq % BLOCK == 0

    abar_re, abar_im, bbar_re, bbar_im = _zoh(
        ssm_a_re[l], ssm_a_im[l], ssm_log_dt[l], ssm_b_re[l], ssm_b_im[l])
    row = lambda a: a.reshape(1, -1)
    q_scale = jnp.where(jnp.arange(w_in.shape[-1]) < ATT_WIDTH, HEAD_DIM ** -0.5, 1.0)
    params = (
        row(norm_mix_pre[l]), (w_in[l] * q_scale).astype(BF16),
        row(abar_re), row(abar_im),
        _block_diag_in(bbar_re), _block_diag_in(bbar_im),
        _block_diag_out(ssm_c_re[l]), _block_diag_out(-ssm_c_im[l]),
        row(ssm_d[l]), w_glu[l].astype(BF16),
        row(norm_att_out[l]), row(norm_ssm_out[l]), w_out[l].astype(BF16), row(norm_mix_post[l]),
    )

    front = BLOCK - N_META
    meta_blk = jnp.concatenate([jnp.zeros((front, D_MODEL), F32), meta_tokens], axis=0)
    cos, sa, sb = _rope_tables(jnp.arange(seq + BLOCK, dtype=jnp.int32) - front)
    h1, k_last, v_last, p_re, p_im, w_up_b, w_down_b = _prompt_mixer(
        x_prompt, meta_blk, cos, sa, sb, attn_sinks[l], params, w_up[l], w_down[l])
    mlp_params = (row(norm_mlp_pre[l]), w_up_b, w_down_b, row(norm_mlp_post[l]))

    pos_s = PAST_LEN + jnp.arange(n_tok, dtype=jnp.int32)
    tabs = _rope_tables(pos_s)
    tables = tuple(jnp.tile(t, (n_seq, 1)) for t in tabs)
    tables_t = tuple(jnp.tile(t.T, (1, n_seq)) for t in tabs)
    sink_col = jnp.repeat(attn_sinks[l], n_tok * GROUP).reshape(-1, 1)
    to_t = lambda a: jnp.transpose(a, (0, 2, 3, 1)).reshape(-1, WINDOW)
    from_t = lambda a, n: jnp.transpose(
        a.reshape(n, N_KV_HEADS, HEAD_DIM, WINDOW), (0, 3, 1, 2))[None]
    w_kv = w_in[l][:, ATT_WIDTH:ATT_WIDTH + 2 * KV_WIDTH].astype(BF16)
    h1s, kwin, vwin, s_re, s_im = _decode_mixer(
        x_sample.reshape(ROWS, D_MODEL), tables, tables_t, sink_col, to_t(cache_k_win[l]), to_t(cache_v_win[l]),
        state_ssm_re[l].reshape(n_seq, N_STATE), state_ssm_im[l].reshape(n_seq, N_STATE),
        w_kv[:, 0:KV_WIDTH].T, w_kv[:, KV_WIDTH:].T, params)
    y_prompt, ys = _mlp(h1.reshape(nb * seq, D_MODEL), h1s, *mlp_params)
    y_prompt = y_prompt.reshape(nb, seq, D_MODEL)
    y_sample = ys.reshape(n_seq, n_tok, D_MODEL)

    win = from_t
    st = lambda a, n: a.reshape(1, n, SSM_GROUPS, SSM_STATE)
    return (y_prompt, y_sample,
            win(k_last, nb), win(v_last, nb), st(p_re, nb), st(p_im, nb),
            win(kwin, n_seq), win(vwin, n_seq), st(s_re, n_seq), st(s_im, n_seq))
```

```python
import math

import jax
import jax.numpy as jnp
import numpy as np
from jax import lax
from jax.experimental import pallas as pl
from jax.experimental.pallas import tpu as pltpu

F32 = jnp.float32
BF16 = jnp.bfloat16

N_META = 16
HEAD_DIM = 64
N_HEADS = 8
N_KV_HEADS = 2
WINDOW = 128
BLOCK = 128
ROPE_DIM = 16
ROPE_THETA = 500000.0
SSM_GROUP = 16
SSM_GROUPS = 32
SSM_STATE = 64
PAST_LEN = 8192
EPS = 1e-6
NEG = -1e30

D_MODEL = 1024
ATT_WIDTH = 512
KV_WIDTH = 128
SSM_WIDTH = 512
N_STATE = SSM_GROUPS * SSM_STATE
D_FF = 4096

ROWS = 512
MLP_ROWS = 1024
LANES = 128
HALF = LANES // 2
SSM_TILE = 256
STATE_TILE = (SSM_TILE // SSM_GROUP) * SSM_STATE
N_SSM_TILES = SSM_WIDTH // SSM_TILE
LB_PER_TILE = STATE_TILE // LANES
N_LB = N_STATE // LANES
FF_CHUNK = 1024
VMEM_LIMIT = 56 * 1024 * 1024


def _dot(a, b):
    return jnp.dot(a, b, preferred_element_type=F32)


def _dot_nt(a, b):
    return lax.dot_general(a, b, (((1,), (1,)), ((), ())), preferred_element_type=F32)


def _rms(x, g):
    return x * lax.rsqrt(jnp.mean(x * x, axis=-1, keepdims=True) + EPS) * g


def _rope(x, cos, sa, sb):
    return x * cos + pltpu.roll(x, 8, axis=1) * sa + pltpu.roll(x, LANES - 8, axis=1) * sb


def _lane_is_lo(shape):
    return lax.broadcasted_iota(jnp.int32, shape, 1) < HALF


def _project_qkv(hn, w_in_ref):
    q = _dot(hn, w_in_ref[:, 0:ATT_WIDTH])
    kv = _dot(hn, w_in_ref[:, ATT_WIDTH:ATT_WIDTH + 2 * KV_WIDTH])
    return q, kv[:, 0:KV_WIDTH], kv[:, KV_WIDTH:2 * KV_WIDTH]


def _project_u(hn, w_in_ref):
    return _dot(hn, w_in_ref[:, ATT_WIDTH + 2 * KV_WIDTH:])


def _ssm_input_tile(ub, c, wbre_ref, wbim_ref, s_ref):
    uc = ub[:, c * SSM_TILE:(c + 1) * SSM_TILE]
    bre, bim = _dot(uc, wbre_ref[c]), _dot(uc, wbim_ref[c])
    for l in range(LB_PER_TILE):
        s_ref[c * LB_PER_TILE + l, 0:ROWS, :] = bre[:, l * LANES:(l + 1) * LANES]
        s_ref[c * LB_PER_TILE + l, ROWS:2 * ROWS, :] = bim[:, l * LANES:(l + 1) * LANES]


def _ssm_input(u, wbre_ref, wbim_ref, s_ref):
    ub = u.astype(BF16)
    for c in range(N_SSM_TILES):
        _ssm_input_tile(ub, c, wbre_ref, wbim_ref, s_ref)


def _ssm_readout_tile(s_ref, c, wcre_ref, wcim_ref):
    blocks = range(c * LB_PER_TILE, (c + 1) * LB_PER_TILE)
    hr = jnp.concatenate([s_ref[l, 0:ROWS, :].astype(BF16) for l in blocks], axis=1)
    hi = jnp.concatenate([s_ref[l, ROWS:2 * ROWS, :].astype(BF16) for l in blocks], axis=1)
    return _dot(hr, wcre_ref[c]) + _dot(hi, wcim_ref[c])


def _ssm_output(s_ref, u, wcre_ref, wcim_ref, d_ref, wglu_ref):
    ys = [_ssm_readout_tile(s_ref, c, wcre_ref, wcim_ref) for c in range(N_SSM_TILES)]
    return _ssm_gate(ys, u, d_ref, wglu_ref)


def _ssm_gate(ys, u, d_ref, wglu_ref):
    y = jnp.concatenate(ys, axis=1) + d_ref[...] * u
    z = 0.5 * y * (1.0 + jnp.tanh(math.sqrt(2.0 / math.pi) * (y + 0.044715 * (y * y * y))))
    gate = 1.0 / (1.0 + jnp.exp(-_dot(z.astype(BF16), wglu_ref[...])))
    return z * gate


def _merge(x, att, s, gatt_ref, wout_ref, gpost_ref):
    a = _rms(att, gatt_ref[...]).astype(BF16)
    m = _dot(a, wout_ref[0:ATT_WIDTH, :]) + _dot(s, wout_ref[ATT_WIDTH:, :])
    return x + _rms(m, gpost_ref[...])


def _div(x, k):
    return lax.shift_right_logical(x, int(math.log2(k)))


def _mod(x, k):
    return lax.bitwise_and(x, k - 1)


def _sink_softmax(logits, bias, sink_col):
    lm = logits + bias
    m =jnp.maximum(jnp.max(lm, axis=-1, keepdims=True), sink_col)
    e = jnp.exp(lm - m)
    den = jnp.sum(e, axis=-1, keepdims=True) + jnp.exp(sink_col - m)
    return e, den


def _prompt_mixer_kernel(
        sink_ref, x_ref, meta_ref, cos_ref, sa_ref, sb_ref, bias_ref, perm_ref, permt_ref,
        gpre_ref, win_ref,
        ar_ref, ai_ref, wbre_ref, wbim_ref, wcre_ref, wcim_ref, d_ref, wglu_ref,
        gatt_ref, gssm_ref, wout_ref, gpost_ref, wupf_ref, wdnf_ref,
        h1_ref, klast_ref, vlast_ref, sre_ref, sim_ref, wupb_ref, wdnb_ref,
        kbuf, vbuf, kcur, vcur, s_ref, hstate, att_ref):
    n = pl.program_id(0)
    wupb_ref[...] = wupf_ref[...].astype(BF16)
    wdnb_ref[...] = wdnf_ref[...].astype(BF16)
    nb = x_ref.shape[0]
    assert 2 * nb == 8
    last = pl.num_programs(0) - 1

    @pl.when(n == 0)
    def _init():
        kbuf[...] = jnp.zeros_like(kbuf)
        vbuf[...] = jnp.zeros_like(vbuf)
        hstate[...] = jnp.zeros_like(hstate)

    meta = meta_ref[...]
    x = x_ref[...].reshape(ROWS, D_MODEL)
    x = jnp.where(n == 0, jnp.concatenate([meta] * nb, axis=0), x)

    hn = _rms(x, gpre_ref[...]).astype(BF16)

    cos, sa, sb = cos_ref[...], sa_ref[...], sb_ref[...]
    lo = _lane_is_lo((BLOCK, LANES))

    u = _project_u(_dot(perm_ref[...], hn).astype(BF16), win_ref)
    is_re =lax.broadcasted_iota(jnp.int32, (2 * nb, LANES), 0) < nb
    a1, a2 = [], []
    for l in range(N_LB):
        cols = slice(l * LANES, (l + 1) * LANES)
        ai = jnp.broadcast_to(ai_ref[:, cols], (2 * nb, LANES))
        a1.append(jnp.broadcast_to(ar_ref[:, cols], (2 * nb, LANES)))
        a2.append(jnp.where(is_re, -ai, ai))

    def swap(t):
        return pltpu.roll(t, nb, axis=0)

    hfin = []

    def chains(blocks):
        for l in blocks:
            h = hstate[l]
            g = swap(h)
            a2n = -a2[l]
            for i in range(BLOCK // 2):
                r_re = slice(i * 2 * nb, (i + 1) * 2 * nb)
                r_im = slice(ROWS + i * 2 * nb, ROWS + (i + 1) * 2 * nb)
                re, im_s = s_ref[l, r_re, :], swap(s_ref[l, r_im, :])
                h0 = a1[l] * h + a2[l] * g + jnp.where(is_re, re, im_s)
                g0 = swap(h0)
                g = a1[l] * g0 + a2n * h0 + jnp.where(is_re, im_s, re)
                h = swap(g)
                s_ref[l, r_re, :] = jnp.where(is_re, h0, g)
                s_ref[l, r_im, :] = jnp.where(is_re, g0, h)
            hstate[l] = h
            hfin.append(h)

    mask = bias_ref[0]
    cur = pl.ds(pl.multiple_of(_mod(n, 2) * BLOCK, BLOCK), BLOCK)
    top = lax.broadcasted_iota(jnp.int32, (2 * BLOCK, 1), 0) < BLOCK

    def variants(t):
        tr = pltpu.roll(t, HALF, axis=1)
        z = jnp.zeros_like(t)
        return (jnp.where(lo, t, z), jnp.where(lo, z, tr),
                jnp.where(lo, tr, z), jnp.where(lo, z, t))

    def attend(b):
        rows = slice(b * BLOCK, (b + 1) * BLOCK)
        kb = _rope(k[rows], cos, sa, sb)
        vb = v[rows]
        for i, (kv_, vv_) in enumerate(zip(variants(kb), variants(vb))):
            kbuf[b, i, cur, :] = kv_.astype(BF16)
            vbuf[b, i, cur, :] = vv_.astype(BF16)

        kcur[b] = kb
        vcur[b] = vb

        qs = []
        for j in range(ATT_WIDTH // LANES):
            qs.append(_rope(q[rows, j * LANES:(j + 1) * LANES], cos, sa, sb).astype(BF16))
        for g in range(N_KV_HEADS):
            qst = jnp.concatenate([qs[2 * g], qs[2 * g + 1]], axis=0)
            o = None
            for half in range(2):
                var = 2 * g + half
                h_top, h_bot = 4 * g + half, 4 * g + 2 + half
                sink_col = jnp.where(top, sink_ref[h_top], sink_ref[h_bot])
                e, den = _sink_softmax(_dot_nt(qst, kbuf[b, var]), mask, sink_col)
                part = _dot(e.astype(BF16), vbuf[b, var]) / den
                o = part if o is None else o + part
            att_ref[rows, (2 * g) * LANES:(2 * g + 1) * LANES] = o[0:BLOCK]
            att_ref[rows, (2 * g + 1) * LANES:(2 * g + 2) * LANES] = o[BLOCK:2 * BLOCK]

    _ssm_input(u, wbre_ref, wbim_ref, s_ref)
    chains(range(N_LB))
    q, k, v = _project_qkv(hn, win_ref)
    ssm_o = _ssm_output(s_ref, u, wcre_ref, wcim_ref, d_ref, wglu_ref)
    ssm_n = _dot(permt_ref[...], _rms(ssm_o, gssm_ref[...]).astype(BF16)).astype(BF16)
    for b in range(nb):
        attend(b)

    h1 = _merge(x, att_ref[...], ssm_n, gatt_ref, wout_ref, gpost_ref)
    h1_ref[...] = h1.reshape(h1_ref.shape)

    @pl.when(n == last)
    def _emit_state():
        for b in range(nb):
            klast_ref[b] = kcur[b].T
            vlast_ref[b] = vcur[b].T
        for l in range(N_LB):
            sre_ref[:, l * LANES:(l + 1) * LANES] = hfin[l][0:nb]
            sim_ref[:, l * LANES:(l + 1) * LANES] = hfin[l][nb:2 * nb]


def _window_bias():
    r = (np.arange(2 * BLOCK) % BLOCK)[:, None]
    phys = np.arange(2 * BLOCK)[None, :]
    tables = []
    for c_min in (2 * BLOCK - N_META, BLOCK - N_META, 0):
        for parity in (0, 1):
            c = phys if parity == 1 else (phys + BLOCK) % (2 * BLOCK)
            ok = (c > r) & (c <= r + WINDOW) & (c >= c_min)
            tables.append(np.where(ok, 0.0, NEG))
    return jnp.asarray(np.stack(tables), dtype=F32)


def _const_spec(shape):
    zeros = (0,) * len(shape)
    return pl.BlockSpec(shape, lambda *_: zeros)


def _prompt_mixer(x_prompt, meta_blk, cos, sa, sb, sinks, p, w_up, w_down):
    nb, seq, _ = x_prompt.shape
    n_blocks = seq // BLOCK + 1
    perm = _row_permutation(nb, BLOCK)
    xmap =lambda n: (0, jnp.maximum(n - 1, 0), 0)
    tmap = lambda n: (n, 0)
    bmap = lambda n: (2 * jnp.minimum(n, 2) + n % 2, 0, 0)
    n_slabs = D_FF // LANES
    assert n_slabs <= n_blocks
    upmap = lambda n: (0, jnp.minimum(n, n_slabs - 1))
    dnmap = lambda n: (jnp.minimum(n, n_slabs - 1), 0)
    in_specs = [
        pl.BlockSpec(memory_space=pltpu.SMEM),
        pl.BlockSpec((nb, BLOCK, D_MODEL), xmap),
        _const_spec((BLOCK, D_MODEL)),
        pl.BlockSpec((BLOCK, LANES), tmap),
        pl.BlockSpec((BLOCK, LANES), tmap),
        pl.BlockSpec((BLOCK, LANES), tmap),
        pl.BlockSpec((1, 2 * BLOCK, 2 * BLOCK), bmap),
        _const_spec((ROWS, ROWS)),
        _const_spec((ROWS, ROWS)),
    ] + [_const_spec(a.shape) for a in p] + [
        pl.BlockSpec((D_MODEL, LANES), upmap),
        pl.BlockSpec((LANES, D_MODEL), dnmap),
    ]
    out_shape = (
        jax.ShapeDtypeStruct((nb, seq, D_MODEL), F32),
        jax.ShapeDtypeStruct((nb, BLOCK, KV_WIDTH), F32),
        jax.ShapeDtypeStruct((nb, BLOCK, KV_WIDTH), F32),
        jax.ShapeDtypeStruct((nb, N_STATE), F32),
        jax.ShapeDtypeStruct((nb, N_STATE), F32),
        jax.ShapeDtypeStruct(w_up.shape, BF16),
        jax.ShapeDtypeStruct(w_down.shape, BF16),
    )
    out_specs = (
        pl.BlockSpec((nb, BLOCK, D_MODEL), xmap),
        _const_spec((nb, BLOCK, KV_WIDTH)),
        _const_spec((nb, BLOCK, KV_WIDTH)),
        _const_spec((nb, N_STATE)),
        _const_spec((nb, N_STATE)),
        pl.BlockSpec((D_MODEL, LANES), upmap),
        pl.BlockSpec((LANES, D_MODEL), dnmap),
    )
    scratch = [
        pltpu.VMEM((nb, 4, 2 * BLOCK, LANES), BF16),
        pltpu.VMEM((nb, 4, 2 * BLOCK, LANES), BF16),
        pltpu.VMEM((nb, BLOCK, KV_WIDTH), F32),
        pltpu.VMEM((nb, BLOCK, KV_WIDTH), F32),
        pltpu.VMEM((N_LB, 2 * ROWS, LANES), F32),
        pltpu.VMEM((N_LB, 2 * nb, LANES), F32),
        pltpu.VMEM((ROWS, ATT_WIDTH), F32),
    ]
    return pl.pallas_call(
        _prompt_mixer_kernel,
        grid=(n_blocks,),
        in_specs=in_specs,
        out_specs=out_specs,
        out_shape=out_shape,
        scratch_shapes=scratch,
        compiler_params=pltpu.CompilerParams(
            dimension_semantics=("arbitrary",), vmem_limit_bytes=VMEM_LIMIT),
        name="prompt_mixer",
    )(sinks, x_prompt, meta_blk, cos, sa, sb, _window_bias(), perm, perm.T, *p, w_up, w_down)


GROUP = 8


def _decode_mixer_kernel(
        x_ref, cos_ref, sa_ref, sb_ref, cost_ref, sat_ref, sbt_ref, sink_ref, perm_ref, permt_ref,
        ck_ref, cv_ref, stre_ref, stim_ref, wkt_ref, wvt_ref,
        gpre_ref, win_ref, ar_ref, ai_ref, wbre_ref, wbim_ref, wcre_ref, wcim_ref,
        d_ref, wglu_ref, gatt_ref, gssm_ref, wout_ref, gpost_ref,
        h1_ref, kwin_ref, vwin_ref, sre_ref, sim_ref,
        qp, knew, vnew, knew_b, vnew_b, ssm_n, att_ref, s_ref):
    g = pl.program_id(0)
    n_seq = stre_ref.shape[0]
    n_tok = ROWS // n_seq
    grp_rows = GROUP * n_tok
    last = pl.num_programs(0) - 1

    @pl.when(g == 0)
    def _project_and_ssm():
        hn = _rms(x_ref[...], gpre_ref[...]).astype(BF16)
        q = _dot(hn, win_ref[:, 0:ATT_WIDTH])
        u = _project_u(_dot(perm_ref[...], hn).astype(BF16), win_ref)
        cos, sa, sb = cos_ref[...], sa_ref[...], sb_ref[...]
        kt = _dot_nt(wkt_ref[...], hn)
        kt = (kt * cost_ref[...] + pltpu.roll(kt, 8, axis=0) * sat_ref[...]
              + pltpu.roll(kt, KV_WIDTH - 8, axis=0) * sbt_ref[...])
        vt = _dot_nt(wvt_ref[...], hn)
        knew[...] = kt
        vnew[...] = vt
        knew_b[...] = kt.astype(BF16)
        vnew_b[...] = vt.astype(BF16)
        lo = _lane_is_lo((ROWS, LANES))
        for j in range(ATT_WIDTH // LANES):
            qj = _rope(q[:, j * LANES:(j + 1) * LANES], cos, sa, sb)
            qr = pltpu.roll(qj, HALF, axis=1)
            z = jnp.zeros_like(qj)
            if j < 2:
                qp[2 * j] = jnp.where(lo, qj, z)
                qp[2 * j + 1] = jnp.where(lo, qr, z)
            else:
                qp[2 * j] = jnp.where(lo, z, qr)
                qp[2 * j + 1] = jnp.where(lo, z, qj)

        _ssm_input(u, wbre_ref, wbim_ref, s_ref)
        for l in range(N_LB):
            cols = slice(l * LANES, (l + 1) * LANES)
            ar, ai = ar_ref[:, cols], ai_ref[:, cols]
            hr, hi = stre_ref[:, cols], stim_ref[:, cols]
            for t in range(n_tok):
                r_re = slice(t * n_seq, (t + 1) * n_seq)
                r_im = slice(ROWS + t * n_seq, ROWS + (t + 1) * n_seq)
                hr, hi = (ar * hr - ai * hi + s_ref[l, r_re, :],
                          ar * hi + ai * hr + s_ref[l, r_im, :])
                s_ref[l, r_re, :] = hr
                s_ref[l, r_im, :] = hi
            sre_ref[:, cols] = hr
            sim_ref[:, cols] = hi
        ssm_o = _ssm_output(s_ref, u, wcre_ref, wcim_ref, d_ref, wglu_ref)
        ssm_n[...] = _dot(permt_ref[...], _rms(ssm_o, gssm_ref[...]).astype(BF16)).astype(BF16)

    grp = pl.ds(pl.multiple_of(g * grp_rows, grp_rows), grp_rows)
    qb = jnp.concatenate([qp[h, grp, :] for h in range(N_HEADS)], axis=0).astype(BF16)
    kc_f = ck_ref[...]
    vc_f = cv_ref[...]
    seq_rows = [slice(b * KV_WIDTH, (b + 1) * KV_WIDTH) for b in range(GROUP)]
    kcat = jnp.concatenate([kc_f[r] for r in seq_rows], axis=1).astype(BF16)
    vcat = jnp.concatenate([vc_f[r] for r in seq_rows], axis=1).astype(BF16)
    n_q = N_HEADS * grp_rows
    n_c = GROUP * WINDOW

    r_c = lax.broadcasted_iota(jnp.int32, (n_q, n_c), 0)
    c_c = lax.broadcasted_iota(jnp.int32, (n_q, n_c), 1)
    mask_c = ((_div(c_c, WINDOW) == _mod(_div(r_c, n_tok), GROUP))
              & (_mod(c_c, WINDOW) > _mod(r_c, n_tok)))
    lc = jnp.where(mask_c, _dot(qb, kcat), NEG)
    tile = pl.ds(pl.multiple_of(_div(g * grp_rows, LANES) * LANES, LANES), LANES)
    first = _mod(g * grp_rows, LANES)
    r_n = lax.broadcasted_iota(jnp.int32, (n_q, LANES), 0)
    c_n = lax.broadcasted_iota(jnp.int32, (n_q, LANES), 1) - first
    mask_n = ((c_n >= 0) & (c_n < grp_rows)
              & (_div(c_n, n_tok) == _mod(_div(r_n, n_tok), GROUP))
              & (_mod(c_n, n_tok) <= _mod(r_n, n_tok)))
    ln = jnp.where(mask_n, _dot(qb, knew_b[:, tile]), NEG)

    sink_col = sink_ref[...]
    m = jnp.maximum(jnp.maximum(jnp.max(lc, axis=-1, keepdims=True),
                                jnp.max(ln, axis=-1, keepdims=True)), sink_col)
    ec = jnp.exp(lc - m)
    en = jnp.exp(ln - m)
    den = (jnp.sum(ec, axis=-1, keepdims=True) + jnp.sum(en, axis=-1, keepdims=True)
           + jnp.exp(sink_col - m))
    o = (_dot_nt(ec.astype(BF16), vcat) + _dot_nt(en.astype(BF16), vnew_b[:, tile])) / den

    lo = _lane_is_lo((grp_rows, LANES))
    for j in range(ATT_WIDTH // LANES):
        o_even = o[(2 * j) * grp_rows:(2 * j + 1) * grp_rows]
        o_odd = o[(2 * j + 1) * grp_rows:(2 * j + 2) * grp_rows]
        if j < 2:
            pair = jnp.where(lo, o_even, pltpu.roll(o_odd, HALF, axis=1))
        else:
            pair = jnp.where(lo, pltpu.roll(o_even, HALF, axis=1), o_odd)
        att_ref[grp, j * LANES:(j + 1) * LANES] = pair

    keep = WINDOW - n_tok
    to_tail = _mod(keep - first + LANES, LANES)
    kt_g = pltpu.roll(knew[:, tile], to_tail, axis=1)
    vt_g = pltpu.roll(vnew[:, tile], to_tail, axis=1)
    is_new = lax.broadcasted_iota(jnp.int32, (KV_WIDTH, WINDOW), 1) >= keep
    for b, rows in enumerate(seq_rows):
        k_b = kt_g if b == 0 else pltpu.roll(kt_g, LANES - n_tok * b, axis=1)
        v_b = vt_g if b == 0 else pltpu.roll(vt_g, LANES - n_tok * b, axis=1)
        kwin_ref[rows, :] = jnp.where(is_new, k_b, pltpu.roll(kc_f[rows], keep, axis=1))
        vwin_ref[rows, :] = jnp.where(is_new, v_b, pltpu.roll(vc_f[rows], keep, axis=1))

    @pl.when(g == last)
    def _merge_out():
        h1_ref[...] = _merge(x_ref[...], att_ref[...], ssm_n[...], gatt_ref, wout_ref, gpost_ref)


def _decode_mixer(x2d, tables, tables_t, sink_col, ck2d, cv2d, st_re, st_im, wkt, wvt, p):
    n_seq = st_re.shape[0]
    n_groups = n_seq // GROUP
    cmap = lambda g: (g, 0)
    perm = _row_permutation(n_seq, ROWS // n_seq)
    head = (x2d, *tables, *tables_t, sink_col, perm, perm.T)
    in_specs = [_const_spec(a.shape) for a in head] + [
        pl.BlockSpec((GROUP * KV_WIDTH, WINDOW), cmap),
        pl.BlockSpec((GROUP * KV_WIDTH, WINDOW), cmap),
    ] + [_const_spec(a.shape) for a in (st_re, st_im, wkt, wvt, *p)]
    out_shape = (
        jax.ShapeDtypeStruct(x2d.shape, F32),
        jax.ShapeDtypeStruct(ck2d.shape, F32),
        jax.ShapeDtypeStruct(cv2d.shape, F32),
        jax.ShapeDtypeStruct(st_re.shape, F32),
        jax.ShapeDtypeStruct(st_im.shape, F32),
    )
    out_specs = (
        _const_spec(x2d.shape),
        pl.BlockSpec((GROUP * KV_WIDTH, WINDOW), cmap),
        pl.BlockSpec((GROUP * KV_WIDTH, WINDOW), cmap),
        _const_spec(st_re.shape), _const_spec(st_im.shape),
    )
    scratch = [
        pltpu.VMEM((N_HEADS, ROWS, LANES), F32),
        pltpu.VMEM((KV_WIDTH, ROWS), F32),
        pltpu.VMEM((KV_WIDTH, ROWS), F32),
        pltpu.VMEM((KV_WIDTH, ROWS), BF16),
        pltpu.VMEM((KV_WIDTH, ROWS), BF16),
        pltpu.VMEM((ROWS, SSM_WIDTH), BF16),
        pltpu.VMEM((ROWS, ATT_WIDTH), F32),
        pltpu.VMEM((N_LB, 2 * ROWS, LANES), F32),
    ]
    return pl.pallas_call(
        _decode_mixer_kernel,
        grid=(n_groups,),
        in_specs=in_specs,
        out_specs=out_specs,
        out_shape=out_shape,
        scratch_shapes=scratch,
        compiler_params=pltpu.CompilerParams(
            dimension_semantics=("arbitrary",), vmem_limit_bytes=VMEM_LIMIT),
        name="decode_mixer",
    )(*head, ck2d, cv2d, st_re, st_im, wkt, wvt, *p)


def _mlp_kernel(xp_ref, xd_ref, gpre_ref, wup_ref, wdn_ref, gpost_ref, op_ref, od_ref):
    i = pl.program_id(0)

    def mlp(x):
        hn = _rms(x, gpre_ref[...]).astype(BF16)
        acc = None
        for c in range(D_FF // FF_CHUNK):
            cols = slice(c * FF_CHUNK, (c + 1) * FF_CHUNK)
            a = jnp.maximum(_dot(hn, wup_ref[:, cols]), 0.0)
            part = _dot((a * a).astype(BF16), wdn_ref[cols, :])
            acc = part if acc is None else acc + part
        return x + _rms(acc, gpost_ref[...])

    @pl.when(i == 0)
    def _decode_rows():
        od_ref[...] = mlp(xd_ref[...])

    @pl.when(i > 0)
    def _prompt_tile():
        op_ref[...] = mlp(xp_ref[...])


def _mlp(xp2d, xd2d, g_pre, w_up, w_down, g_post):
    n_tiles = xp2d.shape[0] // MLP_ROWS
    pmap = lambda i: (jnp.maximum(i - 1, 0), 0)
    return pl.pallas_call(
        _mlp_kernel,
        grid=(n_tiles + 1,),
        in_specs=[pl.BlockSpec((MLP_ROWS, D_MODEL), pmap), _const_spec(xd2d.shape),
                  _const_spec(g_pre.shape), _const_spec(w_up.shape), _const_spec(w_down.shape),
                  _const_spec(g_post.shape)],
        out_specs=(pl.BlockSpec((MLP_ROWS, D_MODEL), pmap), _const_spec(xd2d.shape)),
        out_shape=(jax.ShapeDtypeStruct(xp2d.shape, F32), jax.ShapeDtypeStruct(xd2d.shape, F32)),
        compiler_params=pltpu.CompilerParams(
            dimension_semantics=("arbitrary",), vmem_limit_bytes=VMEM_LIMIT),
        name="mlp",
    )(xp2d, xd2d, g_pre, w_up, w_down, g_post)


def _zoh(a_re, a_im, log_dt, b_re, b_im):
    dt = jnp.exp(log_dt)[:, None]
    mag = jnp.exp(a_re * dt)
    abar_re, abar_im = mag * jnp.cos(a_im * dt), mag * jnp.sin(a_im * dt)
    nr, ni = abar_re - 1.0, abar_im
    den = a_re * a_re + a_im * a_im
    coef_re = (nr * a_re + ni * a_im) / den
    coef_im = (ni * a_re - nr * a_im) / den
    bbar_re = coef_re[..., None] * b_re - coef_im[..., None] * b_im
    bbar_im = coef_re[..., None] * b_im + coef_im[..., None] * b_re
    return abar_re, abar_im, bbar_re, bbar_im


def _same_group_mask():
    row_g = np.arange(SSM_TILE)[:, None] // SSM_GROUP
    col_g = np.arange(STATE_TILE)[None, :] // SSM_STATE
    return row_g == col_g


def _block_diag_in(bbar):
    w = jnp.swapaxes(bbar, 1, 2).reshape(N_SSM_TILES, SSM_TILE, SSM_STATE)
    w = jnp.tile(w, (1, 1, SSM_TILE // SSM_GROUP))
    return jnp.where(_same_group_mask()[None], w, 0.0).astype(BF16)


def _block_diag_out(c):
    w = jnp.swapaxes(c, 1, 2).reshape(N_SSM_TILES, STATE_TILE, SSM_GROUP)
    w = jnp.tile(w, (1, 1, SSM_TILE // SSM_GROUP))
    return jnp.where(_same_group_mask().T[None], w, 0.0).astype(BF16)


def _rope_tables(pos):
    half = ROPE_DIM // 2
    inv = ROPE_THETA ** (-np.arange(half, dtype=np.float64) / half)
    ang = np.asarray(pos, dtype=np.float64)[:, None] * inv[None, :]
    cos, sin = np.cos(ang), np.sin(ang)
    d = np.arange(LANES) % HEAD_DIM
    f = d % half
    cos_t = np.where(d[None, :] < ROPE_DIM, cos[:, f], 1.0)
    sa_t = np.where((d[None, :] >= half) & (d[None, :] < ROPE_DIM), sin[:, f], 0.0)
    sb_t = np.where(d[None, :] < half, -sin[:, f], 0.0)
    return tuple(t.astype(np.float32) for t in (cos_t, sa_t, sb_t))


def _row_permutation(n_outer, n_inner):
    r = np.arange(n_outer * n_inner)
    perm = np.zeros((r.size, r.size), np.float32)
    perm[r, (r % n_outer) * n_inner + r // n_outer] = 1.0
    return jnp.asarray(perm, dtype=BF16)


def kernel(x_prompt, x_sample, cache_k_win, cache_v_win, state_ssm_re, state_ssm_im, meta_tokens, norm_mix_pre, w_in, attn_sinks, ssm_a_re, ssm_a_im, ssm_log_dt, ssm_b_re, ssm_b_im, ssm_c_re, ssm_c_im, ssm_d, w_glu, norm_att_out, norm_ssm_out, w_out, norm_mix_post, norm_mlp_pre, w_up, w_down, norm_mlp_post):
    depth = w_in.shape[0]
    assert depth == 1
    l = 0
    nb, seq, _ = x_prompt.shape
    n_seq, n_tok, _ = x_sample.shape
    assert n_seq * n_tok == ROWS and nb * BLOCK == ROWS and seq % BLOCK == 0

    abar_re, abar_im, bbar_re, bbar_im = _zoh(
        ssm_a_re[l], ssm_a_im[l], ssm_log_dt[l], ssm_b_re[l], ssm_b_im[l])
    row = lambda a: a.reshape(1, -1)
    q_scale = np.where(np.arange(w_in.shape[-1]) < ATT_WIDTH, HEAD_DIM ** -0.5, 1.0).astype(
        np.float32)
    params = (
        row(norm_mix_pre[l]), (w_in[l] * q_scale).astype(BF16),
        row(abar_re), row(abar_im),
        _block_diag_in(bbar_re), _block_diag_in(bbar_im),
        _block_diag_out(ssm_c_re[l]), _block_diag_out(-ssm_c_im[l]),
        row(ssm_d[l]), w_glu[l].astype(BF16),
        row(norm_att_out[l]), row(norm_ssm_out[l]), w_out[l].astype(BF16), row(norm_mix_post[l]),
    )

    front = BLOCK - N_META
    meta_blk = jnp.concatenate([jnp.zeros((front, D_MODEL), F32), meta_tokens], axis=0)
    cos, sa, sb = (jnp.asarray(t) for t in _rope_tables(np.arange(seq + BLOCK) - front))
    h1, k_last, v_last, p_re, p_im, w_up_b, w_down_b = _prompt_mixer(
        x_prompt, meta_blk, cos, sa, sb, attn_sinks[l], params, w_up[l], w_down[l])
    mlp_params = (row(norm_mlp_pre[l]), w_up_b, w_down_b, row(norm_mlp_post[l]))

    tabs = _rope_tables(PAST_LEN + np.arange(n_tok))
    tables = tuple(jnp.asarray(np.tile(t, (n_seq, 1))) for t in tabs)
    tables_t = tuple(jnp.asarray(np.tile(t.T, (1, n_seq))) for t in tabs)
    sink_col = jnp.repeat(attn_sinks[l], n_tok * GROUP).reshape(-1, 1)
    to_t = lambda a: jnp.transpose(a, (0, 2, 3, 1)).reshape(-1, WINDOW)
    from_t = lambda a, n: jnp.transpose(
        a.reshape(n, N_KV_HEADS, HEAD_DIM, WINDOW), (0, 3, 1, 2))[None]
    w_kv = w_in[l][:, ATT_WIDTH:ATT_WIDTH + 2 * KV_WIDTH].astype(BF16)
    h1s, kwin, vwin, s_re, s_im = _decode_mixer(
        x_sample.reshape(ROWS, D_MODEL), tables, tables_t, sink_col, to_t(cache_k_win[l]), to_t(cache_v_win[l]),
        state_ssm_re[l].reshape(n_seq, N_STATE), state_ssm_im[l].reshape(n_seq, N_STATE),
        w_kv[:, 0:KV_WIDTH].T, w_kv[:, KV_WIDTH:].T, params)
    y_prompt, ys = _mlp(h1.reshape(nb * seq, D_MODEL), h1s, *mlp_params)
    y_prompt = y_prompt.reshape(nb, seq, D_MODEL)
    y_sample = ys.reshape(n_seq, n_tok, D_MODEL)

    win = from_t
    st = lambda a, n: a.reshape(1, n, SSM_GROUPS, SSM_STATE)
    return (y_prompt, y_sample,
            win(k_last, nb), win(v_last, nb), st(p_re, nb), st(p_im, nb),
            win(kwin, n_seq), win(vwin, n_seq), st(s_re, n_seq), st(s_im, n_seq))
```

```python
import math

import jax
import jax.numpy as jnp
import numpy as np
from jax import lax
from jax.experimental import pallas as pl
from jax.experimental.pallas import tpu as pltpu

F32 = jnp.float32
BF16 = jnp.bfloat16

N_META = 16
HEAD_DIM = 64
N_HEADS = 8
N_KV_HEADS = 2
WINDOW = 128
BLOCK = 128
ROPE_DIM = 16
ROPE_THETA = 500000.0
SSM_GROUP = 16
SSM_GROUPS = 32
SSM_STATE = 64
PAST_LEN = 8192
EPS = 1e-6
NEG = -1e30

D_MODEL = 1024
ATT_WIDTH = 512
KV_WIDTH = 128
SSM_WIDTH = 512
N_STATE = SSM_GROUPS * SSM_STATE
D_FF = 4096

ROWS = 512
MLP_ROWS = 1024
LANES = 128
HALF = LANES // 2
SSM_TILE = 256
STATE_TILE = (SSM_TILE // SSM_GROUP) * SSM_STATE
N_SSM_TILES = SSM_WIDTH // SSM_TILE
LB_PER_TILE = STATE_TILE // LANES
N_LB = N_STATE // LANES
FF_CHUNK = 1024
VMEM_LIMIT = 56 * 1024 * 1024


def _dot(a, b):
    return jnp.dot(a, b, preferred_element_type=F32)


def _dot_nt(a, b):
    return lax.dot_general(a, b, (((1,), (1,)), ((), ())), preferred_element_type=F32)


def _rms(x, g):
    return x * lax.rsqrt(jnp.mean(x * x, axis=-1, keepdims=True) + EPS) * g


def _rope(x, cos, sa, sb):
    return x * cos + pltpu.roll(x, 8, axis=1) * sa + pltpu.roll(x, LANES - 8, axis=1) * sb


def _lane_is_lo(shape):
    return lax.broadcasted_iota(jnp.int32, shape, 1) < HALF


def _project_qkv(hn, w_in_ref):
    q = _dot(hn, w_in_ref[:, 0:ATT_WIDTH])
    kv = _dot(hn, w_in_ref[:, ATT_WIDTH:ATT_WIDTH + 2 * KV_WIDTH])
    return q, kv[:, 0:KV_WIDTH], kv[:, KV_WIDTH:2 * KV_WIDTH]


def _project_u(hn, w_in_ref):
    return _dot(hn, w_in_ref[:, ATT_WIDTH + 2 * KV_WIDTH:])


def _ssm_input_tile(ub, c, wbre_ref, wbim_ref, s_ref):
    uc = ub[:, c * SSM_TILE:(c + 1) * SSM_TILE]
    bre, bim = _dot(uc, wbre_ref[c]), _dot(uc, wbim_ref[c])
    for l in range(LB_PER_TILE):
        s_ref[c * LB_PER_TILE + l, 0:ROWS, :] = bre[:, l * LANES:(l + 1) * LANES]
        s_ref[c * LB_PER_TILE + l, ROWS:2 * ROWS, :] = bim[:, l * LANES:(l + 1) * LANES]


def _ssm_input(u, wbre_ref, wbim_ref, s_ref):
    ub = u.astype(BF16)
    for c in range(N_SSM_TILES):
        _ssm_input_tile(ub, c, wbre_ref, wbim_ref, s_ref)


def _ssm_readout_tile(s_ref, c, wcre_ref, wcim_ref):
    blocks = range(c * LB_PER_TILE, (c + 1) * LB_PER_TILE)
    hr = jnp.concatenate([s_ref[l, 0:ROWS, :].astype(BF16) for l in blocks], axis=1)
    hi = jnp.concatenate([s_ref[l, ROWS:2 * ROWS, :].astype(BF16) for l in blocks], axis=1)
    return _dot(hr, wcre_ref[c]) + _dot(hi, wcim_ref[c])


def _ssm_output(s_ref, u, wcre_ref, wcim_ref, d_ref, wglu_ref):
    ys = [_ssm_readout_tile(s_ref, c, wcre_ref, wcim_ref) for c in range(N_SSM_TILES)]
    return _ssm_gate(ys, u, d_ref, wglu_ref)


def _ssm_gate(ys, u, d_ref, wglu_ref):
    y = jnp.concatenate(ys, axis=1) + d_ref[...] * u
    z = 0.5 * y * (1.0 + jnp.tanh(math.sqrt(2.0 / math.pi) * (y + 0.044715 * (y * y * y))))
    gate = 1.0 / (1.0 + jnp.exp(-_dot(z.astype(BF16), wglu_ref[...])))
    return z * gate


def _merge(x, att, s, gatt_ref, wout_ref, gpost_ref):
    a = _rms(att, gatt_ref[...]).astype(BF16)
    m = _dot(a, wout_ref[0:ATT_WIDTH, :]) + _dot(s, wout_ref[ATT_WIDTH:, :])
    return x + _rms(m, gpost_ref[...])


def _div(x, k):
    return lax.shift_right_logical(x, int(math.log2(k)))


def _mod(x, k):
    return lax.bitwise_and(x, k - 1)


def _sink_softmax(logits, bias, sink_col):
    lm = logits + bias
    m =jnp.maximum(jnp.max(lm, axis=-1, keepdims=True), sink_col)
    e = jnp.exp(lm - m)
    den = jnp.sum(e, axis=-1, keepdims=True) + jnp.exp(sink_col - m)
    return e, den


def _prompt_mixer_kernel(
        sink_ref, x_ref, meta_ref, tab0_ref, cos_ref, sa_ref, sb_ref, bias_ref, perm_ref, permt_ref,
        gpre_ref, win_ref,
        ar_ref, ai_ref, wbre_ref, wbim_ref, wcre_ref, wcim_ref, d_ref, wglu_ref,
        gatt_ref, gssm_ref, wout_ref, gpost_ref, wupf_ref, wdnf_ref,
        h1_ref, klast_ref, vlast_ref, sre_ref, sim_ref, wupb_ref, wdnb_ref,
        kbuf, vbuf, kcur, vcur, s_ref, hstate, att_ref):
    n = pl.program_id(0)
    wupb_ref[...] = wupf_ref[...].astype(BF16)
    wdnb_ref[...] = wdnf_ref[...].astype(BF16)
    nb = x_ref.shape[0]
    assert 2 * nb == 8
    last = pl.num_programs(0) - 1
    lo = _lane_is_lo((BLOCK, LANES))
    is_re = lax.broadcasted_iota(jnp.int32, (2 * nb, LANES), 0) < nb
    a1, a2 = [], []
    for l in range(N_LB):
        cols = slice(l * LANES, (l + 1) * LANES)
        ai = jnp.broadcast_to(ai_ref[:, cols], (2 * nb, LANES))
        a1.append(jnp.broadcast_to(ar_ref[:, cols], (2 * nb, LANES)))
        a2.append(jnp.where(is_re, -ai, ai))

    def swap(t):
        return pltpu.roll(t, nb, axis=0)

    def variants(t):
        tr = pltpu.roll(t, HALF, axis=1)
        z = jnp.zeros_like(t)
        return (jnp.where(lo, t, z), jnp.where(lo, z, tr),
                jnp.where(lo, tr, z), jnp.where(lo, z, t))

    @pl.when(n == 0)
    def _meta_block():
        kbuf[...] = jnp.zeros_like(kbuf)
        vbuf[...] = jnp.zeros_like(vbuf)
        hm = _rms(meta_ref[...], gpre_ref[...]).astype(BF16)
        _, k0, v0 = _project_qkv(hm, win_ref)
        k0 = _rope(k0, tab0_ref[0], tab0_ref[1], tab0_ref[2])
        for i, (kv_, vv_) in enumerate(zip(variants(k0), variants(v0))):
            for b in range(nb):
                kbuf[b, i, 0:BLOCK, :] = kv_.astype(BF16)
                vbuf[b, i, 0:BLOCK, :] = vv_.astype(BF16)
        um = _project_u(hm, win_ref).astype(BF16)
        for c in range(N_SSM_TILES):
            uc = um[:, c * SSM_TILE:(c + 1) * SSM_TILE]
            bre, bim = _dot(uc, wbre_ref[c]), _dot(uc, wbim_ref[c])
            for j in range(LB_PER_TILE):
                l = c * LB_PER_TILE + j
                cols = slice(j * LANES, (j + 1) * LANES)
                h = jnp.zeros((2 * nb, LANES), F32)
                for t in range(BLOCK - N_META, BLOCK):
                    x_t = jnp.where(is_re, jnp.broadcast_to(bre[t:t + 1, cols], h.shape),
                                    jnp.broadcast_to(bim[t:t + 1, cols], h.shape))
                    h = a1[l] * h + a2[l] * swap(h) + x_t
                hstate[l] = h

    x = x_ref[...].reshape(ROWS, D_MODEL)
    hn = _rms(x, gpre_ref[...]).astype(BF16)
    cos, sa, sb = cos_ref[...], sa_ref[...], sb_ref[...]

    u = _project_u(_dot(perm_ref[...], hn).astype(BF16), win_ref)

    hfin = []

    def chains(blocks):
        for l in blocks:
            h = hstate[l]
            g = swap(h)
            a2n = -a2[l]
            for i in range(BLOCK // 2):
                r_re = slice(i * 2 * nb, (i + 1) * 2 * nb)
                r_im = slice(ROWS + i * 2 * nb, ROWS + (i + 1) * 2 * nb)
                re, im_s = s_ref[l, r_re, :], swap(s_ref[l, r_im, :])
                h0 = a1[l] * h + a2[l] * g + jnp.where(is_re, re, im_s)
                g0 = swap(h0)
                g = a1[l] * g0 + a2n * h0 + jnp.where(is_re, im_s, re)
                h = swap(g)
                s_ref[l, r_re, :] = jnp.where(is_re, h0, g)
                s_ref[l, r_im, :] = jnp.where(is_re, g0, h)
            hstate[l] = h
            hfin.append(h)

    mask = bias_ref[0]
    cur = pl.ds(pl.multiple_of(_mod(n + 1, 2) * BLOCK, BLOCK), BLOCK)
    top = lax.broadcasted_iota(jnp.int32, (2 * BLOCK, 1), 0) < BLOCK

    def attend(b):
        rows = slice(b * BLOCK, (b + 1) * BLOCK)
        kb = _rope(k[rows], cos, sa, sb)
        vb = v[rows]
        for i, (kv_, vv_) in enumerate(zip(variants(kb), variants(vb))):
            kbuf[b, i, cur, :] = kv_.astype(BF16)
            vbuf[b, i, cur, :] = vv_.astype(BF16)

        kcur[b] = kb
        vcur[b] = vb

        qs = []
        for j in range(ATT_WIDTH // LANES):
            qs.append(_rope(q[rows, j * LANES:(j + 1) * LANES], cos, sa, sb).astype(BF16))
        for g in range(N_KV_HEADS):
            qst = jnp.concatenate([qs[2 * g], qs[2 * g + 1]], axis=0)
            o = None
            for half in range(2):
                var = 2 * g + half
                h_top, h_bot = 4 * g + half, 4 * g + 2 + half
                sink_col = jnp.where(top, sink_ref[h_top], sink_ref[h_bot])
                e, den = _sink_softmax(_dot_nt(qst, kbuf[b, var]), mask, sink_col)
                part = _dot(e.astype(BF16), vbuf[b, var]) / den
                o = part if o is None else o + part
            att_ref[rows, (2 * g) * LANES:(2 * g + 1) * LANES] = o[0:BLOCK]
            att_ref[rows, (2 * g + 1) * LANES:(2 * g + 2) * LANES] = o[BLOCK:2 * BLOCK]

    _ssm_input(u, wbre_ref, wbim_ref, s_ref)
    chains(range(N_LB))
    q, k, v = _project_qkv(hn, win_ref)
    ssm_o = _ssm_output(s_ref, u, wcre_ref, wcim_ref, d_ref, wglu_ref)
    ssm_n = _dot(permt_ref[...], _rms(ssm_o, gssm_ref[...]).astype(BF16)).astype(BF16)
    for b in range(nb):
        attend(b)

    h1 = _merge(x, att_ref[...], ssm_n, gatt_ref, wout_ref, gpost_ref)
    h1_ref[...] = h1.reshape(h1_ref.shape)

    @pl.when(n == last)
    def _emit_state():
        for b in range(nb):
            klast_ref[b] = kcur[b].T
            vlast_ref[b] = vcur[b].T
        for l in range(N_LB):
            sre_ref[:, l * LANES:(l + 1) * LANES] = hfin[l][0:nb]
            sim_ref[:, l * LANES:(l + 1) * LANES] = hfin[l][nb:2 * nb]


def _window_bias():
    r = (np.arange(2 * BLOCK) % BLOCK)[:, None]
    phys = np.arange(2 * BLOCK)[None, :]
    tables = []
    for c_min in (2 * BLOCK - N_META, BLOCK - N_META, 0):
        for parity in (0, 1):
            c = phys if parity == 1 else (phys + BLOCK) % (2 * BLOCK)
            ok = (c > r) & (c <= r + WINDOW) & (c >= c_min)
            tables.append(np.where(ok, 0.0, NEG))
    return jnp.asarray(np.stack(tables), dtype=F32)


def _const_spec(shape):
    zeros = (0,) * len(shape)
    return pl.BlockSpec(shape, lambda *_: zeros)


def _prompt_mixer(x_prompt, meta_blk, cos, sa, sb, sinks, p, w_up, w_down):
    nb, seq, _ = x_prompt.shape
    n_blocks = seq // BLOCK
    perm = _row_permutation(nb, BLOCK)
    xmap = lambda n: (0, n, 0)
    tmap = lambda n: (n + 1, 0)
    bmap = lambda n: (2 * jnp.minimum(n + 1, 2) + (n + 1) % 2, 0, 0)
    n_slabs = D_FF // LANES
    assert n_slabs <= n_blocks
    upmap = lambda n: (0, jnp.minimum(n, n_slabs - 1))
    dnmap = lambda n: (jnp.minimum(n, n_slabs - 1), 0)
    tab0 = jnp.stack([cos[0:BLOCK], sa[0:BLOCK], sb[0:BLOCK]])
    in_specs = [
        pl.BlockSpec(memory_space=pltpu.SMEM),
        pl.BlockSpec((nb, BLOCK, D_MODEL), xmap),
        _const_spec((BLOCK, D_MODEL)),
        _const_spec(tab0.shape),
        pl.BlockSpec((BLOCK, LANES), tmap),
        pl.BlockSpec((BLOCK, LANES), tmap),
        pl.BlockSpec((BLOCK, LANES), tmap),
        pl.BlockSpec((1, 2 * BLOCK, 2 * BLOCK), bmap),
        _const_spec((ROWS, ROWS)),
        _const_spec((ROWS, ROWS)),
    ] + [_const_spec(a.shape) for a in p] + [
        pl.BlockSpec((D_MODEL, LANES), upmap),
        pl.BlockSpec((LANES, D_MODEL), dnmap),
    ]
    out_shape = (
        jax.ShapeDtypeStruct((nb, seq, D_MODEL), F32),
        jax.ShapeDtypeStruct((nb, BLOCK, KV_WIDTH), F32),
        jax.ShapeDtypeStruct((nb, BLOCK, KV_WIDTH), F32),
        jax.ShapeDtypeStruct((nb, N_STATE), F32),
        jax.ShapeDtypeStruct((nb, N_STATE), F32),
        jax.ShapeDtypeStruct(w_up.shape, BF16),
        jax.ShapeDtypeStruct(w_down.shape, BF16),
    )
    out_specs = (
        pl.BlockSpec((nb, BLOCK, D_MODEL), xmap),
        _const_spec((nb, BLOCK, KV_WIDTH)),
        _const_spec((nb, BLOCK, KV_WIDTH)),
        _const_spec((nb, N_STATE)),
        _const_spec((nb, N_STATE)),
        pl.BlockSpec((D_MODEL, LANES), upmap),
        pl.BlockSpec((LANES, D_MODEL), dnmap),
    )
    scratch = [
        pltpu.VMEM((nb, 4, 2 * BLOCK, LANES), BF16),
        pltpu.VMEM((nb, 4, 2 * BLOCK, LANES), BF16),
        pltpu.VMEM((nb, BLOCK, KV_WIDTH), F32),
        pltpu.VMEM((nb, BLOCK, KV_WIDTH), F32),
        pltpu.VMEM((N_LB, 2 * ROWS, LANES), F32),
        pltpu.VMEM((N_LB, 2 * nb, LANES), F32),
        pltpu.VMEM((ROWS, ATT_WIDTH), F32),
    ]
    return pl.pallas_call(
        _prompt_mixer_kernel,
        grid=(n_blocks,),
        in_specs=in_specs,
        out_specs=out_specs,
        out_shape=out_shape,
        scratch_shapes=scratch,
        compiler_params=pltpu.CompilerParams(
            dimension_semantics=("arbitrary",), vmem_limit_bytes=VMEM_LIMIT),
        name="prompt_mixer",
    )(sinks, x_prompt, meta_blk, tab0, cos, sa, sb, _window_bias(), perm, perm.T, *p,
      w_up, w_down)


GROUP = 8


def _decode_mixer_kernel(
        x_ref, cos_ref, sa_ref, sb_ref, cost_ref, sat_ref, sbt_ref, sink_ref, perm_ref, permt_ref,
        ck_ref, cv_ref, stre_ref, stim_ref, wkt_ref, wvt_ref,
        gpre_ref, win_ref, ar_ref, ai_ref, wbre_ref, wbim_ref, wcre_ref, wcim_ref,
        d_ref, wglu_ref, gatt_ref, gssm_ref, wout_ref, gpost_ref,
        h1_ref, kwin_ref, vwin_ref, sre_ref, sim_ref,
        qp, knew, vnew, knew_b, vnew_b, ssm_n, att_ref, s_ref):
    g = pl.program_id(0)
    n_seq = stre_ref.shape[0]
    n_tok = ROWS // n_seq
    grp_rows = GROUP * n_tok
    last = pl.num_programs(0) - 1

    @pl.when(g == 0)
    def _project_and_ssm():
        hn = _rms(x_ref[...], gpre_ref[...]).astype(BF16)
        q = _dot(hn, win_ref[:, 0:ATT_WIDTH])
        u = _project_u(_dot(perm_ref[...], hn).astype(BF16), win_ref)
        cos, sa, sb = cos_ref[...], sa_ref[...], sb_ref[...]
        kt = _dot_nt(wkt_ref[...], hn)
        kt = (kt * cost_ref[...] + pltpu.roll(kt, 8, axis=0) * sat_ref[...]
              + pltpu.roll(kt, KV_WIDTH - 8, axis=0) * sbt_ref[...])
        vt = _dot_nt(wvt_ref[...], hn)
        knew[...] = kt
        vnew[...] = vt
        knew_b[...] = kt.astype(BF16)
        vnew_b[...] = vt.astype(BF16)
        lo = _lane_is_lo((ROWS, LANES))
        for j in range(ATT_WIDTH // LANES):
            qj = _rope(q[:, j * LANES:(j + 1) * LANES], cos, sa, sb)
            qr = pltpu.roll(qj, HALF, axis=1)
            z = jnp.zeros_like(qj)
            if j < 2:
                qp[2 * j] = jnp.where(lo, qj, z)
                qp[2 * j + 1] = jnp.where(lo, qr, z)
            else:
                qp[2 * j] = jnp.where(lo, z, qr)
                qp[2 * j + 1] = jnp.where(lo, z, qj)

        _ssm_input(u, wbre_ref, wbim_ref, s_ref)
        for l in range(N_LB):
            cols = slice(l * LANES, (l + 1) * LANES)
            ar, ai = ar_ref[:, cols], ai_ref[:, cols]
            hr, hi = stre_ref[:, cols], stim_ref[:, cols]
            for t in range(n_tok):
                r_re = slice(t * n_seq, (t + 1) * n_seq)
                r_im = slice(ROWS + t * n_seq, ROWS + (t + 1) * n_seq)
                hr, hi = (ar * hr - ai * hi + s_ref[l, r_re, :],
                          ar * hi + ai * hr + s_ref[l, r_im, :])
                s_ref[l, r_re, :] = hr
                s_ref[l, r_im, :] = hi
            sre_ref[:, cols] = hr
            sim_ref[:, cols] = hi
        ssm_o = _ssm_output(s_ref, u, wcre_ref, wcim_ref, d_ref, wglu_ref)
        ssm_n[...] = _dot(permt_ref[...], _rms(ssm_o, gssm_ref[...]).astype(BF16)).astype(BF16)

    grp = pl.ds(pl.multiple_of(g * grp_rows, grp_rows), grp_rows)
    qb = jnp.concatenate([qp[h, grp, :] for h in range(N_HEADS)], axis=0).astype(BF16)
    kc_f = ck_ref[...]
    vc_f = cv_ref[...]
    seq_rows = [slice(b * KV_WIDTH, (b + 1) * KV_WIDTH) for b in range(GROUP)]
    kcat = jnp.concatenate([kc_f[r] for r in seq_rows], axis=1).astype(BF16)
    vcat = jnp.concatenate([vc_f[r] for r in seq_rows], axis=1).astype(BF16)
    n_q = N_HEADS * grp_rows
    n_c = GROUP * WINDOW

    r_c = lax.broadcasted_iota(jnp.int32, (n_q, n_c), 0)
    c_c = lax.broadcasted_iota(jnp.int32, (n_q, n_c), 1)
    mask_c = ((_div(c_c, WINDOW) == _mod(_div(r_c, n_tok), GROUP))
              & (_mod(c_c, WINDOW) > _mod(r_c, n_tok)))
    lc = jnp.where(mask_c, _dot(qb, kcat), NEG)
    tile = pl.ds(pl.multiple_of(_div(g * grp_rows, LANES) * LANES, LANES), LANES)
    first = _mod(g * grp_rows, LANES)
    r_n = lax.broadcasted_iota(jnp.int32, (n_q, LANES), 0)
    c_n = lax.broadcasted_iota(jnp.int32, (n_q, LANES), 1) - first
    mask_n = ((c_n >= 0) & (c_n < grp_rows)
              & (_div(c_n, n_tok) == _mod(_div(r_n, n_tok), GROUP))
              & (_mod(c_n, n_tok) <= _mod(r_n, n_tok)))
    ln = jnp.where(mask_n, _dot(qb, knew_b[:, tile]), NEG)

    sink_col = sink_ref[...]
    m = jnp.maximum(jnp.maximum(jnp.max(lc, axis=-1, keepdims=True),
                                jnp.max(ln, axis=-1, keepdims=True)), sink_col)
    ec = jnp.exp(lc - m)
    en = jnp.exp(ln - m)
    den = (jnp.sum(ec, axis=-1, keepdims=True) + jnp.sum(en, axis=-1, keepdims=True)
           + jnp.exp(sink_col - m))
    o = (_dot_nt(ec.astype(BF16), vcat) + _dot_nt(en.astype(BF16), vnew_b[:, tile])) / den

    lo = _lane_is_lo((grp_rows, LANES))
    for j in range(ATT_WIDTH // LANES):
        o_even = o[(2 * j) * grp_rows:(2 * j + 1) * grp_rows]
        o_odd = o[(2 * j + 1) * grp_rows:(2 * j + 2) * grp_rows]
        if j < 2:
            pair = jnp.where(lo, o_even, pltpu.roll(o_odd, HALF, axis=1))
        else:
            pair = jnp.where(lo, pltpu.roll(o_even, HALF, axis=1), o_odd)
        att_ref[grp, j * LANES:(j + 1) * LANES] = pair

    keep = WINDOW - n_tok
    to_tail = _mod(keep - first + LANES, LANES)
    kt_g = pltpu.roll(knew[:, tile], to_tail, axis=1)
    vt_g = pltpu.roll(vnew[:, tile], to_tail, axis=1)
    is_new = lax.broadcasted_iota(jnp.int32, (KV_WIDTH, WINDOW), 1) >= keep
    for b, rows in enumerate(seq_rows):
        k_b = kt_g if b == 0 else pltpu.roll(kt_g, LANES - n_tok * b, axis=1)
        v_b = vt_g if b == 0 else pltpu.roll(vt_g, LANES - n_tok * b, axis=1)
        kwin_ref[rows, :] = jnp.where(is_new, k_b, pltpu.roll(kc_f[rows], keep, axis=1))
        vwin_ref[rows, :] = jnp.where(is_new, v_b, pltpu.roll(vc_f[rows], keep, axis=1))

    @pl.when(g == last)
    def _merge_out():
        h1_ref[...] = _merge(x_ref[...], att_ref[...], ssm_n[...], gatt_ref, wout_ref, gpost_ref)


def _decode_mixer(x2d, tables, tables_t, sink_col, ck2d, cv2d, st_re, st_im, wkt, wvt, p):
    n_seq = st_re.shape[0]
    n_groups = n_seq // GROUP
    cmap = lambda g: (g, 0)
    perm = _row_permutation(n_seq, ROWS // n_seq)
    head = (x2d, *tables, *tables_t, sink_col, perm, perm.T)
    in_specs = [_const_spec(a.shape) for a in head] + [
        pl.BlockSpec((GROUP * KV_WIDTH, WINDOW), cmap),
        pl.BlockSpec((GROUP * KV_WIDTH, WINDOW), cmap),
    ] + [_const_spec(a.shape) for a in (st_re, st_im, wkt, wvt, *p)]
    out_shape = (
        jax.ShapeDtypeStruct(x2d.shape, F32),
        jax.ShapeDtypeStruct(ck2d.shape, F32),
        jax.ShapeDtypeStruct(cv2d.shape, F32),
        jax.ShapeDtypeStruct(st_re.shape, F32),
        jax.ShapeDtypeStruct(st_im.shape, F32),
    )
    out_specs = (
        _const_spec(x2d.shape),
        pl.BlockSpec((GROUP * KV_WIDTH, WINDOW), cmap),
        pl.BlockSpec((GROUP * KV_WIDTH, WINDOW), cmap),
        _const_spec(st_re.shape), _const_spec(st_im.shape),
    )
    scratch = [
        pltpu.VMEM((N_HEADS, ROWS, LANES), F32),
        pltpu.VMEM((KV_WIDTH, ROWS), F32),
        pltpu.VMEM((KV_WIDTH, ROWS), F32),
        pltpu.VMEM((KV_WIDTH, ROWS), BF16),
        pltpu.VMEM((KV_WIDTH, ROWS), BF16),
        pltpu.VMEM((ROWS, SSM_WIDTH), BF16),
        pltpu.VMEM((ROWS, ATT_WIDTH), F32),
        pltpu.VMEM((N_LB, 2 * ROWS, LANES), F32),
    ]
    return pl.pallas_call(
        _decode_mixer_kernel,
        grid=(n_groups,),
        in_specs=in_specs,
        out_specs=out_specs,
        out_shape=out_shape,
        scratch_shapes=scratch,
        compiler_params=pltpu.CompilerParams(
            dimension_semantics=("arbitrary",), vmem_limit_bytes=VMEM_LIMIT),
        name="decode_mixer",
    )(*head, ck2d, cv2d, st_re, st_im, wkt, wvt, *p)


def _mlp_kernel(xp_ref, xd_ref, gpre_ref, wup_ref, wdn_ref, gpost_ref, op_ref, od_ref):
    i = pl.program_id(0)

    def mlp(x):
        hn = _rms(x, gpre_ref[...]).astype(BF16)
        acc = None
        for c in range(D_FF // FF_CHUNK):
            cols = slice(c * FF_CHUNK, (c + 1) * FF_CHUNK)
            a = jnp.maximum(_dot(hn, wup_ref[:, cols]), 0.0)
            part = _dot((a * a).astype(BF16), wdn_ref[cols, :])
            acc = part if acc is None else acc + part
        return x + _rms(acc, gpost_ref[...])

    @pl.when(i == 0)
    def _decode_rows():
        od_ref[...] = mlp(xd_ref[...])

    @pl.when(i > 0)
    def _prompt_tile():
        op_ref[...] = mlp(xp_ref[...])


def _mlp(xp2d, xd2d, g_pre, w_up, w_down, g_post):
    n_tiles = xp2d.shape[0] // MLP_ROWS
    pmap = lambda i: (jnp.maximum(i - 1, 0), 0)
    return pl.pallas_call(
        _mlp_kernel,
        grid=(n_tiles + 1,),
        in_specs=[pl.BlockSpec((MLP_ROWS, D_MODEL), pmap), _const_spec(xd2d.shape),
                  _const_spec(g_pre.shape), _const_spec(w_up.shape), _const_spec(w_down.shape),
                  _const_spec(g_post.shape)],
        out_specs=(pl.BlockSpec((MLP_ROWS, D_MODEL), pmap), _const_spec(xd2d.shape)),
        out_shape=(jax.ShapeDtypeStruct(xp2d.shape, F32), jax.ShapeDtypeStruct(xd2d.shape, F32)),
        compiler_params=pltpu.CompilerParams(
            dimension_semantics=("arbitrary",), vmem_limit_bytes=VMEM_LIMIT),
        name="mlp",
    )(xp2d, xd2d, g_pre, w_up, w_down, g_post)


def _zoh(a_re, a_im, log_dt, b_re, b_im):
    dt = jnp.exp(log_dt)[:, None]
    mag = jnp.exp(a_re * dt)
    abar_re, abar_im = mag * jnp.cos(a_im * dt), mag * jnp.sin(a_im * dt)
    nr, ni = abar_re - 1.0, abar_im
    den = a_re * a_re + a_im * a_im
    coef_re = (nr * a_re + ni * a_im) / den
    coef_im = (ni * a_re - nr * a_im) / den
    bbar_re = coef_re[..., None] * b_re - coef_im[..., None] * b_im
    bbar_im = coef_re[..., None] * b_im + coef_im[..., None] * b_re
    return abar_re, abar_im, bbar_re, bbar_im


def _same_group_mask():
    row_g = np.arange(SSM_TILE)[:, None] // SSM_GROUP
    col_g = np.arange(STATE_TILE)[None, :] // SSM_STATE
    return row_g == col_g


def _block_diag_in(bbar):
    w = jnp.swapaxes(bbar, 1, 2).reshape(N_SSM_TILES, SSM_TILE, SSM_STATE)
    w = jnp.tile(w, (1, 1, SSM_TILE // SSM_GROUP))
    return jnp.where(_same_group_mask()[None], w, 0.0).astype(BF16)


def _block_diag_out(c):
    w = jnp.swapaxes(c, 1, 2).reshape(N_SSM_TILES, STATE_TILE, SSM_GROUP)
    w = jnp.tile(w, (1, 1, SSM_TILE // SSM_GROUP))
    return jnp.where(_same_group_mask().T[None], w, 0.0).astype(BF16)


def _rope_tables(pos):
    half = ROPE_DIM // 2
    inv = ROPE_THETA ** (-np.arange(half, dtype=np.float64) / half)
    ang = np.asarray(pos, dtype=np.float64)[:, None] * inv[None, :]
    cos, sin = np.cos(ang), np.sin(ang)
    d = np.arange(LANES) % HEAD_DIM
    f = d % half
    cos_t = np.where(d[None, :] < ROPE_DIM, cos[:, f], 1.0)
    sa_t = np.where((d[None, :] >= half) & (d[None, :] < ROPE_DIM), sin[:, f], 0.0)
    sb_t = np.where(d[None, :] < half, -sin[:, f], 0.0)
    return tuple(t.astype(np.float32) for t in (cos_t, sa_t, sb_t))


def _row_permutation(n_outer, n_inner):
    r = np.arange(n_outer * n_inner)
    perm = np.zeros((r.size, r.size), np.float32)
    perm[r, (r % n_outer) * n_inner + r // n_outer] = 1.0
    return jnp.asarray(perm, dtype=BF16)


def kernel(x_prompt, x_sample, cache_k_win, cache_v_win, state_ssm_re, state_ssm_im, meta_tokens, norm_mix_pre, w_in, attn_sinks, ssm_a_re, ssm_a_im, ssm_log_dt, ssm_b_re, ssm_b_im, ssm_c_re, ssm_c_im, ssm_d, w_glu, norm_att_out, norm_ssm_out, w_out, norm_mix_post, norm_mlp_pre, w_up, w_down, norm_mlp_post):
    depth = w_in.shape[0]
    assert depth == 1
    l = 0
    nb, seq, _ = x_prompt.shape
    n_seq, n_tok, _ = x_sample.shape
    assert n_seq * n_tok == ROWS and nb * BLOCK == ROWS and seq % BLOCK == 0

    abar_re, abar_im, bbar_re, bbar_im = _zoh(
        ssm_a_re[l], ssm_a_im[l], ssm_log_dt[l], ssm_b_re[l], ssm_b_im[l])
    row = lambda a: a.reshape(1, -1)
    q_scale = np.where(np.arange(w_in.shape[-1]) < ATT_WIDTH, HEAD_DIM ** -0.5, 1.0).astype(
        np.float32)
    params = (
        row(norm_mix_pre[l]), (w_in[l] * q_scale).astype(BF16),
        row(abar_re), row(abar_im),
        _block_diag_in(bbar_re), _block_diag_in(bbar_im),
        _block_diag_out(ssm_c_re[l]), _block_diag_out(-ssm_c_im[l]),
        row(ssm_d[l]), w_glu[l].astype(BF16),
        row(norm_att_out[l]), row(norm_ssm_out[l]), w_out[l].astype(BF16), row(norm_mix_post[l]),
    )

    front = BLOCK - N_META
    meta_blk = jnp.concatenate([jnp.zeros((front, D_MODEL), F32), meta_tokens], axis=0)
    cos, sa, sb = (jnp.asarray(t) for t in _rope_tables(np.arange(seq + BLOCK) - front))
    h1, k_last, v_last, p_re, p_im, w_up_b, w_down_b = _prompt_mixer(
        x_prompt, meta_blk, cos, sa, sb, attn_sinks[l], params, w_up[l], w_down[l])
    mlp_params = (row(norm_mlp_pre[l]), w_up_b, w_down_b, row(norm_mlp_post[l]))

    tabs = _rope_tables(PAST_LEN + np.arange(n_tok))
    tables = tuple(jnp.asarray(np.tile(t, (n_seq, 1))) for t in tabs)
    tables_t = tuple(jnp.asarray(np.tile(t.T, (1, n_seq))) for t in tabs)
    sink_col = jnp.repeat(attn_sinks[l], n_tok * GROUP).reshape(-1, 1)
    to_t = lambda a: jnp.transpose(a, (0, 2, 3, 1)).reshape(-1, WINDOW)
    from_t = lambda a, n: jnp.transpose(
        a.reshape(n, N_KV_HEADS, HEAD_DIM, WINDOW), (0, 3, 1, 2))[None]
    w_kv = w_in[l][:, ATT_WIDTH:ATT_WIDTH + 2 * KV_WIDTH].astype(BF16)
    h1s, kwin, vwin, s_re, s_im = _decode_mixer(
        x_sample.reshape(ROWS, D_MODEL), tables, tables_t, sink_col, to_t(cache_k_win[l]), to_t(cache_v_win[l]),
        state_ssm_re[l].reshape(n_seq, N_STATE), state_ssm_im[l].reshape(n_seq, N_STATE),
        w_kv[:, 0:KV_WIDTH].T, w_kv[:, KV_WIDTH:].T, params)
    y_prompt, ys = _mlp(h1.reshape(nb * seq, D_MODEL), h1s, *mlp_params)
    y_prompt = y_prompt.reshape(nb, seq, D_MODEL)
    y_sample = ys.reshape(n_seq, n_tok, D_MODEL)

    win = from_t
    st = lambda a, n: a.reshape(1, n, SSM_GROUPS, SSM_STATE)
    return (y_prompt, y_sample,
            win(k_last, nb), win(v_last, nb), st(p_re, nb), st(p_im, nb),
            win(kwin, n_seq), win(vwin, n_seq), st(s_re, n_seq), st(s_im, n_seq))
```

```python
import math

import jax
import jax.numpy as jnp
import numpy as np
from jax import lax
from jax.experimental import pallas as pl
from jax.experimental.pallas import tpu as pltpu

F32 = jnp.float32
BF16 = jnp.bfloat16

N_META = 16
HEAD_DIM = 64
N_HEADS = 8
N_KV_HEADS = 2
WINDOW = 128
BLOCK = 128
ROPE_DIM = 16
ROPE_THETA = 500000.0
SSM_GROUP = 16
SSM_GROUPS = 32
SSM_STATE = 64
PAST_LEN = 8192
EPS = 1e-6
NEG = -1e30

D_MODEL = 1024
ATT_WIDTH = 512
KV_WIDTH = 128
SSM_WIDTH = 512
N_STATE = SSM_GROUPS * SSM_STATE
D_FF = 4096

ROWS = 512
MLP_ROWS = 1024
LANES = 128
HALF = LANES // 2
SSM_TILE = 256
STATE_TILE = (SSM_TILE // SSM_GROUP) * SSM_STATE
N_SSM_TILES = SSM_WIDTH // SSM_TILE
LB_PER_TILE = STATE_TILE // LANES
N_LB = N_STATE // LANES
FF_CHUNK = 1024
VMEM_LIMIT = 56 * 1024 * 1024


def _dot(a, b):
    return jnp.dot(a, b, preferred_element_type=F32)


def _dot_nt(a, b):
    return lax.dot_general(a, b, (((1,), (1,)), ((), ())), preferred_element_type=F32)


def _rms(x, g):
    return x * lax.rsqrt(jnp.mean(x * x, axis=-1, keepdims=True) + EPS) * g


def _rope(x, cos, sa, sb):
    return x * cos + pltpu.roll(x, 8, axis=1) * sa + pltpu.roll(x, LANES - 8, axis=1) * sb


def _lane_is_lo(shape):
    return lax.broadcasted_iota(jnp.int32, shape, 1) < HALF


def _project_qkv(hn, w_in_ref):
    q = _dot(hn, w_in_ref[:, 0:ATT_WIDTH])
    kv = _dot(hn, w_in_ref[:, ATT_WIDTH:ATT_WIDTH + 2 * KV_WIDTH])
    return q, kv[:, 0:KV_WIDTH], kv[:, KV_WIDTH:2 * KV_WIDTH]


def _project_u(hn, w_in_ref):
    return _dot(hn, w_in_ref[:, ATT_WIDTH + 2 * KV_WIDTH:])


def _ssm_input_tile(ub, c, wbre_ref, wbim_ref, s_ref):
    uc = ub[:, c * SSM_TILE:(c + 1) * SSM_TILE]
    bre, bim = _dot(uc, wbre_ref[c]), _dot(uc, wbim_ref[c])
    for l in range(LB_PER_TILE):
        s_ref[c * LB_PER_TILE + l, 0:ROWS, :] = bre[:, l * LANES:(l + 1) * LANES]
        s_ref[c * LB_PER_TILE + l, ROWS:2 * ROWS, :] = bim[:, l * LANES:(l + 1) * LANES]


def _ssm_input(u, wbre_ref, wbim_ref, s_ref):
    ub = u.astype(BF16)
    for c in range(N_SSM_TILES):
        _ssm_input_tile(ub, c, wbre_ref, wbim_ref, s_ref)


def _ssm_readout_tile(s_ref, c, wcre_ref, wcim_ref):
    blocks = range(c * LB_PER_TILE, (c + 1) * LB_PER_TILE)
    hr = jnp.concatenate([s_ref[l, 0:ROWS, :].astype(BF16) for l in blocks], axis=1)
    hi = jnp.concatenate([s_ref[l, ROWS:2 * ROWS, :].astype(BF16) for l in blocks], axis=1)
    return _dot(hr, wcre_ref[c]) + _dot(hi, wcim_ref[c])


def _ssm_output(s_ref, u, wcre_ref, wcim_ref, d_ref, wglu_ref):
    ys = [_ssm_readout_tile(s_ref, c, wcre_ref, wcim_ref) for c in range(N_SSM_TILES)]
    return _ssm_gate(ys, u, d_ref, wglu_ref)


def _ssm_gate(ys, u, d_ref, wglu_ref):
    y = jnp.concatenate(ys, axis=1) + d_ref[...] * u
    z = 0.5 * y * (1.0 + jnp.tanh(math.sqrt(2.0 / math.pi) * (y + 0.044715 * (y * y * y))))
    gate = 1.0 / (1.0 + jnp.exp(-_dot(z.astype(BF16), wglu_ref[...])))
    return z * gate


def _merge(x, att, s, gatt_ref, wout_ref, gpost_ref):
    a = _rms(att, gatt_ref[...]).astype(BF16)
    m = _dot(a, wout_ref[0:ATT_WIDTH, :]) + _dot(s, wout_ref[ATT_WIDTH:, :])
    return x + _rms(m, gpost_ref[...])


def _div(x, k):
    return lax.shift_right_logical(x, int(math.log2(k)))


def _mod(x, k):
    return lax.bitwise_and(x, k - 1)


def _sink_softmax(logits, bias, sink_col):
    lm = logits + bias
    m =jnp.maximum(jnp.max(lm, axis=-1, keepdims=True), sink_col)
    e = jnp.exp(lm - m)
    den = jnp.sum(e, axis=-1, keepdims=True) + jnp.exp(sink_col - m)
    return e, den


def _prompt_mixer_kernel(
        sink_ref, x_ref, meta_ref, tab0_ref, cos_ref, sa_ref, sb_ref, bias_ref, perm_ref, permt_ref,
        gpre_ref, win_ref,
        ar_ref, ai_ref, wbre_ref, wbim_ref, wcre_ref, wcim_ref, d_ref, wglu_ref,
        gatt_ref, gssm_ref, wout_ref, gpost_ref, wupf_ref, wdnf_ref,
        h1_ref, klast_ref, vlast_ref, sre_ref, sim_ref, wupb_ref, wdnb_ref,
        kbuf, vbuf, kcur, vcur, s_ref, hstate, att_ref):
    n = pl.program_id(0)
    wupb_ref[...] = wupf_ref[...].astype(BF16)
    wdnb_ref[...] = wdnf_ref[...].astype(BF16)
    nb = x_ref.shape[0]
    assert 2 * nb == 8
    last = pl.num_programs(0) - 1
    lo = _lane_is_lo((BLOCK, LANES))
    is_re = lax.broadcasted_iota(jnp.int32, (2 * nb, LANES), 0) < nb
    a1, a2 = [], []
    for l in range(N_LB):
        cols = slice(l * LANES, (l + 1) * LANES)
        ai = jnp.broadcast_to(ai_ref[:, cols], (2 * nb, LANES))
        a1.append(jnp.broadcast_to(ar_ref[:, cols], (2 * nb, LANES)))
        a2.append(jnp.where(is_re, -ai, ai))

    def swap(t):
        return pltpu.roll(t, nb, axis=0)

    def variants(t):
        tr = pltpu.roll(t, HALF, axis=1)
        z = jnp.zeros_like(t)
        return (jnp.where(lo, t, z), jnp.where(lo, z, tr),
                jnp.where(lo, tr, z), jnp.where(lo, z, t))

    @pl.when(n == 0)
    def _meta_block():
        kbuf[...] = jnp.zeros_like(kbuf)
        vbuf[...] = jnp.zeros_like(vbuf)
        hm = _rms(meta_ref[...], gpre_ref[...]).astype(BF16)
        _, k0, v0 = _project_qkv(hm, win_ref)
        k0 = _rope(k0, tab0_ref[0], tab0_ref[1], tab0_ref[2])
        for i, (kv_, vv_) in enumerate(zip(variants(k0), variants(v0))):
            for b in range(nb):
                kbuf[b, i, 0:BLOCK, :] = kv_.astype(BF16)
                vbuf[b, i, 0:BLOCK, :] = vv_.astype(BF16)
        um = _project_u(hm, win_ref).astype(BF16)
        for c in range(N_SSM_TILES):
            uc = um[:, c * SSM_TILE:(c + 1) * SSM_TILE]
            bre, bim = _dot(uc, wbre_ref[c]), _dot(uc, wbim_ref[c])
            for j in range(LB_PER_TILE):
                l = c * LB_PER_TILE + j
                cols = slice(j * LANES, (j + 1) * LANES)
                h = jnp.zeros((2 * nb, LANES), F32)
                for t in range(BLOCK - N_META, BLOCK):
                    x_t = jnp.where(is_re, jnp.broadcast_to(bre[t:t + 1, cols], h.shape),
                                    jnp.broadcast_to(bim[t:t + 1, cols], h.shape))
                    h = a1[l] * h + a2[l] * swap(h) + x_t
                hstate[l] = h

    x = x_ref[...].reshape(ROWS, D_MODEL)
    hn = _rms(x, gpre_ref[...]).astype(BF16)
    cos, sa, sb = cos_ref[...], sa_ref[...], sb_ref[...]

    u = _project_u(_dot(perm_ref[...], hn).astype(BF16), win_ref)

    hfin = []

    def chains(blocks):
        for l in blocks:
            h = hstate[l]
            g = swap(h)
            a2n = -a2[l]
            for i in range(BLOCK // 2):
                r_re = slice(i * 2 * nb, (i + 1) * 2 * nb)
                r_im = slice(ROWS + i * 2 * nb, ROWS + (i + 1) * 2 * nb)
                re, im_s = s_ref[l, r_re, :], swap(s_ref[l, r_im, :])
                h0 = a1[l] * h + a2[l] * g + jnp.where(is_re, re, im_s)
                g0 = swap(h0)
                g = a1[l] * g0 + a2n * h0 + jnp.where(is_re, im_s, re)
                h = swap(g)
                s_ref[l, r_re, :] = jnp.where(is_re, h0, g)
                s_ref[l, r_im, :] = jnp.where(is_re, g0, h)
            hstate[l] = h
            hfin.append(h)

    mask = bias_ref[0]
    cur = pl.ds(pl.multiple_of(_mod(n + 1, 2) * BLOCK, BLOCK), BLOCK)
    top = lax.broadcasted_iota(jnp.int32, (2 * BLOCK, 1), 0) < BLOCK

    def attend(b):
        rows = slice(b * BLOCK, (b + 1) * BLOCK)
        kb = _rope(k[rows], cos, sa, sb)
        vb = v[rows]
        for i, (kv_, vv_) in enumerate(zip(variants(kb), variants(vb))):
            kbuf[b, i, cur, :] = kv_.astype(BF16)
            vbuf[b, i, cur, :] = vv_.astype(BF16)

        kcur[b] = kb
        vcur[b] = vb

        qs = []
        for j in range(ATT_WIDTH // LANES):
            qs.append(_rope(q[rows, j * LANES:(j + 1) * LANES], cos, sa, sb).astype(BF16))
        for g in range(N_KV_HEADS):
            qst = jnp.concatenate([qs[2 * g], qs[2 * g + 1]], axis=0)
            o = None
            for half in range(2):
                var = 2 * g + half
                h_top, h_bot = 4 * g + half, 4 * g + 2 + half
                sink_col = jnp.where(top, sink_ref[h_top], sink_ref[h_bot])
                e, den = _sink_softmax(_dot_nt(qst, kbuf[b, var]), mask, sink_col)
                part = _dot(e.astype(BF16), vbuf[b, var]) / den
                o = part if o is None else o + part
            att_ref[rows, (2 * g) * LANES:(2 * g + 1) * LANES] = o[0:BLOCK]
            att_ref[rows, (2 * g + 1) * LANES:(2 * g + 2) * LANES] = o[BLOCK:2 * BLOCK]

    _ssm_input(u, wbre_ref, wbim_ref, s_ref)
    chains(range(N_LB))
    q, k, v = _project_qkv(hn, win_ref)
    ssm_o = _ssm_output(s_ref, u, wcre_ref, wcim_ref, d_ref, wglu_ref)
    ssm_n = _dot(permt_ref[...], _rms(ssm_o, gssm_ref[...]).astype(BF16)).astype(BF16)
    for b in range(nb):
        attend(b)

    h1 = _merge(x, att_ref[...], ssm_n, gatt_ref, wout_ref, gpost_ref)
    h1_ref[...] = h1.reshape(h1_ref.shape)

    @pl.when(n == last)
    def _emit_state():
        for b in range(nb):
            klast_ref[b] = kcur[b].T
            vlast_ref[b] = vcur[b].T
        for l in range(N_LB):
            sre_ref[:, l * LANES:(l + 1) * LANES] = hfin[l][0:nb]
            sim_ref[:, l * LANES:(l + 1) * LANES] = hfin[l][nb:2 * nb]


def _window_bias():
    r = (np.arange(2 * BLOCK) % BLOCK)[:, None]
    phys = np.arange(2 * BLOCK)[None, :]
    tables = []
    for c_min in (2 * BLOCK - N_META, BLOCK - N_META, 0):
        for parity in (0, 1):
            c = phys if parity == 1 else (phys + BLOCK) % (2 * BLOCK)
            ok = (c > r) & (c <= r + WINDOW) & (c >= c_min)
            tables.append(np.where(ok, 0.0, NEG))
    return jnp.asarray(np.stack(tables), dtype=F32)


def _const_spec(shape):
    zeros = (0,) * len(shape)
    return pl.BlockSpec(shape, lambda *_: zeros)


def _prompt_mixer(x_prompt, meta_blk, cos, sa, sb, sinks, p, w_up, w_down):
    nb, seq, _ = x_prompt.shape
    n_blocks = seq // BLOCK
    perm = _row_permutation(nb, BLOCK)
    xmap = lambda n: (0, n, 0)
    tmap = lambda n: (n + 1, 0)
    bmap = lambda n: (2 * jnp.minimum(n + 1, 2) + (n + 1) % 2, 0, 0)
    n_slabs = D_FF // LANES
    assert n_slabs <= n_blocks
    upmap = lambda n: (0, jnp.minimum(n, n_slabs - 1))
    dnmap = lambda n: (jnp.minimum(n, n_slabs - 1), 0)
    tab0 = jnp.stack([cos[0:BLOCK], sa[0:BLOCK], sb[0:BLOCK]])
    in_specs = [
        pl.BlockSpec(memory_space=pltpu.SMEM),
        pl.BlockSpec((nb, BLOCK, D_MODEL), xmap),
        _const_spec((BLOCK, D_MODEL)),
        _const_spec(tab0.shape),
        pl.BlockSpec((BLOCK, LANES), tmap),
        pl.BlockSpec((BLOCK, LANES), tmap),
        pl.BlockSpec((BLOCK, LANES), tmap),
        pl.BlockSpec((1, 2 * BLOCK, 2 * BLOCK), bmap),
        _const_spec((ROWS, ROWS)),
        _const_spec((ROWS, ROWS)),
    ] + [_const_spec(a.shape) for a in p] + [
        pl.BlockSpec((D_MODEL, LANES), upmap),
        pl.BlockSpec((LANES, D_MODEL), dnmap),
    ]
    out_shape = (
        jax.ShapeDtypeStruct((nb, seq, D_MODEL), F32),
        jax.ShapeDtypeStruct((nb, BLOCK, KV_WIDTH), F32),
        jax.ShapeDtypeStruct((nb, BLOCK, KV_WIDTH), F32),
        jax.ShapeDtypeStruct((nb, N_STATE), F32),
        jax.ShapeDtypeStruct((nb, N_STATE), F32),
        jax.ShapeDtypeStruct(w_up.shape, BF16),
        jax.ShapeDtypeStruct(w_down.shape, BF16),
    )
    out_specs = (
        pl.BlockSpec((nb, BLOCK, D_MODEL), xmap),
        _const_spec((nb, BLOCK, KV_WIDTH)),
        _const_spec((nb, BLOCK, KV_WIDTH)),
        _const_spec((nb, N_STATE)),
        _const_spec((nb, N_STATE)),
        pl.BlockSpec((D_MODEL, LANES), upmap),
        pl.BlockSpec((LANES, D_MODEL), dnmap),
    )
    scratch = [
        pltpu.VMEM((nb, 4, 2 * BLOCK, LANES), BF16),
        pltpu.VMEM((nb, 4, 2 * BLOCK, LANES), BF16),
        pltpu.VMEM((nb, BLOCK, KV_WIDTH), F32),
        pltpu.VMEM((nb, BLOCK, KV_WIDTH), F32),
        pltpu.VMEM((N_LB, 2 * ROWS, LANES), F32),
        pltpu.VMEM((N_LB, 2 * nb, LANES), F32),
        pltpu.VMEM((ROWS, ATT_WIDTH), F32),
    ]
    return pl.pallas_call(
        _prompt_mixer_kernel,
        grid=(n_blocks,),
        in_specs=in_specs,
        out_specs=out_specs,
        out_shape=out_shape,
        scratch_shapes=scratch,
        compiler_params=pltpu.CompilerParams(
            dimension_semantics=("arbitrary",), vmem_limit_bytes=VMEM_LIMIT),
        name="prompt_mixer",
    )(sinks, x_prompt, meta_blk, tab0, cos, sa, sb, _window_bias(), perm, perm.T, *p,
      w_up, w_down)


GROUP = 8
GROUPS_PER_STEP = 2


def _decode_mixer_kernel(
        x_ref, cos_ref, sa_ref, sb_ref, cost_ref, sat_ref, sbt_ref, sink_ref, perm_ref, permt_ref,
        ck_ref, cv_ref, stre_ref, stim_ref, wkt_ref, wvt_ref,
        gpre_ref, win_ref, ar_ref, ai_ref, wbre_ref, wbim_ref, wcre_ref, wcim_ref,
        d_ref, wglu_ref, gatt_ref, gssm_ref, wout_ref, gpost_ref,
        h1_ref, kwin_ref, vwin_ref, sre_ref, sim_ref,
        qp, knew, vnew, knew_b, vnew_b, ssm_n, att_ref, s_ref):
    step = pl.program_id(0)
    n_seq = stre_ref.shape[0]
    n_tok = ROWS // n_seq
    grp_rows = GROUP * n_tok
    last = pl.num_programs(0) - 1

    @pl.when(step == 0)
    def _project_and_ssm():
        hn = _rms(x_ref[...], gpre_ref[...]).astype(BF16)
        q = _dot(hn, win_ref[:, 0:ATT_WIDTH])
        u = _project_u(_dot(perm_ref[...], hn).astype(BF16), win_ref)
        cos, sa, sb = cos_ref[...], sa_ref[...], sb_ref[...]
        kt = _dot_nt(wkt_ref[...], hn)
        kt = (kt * cost_ref[...] + pltpu.roll(kt, 8, axis=0) * sat_ref[...]
              + pltpu.roll(kt, KV_WIDTH - 8, axis=0) * sbt_ref[...])
        vt = _dot_nt(wvt_ref[...], hn)
        knew[...] = kt
        vnew[...] = vt
        knew_b[...] = kt.astype(BF16)
        vnew_b[...] = vt.astype(BF16)
        lo = _lane_is_lo((ROWS, LANES))
        for j in range(ATT_WIDTH // LANES):
            qj = _rope(q[:, j * LANES:(j + 1) * LANES], cos, sa, sb)
            qr = pltpu.roll(qj, HALF, axis=1)
            z = jnp.zeros_like(qj)
            if j < 2:
                qp[2 * j] = jnp.where(lo, qj, z)
                qp[2 * j + 1] = jnp.where(lo, qr, z)
            else:
                qp[2 * j] = jnp.where(lo, z, qr)
                qp[2 * j + 1] = jnp.where(lo, z, qj)

        _ssm_input(u, wbre_ref, wbim_ref, s_ref)
        for l in range(N_LB):
            cols = slice(l * LANES, (l + 1) * LANES)
            ar, ai = ar_ref[:, cols], ai_ref[:, cols]
            hr, hi = stre_ref[:, cols], stim_ref[:, cols]
            for t in range(n_tok):
                r_re = slice(t * n_seq, (t + 1) * n_seq)
                r_im = slice(ROWS + t * n_seq, ROWS + (t + 1) * n_seq)
                hr, hi = (ar * hr - ai * hi + s_ref[l, r_re, :],
                          ar * hi + ai * hr + s_ref[l, r_im, :])
                s_ref[l, r_re, :] = hr
                s_ref[l, r_im, :] = hi
            sre_ref[:, cols] = hr
            sim_ref[:, cols] = hi
        ssm_o = _ssm_output(s_ref, u, wcre_ref, wcim_ref, d_ref, wglu_ref)
        ssm_n[...] = _dot(permt_ref[...], _rms(ssm_o, gssm_ref[...]).astype(BF16)).astype(BF16)

    def attend_group(g, base):
        grp = pl.ds(pl.multiple_of(g * grp_rows, grp_rows), grp_rows)
        qb = jnp.concatenate([qp[h, grp, :] for h in range(N_HEADS)], axis=0).astype(BF16)
        seq_rows = [slice(base + b * KV_WIDTH, base + (b + 1) * KV_WIDTH) for b in range(GROUP)]
        kc_f = [ck_ref[r, :] for r in seq_rows]
        vc_f = [cv_ref[r, :] for r in seq_rows]
        kcat = jnp.concatenate(kc_f, axis=1).astype(BF16)
        vcat = jnp.concatenate(vc_f, axis=1).astype(BF16)
        n_q = N_HEADS * grp_rows
        n_c = GROUP * WINDOW

        r_c = lax.broadcasted_iota(jnp.int32, (n_q, n_c), 0)
        c_c = lax.broadcasted_iota(jnp.int32, (n_q, n_c), 1)
        mask_c = ((_div(c_c, WINDOW) == _mod(_div(r_c, n_tok), GROUP))
                  & (_mod(c_c, WINDOW) > _mod(r_c, n_tok)))
        lc = jnp.where(mask_c, _dot(qb, kcat), NEG)
        tile = pl.ds(pl.multiple_of(_div(g * grp_rows, LANES) * LANES, LANES), LANES)
        first = _mod(g * grp_rows, LANES)
        r_n = lax.broadcasted_iota(jnp.int32, (n_q, LANES), 0)
        c_n = lax.broadcasted_iota(jnp.int32, (n_q, LANES), 1) - first
        mask_n = ((c_n >= 0) & (c_n < grp_rows)
                  & (_div(c_n, n_tok) == _mod(_div(r_n, n_tok), GROUP))
                  & (_mod(c_n, n_tok) <= _mod(r_n, n_tok)))
        ln = jnp.where(mask_n, _dot(qb, knew_b[:, tile]), NEG)

        sink_col = sink_ref[...]
        m = jnp.maximum(jnp.maximum(jnp.max(lc, axis=-1, keepdims=True),
                                    jnp.max(ln, axis=-1, keepdims=True)), sink_col)
        ec = jnp.exp(lc - m)
        en = jnp.exp(ln - m)
        den = (jnp.sum(ec, axis=-1, keepdims=True) + jnp.sum(en, axis=-1, keepdims=True)
               + jnp.exp(sink_col - m))
        o = (_dot_nt(ec.astype(BF16), vcat) + _dot_nt(en.astype(BF16), vnew_b[:, tile])) / den

        lo = _lane_is_lo((grp_rows, LANES))
        for j in range(ATT_WIDTH // LANES):
            o_even = o[(2 * j) * grp_rows:(2 * j + 1) * grp_rows]
            o_odd = o[(2 * j + 1) * grp_rows:(2 * j + 2) * grp_rows]
            if j < 2:
                pair = jnp.where(lo, o_even, pltpu.roll(o_odd, HALF, axis=1))
            else:
                pair = jnp.where(lo, pltpu.roll(o_even, HALF, axis=1), o_odd)
            att_ref[grp, j * LANES:(j + 1) * LANES] = pair

        keep = WINDOW - n_tok
        to_tail = _mod(keep - first + LANES, LANES)
        kt_g = pltpu.roll(knew[:, tile], to_tail, axis=1)
        vt_g = pltpu.roll(vnew[:, tile], to_tail, axis=1)
        is_new = lax.broadcasted_iota(jnp.int32, (KV_WIDTH, WINDOW), 1) >= keep
        for b, rows in enumerate(seq_rows):
            k_b = kt_g if b == 0 else pltpu.roll(kt_g, LANES - n_tok * b, axis=1)
            v_b = vt_g if b == 0 else pltpu.roll(vt_g, LANES - n_tok * b, axis=1)
            kwin_ref[rows, :] = jnp.where(is_new, k_b, pltpu.roll(kc_f[b], keep, axis=1))
            vwin_ref[rows, :] = jnp.where(is_new, v_b, pltpu.roll(vc_f[b], keep, axis=1))

    for sub in range(GROUPS_PER_STEP):
        attend_group(step * GROUPS_PER_STEP + sub, sub * GROUP * KV_WIDTH)

    @pl.when(step == last)
    def _merge_out():
        h1_ref[...] = _merge(x_ref[...], att_ref[...], ssm_n[...], gatt_ref, wout_ref, gpost_ref)


def _decode_mixer(x2d, tables, tables_t, sink_col, ck2d, cv2d, st_re, st_im, wkt, wvt, p):
    n_seq = st_re.shape[0]
    n_steps = n_seq // (GROUP * GROUPS_PER_STEP)
    cmap = lambda g: (g, 0)
    perm = _row_permutation(n_seq, ROWS // n_seq)
    head = (x2d, *tables, *tables_t, sink_col, perm, perm.T)
    in_specs = [_const_spec(a.shape) for a in head] + [
        pl.BlockSpec((GROUPS_PER_STEP * GROUP * KV_WIDTH, WINDOW), cmap),
        pl.BlockSpec((GROUPS_PER_STEP * GROUP * KV_WIDTH, WINDOW), cmap),
    ] + [_const_spec(a.shape) for a in (st_re, st_im, wkt, wvt, *p)]
    out_shape = (
        jax.ShapeDtypeStruct(x2d.shape, F32),
        jax.ShapeDtypeStruct(ck2d.shape, F32),
        jax.ShapeDtypeStruct(cv2d.shape, F32),
        jax.ShapeDtypeStruct(st_re.shape, F32),
        jax.ShapeDtypeStruct(st_im.shape, F32),
    )
    out_specs = (
        _const_spec(x2d.shape),
        pl.BlockSpec((GROUPS_PER_STEP * GROUP * KV_WIDTH, WINDOW), cmap),
        pl.BlockSpec((GROUPS_PER_STEP * GROUP * KV_WIDTH, WINDOW), cmap),
        _const_spec(st_re.shape), _const_spec(st_im.shape),
    )
    scratch = [
        pltpu.VMEM((N_HEADS, ROWS, LANES), F32),
        pltpu.VMEM((KV_WIDTH, ROWS), F32),
        pltpu.VMEM((KV_WIDTH, ROWS), F32),
        pltpu.VMEM((KV_WIDTH, ROWS), BF16),
        pltpu.VMEM((KV_WIDTH, ROWS), BF16),
        pltpu.VMEM((ROWS, SSM_WIDTH), BF16),
        pltpu.VMEM((ROWS, ATT_WIDTH), F32),
        pltpu.VMEM((N_LB, 2 * ROWS, LANES), F32),
    ]
    return pl.pallas_call(
        _decode_mixer_kernel,
        grid=(n_steps,),
        in_specs=in_specs,
        out_specs=out_specs,
        out_shape=out_shape,
        scratch_shapes=scratch,
        compiler_params=pltpu.CompilerParams(
            dimension_semantics=("arbitrary",), vmem_limit_bytes=VMEM_LIMIT),
        name="decode_mixer",
    )(*head, ck2d, cv2d, st_re, st_im, wkt, wvt, *p)


def _mlp_kernel(xp_ref, xd_ref, gpre_ref, wup_ref, wdn_ref, gpost_ref, op_ref, od_ref):
    i = pl.program_id(0)

    def mlp(x):
        hn = _rms(x, gpre_ref[...]).astype(BF16)
        acc = None
        for c in range(D_FF // FF_CHUNK):
            cols = slice(c * FF_CHUNK, (c + 1) * FF_CHUNK)
            a = jnp.maximum(_dot(hn, wup_ref[:, cols]), 0.0)
            part = _dot((a * a).astype(BF16), wdn_ref[cols, :])
            acc = part if acc is None else acc + part
        return x + _rms(acc, gpost_ref[...])

    @pl.when(i == 0)
    def _decode_rows():
        od_ref[...] = mlp(xd_ref[...])

    @pl.when(i > 0)
    def _prompt_tile():
        op_ref[...] = mlp(xp_ref[...])


def _mlp(xp2d, xd2d, g_pre, w_up, w_down, g_post):
    n_tiles = xp2d.shape[0] // MLP_ROWS
    pmap = lambda i: (jnp.maximum(i - 1, 0), 0)
    return pl.pallas_call(
        _mlp_kernel,
        grid=(n_tiles + 1,),
        in_specs=[pl.BlockSpec((MLP_ROWS, D_MODEL), pmap), _const_spec(xd2d.shape),
                  _const_spec(g_pre.shape), _const_spec(w_up.shape), _const_spec(w_down.shape),
                  _const_spec(g_post.shape)],
        out_specs=(pl.BlockSpec((MLP_ROWS, D_MODEL), pmap), _const_spec(xd2d.shape)),
        out_shape=(jax.ShapeDtypeStruct(xp2d.shape, F32), jax.ShapeDtypeStruct(xd2d.shape, F32)),
        compiler_params=pltpu.CompilerParams(
            dimension_semantics=("arbitrary",), vmem_limit_bytes=VMEM_LIMIT),
        name="mlp",
    )(xp2d, xd2d, g_pre, w_up, w_down, g_post)


def _zoh(a_re, a_im, log_dt, b_re, b_im):
    dt = jnp.exp(log_dt)[:, None]
    mag = jnp.exp(a_re * dt)
    abar_re, abar_im = mag * jnp.cos(a_im * dt), mag * jnp.sin(a_im * dt)
    nr, ni = abar_re - 1.0, abar_im
    den = a_re * a_re + a_im * a_im
    coef_re = (nr * a_re + ni * a_im) / den
    coef_im = (ni * a_re - nr * a_im) / den
    bbar_re = coef_re[..., None] * b_re - coef_im[..., None] * b_im
    bbar_im = coef_re[..., None] * b_im + coef_im[..., None] * b_re
    return abar_re, abar_im, bbar_re, bbar_im


def _same_group_mask():
    row_g = np.arange(SSM_TILE)[:, None] // SSM_GROUP
    col_g = np.arange(STATE_TILE)[None, :] // SSM_STATE
    return row_g == col_g


def _block_diag_in(bbar):
    w = jnp.swapaxes(bbar, 1, 2).reshape(N_SSM_TILES, SSM_TILE, SSM_STATE)
    w = jnp.tile(w, (1, 1, SSM_TILE // SSM_GROUP))
    return jnp.where(_same_group_mask()[None], w, 0.0).astype(BF16)


def _block_diag_out(c):
    w = jnp.swapaxes(c, 1, 2).reshape(N_SSM_TILES, STATE_TILE, SSM_GROUP)
    w = jnp.tile(w, (1, 1, SSM_TILE // SSM_GROUP))
    return jnp.where(_same_group_mask().T[None], w, 0.0).astype(BF16)


def _rope_tables(pos):
    half = ROPE_DIM // 2
    inv = ROPE_THETA ** (-np.arange(half, dtype=np.float64) / half)
    ang = np.asarray(pos, dtype=np.float64)[:, None] * inv[None, :]
    cos, sin = np.cos(ang), np.sin(ang)
    d = np.arange(LANES) % HEAD_DIM
    f = d % half
    cos_t = np.where(d[None, :] < ROPE_DIM, cos[:, f], 1.0)
    sa_t = np.where((d[None, :] >= half) & (d[None, :] < ROPE_DIM), sin[:, f], 0.0)
    sb_t = np.where(d[None, :] < half, -sin[:, f], 0.0)
    return tuple(t.astype(np.float32) for t in (cos_t, sa_t, sb_t))


def _row_permutation(n_outer, n_inner):
    r = np.arange(n_outer * n_inner)
    perm = np.zeros((r.size, r.size), np.float32)
    perm[r, (r % n_outer) * n_inner + r // n_outer] = 1.0
    return jnp.asarray(perm, dtype=BF16)


def kernel(x_prompt, x_sample, cache_k_win, cache_v_win, state_ssm_re, state_ssm_im, meta_tokens, norm_mix_pre, w_in, attn_sinks, ssm_a_re, ssm_a_im, ssm_log_dt, ssm_b_re, ssm_b_im, ssm_c_re, ssm_c_im, ssm_d, w_glu, norm_att_out, norm_ssm_out, w_out, norm_mix_post, norm_mlp_pre, w_up, w_down, norm_mlp_post):
    depth = w_in.shape[0]
    assert depth == 1
    l = 0
    nb, seq, _ = x_prompt.shape
    n_seq, n_tok, _ = x_sample.shape
    assert n_seq * n_tok == ROWS and nb * BLOCK == ROWS and seq % BLOCK == 0

    abar_re, abar_im, bbar_re, bbar_im = _zoh(
        ssm_a_re[l], ssm_a_im[l], ssm_log_dt[l], ssm_b_re[l], ssm_b_im[l])
    row = lambda a: a.reshape(1, -1)
    q_scale = np.where(np.arange(w_in.shape[-1]) < ATT_WIDTH, HEAD_DIM ** -0.5, 1.0).astype(
        np.float32)
    params = (
        row(norm_mix_pre[l]), (w_in[l] * q_scale).astype(BF16),
        row(abar_re), row(abar_im),
        _block_diag_in(bbar_re), _block_diag_in(bbar_im),
        _block_diag_out(ssm_c_re[l]), _block_diag_out(-ssm_c_im[l]),
        row(ssm_d[l]), w_glu[l].astype(BF16),
        row(norm_att_out[l]), row(norm_ssm_out[l]), w_out[l].astype(BF16), row(norm_mix_post[l]),
    )

    front = BLOCK - N_META
    meta_blk = jnp.concatenate([jnp.zeros((front, D_MODEL), F32), meta_tokens], axis=0)
    cos, sa, sb = (jnp.asarray(t) for t in _rope_tables(np.arange(seq + BLOCK) - front))
    h1, k_last, v_last, p_re, p_im, w_up_b, w_down_b = _prompt_mixer(
        x_prompt, meta_blk, cos, sa, sb, attn_sinks[l], params, w_up[l], w_down[l])
    mlp_params = (row(norm_mlp_pre[l]), w_up_b, w_down_b, row(norm_mlp_post[l]))

    tabs = _rope_tables(PAST_LEN + np.arange(n_tok))
    tables = tuple(jnp.asarray(np.tile(t, (n_seq, 1))) for t in tabs)
    tables_t = tuple(jnp.asarray(np.tile(t.T, (1, n_seq))) for t in tabs)
    sink_col = jnp.repeat(attn_sinks[l], n_tok * GROUP).reshape(-1, 1)
    to_t = lambda a: jnp.transpose(a, (0, 2, 3, 1)).reshape(-1, WINDOW)
    from_t = lambda a, n: jnp.transpose(
        a.reshape(n, N_KV_HEADS, HEAD_DIM, WINDOW), (0, 3, 1, 2))[None]
    w_kv = w_in[l][:, ATT_WIDTH:ATT_WIDTH + 2 * KV_WIDTH].astype(BF16)
    h1s, kwin, vwin, s_re, s_im = _decode_mixer(
        x_sample.reshape(ROWS, D_MODEL), tables, tables_t, sink_col, to_t(cache_k_win[l]), to_t(cache_v_win[l]),
        state_ssm_re[l].reshape(n_seq, N_STATE), state_ssm_im[l].reshape(n_seq, N_STATE),
        w_kv[:, 0:KV_WIDTH].T, w_kv[:, KV_WIDTH:].T, params)
    y_prompt, ys = _mlp(h1.reshape(nb * seq, D_MODEL), h1s, *mlp_params)
    y_prompt = y_prompt.reshape(nb, seq, D_MODEL)
    y_sample = ys.reshape(n_seq, n_tok, D_MODEL)

    win = from_t
    st = lambda a, n: a.reshape(1, n, SSM_GROUPS, SSM_STATE)
    return (y_prompt, y_sample,
            win(k_last, nb), win(v_last, nb), st(p_re, nb), st(p_im, nb),
            win(kwin, n_seq), win(vwin, n_seq), st(s_re, n_seq), st(s_im, n_seq))
```

```python
import math

import jax
import jax.numpy as jnp
import numpy as np
from jax import lax
from jax.experimental import pallas as pl
from jax.experimental.pallas import tpu as pltpu

F32 = jnp.float32
BF16 = jnp.bfloat16

N_META = 16
HEAD_DIM = 64
N_HEADS = 8
N_KV_HEADS = 2
WINDOW = 128
BLOCK = 128
ROPE_DIM = 16
ROPE_HALF = ROPE_DIM // 2
ROPE_THETA = 500000.0
SSM_GROUP = 16
SSM_GROUPS = 32
SSM_STATE = 64
PAST_LEN = 8192
EPS = 1e-6
NEG = -1e30

D_MODEL = 1024
ATT_WIDTH = 512
KV_WIDTH = 128
SSM_WIDTH = 512
N_STATE = SSM_GROUPS * SSM_STATE
D_FF = 4096

ROWS = 512
MLP_ROWS = 1024
LANES = 128
HALF = LANES // 2
SSM_TILE = 256
STATE_TILE = (SSM_TILE // SSM_GROUP) * SSM_STATE
N_SSM_TILES = SSM_WIDTH // SSM_TILE
LB_PER_TILE = STATE_TILE // LANES
N_LB = N_STATE // LANES
FF_CHUNK = 1024
VMEM_LIMIT = 56 * 1024 * 1024


def _dot(a, b):
    return jnp.dot(a, b, preferred_element_type=F32)


def _dot_nt(a, b):
    return lax.dot_general(a, b, (((1,), (1,)), ((), ())), preferred_element_type=F32)


def _rms(x, g):
    return x * lax.rsqrt(jnp.mean(x * x, axis=-1, keepdims=True) + EPS) * g


def _rope(x, cos, sa, sb):
    return (x * cos + pltpu.roll(x, ROPE_HALF, axis=1) * sa
            + pltpu.roll(x, LANES - ROPE_HALF, axis=1) * sb)


def _lane_is_lo(shape):
    return lax.broadcasted_iota(jnp.int32, shape, 1) < HALF


def _project_qkv(hn, w_in_ref):
    q = _dot(hn, w_in_ref[:, 0:ATT_WIDTH])
    kv = _dot(hn, w_in_ref[:, ATT_WIDTH:ATT_WIDTH + 2 * KV_WIDTH])
    return q, kv[:, 0:KV_WIDTH], kv[:, KV_WIDTH:2 * KV_WIDTH]


def _project_u(hn, w_in_ref):
    return _dot(hn, w_in_ref[:, ATT_WIDTH + 2 * KV_WIDTH:])


def _ssm_input_tile(ub, c, wbre_ref, wbim_ref, s_ref):
    uc = ub[:, c * SSM_TILE:(c + 1) * SSM_TILE]
    bre, bim = _dot(uc, wbre_ref[c]), _dot(uc, wbim_ref[c])
    for l in range(LB_PER_TILE):
        s_ref[c * LB_PER_TILE + l, 0:ROWS, :] = bre[:, l * LANES:(l + 1) * LANES]
        s_ref[c * LB_PER_TILE + l, ROWS:2 * ROWS, :] = bim[:, l * LANES:(l + 1) * LANES]


def _ssm_input(u, wbre_ref, wbim_ref, s_ref):
    ub = u.astype(BF16)
    for c in range(N_SSM_TILES):
        _ssm_input_tile(ub, c, wbre_ref, wbim_ref, s_ref)


def _ssm_readout_tile(s_ref, c, wcre_ref, wcim_ref):
    blocks = range(c * LB_PER_TILE, (c + 1) * LB_PER_TILE)
    hr = jnp.concatenate([s_ref[l, 0:ROWS, :].astype(BF16) for l in blocks], axis=1)
    hi = jnp.concatenate([s_ref[l, ROWS:2 * ROWS, :].astype(BF16) for l in blocks], axis=1)
    return _dot(hr, wcre_ref[c]) + _dot(hi, wcim_ref[c])


def _ssm_output(s_ref, u, wcre_ref, wcim_ref, d_ref, wglu_ref):
    ys = [_ssm_readout_tile(s_ref, c, wcre_ref, wcim_ref) for c in range(N_SSM_TILES)]
    return _ssm_gate(ys, u, d_ref, wglu_ref)


def _ssm_gate(ys, u, d_ref, wglu_ref):
    y = jnp.concatenate(ys, axis=1) + d_ref[...] * u
    z = 0.5 * y * (1.0 + jnp.tanh(math.sqrt(2.0 / math.pi) * (y + 0.044715 * (y * y * y))))
    gate = 1.0 / (1.0 + jnp.exp(-_dot(z.astype(BF16), wglu_ref[...])))
    return z * gate


def _merge(x, att, s, gatt_ref, wout_ref, gpost_ref):
    a = _rms(att, gatt_ref[...]).astype(BF16)
    m = _dot(a, wout_ref[0:ATT_WIDTH, :]) + _dot(s, wout_ref[ATT_WIDTH:, :])
    return x + _rms(m, gpost_ref[...])


def _div(x, k):
    return lax.shift_right_logical(x, int(math.log2(k)))


def _mod(x, k):
    return lax.bitwise_and(x, k - 1)


def _sink_softmax(logits, bias, sink_col):
    lm = logits + bias
    m =jnp.maximum(jnp.max(lm, axis=-1, keepdims=True), sink_col)
    e = jnp.exp(lm - m)
    den = jnp.sum(e, axis=-1, keepdims=True) + jnp.exp(sink_col - m)
    return e, den


def _prompt_mixer_kernel(
        sink_ref, x_ref, meta_ref, tab0_ref, cos_ref, sa_ref, sb_ref, bias_ref, perm_ref, permt_ref,
        gpre_ref, win_ref,
        ar_ref, ai_ref, wbre_ref, wbim_ref, wcre_ref, wcim_ref, d_ref, wglu_ref,
        gatt_ref, gssm_ref, wout_ref, gpost_ref, wupf_ref, wdnf_ref,
        h1_ref, klast_ref, vlast_ref, sre_ref, sim_ref, wupb_ref, wdnb_ref,
        kbuf, vbuf, kcur, vcur, s_ref, hstate, att_ref):
    n = pl.program_id(0)
    wupb_ref[...] = wupf_ref[...].astype(BF16)
    wdnb_ref[...] = wdnf_ref[...].astype(BF16)
    nb = x_ref.shape[0]
    assert 2 * nb == 8
    last = pl.num_programs(0) - 1
    lo = _lane_is_lo((BLOCK, LANES))
    is_re = lax.broadcasted_iota(jnp.int32, (2 * nb, LANES), 0) < nb
    a1, a2 = [], []
    for l in range(N_LB):
        cols = slice(l * LANES, (l + 1) * LANES)
        ai = jnp.broadcast_to(ai_ref[:, cols], (2 * nb, LANES))
        a1.append(jnp.broadcast_to(ar_ref[:, cols], (2 * nb, LANES)))
        a2.append(jnp.where(is_re, -ai, ai))

    def swap(t):
        return pltpu.roll(t, nb, axis=0)

    def variants(t):
        tr = pltpu.roll(t, HALF, axis=1)
        z = jnp.zeros_like(t)
        return (jnp.where(lo, t, z), jnp.where(lo, z, tr),
                jnp.where(lo, tr, z), jnp.where(lo, z, t))

    @pl.when(n == 0)
    def _meta_block():
        kbuf[...] = jnp.zeros_like(kbuf)
        vbuf[...] = jnp.zeros_like(vbuf)
        hm = _rms(meta_ref[...], gpre_ref[...]).astype(BF16)
        _, k0, v0 = _project_qkv(hm, win_ref)
        k0 = _rope(k0, tab0_ref[0], tab0_ref[1], tab0_ref[2])
        for i, (kv_, vv_) in enumerate(zip(variants(k0), variants(v0))):
            for b in range(nb):
                kbuf[b, i, 0:BLOCK, :] = kv_.astype(BF16)
                vbuf[b, i, 0:BLOCK, :] = vv_.astype(BF16)
        um = _project_u(hm, win_ref).astype(BF16)
        for c in range(N_SSM_TILES):
            uc = um[:, c * SSM_TILE:(c + 1) * SSM_TILE]
            bre, bim = _dot(uc, wbre_ref[c]), _dot(uc, wbim_ref[c])
            for j in range(LB_PER_TILE):
                l = c * LB_PER_TILE + j
                cols = slice(j * LANES, (j + 1) * LANES)
                h = jnp.zeros((2 * nb, LANES), F32)
                for t in range(BLOCK - N_META, BLOCK):
                    x_t = jnp.where(is_re, jnp.broadcast_to(bre[t:t + 1, cols], h.shape),
                                    jnp.broadcast_to(bim[t:t + 1, cols], h.shape))
                    h = a1[l] * h + a2[l] * swap(h) + x_t
                hstate[l] = h

    x = x_ref[...].reshape(ROWS, D_MODEL)
    hn = _rms(x, gpre_ref[...]).astype(BF16)
    cos, sa, sb = cos_ref[...], sa_ref[...], sb_ref[...]

    u = _project_u(_dot(perm_ref[...], hn).astype(BF16), win_ref)

    hfin = []

    def chains(blocks):
        for l in blocks:
            h = hstate[l]
            g = swap(h)
            a2n = -a2[l]
            for i in range(BLOCK // 2):
                r_re = slice(i * 2 * nb, (i + 1) * 2 * nb)
                r_im = slice(ROWS + i * 2 * nb, ROWS + (i + 1) * 2 * nb)
                re, im_s = s_ref[l, r_re, :], swap(s_ref[l, r_im, :])
                h0 = a1[l] * h + a2[l] * g + jnp.where(is_re, re, im_s)
                g0 = swap(h0)
                g = a1[l] * g0 + a2n * h0 + jnp.where(is_re, im_s, re)
                h = swap(g)
                s_ref[l, r_re, :] = jnp.where(is_re, h0, g)
                s_ref[l, r_im, :] = jnp.where(is_re, g0, h)
            hstate[l] = h
            hfin.append(h)

    mask = bias_ref[2 * jnp.minimum(n + 1, 2) + _mod(n + 1, 2)]
    cur = pl.ds(pl.multiple_of(_mod(n + 1, 2) * BLOCK, BLOCK), BLOCK)
    top = lax.broadcasted_iota(jnp.int32, (2 * BLOCK, 1), 0) < BLOCK

    def attend(b):
        rows = slice(b * BLOCK, (b + 1) * BLOCK)
        kb = _rope(k[rows], cos, sa, sb)
        vb = v[rows]
        for i, (kv_, vv_) in enumerate(zip(variants(kb), variants(vb))):
            kbuf[b, i, cur, :] = kv_.astype(BF16)
            vbuf[b, i, cur, :] = vv_.astype(BF16)

        kcur[b] = kb
        vcur[b] = vb

        qs = []
        for j in range(ATT_WIDTH // LANES):
            qs.append(_rope(q[rows, j * LANES:(j + 1) * LANES], cos, sa, sb).astype(BF16))
        for g in range(N_KV_HEADS):
            qst = jnp.concatenate([qs[2 * g], qs[2 * g + 1]], axis=0)
            o = None
            for half in range(2):
                var = 2 * g + half
                h_top, h_bot = 4 * g + half, 4 * g + 2 + half
                sink_col = jnp.where(top, sink_ref[h_top], sink_ref[h_bot])
                e, den = _sink_softmax(_dot_nt(qst, kbuf[b, var]), mask, sink_col)
                part = _dot(e.astype(BF16), vbuf[b, var]) / den
                o = part if o is None else o + part
            att_ref[rows, (2 * g) * LANES:(2 * g + 1) * LANES] = o[0:BLOCK]
            att_ref[rows, (2 * g + 1) * LANES:(2 * g + 2) * LANES] = o[BLOCK:2 * BLOCK]

    _ssm_input(u, wbre_ref, wbim_ref, s_ref)
    chains(range(N_LB))
    q, k, v = _project_qkv(hn, win_ref)
    ssm_o = _ssm_output(s_ref, u, wcre_ref, wcim_ref, d_ref, wglu_ref)
    ssm_n = _dot(permt_ref[...], _rms(ssm_o, gssm_ref[...]).astype(BF16)).astype(BF16)
    for b in range(nb):
        attend(b)

    h1 = _merge(x, att_ref[...], ssm_n, gatt_ref, wout_ref, gpost_ref)
    h1_ref[...] = h1.reshape(h1_ref.shape)

    @pl.when(n == last)
    def _emit_state():
        for b in range(nb):
            klast_ref[b] = kcur[b].T
            vlast_ref[b] = vcur[b].T
        for l in range(N_LB):
            sre_ref[:, l * LANES:(l + 1) * LANES] = hfin[l][0:nb]
            sim_ref[:, l * LANES:(l + 1) * LANES] = hfin[l][nb:2 * nb]


def _window_bias():
    r = (np.arange(2 * BLOCK) % BLOCK)[:, None]
    phys = np.arange(2 * BLOCK)[None, :]
    tables = []
    for c_min in (2 * BLOCK - N_META, BLOCK - N_META, 0):
        for parity in (0, 1):
            c = phys if parity == 1 else (phys + BLOCK) % (2 * BLOCK)
            ok = (c > r) & (c <= r + WINDOW) & (c >= c_min)
            tables.append(np.where(ok, 0.0, NEG))
    return jnp.asarray(np.stack(tables), dtype=F32)


def _const_spec(shape):
    zeros = (0,) * len(shape)
    return pl.BlockSpec(shape, lambda *_: zeros)


def _prompt_mixer(x_prompt, meta_blk, cos, sa, sb, sinks, p, w_up, w_down):
    nb, seq, _ = x_prompt.shape
    n_blocks = seq // BLOCK
    perm = _row_permutation(nb, BLOCK)
    xmap = lambda n: (0, n, 0)
    tmap = lambda n: (n + 1, 0)
    n_slabs = D_FF // LANES
    assert n_slabs <= n_blocks
    upmap = lambda n: (0, jnp.minimum(n, n_slabs - 1))
    dnmap = lambda n: (jnp.minimum(n, n_slabs - 1), 0)
    tab0 = jnp.stack([cos[0:BLOCK], sa[0:BLOCK], sb[0:BLOCK]])
    in_specs = [
        pl.BlockSpec(memory_space=pltpu.SMEM),
        pl.BlockSpec((nb, BLOCK, D_MODEL), xmap),
        _const_spec((BLOCK, D_MODEL)),
        _const_spec(tab0.shape),
        pl.BlockSpec((BLOCK, LANES), tmap),
        pl.BlockSpec((BLOCK, LANES), tmap),
        pl.BlockSpec((BLOCK, LANES), tmap),
        _const_spec((6, 2 * BLOCK, 2 * BLOCK)),
        _const_spec((ROWS, ROWS)),
        _const_spec((ROWS, ROWS)),
    ] + [_const_spec(a.shape) for a in p] + [
        pl.BlockSpec((D_MODEL, LANES), upmap),
        pl.BlockSpec((LANES, D_MODEL), dnmap),
    ]
    out_shape = (
        jax.ShapeDtypeStruct((nb, seq, D_MODEL), F32),
        jax.ShapeDtypeStruct((nb, BLOCK, KV_WIDTH), F32),
        jax.ShapeDtypeStruct((nb, BLOCK, KV_WIDTH), F32),
        jax.ShapeDtypeStruct((nb, N_STATE), F32),
        jax.ShapeDtypeStruct((nb, N_STATE), F32),
        jax.ShapeDtypeStruct(w_up.shape, BF16),
        jax.ShapeDtypeStruct(w_down.shape, BF16),
    )
    out_specs = (
        pl.BlockSpec((nb, BLOCK, D_MODEL), xmap),
        _const_spec((nb, BLOCK, KV_WIDTH)),
        _const_spec((nb, BLOCK, KV_WIDTH)),
        _const_spec((nb, N_STATE)),
        _const_spec((nb, N_STATE)),
        pl.BlockSpec((D_MODEL, LANES), upmap),
        pl.BlockSpec((LANES, D_MODEL), dnmap),
    )
    scratch = [
        pltpu.VMEM((nb, 4, 2 * BLOCK, LANES), BF16),
        pltpu.VMEM((nb, 4, 2 * BLOCK, LANES), BF16),
        pltpu.VMEM((nb, BLOCK, KV_WIDTH), F32),
        pltpu.VMEM((nb, BLOCK, KV_WIDTH), F32),
        pltpu.VMEM((N_LB, 2 * ROWS, LANES), F32),
        pltpu.VMEM((N_LB, 2 * nb, LANES), F32),
        pltpu.VMEM((ROWS, ATT_WIDTH), F32),
    ]
    return pl.pallas_call(
        _prompt_mixer_kernel,
        grid=(n_blocks,),
        in_specs=in_specs,
        out_specs=out_specs,
        out_shape=out_shape,
        scratch_shapes=scratch,
        compiler_params=pltpu.CompilerParams(
            dimension_semantics=("arbitrary",), vmem_limit_bytes=VMEM_LIMIT),
        name="prompt_mixer",
    )(sinks, x_prompt, meta_blk, tab0, cos, sa, sb, _window_bias(), perm, perm.T, *p,
      w_up, w_down)


GROUP = 8
GROUPS_PER_STEP = 4


def _decode_mixer_kernel(
        x_ref, cos_ref, sa_ref, sb_ref, cost_ref, sat_ref, sbt_ref, sink_ref, perm_ref, permt_ref,
        ck_ref, cv_ref, stre_ref, stim_ref, wkt_ref, wvt_ref,
        gpre_ref, win_ref, ar_ref, ai_ref, wbre_ref, wbim_ref, wcre_ref, wcim_ref,
        d_ref, wglu_ref, gatt_ref, gssm_ref, wout_ref, gpost_ref,
        h1_ref, kwin_ref, vwin_ref, sre_ref, sim_ref,
        qp, knew, vnew, knew_b, vnew_b, ssm_n, att_ref, s_ref):
    step = pl.program_id(0)
    n_seq = stre_ref.shape[0]
    n_tok = ROWS // n_seq
    grp_rows = GROUP * n_tok
    last = pl.num_programs(0) - 1

    @pl.when(step == 0)
    def _project_and_ssm():
        hn = _rms(x_ref[...], gpre_ref[...]).astype(BF16)
        q = _dot(hn, win_ref[:, 0:ATT_WIDTH])
        u = _project_u(_dot(perm_ref[...], hn).astype(BF16), win_ref)
        cos, sa, sb = cos_ref[...], sa_ref[...], sb_ref[...]
        kt = _dot_nt(wkt_ref[...], hn)
        kt = (kt * cost_ref[...] + pltpu.roll(kt, ROPE_HALF, axis=0) * sat_ref[...]
              + pltpu.roll(kt, KV_WIDTH - ROPE_HALF, axis=0) * sbt_ref[...])
        vt = _dot_nt(wvt_ref[...], hn)
        knew[...] = kt
        vnew[...] = vt
        knew_b[...] = kt.astype(BF16)
        vnew_b[...] = vt.astype(BF16)
        lo = _lane_is_lo((ROWS, LANES))
        for j in range(ATT_WIDTH // LANES):
            qj = _rope(q[:, j * LANES:(j + 1) * LANES], cos, sa, sb)
            qr = pltpu.roll(qj, HALF, axis=1)
            z = jnp.zeros_like(qj)
            if j < 2:
                qp[2 * j] = jnp.where(lo, qj, z)
                qp[2 * j + 1] = jnp.where(lo, qr, z)
            else:
                qp[2 * j] = jnp.where(lo, z, qr)
                qp[2 * j + 1] = jnp.where(lo, z, qj)

        _ssm_input(u, wbre_ref, wbim_ref, s_ref)
        for l in range(N_LB):
            cols = slice(l * LANES, (l + 1) * LANES)
            ar, ai = ar_ref[:, cols], ai_ref[:, cols]
            hr, hi = stre_ref[:, cols], stim_ref[:, cols]
            for t in range(n_tok):
                r_re = slice(t * n_seq, (t + 1) * n_seq)
                r_im = slice(ROWS + t * n_seq, ROWS + (t + 1) * n_seq)
                hr, hi = (ar * hr - ai * hi + s_ref[l, r_re, :],
                          ar * hi + ai * hr + s_ref[l, r_im, :])
                s_ref[l, r_re, :] = hr
                s_ref[l, r_im, :] = hi
            sre_ref[:, cols] = hr
            sim_ref[:, cols] = hi
        ssm_o = _ssm_output(s_ref, u, wcre_ref, wcim_ref, d_ref, wglu_ref)
        ssm_n[...] = _dot(permt_ref[...], _rms(ssm_o, gssm_ref[...]).astype(BF16)).astype(BF16)

    def attend_group(g, base):
        grp = pl.ds(pl.multiple_of(g * grp_rows, grp_rows), grp_rows)
        qb = jnp.concatenate([qp[h, grp, :] for h in range(N_HEADS)], axis=0).astype(BF16)
        seq_rows = [slice(base + b * KV_WIDTH, base + (b + 1) * KV_WIDTH) for b in range(GROUP)]
        kc_f = [ck_ref[r, :] for r in seq_rows]
        vc_f = [cv_ref[r, :] for r in seq_rows]
        kcat = jnp.concatenate(kc_f, axis=1).astype(BF16)
        vcat = jnp.concatenate(vc_f, axis=1).astype(BF16)
        n_q = N_HEADS * grp_rows
        n_c = GROUP * WINDOW

        r_c = lax.broadcasted_iota(jnp.int32, (n_q, n_c), 0)
        c_c = lax.broadcasted_iota(jnp.int32, (n_q, n_c), 1)
        mask_c = ((_div(c_c, WINDOW) == _mod(_div(r_c, n_tok), GROUP))
                  & (_mod(c_c, WINDOW) > _mod(r_c, n_tok)))
        lc = jnp.where(mask_c, _dot(qb, kcat), NEG)
        tile = pl.ds(pl.multiple_of(_div(g * grp_rows, LANES) * LANES, LANES), LANES)
        first = _mod(g * grp_rows, LANES)
        r_n = lax.broadcasted_iota(jnp.int32, (n_q, LANES), 0)
        c_n = lax.broadcasted_iota(jnp.int32, (n_q, LANES), 1) - first
        mask_n = ((c_n >= 0) & (c_n < grp_rows)
                  & (_div(c_n, n_tok) == _mod(_div(r_n, n_tok), GROUP))
                  & (_mod(c_n, n_tok) <= _mod(r_n, n_tok)))
        ln = jnp.where(mask_n, _dot(qb, knew_b[:, tile]), NEG)

        sink_col = sink_ref[...]
        m = jnp.maximum(jnp.maximum(jnp.max(lc, axis=-1, keepdims=True),
                                    jnp.max(ln, axis=-1, keepdims=True)), sink_col)
        ec = jnp.exp(lc - m)
        en = jnp.exp(ln - m)
        den = (jnp.sum(ec, axis=-1, keepdims=True) + jnp.sum(en, axis=-1, keepdims=True)
               + jnp.exp(sink_col - m))
        o = (_dot_nt(ec.astype(BF16), vcat) + _dot_nt(en.astype(BF16), vnew_b[:, tile])) / den

        lo = _lane_is_lo((grp_rows, LANES))
        for j in range(ATT_WIDTH // LANES):
            o_even = o[(2 * j) * grp_rows:(2 * j + 1) * grp_rows]
            o_odd = o[(2 * j + 1) * grp_rows:(2 * j + 2) * grp_rows]
            if j < 2:
                pair = jnp.where(lo, o_even, pltpu.roll(o_odd, HALF, axis=1))
            else:
                pair = jnp.where(lo, pltpu.roll(o_even, HALF, axis=1), o_odd)
            att_ref[grp, j * LANES:(j + 1) * LANES] = pair

        keep = WINDOW - n_tok
        to_tail = _mod(keep - first + LANES, LANES)
        kt_g = pltpu.roll(knew[:, tile], to_tail, axis=1)
        vt_g = pltpu.roll(vnew[:, tile], to_tail, axis=1)
        is_new = lax.broadcasted_iota(jnp.int32, (KV_WIDTH, WINDOW), 1) >= keep
        for b, rows in enumerate(seq_rows):
            k_b = kt_g if b == 0 else pltpu.roll(kt_g, LANES - n_tok * b, axis=1)
            v_b = vt_g if b == 0 else pltpu.roll(vt_g, LANES - n_tok * b, axis=1)
            kwin_ref[rows, :] = jnp.where(is_new, k_b, pltpu.roll(kc_f[b], keep, axis=1))
            vwin_ref[rows, :] = jnp.where(is_new, v_b, pltpu.roll(vc_f[b], keep, axis=1))

    for sub in range(GROUPS_PER_STEP):
        attend_group(step * GROUPS_PER_STEP + sub, sub * GROUP * KV_WIDTH)

    @pl.when(step == last)
    def _merge_out():
        h1_ref[...] = _merge(x_ref[...], att_ref[...], ssm_n[...], gatt_ref, wout_ref, gpost_ref)


def _decode_mixer(x2d, tables, tables_t, sink_col, ck2d, cv2d, st_re, st_im, wkt, wvt, p):
    n_seq = st_re.shape[0]
    n_steps = n_seq // (GROUP * GROUPS_PER_STEP)
    cmap = lambda g: (g, 0)
    perm = _row_permutation(n_seq, ROWS // n_seq)
    head = (x2d, *tables, *tables_t, sink_col, perm, perm.T)
    in_specs = [_const_spec(a.shape) for a in head] + [
        pl.BlockSpec((GROUPS_PER_STEP * GROUP * KV_WIDTH, WINDOW), cmap),
        pl.BlockSpec((GROUPS_PER_STEP * GROUP * KV_WIDTH, WINDOW), cmap),
    ] + [_const_spec(a.shape) for a in (st_re, st_im, wkt, wvt, *p)]
    out_shape = (
        jax.ShapeDtypeStruct(x2d.shape, F32),
        jax.ShapeDtypeStruct(ck2d.shape, F32),
        jax.ShapeDtypeStruct(cv2d.shape, F32),
        jax.ShapeDtypeStruct(st_re.shape, F32),
        jax.ShapeDtypeStruct(st_im.shape, F32),
    )
    out_specs = (
        _const_spec(x2d.shape),
        pl.BlockSpec((GROUPS_PER_STEP * GROUP * KV_WIDTH, WINDOW), cmap),
        pl.BlockSpec((GROUPS_PER_STEP * GROUP * KV_WIDTH, WINDOW), cmap),
        _const_spec(st_re.shape), _const_spec(st_im.shape),
    )
    scratch = [
        pltpu.VMEM((N_HEADS, ROWS, LANES), F32),
        pltpu.VMEM((KV_WIDTH, ROWS), F32),
        pltpu.VMEM((KV_WIDTH, ROWS), F32),
        pltpu.VMEM((KV_WIDTH, ROWS), BF16),
        pltpu.VMEM((KV_WIDTH, ROWS), BF16),
        pltpu.VMEM((ROWS, SSM_WIDTH), BF16),
        pltpu.VMEM((ROWS, ATT_WIDTH), F32),
        pltpu.VMEM((N_LB, 2 * ROWS, LANES), F32),
    ]
    return pl.pallas_call(
        _decode_mixer_kernel,
        grid=(n_steps,),
        in_specs=in_specs,
        out_specs=out_specs,
        out_shape=out_shape,
        scratch_shapes=scratch,
        compiler_params=pltpu.CompilerParams(
            dimension_semantics=("arbitrary",), vmem_limit_bytes=VMEM_LIMIT),
        name="decode_mixer",
    )(*head, ck2d, cv2d, st_re, st_im, wkt, wvt, *p)


def _mlp_kernel(xp_ref, xd_ref, gpre_ref, wup_ref, wdn_ref, gpost_ref, op_ref, od_ref):
    i = pl.program_id(0)

    def mlp(x):
        hn = _rms(x, gpre_ref[...]).astype(BF16)
        acc = None
        for c in range(D_FF // FF_CHUNK):
            cols = slice(c * FF_CHUNK, (c + 1) * FF_CHUNK)
            a = jnp.maximum(_dot(hn, wup_ref[:, cols]), 0.0)
            part = _dot((a * a).astype(BF16), wdn_ref[cols, :])
            acc = part if acc is None else acc + part
        return x + _rms(acc, gpost_ref[...])

    @pl.when(i == 0)
    def _decode_rows():
        od_ref[...] = mlp(xd_ref[...])

    @pl.when(i > 0)
    def _prompt_tile():
        op_ref[...] = mlp(xp_ref[...])


def _mlp(xp2d, xd2d, g_pre, w_up, w_down, g_post):
    n_tiles = xp2d.shape[0] // MLP_ROWS
    pmap = lambda i: (jnp.maximum(i - 1, 0), 0)
    return pl.pallas_call(
        _mlp_kernel,
        grid=(n_tiles + 1,),
        in_specs=[pl.BlockSpec((MLP_ROWS, D_MODEL), pmap), _const_spec(xd2d.shape),
                  _const_spec(g_pre.shape), _const_spec(w_up.shape), _const_spec(w_down.shape),
                  _const_spec(g_post.shape)],
        out_specs=(pl.BlockSpec((MLP_ROWS, D_MODEL), pmap), _const_spec(xd2d.shape)),
        out_shape=(jax.ShapeDtypeStruct(xp2d.shape, F32), jax.ShapeDtypeStruct(xd2d.shape, F32)),
        compiler_params=pltpu.CompilerParams(
            dimension_semantics=("arbitrary",), vmem_limit_bytes=VMEM_LIMIT),
        name="mlp",
    )(xp2d, xd2d, g_pre, w_up, w_down, g_post)


def _zoh(a_re, a_im, log_dt, b_re, b_im):
    dt = jnp.exp(log_dt)[:, None]
    mag = jnp.exp(a_re * dt)
    abar_re, abar_im = mag * jnp.cos(a_im * dt), mag * jnp.sin(a_im * dt)
    nr, ni = abar_re - 1.0, abar_im
    den = a_re * a_re + a_im * a_im
    coef_re = (nr * a_re + ni * a_im) / den
    coef_im = (ni * a_re - nr * a_im) / den
    bbar_re = coef_re[..., None] * b_re - coef_im[..., None] * b_im
    bbar_im = coef_re[..., None] * b_im + coef_im[..., None] * b_re
    return abar_re, abar_im, bbar_re, bbar_im


def _same_group_mask():
    row_g = np.arange(SSM_TILE)[:, None] // SSM_GROUP
    col_g = np.arange(STATE_TILE)[None, :] // SSM_STATE
    return row_g == col_g


def _block_diag_in(bbar):
    w = jnp.swapaxes(bbar, 1, 2).reshape(N_SSM_TILES, SSM_TILE, SSM_STATE)
    w = jnp.tile(w, (1, 1, SSM_TILE // SSM_GROUP))
    return jnp.where(_same_group_mask()[None], w, 0.0).astype(BF16)


def _block_diag_out(c):
    w = jnp.swapaxes(c, 1, 2).reshape(N_SSM_TILES, STATE_TILE, SSM_GROUP)
    w = jnp.tile(w, (1, 1, SSM_TILE // SSM_GROUP))
    return jnp.where(_same_group_mask().T[None], w, 0.0).astype(BF16)


def _rope_tables(pos):
    half = ROPE_DIM // 2
    inv = ROPE_THETA ** (-np.arange(half, dtype=np.float64) / half)
    ang = np.asarray(pos, dtype=np.float64)[:, None] * inv[None, :]
    cos, sin = np.cos(ang), np.sin(ang)
    d = np.arange(LANES) % HEAD_DIM
    f = d % half
    cos_t = np.where(d[None, :] < ROPE_DIM, cos[:, f], 1.0)
    sa_t = np.where((d[None, :] >= half) & (d[None, :] < ROPE_DIM), sin[:, f], 0.0)
    sb_t = np.where(d[None, :] < half, -sin[:, f], 0.0)
    return tuple(t.astype(np.float32) for t in (cos_t, sa_t, sb_t))


def _row_permutation(n_outer, n_inner):
    r = np.arange(n_outer * n_inner)
    perm = np.zeros((r.size, r.size), np.float32)
    perm[r, (r % n_outer) * n_inner + r // n_outer] = 1.0
    return jnp.asarray(perm, dtype=BF16)


def kernel(x_prompt, x_sample, cache_k_win, cache_v_win, state_ssm_re, state_ssm_im, meta_tokens, norm_mix_pre, w_in, attn_sinks, ssm_a_re, ssm_a_im, ssm_log_dt, ssm_b_re, ssm_b_im, ssm_c_re, ssm_c_im, ssm_d, w_glu, norm_att_out, norm_ssm_out, w_out, norm_mix_post, norm_mlp_pre, w_up, w_down, norm_mlp_post):
    depth = w_in.shape[0]
    assert depth == 1
    l = 0
    nb, seq, _ = x_prompt.shape
    n_seq, n_tok, _ = x_sample.shape
    assert n_seq * n_tok == ROWS and nb * BLOCK == ROWS and seq % BLOCK == 0

    abar_re, abar_im, bbar_re, bbar_im = _zoh(
        ssm_a_re[l], ssm_a_im[l], ssm_log_dt[l], ssm_b_re[l], ssm_b_im[l])
    row = lambda a: a.reshape(1, -1)
    q_scale = np.where(np.arange(w_in.shape[-1]) < ATT_WIDTH, HEAD_DIM ** -0.5, 1.0).astype(
        np.float32)
    params = (
        row(norm_mix_pre[l]), (w_in[l] * q_scale).astype(BF16),
        row(abar_re), row(abar_im),
        _block_diag_in(bbar_re), _block_diag_in(bbar_im),
        _block_diag_out(ssm_c_re[l]), _block_diag_out(-ssm_c_im[l]),
        row(ssm_d[l]), w_glu[l].astype(BF16),
        row(norm_att_out[l]), row(norm_ssm_out[l]), w_out[l].astype(BF16), row(norm_mix_post[l]),
    )

    front = BLOCK - N_META
    meta_blk = jnp.concatenate([jnp.zeros((front, D_MODEL), F32), meta_tokens], axis=0)
    cos, sa, sb = (jnp.asarray(t) for t in _rope_tables(np.arange(seq + BLOCK) - front))
    h1, k_last, v_last, p_re, p_im, w_up_b, w_down_b = _prompt_mixer(
        x_prompt, meta_blk, cos, sa, sb, attn_sinks[l], params, w_up[l], w_down[l])
    mlp_params = (row(norm_mlp_pre[l]), w_up_b, w_down_b, row(norm_mlp_post[l]))

    tabs = _rope_tables(PAST_LEN + np.arange(n_tok))
    tables = tuple(jnp.asarray(np.tile(t, (n_seq, 1))) for t in tabs)
    tables_t = tuple(jnp.asarray(np.tile(t.T, (1, n_seq))) for t in tabs)
    sink_col = jnp.repeat(attn_sinks[l], n_tok * GROUP).reshape(-1, 1)
    to_t = lambda a: jnp.transpose(a, (0, 2, 3, 1)).reshape(-1, WINDOW)
    from_t = lambda a, n: jnp.transpose(
        a.reshape(n, N_KV_HEADS, HEAD_DIM, WINDOW), (0, 3, 1, 2))[None]
    w_kv = w_in[l][:, ATT_WIDTH:ATT_WIDTH + 2 * KV_WIDTH].astype(BF16)
    h1s, kwin, vwin, s_re, s_im = _decode_mixer(
        x_sample.reshape(ROWS, D_MODEL), tables, tables_t, sink_col, to_t(cache_k_win[l]), to_t(cache_v_win[l]),
        state_ssm_re[l].reshape(n_seq, N_STATE), state_ssm_im[l].reshape(n_seq, N_STATE),
        w_kv[:, 0:KV_WIDTH].T, w_kv[:, KV_WIDTH:].T, params)
    y_prompt, ys = _mlp(h1.reshape(nb * seq, D_MODEL), h1s, *mlp_params)
    y_prompt = y_prompt.reshape(nb, seq, D_MODEL)
    y_sample = ys.reshape(n_seq, n_tok, D_MODEL)

    win = from_t
    st = lambda a, n: a.reshape(1, n, SSM_GROUPS, SSM_STATE)
    return (y_prompt, y_sample,
            win(k_last, nb), win(v_last, nb), st(p_re, nb), st(p_im, nb),
            win(kwin, n_seq), win(vwin, n_seq), st(s_re, n_seq), st(s_im, n_seq))
```

```python
import math

import jax
import jax.numpy as jnp
import numpy as np
from jax import lax
from jax.experimental import pallas as pl
from jax.experimental.pallas import tpu as pltpu

F32 = jnp.float32
BF16 = jnp.bfloat16

N_META = 16
HEAD_DIM = 64
N_HEADS = 8
N_KV_HEADS = 2
WINDOW = 128
BLOCK = 128
ROPE_DIM = 16
ROPE_HALF = ROPE_DIM // 2
ROPE_THETA = 500000.0
SSM_GROUP = 16
SSM_GROUPS = 32
SSM_STATE = 64
PAST_LEN = 8192
EPS = 1e-6
NEG = -1e30

D_MODEL = 1024
ATT_WIDTH = 512
KV_WIDTH = 128
SSM_WIDTH = 512
N_STATE = SSM_GROUPS * SSM_STATE
D_FF = 4096

ROWS = 512
MLP_ROWS = 1024
LANES = 128
HALF = LANES // 2
SSM_TILE = 256
STATE_TILE = (SSM_TILE // SSM_GROUP) * SSM_STATE
N_SSM_TILES = SSM_WIDTH // SSM_TILE
LB_PER_TILE = STATE_TILE // LANES
N_LB = N_STATE // LANES
FF_CHUNK = 1024
VMEM_LIMIT = 56 * 1024 * 1024


def _dot(a, b):
    return jnp.dot(a, b, preferred_element_type=F32)


def _dot_nt(a, b):
    return lax.dot_general(a, b, (((1,), (1,)), ((), ())), preferred_element_type=F32)


def _rms(x, g):
    return x * lax.rsqrt(jnp.mean(x * x, axis=-1, keepdims=True) + EPS) * g


def _rope(x, cos, sa, sb):
    return (x * cos + pltpu.roll(x, ROPE_HALF, axis=1) * sa
            + pltpu.roll(x, LANES - ROPE_HALF, axis=1) * sb)


def _lane_is_lo(shape):
    return lax.broadcasted_iota(jnp.int32, shape, 1) < HALF


def _project_qkv(hn, w_in_ref):
    q = _dot(hn, w_in_ref[:, 0:ATT_WIDTH])
    kv = _dot(hn, w_in_ref[:, ATT_WIDTH:ATT_WIDTH + 2 * KV_WIDTH])
    return q, kv[:, 0:KV_WIDTH], kv[:, KV_WIDTH:2 * KV_WIDTH]


def _project_u(hn, w_in_ref):
    return _dot(hn, w_in_ref[:, ATT_WIDTH + 2 * KV_WIDTH:])


def _ssm_input_tile(ub, c, wbre_ref, wbim_ref, s_ref):
    uc = ub[:, c * SSM_TILE:(c + 1) * SSM_TILE]
    bre, bim = _dot(uc, wbre_ref[c]), _dot(uc, wbim_ref[c])
    for l in range(LB_PER_TILE):
        s_ref[c * LB_PER_TILE + l, 0:ROWS, :] = bre[:, l * LANES:(l + 1) * LANES]
        s_ref[c * LB_PER_TILE + l, ROWS:2 * ROWS, :] = bim[:, l * LANES:(l + 1) * LANES]


def _ssm_input(u, wbre_ref, wbim_ref, s_ref):
    ub = u.astype(BF16)
    for c in range(N_SSM_TILES):
        _ssm_input_tile(ub, c, wbre_ref, wbim_ref, s_ref)


def _ssm_readout_tile(s_ref, c, wcre_ref, wcim_ref):
    blocks = range(c * LB_PER_TILE, (c + 1) * LB_PER_TILE)
    hr = jnp.concatenate([s_ref[l, 0:ROWS, :].astype(BF16) for l in blocks], axis=1)
    hi = jnp.concatenate([s_ref[l, ROWS:2 * ROWS, :].astype(BF16) for l in blocks], axis=1)
    return _dot(hr, wcre_ref[c]) + _dot(hi, wcim_ref[c])


def _ssm_output(s_ref, u, wcre_ref, wcim_ref, d_ref, wglu_ref):
    ys = [_ssm_readout_tile(s_ref, c, wcre_ref, wcim_ref) for c in range(N_SSM_TILES)]
    return _ssm_gate(ys, u, d_ref, wglu_ref)


def _ssm_gate(ys, u, d_ref, wglu_ref):
    y = jnp.concatenate(ys, axis=1) + d_ref[...] * u
    z = 0.5 * y * (1.0 + jnp.tanh(math.sqrt(2.0 / math.pi) * (y + 0.044715 * (y * y * y))))
    gate = 1.0 / (1.0 + jnp.exp(-_dot(z.astype(BF16), wglu_ref[...])))
    return z * gate


def _merge(x, att, s, gatt_ref, wout_ref, gpost_ref):
    a = _rms(att, gatt_ref[...]).astype(BF16)
    m = _dot(a, wout_ref[0:ATT_WIDTH, :]) + _dot(s, wout_ref[ATT_WIDTH:, :])
    return x + _rms(m, gpost_ref[...])


def _div(x, k):
    return lax.shift_right_logical(x, int(math.log2(k)))


def _mod(x, k):
    return lax.bitwise_and(x, k - 1)


def _sink_softmax(logits, bias, sink_col):
    lm = logits + bias
    m =jnp.maximum(jnp.max(lm, axis=-1, keepdims=True), sink_col)
    e = jnp.exp(lm - m)
    den = jnp.sum(e, axis=-1, keepdims=True) + jnp.exp(sink_col - m)
    return e, den


def _prompt_mixer_kernel(
        sink_ref, x_ref, meta_ref, tab0_ref, cos_ref, sa_ref, sb_ref, bias_ref, perm_ref, permt_ref,
        gpre_ref, win_ref,
        ar_ref, ai_ref, wbre_ref, wbim_ref, wcre_ref, wcim_ref, d_ref, wglu_ref,
        gatt_ref, gssm_ref, wout_ref, gpost_ref, wupf_ref, wdnf_ref,
        h1_ref, klast_ref, vlast_ref, sre_ref, sim_ref, wupb_ref, wdnb_ref,
        kbuf, vbuf, kcur, vcur, s_ref, hstate, att_ref):
    n = pl.program_id(0)
    wupb_ref[...] = wupf_ref[...].astype(BF16)
    wdnb_ref[...] = wdnf_ref[...].astype(BF16)
    nb = x_ref.shape[0]
    assert 2 * nb == 8
    last = pl.num_programs(0) - 1
    lo = _lane_is_lo((BLOCK, LANES))
    is_re = lax.broadcasted_iota(jnp.int32, (2 * nb, LANES), 0) < nb
    a1, a2 = [], []
    for l in range(N_LB):
        cols = slice(l * LANES, (l + 1) * LANES)
        ai = jnp.broadcast_to(ai_ref[:, cols], (2 * nb, LANES))
        a1.append(jnp.broadcast_to(ar_ref[:, cols], (2 * nb, LANES)))
        a2.append(jnp.where(is_re, -ai, ai))

    def swap(t):
        return pltpu.roll(t, nb, axis=0)

    def variants(t):
        tr = pltpu.roll(t, HALF, axis=1)
        z = jnp.zeros_like(t)
        return (jnp.where(lo, t, z), jnp.where(lo, z, tr),
                jnp.where(lo, tr, z), jnp.where(lo, z, t))

    @pl.when(n == 0)
    def _meta_block():
        kbuf[...] = jnp.zeros_like(kbuf)
        vbuf[...] = jnp.zeros_like(vbuf)
        hm = _rms(meta_ref[...], gpre_ref[...]).astype(BF16)
        _, k0, v0 = _project_qkv(hm, win_ref)
        k0 = _rope(k0, tab0_ref[0], tab0_ref[1], tab0_ref[2])
        for i, (kv_, vv_) in enumerate(zip(variants(k0), variants(v0))):
            for b in range(nb):
                kbuf[b, i, 0:BLOCK, :] = kv_.astype(BF16)
                vbuf[b, i, 0:BLOCK, :] = vv_.astype(BF16)
        um = _project_u(hm, win_ref).astype(BF16)
        for c in range(N_SSM_TILES):
            uc = um[:, c * SSM_TILE:(c + 1) * SSM_TILE]
            bre, bim = _dot(uc, wbre_ref[c]), _dot(uc, wbim_ref[c])
            for j in range(LB_PER_TILE):
                l = c * LB_PER_TILE + j
                cols = slice(j * LANES, (j + 1) * LANES)
                h = jnp.zeros((2 * nb, LANES), F32)
                for t in range(BLOCK - N_META, BLOCK):
                    x_t = jnp.where(is_re, jnp.broadcast_to(bre[t:t + 1, cols], h.shape),
                                    jnp.broadcast_to(bim[t:t + 1, cols], h.shape))
                    h = a1[l] * h + a2[l] * swap(h) + x_t
                hstate[l] = h

    x = x_ref[...].reshape(ROWS, D_MODEL)
    hn = _rms(x, gpre_ref[...]).astype(BF16)
    cos, sa, sb = cos_ref[...], sa_ref[...], sb_ref[...]

    u = _project_u(_dot(perm_ref[...], hn).astype(BF16), win_ref)

    hfin = []

    def chains(blocks):
        for l in blocks:
            h = hstate[l]
            g = swap(h)
            a2n = -a2[l]
            for i in range(BLOCK // 2):
                r_re = slice(i * 2 * nb, (i + 1) * 2 * nb)
                r_im = slice(ROWS + i * 2 * nb, ROWS + (i + 1) * 2 * nb)
                re, im_s = s_ref[l, r_re, :], swap(s_ref[l, r_im, :])
                h0 = a1[l] * h + a2[l] * g + jnp.where(is_re, re, im_s)
                g0 = swap(h0)
                g = a1[l] * g0 + a2n * h0 + jnp.where(is_re, im_s, re)
                h = swap(g)
                s_ref[l, r_re, :] = jnp.where(is_re, h0, g)
                s_ref[l, r_im, :] = jnp.where(is_re, g0, h)
            hstate[l] = h
            hfin.append(h)

    mask = bias_ref[0]
    cur = pl.ds(pl.multiple_of(_mod(n + 1, 2) * BLOCK, BLOCK), BLOCK)
    top = lax.broadcasted_iota(jnp.int32, (2 * BLOCK, 1), 0) < BLOCK

    def attend(b):
        rows = slice(b * BLOCK, (b + 1) * BLOCK)
        kb = _rope(k[rows], cos, sa, sb)
        vb = v[rows]
        for i, (kv_, vv_) in enumerate(zip(variants(kb), variants(vb))):
            kbuf[b, i, cur, :] = kv_.astype(BF16)
            vbuf[b, i, cur, :] = vv_.astype(BF16)

        kcur[b] = kb
        vcur[b] = vb

        qs = []
        for j in range(ATT_WIDTH // LANES):
            qs.append(_rope(q[rows, j * LANES:(j + 1) * LANES], cos, sa, sb).astype(BF16))
        for g in range(N_KV_HEADS):
            qst = jnp.concatenate([qs[2 * g], qs[2 * g + 1]], axis=0)
            o = None
            for half in range(2):
                var = 2 * g + half
                h_top, h_bot = 4 * g + half, 4 * g + 2 + half
                sink_col = jnp.where(top, sink_ref[h_top], sink_ref[h_bot])
                e, den = _sink_softmax(_dot_nt(qst, kbuf[b, var]), mask, sink_col)
                part = _dot(e.astype(BF16), vbuf[b, var]) / den
                o = part if o is None else o + part
            att_ref[rows, (2 * g) * LANES:(2 * g + 1) * LANES] = o[0:BLOCK]
            att_ref[rows, (2 * g + 1) * LANES:(2 * g + 2) * LANES] = o[BLOCK:2 * BLOCK]

    _ssm_input(u, wbre_ref, wbim_ref, s_ref)
    chains(range(N_LB))
    q, k, v = _project_qkv(hn, win_ref)
    ssm_o = _ssm_output(s_ref, u, wcre_ref, wcim_ref, d_ref, wglu_ref)
    ssm_n = _dot(permt_ref[...], _rms(ssm_o, gssm_ref[...]).astype(BF16)).astype(BF16)
    for b in range(nb):
        attend(b)

    h1 = _merge(x, att_ref[...], ssm_n, gatt_ref, wout_ref, gpost_ref)
    h1_ref[...] = h1.reshape(h1_ref.shape)

    @pl.when(n == last)
    def _emit_state():
        for b in range(nb):
            klast_ref[b] = kcur[b].T
            vlast_ref[b] = vcur[b].T
        for l in range(N_LB):
            sre_ref[:, l * LANES:(l + 1) * LANES] = hfin[l][0:nb]
            sim_ref[:, l * LANES:(l + 1) * LANES] = hfin[l][nb:2 * nb]


def _window_bias():
    r = (np.arange(2 * BLOCK) % BLOCK)[:, None]
    phys = np.arange(2 * BLOCK)[None, :]
    tables = []
    for c_min in (2 * BLOCK - N_META, BLOCK - N_META, 0):
        for parity in (0, 1):
            c = phys if parity == 1 else (phys + BLOCK) % (2 * BLOCK)
            ok = (c > r) & (c <= r + WINDOW) & (c >= c_min)
            tables.append(np.where(ok, 0.0, NEG))
    return jnp.asarray(np.stack(tables), dtype=F32)


def _const_spec(shape):
    zeros = (0,) * len(shape)
    return pl.BlockSpec(shape, lambda *_: zeros)


def _prompt_mixer(x_prompt, meta_blk, cos, sa, sb, sinks, p, w_up, w_down):
    nb, seq, _ = x_prompt.shape
    n_blocks = seq // BLOCK
    perm = _row_permutation(nb, BLOCK)
    xmap = lambda n: (0, n, 0)
    tmap = lambda n: (n + 1, 0)
    bmap = lambda n: (2 * jnp.minimum(n + 1, 2) + (n + 1) % 2, 0, 0)
    n_slabs = D_FF // LANES
    assert n_slabs <= n_blocks
    upmap = lambda n: (0, jnp.minimum(n, n_slabs - 1))
    dnmap = lambda n: (jnp.minimum(n, n_slabs - 1), 0)
    tab0 = jnp.stack([cos[0:BLOCK], sa[0:BLOCK], sb[0:BLOCK]])
    in_specs = [
        pl.BlockSpec(memory_space=pltpu.SMEM),
        pl.BlockSpec((nb, BLOCK, D_MODEL), xmap),
        _const_spec((BLOCK, D_MODEL)),
        _const_spec(tab0.shape),
        pl.BlockSpec((BLOCK, LANES), tmap),
        pl.BlockSpec((BLOCK, LANES), tmap),
        pl.BlockSpec((BLOCK, LANES), tmap),
        pl.BlockSpec((1, 2 * BLOCK, 2 * BLOCK), bmap),
        _const_spec((ROWS, ROWS)),
        _const_spec((ROWS, ROWS)),
    ] + [_const_spec(a.shape) for a in p] + [
        pl.BlockSpec((D_MODEL, LANES), upmap),
        pl.BlockSpec((LANES, D_MODEL), dnmap),
    ]
    out_shape = (
        jax.ShapeDtypeStruct((nb, seq, D_MODEL), F32),
        jax.ShapeDtypeStruct((nb, BLOCK, KV_WIDTH), F32),
        jax.ShapeDtypeStruct((nb, BLOCK, KV_WIDTH), F32),
        jax.ShapeDtypeStruct((nb, N_STATE), F32),
        jax.ShapeDtypeStruct((nb, N_STATE), F32),
        jax.ShapeDtypeStruct(w_up.shape, BF16),
        jax.ShapeDtypeStruct(w_down.shape, BF16),
    )
    out_specs = (
        pl.BlockSpec((nb, BLOCK, D_MODEL), xmap),
        _const_spec((nb, BLOCK, KV_WIDTH)),
        _const_spec((nb, BLOCK, KV_WIDTH)),
        _const_spec((nb, N_STATE)),
        _const_spec((nb, N_STATE)),
        pl.BlockSpec((D_MODEL, LANES), upmap),
        pl.BlockSpec((LANES, D_MODEL), dnmap),
    )
    scratch = [
        pltpu.VMEM((nb, 4, 2 * BLOCK, LANES), BF16),
        pltpu.VMEM((nb, 4, 2 * BLOCK, LANES), BF16),
        pltpu.VMEM((nb, BLOCK, KV_WIDTH), F32),
        pltpu.VMEM((nb, BLOCK, KV_WIDTH), F32),
        pltpu.VMEM((N_LB, 2 * ROWS, LANES), F32),
        pltpu.VMEM((N_LB, 2 * nb, LANES), F32),
        pltpu.VMEM((ROWS, ATT_WIDTH), F32),
    ]
    return pl.pallas_call(
        _prompt_mixer_kernel,
        grid=(n_blocks,),
        in_specs=in_specs,
        out_specs=out_specs,
        out_shape=out_shape,
        scratch_shapes=scratch,
        compiler_params=pltpu.CompilerParams(
            dimension_semantics=("arbitrary",), vmem_limit_bytes=VMEM_LIMIT),
        name="prompt_mixer",
    )(sinks, x_prompt, meta_blk, tab0, cos, sa, sb, _window_bias(), perm, perm.T, *p,
      w_up, w_down)


GROUP = 8
GROUPS_PER_STEP = 4


def _decode_mixer_kernel(
        x_ref, cos_ref, sa_ref, sb_ref, cost_ref, sat_ref, sbt_ref, sink_ref, perm_ref, permt_ref,
        ck_ref, cv_ref, stre_ref, stim_ref, wkt_ref, wvt_ref,
        gpre_ref, win_ref, ar_ref, ai_ref, wbre_ref, wbim_ref, wcre_ref, wcim_ref,
        d_ref, wglu_ref, gatt_ref, gssm_ref, wout_ref, gpost_ref,
        h1_ref, kwin_ref, vwin_ref, sre_ref, sim_ref,
        qp, knew, vnew, knew_b, vnew_b, ssm_n, att_ref, s_ref):
    step = pl.program_id(0)
    n_seq = stre_ref.shape[0]
    n_tok = ROWS // n_seq
    grp_rows = GROUP * n_tok
    last = pl.num_programs(0) - 1

    @pl.when(step == 0)
    def _project_and_ssm():
        hn = _rms(x_ref[...], gpre_ref[...]).astype(BF16)
        q = _dot(hn, win_ref[:, 0:ATT_WIDTH])
        u = _project_u(_dot(perm_ref[...], hn).astype(BF16), win_ref)
        cos, sa, sb = cos_ref[...], sa_ref[...], sb_ref[...]
        kt = _dot_nt(wkt_ref[...], hn)
        kt = (kt * cost_ref[...] + pltpu.roll(kt, ROPE_HALF, axis=0) * sat_ref[...]
              + pltpu.roll(kt, KV_WIDTH - ROPE_HALF, axis=0) * sbt_ref[...])
        vt = _dot_nt(wvt_ref[...], hn)
        knew[...] = kt
        vnew[...] = vt
        knew_b[...] = kt.astype(BF16)
        vnew_b[...] = vt.astype(BF16)
        lo = _lane_is_lo((ROWS, LANES))
        for j in range(ATT_WIDTH // LANES):
            qj = _rope(q[:, j * LANES:(j + 1) * LANES], cos, sa, sb)
            qr = pltpu.roll(qj, HALF, axis=1)
            z = jnp.zeros_like(qj)
            if j < 2:
                qp[2 * j] = jnp.where(lo, qj, z)
                qp[2 * j + 1] = jnp.where(lo, qr, z)
            else:
                qp[2 * j] = jnp.where(lo, z, qr)
                qp[2 * j + 1] = jnp.where(lo, z, qj)

        _ssm_input(u, wbre_ref, wbim_ref, s_ref)
        for l in range(N_LB):
            cols = slice(l * LANES, (l + 1) * LANES)
            ar, ai = ar_ref[:, cols], ai_ref[:, cols]
            hr, hi = stre_ref[:, cols], stim_ref[:, cols]
            for t in range(n_tok):
                r_re = slice(t * n_seq, (t + 1) * n_seq)
                r_im = slice(ROWS + t * n_seq, ROWS + (t + 1) * n_seq)
                hr, hi = (ar * hr - ai * hi + s_ref[l, r_re, :],
                          ar * hi + ai * hr + s_ref[l, r_im, :])
                s_ref[l, r_re, :] = hr
                s_ref[l, r_im, :] = hi
            sre_ref[:, cols] = hr
            sim_ref[:, cols] = hi
        ssm_o = _ssm_output(s_ref, u, wcre_ref, wcim_ref, d_ref, wglu_ref)
        ssm_n[...] = _dot(permt_ref[...], _rms(ssm_o, gssm_ref[...]).astype(BF16)).astype(BF16)

    def attend_group(g, base):
        grp = pl.ds(pl.multiple_of(g * grp_rows, grp_rows), grp_rows)
        qb = jnp.concatenate([qp[h, grp, :] for h in range(N_HEADS)], axis=0).astype(BF16)
        seq_rows = [slice(base + b * KV_WIDTH, base + (b + 1) * KV_WIDTH) for b in range(GROUP)]
        kc_f = [ck_ref[r, :] for r in seq_rows]
        vc_f = [cv_ref[r, :] for r in seq_rows]
        kcat = jnp.concatenate(kc_f, axis=1).astype(BF16)
        vcat = jnp.concatenate(vc_f, axis=1).astype(BF16)
        n_q = N_HEADS * grp_rows
        n_c = GROUP * WINDOW

        r_c = lax.broadcasted_iota(jnp.int32, (n_q, n_c), 0)
        c_c = lax.broadcasted_iota(jnp.int32, (n_q, n_c), 1)
        mask_c = ((_div(c_c, WINDOW) == _mod(_div(r_c, n_tok), GROUP))
                  & (_mod(c_c, WINDOW) > _mod(r_c, n_tok)))
        lc = jnp.where(mask_c, _dot(qb, kcat), NEG)
        tile = pl.ds(pl.multiple_of(_div(g * grp_rows, LANES) * LANES, LANES), LANES)
        first = _mod(g * grp_rows, LANES)
        r_n = lax.broadcasted_iota(jnp.int32, (n_q, LANES), 0)
        c_n = lax.broadcasted_iota(jnp.int32, (n_q, LANES), 1) - first
        mask_n = ((c_n >= 0) & (c_n < grp_rows)
                  & (_div(c_n, n_tok) == _mod(_div(r_n, n_tok), GROUP))
                  & (_mod(c_n, n_tok) <= _mod(r_n, n_tok)))
        ln = jnp.where(mask_n, _dot(qb, knew_b[:, tile]), NEG)

        sink_col = sink_ref[...]
        m = jnp.maximum(jnp.maximum(jnp.max(lc, axis=-1, keepdims=True),
                                    jnp.max(ln, axis=-1, keepdims=True)), sink_col)
        ec = jnp.exp(lc - m)
        en = jnp.exp(ln - m)
        den = (jnp.sum(ec, axis=-1, keepdims=True) + jnp.sum(en, axis=-1, keepdims=True)
               + jnp.exp(sink_col - m))
        o = (_dot_nt(ec.astype(BF16), vcat) + _dot_nt(en.astype(BF16), vnew_b[:, tile])) / den

        lo = _lane_is_lo((grp_rows, LANES))
        for j in range(ATT_WIDTH // LANES):
            o_even = o[(2 * j) * grp_rows:(2 * j + 1) * grp_rows]
            o_odd = o[(2 * j + 1) * grp_rows:(2 * j + 2) * grp_rows]
            if j < 2:
                pair = jnp.where(lo, o_even, pltpu.roll(o_odd, HALF, axis=1))
            else:
                pair = jnp.where(lo, pltpu.roll(o_even, HALF, axis=1), o_odd)
            att_ref[grp, j * LANES:(j + 1) * LANES] = pair

        keep = WINDOW - n_tok
        to_tail = _mod(keep - first + LANES, LANES)
        kt_g = pltpu.roll(knew[:, tile], to_tail, axis=1)
        vt_g = pltpu.roll(vnew[:, tile], to_tail, axis=1)
        is_new = lax.broadcasted_iota(jnp.int32, (KV_WIDTH, WINDOW), 1) >= keep
        for b, rows in enumerate(seq_rows):
            k_b = kt_g if b == 0 else pltpu.roll(kt_g, LANES - n_tok * b, axis=1)
            v_b = vt_g if b == 0 else pltpu.roll(vt_g, LANES - n_tok * b, axis=1)
            kwin_ref[rows, :] = jnp.where(is_new, k_b, pltpu.roll(kc_f[b], keep, axis=1))
            vwin_ref[rows, :] = jnp.where(is_new, v_b, pltpu.roll(vc_f[b], keep, axis=1))

    for sub in range(GROUPS_PER_STEP):
        attend_group(step * GROUPS_PER_STEP + sub, sub * GROUP * KV_WIDTH)

    @pl.when(step == last)
    def _merge_out():
        h1_ref[...] = _merge(x_ref[...], att_ref[...], ssm_n[...], gatt_ref, wout_ref, gpost_ref)


def _decode_mixer(x2d, tables, tables_t, sink_col, ck2d, cv2d, st_re, st_im, wkt, wvt, p):
    n_seq = st_re.shape[0]
    n_steps = n_seq // (GROUP * GROUPS_PER_STEP)
    cmap = lambda g: (g, 0)
    perm = _row_permutation(n_seq, ROWS // n_seq)
    head = (x2d, *tables, *tables_t, sink_col, perm, perm.T)
    in_specs = [_const_spec(a.shape) for a in head] + [
        pl.BlockSpec((GROUPS_PER_STEP * GROUP * KV_WIDTH, WINDOW), cmap),
        pl.BlockSpec((GROUPS_PER_STEP * GROUP * KV_WIDTH, WINDOW), cmap),
    ] + [_const_spec(a.shape) for a in (st_re, st_im, wkt, wvt, *p)]
    out_shape = (
        jax.ShapeDtypeStruct(x2d.shape, F32),
        jax.ShapeDtypeStruct(ck2d.shape, F32),
        jax.ShapeDtypeStruct(cv2d.shape, F32),
        jax.ShapeDtypeStruct(st_re.shape, F32),
        jax.ShapeDtypeStruct(st_im.shape, F32),
    )
    out_specs = (
        _const_spec(x2d.shape),
        pl.BlockSpec((GROUPS_PER_STEP * GROUP * KV_WIDTH, WINDOW), cmap),
        pl.BlockSpec((GROUPS_PER_STEP * GROUP * KV_WIDTH, WINDOW), cmap),
        _const_spec(st_re.shape), _const_spec(st_im.shape),
    )
    scratch = [
        pltpu.VMEM((N_HEADS, ROWS, LANES), F32),
        pltpu.VMEM((KV_WIDTH, ROWS), F32),
        pltpu.VMEM((KV_WIDTH, ROWS), F32),
        pltpu.VMEM((KV_WIDTH, ROWS), BF16),
        pltpu.VMEM((KV_WIDTH, ROWS), BF16),
        pltpu.VMEM((ROWS, SSM_WIDTH), BF16),
        pltpu.VMEM((ROWS, ATT_WIDTH), F32),
        pltpu.VMEM((N_LB, 2 * ROWS, LANES), F32),
    ]
    return pl.pallas_call(
        _decode_mixer_kernel,
        grid=(n_steps,),
        in_specs=in_specs,
        out_specs=out_specs,
        out_shape=out_shape,
        scratch_shapes=scratch,
        compiler_params=pltpu.CompilerParams(
            dimension_semantics=("arbitrary",), vmem_limit_bytes=VMEM_LIMIT),
        name="decode_mixer",
    )(*head, ck2d, cv2d, st_re, st_im, wkt, wvt, *p)


def _mlp_kernel(xp_ref, xd_ref, gpre_ref, wup_ref, wdn_ref, gpost_ref, op_ref, od_ref):
    i = pl.program_id(0)

    def mlp(x):
        hn = _rms(x, gpre_ref[...]).astype(BF16)
        acc = None
        for c in range(D_FF // FF_CHUNK):
            cols = slice(c * FF_CHUNK, (c + 1) * FF_CHUNK)
            a = jnp.maximum(_dot(hn, wup_ref[:, cols]), 0.0)
            part = _dot((a * a).astype(BF16), wdn_ref[cols, :])
            acc = part if acc is None else acc + part
        return x + _rms(acc, gpost_ref[...])

    @pl.when(i == 0)
    def _decode_rows():
        od_ref[...] = mlp(xd_ref[...])

    @pl.when(i > 0)
    def _prompt_tile():
        op_ref[...] = mlp(xp_ref[...])


def _mlp(xp2d, xd2d, g_pre, w_up, w_down, g_post):
    n_tiles = xp2d.shape[0] // MLP_ROWS
    pmap = lambda i: (jnp.maximum(i - 1, 0), 0)
    return pl.pallas_call(
        _mlp_kernel,
        grid=(n_tiles + 1,),
        in_specs=[pl.BlockSpec((MLP_ROWS, D_MODEL), pmap), _const_spec(xd2d.shape),
                  _const_spec(g_pre.shape), _const_spec(w_up.shape), _const_spec(w_down.shape),
                  _const_spec(g_post.shape)],
        out_specs=(pl.BlockSpec((MLP_ROWS, D_MODEL), pmap), _const_spec(xd2d.shape)),
        out_shape=(jax.ShapeDtypeStruct(xp2d.shape, F32), jax.ShapeDtypeStruct(xd2d.shape, F32)),
        compiler_params=pltpu.CompilerParams(
            dimension_semantics=("arbitrary",), vmem_limit_bytes=VMEM_LIMIT),
        name="mlp",
    )(xp2d, xd2d, g_pre, w_up, w_down, g_post)


def _zoh(a_re, a_im, log_dt, b_re, b_im):
    dt = jnp.exp(log_dt)[:, None]
    mag = jnp.exp(a_re * dt)
    abar_re, abar_im = mag * jnp.cos(a_im * dt), mag * jnp.sin(a_im * dt)
    nr, ni = abar_re - 1.0, abar_im
    den = a_re * a_re + a_im * a_im
    coef_re = (nr * a_re + ni * a_im) / den
    coef_im = (ni * a_re - nr * a_im) / den
    bbar_re = coef_re[..., None] * b_re - coef_im[..., None] * b_im
    bbar_im = coef_re[..., None] * b_im + coef_im[..., None] * b_re
    return abar_re, abar_im, bbar_re, bbar_im


def _same_group_mask():
    row_g = np.arange(SSM_TILE)[:, None] // SSM_GROUP
    col_g = np.arange(STATE_TILE)[None, :] // SSM_STATE
    return row_g == col_g


def _block_diag_in(bbar):
    w = jnp.swapaxes(bbar, 1, 2).reshape(N_SSM_TILES, SSM_TILE, SSM_STATE)
    w = jnp.tile(w, (1, 1, SSM_TILE // SSM_GROUP))
    return jnp.where(_same_group_mask()[None], w, 0.0).astype(BF16)


def _block_diag_out(c):
    w = jnp.swapaxes(c, 1, 2).reshape(N_SSM_TILES, STATE_TILE, SSM_GROUP)
    w = jnp.tile(w, (1, 1, SSM_TILE // SSM_GROUP))
    return jnp.where(_same_group_mask().T[None], w, 0.0).astype(BF16)


def _rope_tables(pos):
    half = ROPE_DIM // 2
    inv = ROPE_THETA ** (-np.arange(half, dtype=np.float64) / half)
    ang = np.asarray(pos, dtype=np.float64)[:, None] * inv[None, :]
    cos, sin = np.cos(ang), np.sin(ang)
    d = np.arange(LANES) % HEAD_DIM
    f = d % half
    cos_t = np.where(d[None, :] < ROPE_DIM, cos[:, f], 1.0)
    sa_t = np.where((d[None, :] >= half) & (d[None, :] < ROPE_DIM), sin[:, f], 0.0)
    sb_t = np.where(d[None, :] < half, -sin[:, f], 0.0)
    return tuple(t.astype(np.float32) for t in (cos_t, sa_t, sb_t))


def _row_permutation(n_outer, n_inner):
    r = np.arange(n_outer * n_inner)
    perm = np.zeros((r.size, r.size), np.float32)
    perm[r, (r % n_outer) * n_inner + r // n_outer] = 1.0
    return jnp.asarray(perm, dtype=BF16)


def kernel(x_prompt, x_sample, cache_k_win, cache_v_win, state_ssm_re, state_ssm_im, meta_tokens, norm_mix_pre, w_in, attn_sinks, ssm_a_re, ssm_a_im, ssm_log_dt, ssm_b_re, ssm_b_im, ssm_c_re, ssm_c_im, ssm_d, w_glu, norm_att_out, norm_ssm_out, w_out, norm_mix_post, norm_mlp_pre, w_up, w_down, norm_mlp_post):
    depth = w_in.shape[0]
    assert depth == 1
    l = 0
    nb, seq, _ = x_prompt.shape
    n_seq, n_tok, _ = x_sample.shape
    assert n_seq * n_tok == ROWS and nb * BLOCK == ROWS and seq % BLOCK == 0

    abar_re, abar_im, bbar_re, bbar_im = _zoh(
        ssm_a_re[l], ssm_a_im[l], ssm_log_dt[l], ssm_b_re[l], ssm_b_im[l])
    row = lambda a: a.reshape(1, -1)
    q_scale = np.where(np.arange(w_in.shape[-1]) < ATT_WIDTH, HEAD_DIM ** -0.5, 1.0).astype(
        np.float32)
    params = (
        row(norm_mix_pre[l]), (w_in[l] * q_scale).astype(BF16),
        row(abar_re), row(abar_im),
        _block_diag_in(bbar_re), _block_diag_in(bbar_im),
        _block_diag_out(ssm_c_re[l]), _block_diag_out(-ssm_c_im[l]),
        row(ssm_d[l]), w_glu[l].astype(BF16),
        row(norm_att_out[l]), row(norm_ssm_out[l]), w_out[l].astype(BF16), row(norm_mix_post[l]),
    )

    front = BLOCK - N_META
    meta_blk = jnp.concatenate([jnp.zeros((front, D_MODEL), F32), meta_tokens], axis=0)
    cos, sa, sb = (jnp.asarray(t) for t in _rope_tables(np.arange(seq + BLOCK) - front))
    h1, k_last, v_last, p_re, p_im, w_up_b, w_down_b = _prompt_mixer(
        x_prompt, meta_blk, cos, sa, sb, attn_sinks[l], params, w_up[l], w_down[l])
    mlp_params = (row(norm_mlp_pre[l]), w_up_b, w_down_b, row(norm_mlp_post[l]))

    tabs = _rope_tables(PAST_LEN + np.arange(n_tok))
    tables = tuple(jnp.asarray(np.tile(t, (n_seq, 1))) for t in tabs)
    tables_t = tuple(jnp.asarray(np.tile(t.T, (1, n_seq))) for t in tabs)
    sink_col = jnp.repeat(attn_sinks[l], n_tok * GROUP).reshape(-1, 1)
    to_t = lambda a: jnp.transpose(a, (0, 2, 3, 1)).reshape(-1, WINDOW)
    from_t = lambda a, n: jnp.transpose(
        a.reshape(n, N_KV_HEADS, HEAD_DIM, WINDOW), (0, 3, 1, 2))[None]
    w_kv = w_in[l][:, ATT_WIDTH:ATT_WIDTH + 2 * KV_WIDTH].astype(BF16)
    h1s, kwin, vwin, s_re, s_im = _decode_mixer(
        x_sample.reshape(ROWS, D_MODEL), tables, tables_t, sink_col, to_t(cache_k_win[l]), to_t(cache_v_win[l]),
        state_ssm_re[l].reshape(n_seq, N_STATE), state_ssm_im[l].reshape(n_seq, N_STATE),
        w_kv[:, 0:KV_WIDTH].T, w_kv[:, KV_WIDTH:].T, params)
    y_prompt, ys = _mlp(h1.reshape(nb * seq, D_MODEL), h1s, *mlp_params)
    y_prompt = y_prompt.reshape(nb, seq, D_MODEL)
    y_sample = ys.reshape(n_seq, n_tok, D_MODEL)

    win = from_t
    st = lambda a, n: a.reshape(1, n, SSM_GROUPS, SSM_STATE)
    return (y_prompt, y_sample,
            win(k_last, nb), win(v_last, nb), st(p_re, nb), st(p_im, nb),
            win(kwin, n_seq), win(vwin, n_seq), st(s_re, n_seq), st(s_im, n_seq))
```

```python
import math

import jax
import jax.numpy as jnp
import numpy as np
from jax import lax
from jax.experimental import pallas as pl
from jax.experimental.pallas import tpu as pltpu

F32 = jnp.float32
BF16 = jnp.bfloat16

N_META = 16
HEAD_DIM = 64
N_HEADS = 8
N_KV_HEADS = 2
WINDOW = 128
BLOCK = 128
ROPE_DIM = 16
ROPE_HALF = ROPE_DIM // 2
ROPE_THETA = 500000.0
SSM_GROUP = 16
SSM_GROUPS = 32
SSM_STATE = 64
PAST_LEN = 8192
EPS = 1e-6
NEG = -1e30

D_MODEL = 1024
ATT_WIDTH = 512
KV_WIDTH = 128
SSM_WIDTH = 512
N_STATE = SSM_GROUPS * SSM_STATE
D_FF = 4096

ROWS = 512
MLP_ROWS = 1024
LANES = 128
HALF = LANES // 2
SSM_TILE = 256
STATE_TILE = (SSM_TILE // SSM_GROUP) * SSM_STATE
N_SSM_TILES = SSM_WIDTH // SSM_TILE
LB_PER_TILE = STATE_TILE // LANES
N_LB = N_STATE // LANES
FF_CHUNK = 1024
VMEM_LIMIT = 56 * 1024 * 1024


def _dot(a, b):
    return jnp.dot(a, b, preferred_element_type=F32)


def _dot_nt(a, b):
    return lax.dot_general(a, b, (((1,), (1,)), ((), ())), preferred_element_type=F32)


def _rms(x, g):
    return x * lax.rsqrt(jnp.mean(x * x, axis=-1, keepdims=True) + EPS) * g


def _rope(x, cos, sa, sb):
    return (x * cos + pltpu.roll(x, ROPE_HALF, axis=1) * sa
            + pltpu.roll(x, LANES - ROPE_HALF, axis=1) * sb)


def _lane_is_lo(shape):
    return lax.broadcasted_iota(jnp.int32, shape, 1) < HALF


def _project_qkv(hn, w_in_ref):
    q = _dot(hn, w_in_ref[:, 0:ATT_WIDTH])
    kv = _dot(hn, w_in_ref[:, ATT_WIDTH:ATT_WIDTH + 2 * KV_WIDTH])
    return q, kv[:, 0:KV_WIDTH], kv[:, KV_WIDTH:2 * KV_WIDTH]


def _project_u(hn, w_in_ref):
    return _dot(hn, w_in_ref[:, ATT_WIDTH + 2 * KV_WIDTH:])


def _ssm_input_tile(ub, c, wbre_ref, wbim_ref, s_ref):
    uc = ub[:, c * SSM_TILE:(c + 1) * SSM_TILE]
    bre, bim = _dot(uc, wbre_ref[c]), _dot(uc, wbim_ref[c])
    for l in range(LB_PER_TILE):
        s_ref[c * LB_PER_TILE + l, 0:ROWS, :] = bre[:, l * LANES:(l + 1) * LANES]
        s_ref[c * LB_PER_TILE + l, ROWS:2 * ROWS, :] = bim[:, l * LANES:(l + 1) * LANES]


def _ssm_input(u, wbre_ref, wbim_ref, s_ref):
    ub = u.astype(BF16)
    for c in range(N_SSM_TILES):
        _ssm_input_tile(ub, c, wbre_ref, wbim_ref, s_ref)


def _ssm_readout_tile(s_ref, c, wcre_ref, wcim_ref):
    blocks = range(c * LB_PER_TILE, (c + 1) * LB_PER_TILE)
    hr = jnp.concatenate([s_ref[l, 0:ROWS, :].astype(BF16) for l in blocks], axis=1)
    hi = jnp.concatenate([s_ref[l, ROWS:2 * ROWS, :].astype(BF16) for l in blocks], axis=1)
    return _dot(hr, wcre_ref[c]) + _dot(hi, wcim_ref[c])


def _ssm_output(s_ref, u, wcre_ref, wcim_ref, d_ref, wglu_ref):
    ys = [_ssm_readout_tile(s_ref, c, wcre_ref, wcim_ref) for c in range(N_SSM_TILES)]
    return _ssm_gate(ys, u, d_ref, wglu_ref)


def _ssm_gate(ys, u, d_ref, wglu_ref):
    y = jnp.concatenate(ys, axis=1) + d_ref[...] * u
    z = 0.5 * y * (1.0 + jnp.tanh(math.sqrt(2.0 / math.pi) * (y + 0.044715 * (y * y * y))))
    gate = 1.0 / (1.0 + jnp.exp(-_dot(z.astype(BF16), wglu_ref[...])))
    return z * gate


def _merge(x, att, s, gatt_ref, wout_ref, gpost_ref):
    a = _rms(att, gatt_ref[...]).astype(BF16)
    m = _dot(a, wout_ref[0:ATT_WIDTH, :]) + _dot(s, wout_ref[ATT_WIDTH:, :])
    return x + _rms(m, gpost_ref[...])


def _div(x, k):
    return lax.shift_right_logical(x, int(math.log2(k)))


def _mod(x, k):
    return lax.bitwise_and(x, k - 1)


def _sink_softmax(logits, bias, sink_col):
    lm = logits + bias
    m =jnp.maximum(jnp.max(lm, axis=-1, keepdims=True), sink_col)
    e = jnp.exp(lm - m)
    den = jnp.sum(e, axis=-1, keepdims=True) + jnp.exp(sink_col - m)
    return e, den


def _prompt_mixer_kernel(
        sink_ref, x_ref, meta_ref, tab0_ref, cos_ref, sa_ref, sb_ref, bias_ref, perm_ref, permt_ref,
        gpre_ref, win_ref,
        ar_ref, ai_ref, wbre_ref, wbim_ref, wcre_ref, wcim_ref, d_ref, wglu_ref,
        gatt_ref, gssm_ref, wout_ref, gpost_ref, wupf_ref, wdnf_ref,
        h1_ref, klast_ref, vlast_ref, sre_ref, sim_ref, wupb_ref, wdnb_ref,
        kbuf, vbuf, kcur, vcur, s_ref, hstate, att_ref):
    n = pl.program_id(0)
    wupb_ref[...] = wupf_ref[...].astype(BF16)
    wdnb_ref[...] = wdnf_ref[...].astype(BF16)
    nb = x_ref.shape[0]
    assert 2 * nb == 8
    last = pl.num_programs(0) - 1
    lo = _lane_is_lo((BLOCK, LANES))
    is_re = lax.broadcasted_iota(jnp.int32, (2 * nb, LANES), 0) < nb
    a1, a2 = [], []
    for l in range(N_LB):
        cols = slice(l * LANES, (l + 1) * LANES)
        ai = jnp.broadcast_to(ai_ref[:, cols], (2 * nb, LANES))
        a1.append(jnp.broadcast_to(ar_ref[:, cols], (2 * nb, LANES)))
        a2.append(jnp.where(is_re, -ai, ai))

    def swap(t):
        return pltpu.roll(t, nb, axis=0)

    def variants(t):
        tr = pltpu.roll(t, HALF, axis=1)
        z = jnp.zeros_like(t)
        return (jnp.where(lo, t, z), jnp.where(lo, z, tr),
                jnp.where(lo, tr, z), jnp.where(lo, z, t))

    @pl.when(n == 0)
    def _meta_block():
        kbuf[...] = jnp.zeros_like(kbuf)
        vbuf[...] = jnp.zeros_like(vbuf)
        hm = _rms(meta_ref[...], gpre_ref[...]).astype(BF16)
        _, k0, v0 = _project_qkv(hm, win_ref)
        k0 = _rope(k0, tab0_ref[0], tab0_ref[1], tab0_ref[2])
        for i, (kv_, vv_) in enumerate(zip(variants(k0), variants(v0))):
            for b in range(nb):
                kbuf[b, i, 0:BLOCK, :] = kv_.astype(BF16)
                vbuf[b, i, 0:BLOCK, :] = vv_.astype(BF16)
        um = _project_u(hm, win_ref).astype(BF16)
        for c in range(N_SSM_TILES):
            uc = um[:, c * SSM_TILE:(c + 1) * SSM_TILE]
            bre, bim = _dot(uc, wbre_ref[c]), _dot(uc, wbim_ref[c])
            for j in range(LB_PER_TILE):
                l = c * LB_PER_TILE + j
                cols = slice(j * LANES, (j + 1) * LANES)
                h = jnp.zeros((2 * nb, LANES), F32)
                for t in range(BLOCK - N_META, BLOCK):
                    x_t = jnp.where(is_re, jnp.broadcast_to(bre[t:t + 1, cols], h.shape),
                                    jnp.broadcast_to(bim[t:t + 1, cols], h.shape))
                    h = a1[l] * h + a2[l] * swap(h) + x_t
                hstate[l] = h

    x = x_ref[...].reshape(ROWS, D_MODEL)
    hn = _rms(x, gpre_ref[...]).astype(BF16)
    cos, sa, sb = cos_ref[...], sa_ref[...], sb_ref[...]

    u = _project_u(_dot(perm_ref[...], hn).astype(BF16), win_ref)

    hfin = []

    def chains(blocks):
        for l in blocks:
            h = hstate[l]
            g = swap(h)
            a2n = -a2[l]
            for i in range(BLOCK // 2):
                r_re = slice(i * 2 * nb, (i + 1) * 2 * nb)
                r_im = slice(ROWS + i * 2 * nb, ROWS + (i + 1) * 2 * nb)
                re, im_s = s_ref[l, r_re, :], swap(s_ref[l, r_im, :])
                h0 = a1[l] * h + a2[l] * g + jnp.where(is_re, re, im_s)
                g0 = swap(h0)
                g = a1[l] * g0 + a2n * h0 + jnp.where(is_re, im_s, re)
                h = swap(g)
                s_ref[l, r_re, :] = jnp.where(is_re, h0, g)
                s_ref[l, r_im, :] = jnp.where(is_re, g0, h)
            hstate[l] = h
            hfin.append(h)

    mask = bias_ref[0]
    cur = pl.ds(pl.multiple_of(_mod(n + 1, 2) * BLOCK, BLOCK), BLOCK)
    top = lax.broadcasted_iota(jnp.int32, (2 * BLOCK, 1), 0) < BLOCK

    def attend(b):
        rows = slice(b * BLOCK, (b + 1) * BLOCK)
        kb = _rope(k[rows], cos, sa, sb)
        vb = v[rows]
        for i, (kv_, vv_) in enumerate(zip(variants(kb), variants(vb))):
            kbuf[b, i, cur, :] = kv_.astype(BF16)
            vbuf[b, i, cur, :] = vv_.astype(BF16)

        kcur[b] = kb
        vcur[b] = vb

        qs = []
        for j in range(ATT_WIDTH // LANES):
            qs.append(_rope(q[rows, j * LANES:(j + 1) * LANES], cos, sa, sb).astype(BF16))
        for g in range(N_KV_HEADS):
            qst = jnp.concatenate([qs[2 * g], qs[2 * g + 1]], axis=0)
            o = None
            for half in range(2):
                var = 2 * g + half
                h_top, h_bot = 4 * g + half, 4 * g + 2 + half
                sink_col = jnp.where(top, sink_ref[h_top], sink_ref[h_bot])
                e, den = _sink_softmax(_dot_nt(qst, kbuf[b, var]), mask, sink_col)
                part = _dot(e.astype(BF16), vbuf[b, var]) / den
                o = part if o is None else o + part
            att_ref[rows, (2 * g) * LANES:(2 * g + 1) * LANES] = o[0:BLOCK]
            att_ref[rows, (2 * g + 1) * LANES:(2 * g + 2) * LANES] = o[BLOCK:2 * BLOCK]

    _ssm_input(u, wbre_ref, wbim_ref, s_ref)
    chains(range(N_LB))
    q, k, v = _project_qkv(hn, win_ref)
    ssm_o = _ssm_output(s_ref, u, wcre_ref, wcim_ref, d_ref, wglu_ref)
    ssm_n = _dot(permt_ref[...], _rms(ssm_o, gssm_ref[...]).astype(BF16)).astype(BF16)
    for b in range(nb):
        attend(b)

    h1 = _merge(x, att_ref[...], ssm_n, gatt_ref, wout_ref, gpost_ref)
    h1_ref[...] = h1.reshape(h1_ref.shape)

    @pl.when(n == last)
    def _emit_state():
        for b in range(nb):
            klast_ref[b] = kcur[b].T
            vlast_ref[b] = vcur[b].T
        for l in range(N_LB):
            sre_ref[:, l * LANES:(l + 1) * LANES] = hfin[l][0:nb]
            sim_ref[:, l * LANES:(l + 1) * LANES] = hfin[l][nb:2 * nb]


def _window_bias():
    r = (np.arange(2 * BLOCK) % BLOCK)[:, None]
    phys = np.arange(2 * BLOCK)[None, :]
    tables = []
    for c_min in (2 * BLOCK - N_META, BLOCK - N_META, 0):
        for parity in (0, 1):
            c = phys if parity == 1 else (phys + BLOCK) % (2 * BLOCK)
            ok = (c > r) & (c <= r + WINDOW) & (c >= c_min)
            tables.append(np.where(ok, 0.0, NEG))
    return jnp.asarray(np.stack(tables), dtype=F32)


def _const_spec(shape):
    zeros = (0,) * len(shape)
    return pl.BlockSpec(shape, lambda *_: zeros)


def _prompt_mixer(x_prompt, meta_blk, cos, sa, sb, sinks, p, w_up, w_down):
    nb, seq, _ = x_prompt.shape
    n_blocks = seq // BLOCK
    perm = _row_permutation(nb, BLOCK)
    xmap = lambda n: (0, n, 0)
    tmap = lambda n: (n + 1, 0)
    bmap = lambda n: (2 * jnp.minimum(n + 1, 2) + (n + 1) % 2, 0, 0)
    n_slabs = D_FF // LANES
    assert n_slabs <= n_blocks
    upmap = lambda n: (0, jnp.minimum(n, n_slabs - 1))
    dnmap = lambda n: (jnp.minimum(n, n_slabs - 1), 0)
    tab0 = jnp.stack([cos[0:BLOCK], sa[0:BLOCK], sb[0:BLOCK]])
    in_specs = [
        pl.BlockSpec(memory_space=pltpu.SMEM),
        pl.BlockSpec((nb, BLOCK, D_MODEL), xmap),
        _const_spec((BLOCK, D_MODEL)),
        _const_spec(tab0.shape),
        pl.BlockSpec((BLOCK, LANES), tmap),
        pl.BlockSpec((BLOCK, LANES), tmap),
        pl.BlockSpec((BLOCK, LANES), tmap),
        pl.BlockSpec((1, 2 * BLOCK, 2 * BLOCK), bmap),
        _const_spec((ROWS, ROWS)),
        _const_spec((ROWS, ROWS)),
    ] + [_const_spec(a.shape) for a in p] + [
        pl.BlockSpec((D_MODEL, LANES), upmap),
        pl.BlockSpec((LANES, D_MODEL), dnmap),
    ]
    out_shape = (
        jax.ShapeDtypeStruct((nb, seq, D_MODEL), F32),
        jax.ShapeDtypeStruct((nb, BLOCK, KV_WIDTH), F32),
        jax.ShapeDtypeStruct((nb, BLOCK, KV_WIDTH), F32),
        jax.ShapeDtypeStruct((nb, N_STATE), F32),
        jax.ShapeDtypeStruct((nb, N_STATE), F32),
        jax.ShapeDtypeStruct(w_up.shape, BF16),
        jax.ShapeDtypeStruct(w_down.shape, BF16),
    )
    out_specs = (
        pl.BlockSpec((nb, BLOCK, D_MODEL), xmap),
        _const_spec((nb, BLOCK, KV_WIDTH)),
        _const_spec((nb, BLOCK, KV_WIDTH)),
        _const_spec((nb, N_STATE)),
        _const_spec((nb, N_STATE)),
        pl.BlockSpec((D_MODEL, LANES), upmap),
        pl.BlockSpec((LANES, D_MODEL), dnmap),
    )
    scratch = [
        pltpu.VMEM((nb, 4, 2 * BLOCK, LANES), BF16),
        pltpu.VMEM((nb, 4, 2 * BLOCK, LANES), BF16),
        pltpu.VMEM((nb, BLOCK, KV_WIDTH), F32),
        pltpu.VMEM((nb, BLOCK, KV_WIDTH), F32),
        pltpu.VMEM((N_LB, 2 * ROWS, LANES), F32),
        pltpu.VMEM((N_LB, 2 * nb, LANES), F32),
        pltpu.VMEM((ROWS, ATT_WIDTH), F32),
    ]
    return pl.pallas_call(
        _prompt_mixer_kernel,
        grid=(n_blocks,),
        in_specs=in_specs,
        out_specs=out_specs,
        out_shape=out_shape,
        scratch_shapes=scratch,
        compiler_params=pltpu.CompilerParams(
            dimension_semantics=("arbitrary",), vmem_limit_bytes=VMEM_LIMIT),
        name="prompt_mixer",
    )(sinks, x_prompt, meta_blk, tab0, cos, sa, sb, _window_bias(), perm, perm.T, *p,
      w_up, w_down)


GROUP = 8
GROUPS_PER_STEP = 4


def _decode_mixer_kernel(
        x_ref, cos_ref, sa_ref, sb_ref, cost_ref, sat_ref, sbt_ref, sink_ref, perm_ref, permt_ref,
        ck_ref, cv_ref, stre_ref, stim_ref, wkt_ref, wvt_ref,
        gpre_ref, win_ref, ar_ref, ai_ref, wbre_ref, wbim_ref, wcre_ref, wcim_ref,
        d_ref, wglu_ref, gatt_ref, gssm_ref, wout_ref, gpost_ref,
        h1_ref, kwin_ref, vwin_ref, sre_ref, sim_ref,
        qp, knew, vnew, knew_b, vnew_b, ssm_n, att_ref, s_ref):
    step = pl.program_id(0)
    n_seq = stre_ref.shape[1]
    n_tok = ROWS // n_seq
    grp_rows = GROUP * n_tok
    last = pl.num_programs(0) - 1

    @pl.when(step == 0)
    def _project_and_ssm():
        hn = _rms(x_ref[...], gpre_ref[...]).astype(BF16)
        q = _dot(hn, win_ref[:, 0:ATT_WIDTH])
        u = _project_u(_dot(perm_ref[...], hn).astype(BF16), win_ref)
        cos, sa, sb = cos_ref[...], sa_ref[...], sb_ref[...]
        kt = _dot_nt(wkt_ref[...], hn)
        kt = (kt * cost_ref[...] + pltpu.roll(kt, ROPE_HALF, axis=0) * sat_ref[...]
              + pltpu.roll(kt, KV_WIDTH - ROPE_HALF, axis=0) * sbt_ref[...])
        vt = _dot_nt(wvt_ref[...], hn)
        knew[...] = kt
        vnew[...] = vt
        knew_b[...] = kt.astype(BF16)
        vnew_b[...] = vt.astype(BF16)
        lo = _lane_is_lo((ROWS, LANES))
        for j in range(ATT_WIDTH // LANES):
            qj = _rope(q[:, j * LANES:(j + 1) * LANES], cos, sa, sb)
            qr = pltpu.roll(qj, HALF, axis=1)
            z = jnp.zeros_like(qj)
            if j < 2:
                qp[2 * j] = jnp.where(lo, qj, z)
                qp[2 * j + 1] = jnp.where(lo, qr, z)
            else:
                qp[2 * j] = jnp.where(lo, z, qr)
                qp[2 * j + 1] = jnp.where(lo, z, qj)

        _ssm_input(u, wbre_ref, wbim_ref, s_ref)
        for l in range(N_LB):
            cols = slice(l * LANES, (l + 1) * LANES)
            ar, ai = ar_ref[:, cols], ai_ref[:, cols]
            hr, hi = stre_ref[cols, :].T, stim_ref[cols, :].T
            for t in range(n_tok):
                r_re = slice(t * n_seq, (t + 1) * n_seq)
                r_im = slice(ROWS + t * n_seq, ROWS + (t + 1) * n_seq)
                hr, hi = (ar * hr - ai * hi + s_ref[l, r_re, :],
                          ar * hi + ai * hr + s_ref[l, r_im, :])
                s_ref[l, r_re, :] = hr
                s_ref[l, r_im, :] = hi
            sre_ref[cols, :] = hr.T
            sim_ref[cols, :] = hi.T
        ssm_o = _ssm_output(s_ref, u, wcre_ref, wcim_ref, d_ref, wglu_ref)
        ssm_n[...] = _dot(permt_ref[...], _rms(ssm_o, gssm_ref[...]).astype(BF16)).astype(BF16)

    def attend_group(g, base):
        grp = pl.ds(pl.multiple_of(g * grp_rows, grp_rows), grp_rows)
        qb = jnp.concatenate([qp[h, grp, :] for h in range(N_HEADS)], axis=0).astype(BF16)
        seq_rows = [slice(base + b * KV_WIDTH, base + (b + 1) * KV_WIDTH) for b in range(GROUP)]
        kc_f = [ck_ref[r, :] for r in seq_rows]
        vc_f = [cv_ref[r, :] for r in seq_rows]
        kcat = jnp.concatenate(kc_f, axis=1).astype(BF16)
        vcat = jnp.concatenate(vc_f, axis=1).astype(BF16)
        n_q = N_HEADS * grp_rows
        n_c = GROUP * WINDOW

        r_c = lax.broadcasted_iota(jnp.int32, (n_q, n_c), 0)
        c_c = lax.broadcasted_iota(jnp.int32, (n_q, n_c), 1)
        mask_c = ((_div(c_c, WINDOW) == _mod(_div(r_c, n_tok), GROUP))
                  & (_mod(c_c, WINDOW) > _mod(r_c, n_tok)))
        lc = jnp.where(mask_c, _dot(qb, kcat), NEG)
        tile = pl.ds(pl.multiple_of(_div(g * grp_rows, LANES) * LANES, LANES), LANES)
        first = _mod(g * grp_rows, LANES)
        r_n = lax.broadcasted_iota(jnp.int32, (n_q, LANES), 0)
        c_n = lax.broadcasted_iota(jnp.int32, (n_q, LANES), 1) - first
        mask_n = ((c_n >= 0) & (c_n < grp_rows)
                  & (_div(c_n, n_tok) == _mod(_div(r_n, n_tok), GROUP))
                  & (_mod(c_n, n_tok) <= _mod(r_n, n_tok)))
        ln = jnp.where(mask_n, _dot(qb, knew_b[:, tile]), NEG)

        sink_col = sink_ref[...]
        m = jnp.maximum(jnp.maximum(jnp.max(lc, axis=-1, keepdims=True),
                                    jnp.max(ln, axis=-1, keepdims=True)), sink_col)
        ec = jnp.exp(lc - m)
        en = jnp.exp(ln - m)
        den = (jnp.sum(ec, axis=-1, keepdims=True) + jnp.sum(en, axis=-1, keepdims=True)
               + jnp.exp(sink_col - m))
        o = (_dot_nt(ec.astype(BF16), vcat) + _dot_nt(en.astype(BF16), vnew_b[:, tile])) / den

        lo = _lane_is_lo((grp_rows, LANES))
        for j in range(ATT_WIDTH // LANES):
            o_even = o[(2 * j) * grp_rows:(2 * j + 1) * grp_rows]
            o_odd = o[(2 * j + 1) * grp_rows:(2 * j + 2) * grp_rows]
            if j < 2:
                pair = jnp.where(lo, o_even, pltpu.roll(o_odd, HALF, axis=1))
            else:
                pair = jnp.where(lo, pltpu.roll(o_even, HALF, axis=1), o_odd)
            att_ref[grp, j * LANES:(j + 1) * LANES] = pair

        keep = WINDOW - n_tok
        to_tail = _mod(keep - first + LANES, LANES)
        kt_g = pltpu.roll(knew[:, tile], to_tail, axis=1)
        vt_g = pltpu.roll(vnew[:, tile], to_tail, axis=1)
        is_new = lax.broadcasted_iota(jnp.int32, (KV_WIDTH, WINDOW), 1) >= keep
        for b, rows in enumerate(seq_rows):
            k_b = kt_g if b == 0 else pltpu.roll(kt_g, LANES - n_tok * b, axis=1)
            v_b = vt_g if b == 0 else pltpu.roll(vt_g, LANES - n_tok * b, axis=1)
            kwin_ref[rows, :] = jnp.where(is_new, k_b, pltpu.roll(kc_f[b], keep, axis=1))
            vwin_ref[rows, :] = jnp.where(is_new, v_b, pltpu.roll(vc_f[b], keep, axis=1))

    for sub in range(GROUPS_PER_STEP):
        attend_group(step * GROUPS_PER_STEP + sub, sub * GROUP * KV_WIDTH)

    @pl.when(step == last)
    def _merge_out():
        h1_ref[...] = _merge(x_ref[...], att_ref[...], ssm_n[...], gatt_ref, wout_ref, gpost_ref)


def _decode_mixer(x2d, tables, tables_t, sink_col, ck2d, cv2d, st_re, st_im, wkt, wvt, p):
    n_seq = st_re.shape[1]
    n_steps = n_seq // (GROUP * GROUPS_PER_STEP)
    cmap = lambda g: (g, 0)
    perm = _row_permutation(n_seq, ROWS // n_seq)
    head = (x2d, *tables, *tables_t, sink_col, perm, perm.T)
    in_specs = [_const_spec(a.shape) for a in head] + [
        pl.BlockSpec((GROUPS_PER_STEP * GROUP * KV_WIDTH, WINDOW), cmap),
        pl.BlockSpec((GROUPS_PER_STEP * GROUP * KV_WIDTH, WINDOW), cmap),
    ] + [_const_spec(a.shape) for a in (st_re, st_im, wkt, wvt, *p)]
    out_shape = (
        jax.ShapeDtypeStruct(x2d.shape, F32),
        jax.ShapeDtypeStruct(ck2d.shape, F32),
        jax.ShapeDtypeStruct(cv2d.shape, F32),
        jax.ShapeDtypeStruct(st_re.shape, F32),
        jax.ShapeDtypeStruct(st_im.shape, F32),
    )
    out_specs = (
        _const_spec(x2d.shape),
        pl.BlockSpec((GROUPS_PER_STEP * GROUP * KV_WIDTH, WINDOW), cmap),
        pl.BlockSpec((GROUPS_PER_STEP * GROUP * KV_WIDTH, WINDOW), cmap),
        _const_spec(st_re.shape), _const_spec(st_im.shape),
    )
    scratch = [
        pltpu.VMEM((N_HEADS, ROWS, LANES), F32),
        pltpu.VMEM((KV_WIDTH, ROWS), F32),
        pltpu.VMEM((KV_WIDTH, ROWS), F32),
        pltpu.VMEM((KV_WIDTH, ROWS), BF16),
        pltpu.VMEM((KV_WIDTH, ROWS), BF16),
        pltpu.VMEM((ROWS, SSM_WIDTH), BF16),
        pltpu.VMEM((ROWS, ATT_WIDTH), F32),
        pltpu.VMEM((N_LB, 2 * ROWS, LANES), F32),
    ]
    return pl.pallas_call(
        _decode_mixer_kernel,
        grid=(n_steps,),
        in_specs=in_specs,
        out_specs=out_specs,
        out_shape=out_shape,
        scratch_shapes=scratch,
        compiler_params=pltpu.CompilerParams(
            dimension_semantics=("arbitrary",), vmem_limit_bytes=VMEM_LIMIT),
        name="decode_mixer",
    )(*head, ck2d, cv2d, st_re, st_im, wkt, wvt, *p)


def _mlp_kernel(xp_ref, xd_ref, gpre_ref, wup_ref, wdn_ref, gpost_ref, op_ref, od_ref):
    i = pl.program_id(0)

    def mlp(x):
        hn = _rms(x, gpre_ref[...]).astype(BF16)
        acc = None
        for c in range(D_FF // FF_CHUNK):
            cols = slice(c * FF_CHUNK, (c + 1) * FF_CHUNK)
            a = jnp.maximum(_dot(hn, wup_ref[:, cols]), 0.0)
            part = _dot((a * a).astype(BF16), wdn_ref[cols, :])
            acc = part if acc is None else acc + part
        return x + _rms(acc, gpost_ref[...])

    @pl.when(i == 0)
    def _decode_rows():
        od_ref[...] = mlp(xd_ref[...])

    @pl.when(i > 0)
    def _prompt_tile():
        op_ref[...] = mlp(xp_ref[...])


def _mlp(xp2d, xd2d, g_pre, w_up, w_down, g_post):
    n_tiles = xp2d.shape[0] // MLP_ROWS
    pmap = lambda i: (jnp.maximum(i - 1, 0), 0)
    return pl.pallas_call(
        _mlp_kernel,
        grid=(n_tiles + 1,),
        in_specs=[pl.BlockSpec((MLP_ROWS, D_MODEL), pmap), _const_spec(xd2d.shape),
                  _const_spec(g_pre.shape), _const_spec(w_up.shape), _const_spec(w_down.shape),
                  _const_spec(g_post.shape)],
        out_specs=(pl.BlockSpec((MLP_ROWS, D_MODEL), pmap), _const_spec(xd2d.shape)),
        out_shape=(jax.ShapeDtypeStruct(xp2d.shape, F32), jax.ShapeDtypeStruct(xd2d.shape, F32)),
        compiler_params=pltpu.CompilerParams(
            dimension_semantics=("arbitrary",), vmem_limit_bytes=VMEM_LIMIT),
        name="mlp",
    )(xp2d, xd2d, g_pre, w_up, w_down, g_post)


def _zoh(a_re, a_im, log_dt, b_re, b_im):
    dt = jnp.exp(log_dt)[:, None]
    mag = jnp.exp(a_re * dt)
    abar_re, abar_im = mag * jnp.cos(a_im * dt), mag * jnp.sin(a_im * dt)
    nr, ni = abar_re - 1.0, abar_im
    den = a_re * a_re + a_im * a_im
    coef_re = (nr * a_re + ni * a_im) / den
    coef_im = (ni * a_re - nr * a_im) / den
    bbar_re = coef_re[..., None] * b_re - coef_im[..., None] * b_im
    bbar_im = coef_re[..., None] * b_im + coef_im[..., None] * b_re
    return abar_re, abar_im, bbar_re, bbar_im


def _same_group_mask():
    row_g = np.arange(SSM_TILE)[:, None] // SSM_GROUP
    col_g = np.arange(STATE_TILE)[None, :] // SSM_STATE
    return row_g == col_g


def _block_diag_in(bbar):
    w = jnp.swapaxes(bbar, 1, 2).reshape(N_SSM_TILES, SSM_TILE, SSM_STATE)
    w = jnp.tile(w, (1, 1, SSM_TILE // SSM_GROUP))
    return jnp.where(_same_group_mask()[None], w, 0.0).astype(BF16)


def _block_diag_out(c):
    w = jnp.swapaxes(c, 1, 2).reshape(N_SSM_TILES, STATE_TILE, SSM_GROUP)
    w = jnp.tile(w, (1, 1, SSM_TILE // SSM_GROUP))
    return jnp.where(_same_group_mask().T[None], w, 0.0).astype(BF16)


def _rope_tables(pos):
    half = ROPE_DIM // 2
    inv = ROPE_THETA ** (-np.arange(half, dtype=np.float64) / half)
    ang = np.asarray(pos, dtype=np.float64)[:, None] * inv[None, :]
    cos, sin = np.cos(ang), np.sin(ang)
    d = np.arange(LANES) % HEAD_DIM
    f = d % half
    cos_t = np.where(d[None, :] < ROPE_DIM, cos[:, f], 1.0)
    sa_t = np.where((d[None, :] >= half) & (d[None, :] < ROPE_DIM), sin[:, f], 0.0)
    sb_t = np.where(d[None, :] < half, -sin[:, f], 0.0)
    return tuple(t.astype(np.float32) for t in (cos_t, sa_t, sb_t))


def _row_permutation(n_outer, n_inner):
    r = np.arange(n_outer * n_inner)
    perm = np.zeros((r.size, r.size), np.float32)
    perm[r, (r % n_outer) * n_inner + r // n_outer] = 1.0
    return jnp.asarray(perm, dtype=BF16)


def kernel(x_prompt, x_sample, cache_k_win, cache_v_win, state_ssm_re, state_ssm_im, meta_tokens, norm_mix_pre, w_in, attn_sinks, ssm_a_re, ssm_a_im, ssm_log_dt, ssm_b_re, ssm_b_im, ssm_c_re, ssm_c_im, ssm_d, w_glu, norm_att_out, norm_ssm_out, w_out, norm_mix_post, norm_mlp_pre, w_up, w_down, norm_mlp_post):
    depth = w_in.shape[0]
    assert depth == 1
    l = 0
    nb, seq, _ = x_prompt.shape
    n_seq, n_tok, _ = x_sample.shape
    assert n_seq * n_tok == ROWS and nb * BLOCK == ROWS and seq % BLOCK == 0

    abar_re, abar_im, bbar_re, bbar_im = _zoh(
        ssm_a_re[l], ssm_a_im[l], ssm_log_dt[l], ssm_b_re[l], ssm_b_im[l])
    row = lambda a: a.reshape(1, -1)
    q_scale = np.where(np.arange(w_in.shape[-1]) < ATT_WIDTH, HEAD_DIM ** -0.5, 1.0).astype(
        np.float32)
    params = (
        row(norm_mix_pre[l]), (w_in[l] * q_scale).astype(BF16),
        row(abar_re), row(abar_im),
        _block_diag_in(bbar_re), _block_diag_in(bbar_im),
        _block_diag_out(ssm_c_re[l]), _block_diag_out(-ssm_c_im[l]),
        row(ssm_d[l]), w_glu[l].astype(BF16),
        row(norm_att_out[l]), row(norm_ssm_out[l]), w_out[l].astype(BF16), row(norm_mix_post[l]),
    )

    front = BLOCK - N_META
    meta_blk = jnp.concatenate([jnp.zeros((front, D_MODEL), F32), meta_tokens], axis=0)
    cos, sa, sb = (jnp.asarray(t) for t in _rope_tables(np.arange(seq + BLOCK) - front))
    h1, k_last, v_last, p_re, p_im, w_up_b, w_down_b = _prompt_mixer(
        x_prompt, meta_blk, cos, sa, sb, attn_sinks[l], params, w_up[l], w_down[l])
    mlp_params = (row(norm_mlp_pre[l]), w_up_b, w_down_b, row(norm_mlp_post[l]))

    tabs = _rope_tables(PAST_LEN + np.arange(n_tok))
    tables = tuple(jnp.asarray(np.tile(t, (n_seq, 1))) for t in tabs)
    tables_t = tuple(jnp.asarray(np.tile(t.T, (1, n_seq))) for t in tabs)
    sink_col = jnp.repeat(attn_sinks[l], n_tok * GROUP).reshape(-1, 1)
    to_t = lambda a: jnp.transpose(a, (0, 2, 3, 1)).reshape(-1, WINDOW)
    from_t = lambda a, n: jnp.transpose(
        a.reshape(n, N_KV_HEADS, HEAD_DIM, WINDOW), (0, 3, 1, 2))[None]
    st_in = lambda a: jnp.transpose(a, (1, 2, 0)).reshape(N_STATE, n_seq)
    st_out = lambda a: jnp.transpose(a.reshape(SSM_GROUPS, SSM_STATE, n_seq), (2, 0, 1))[None]
    w_kv = w_in[l][:, ATT_WIDTH:ATT_WIDTH + 2 * KV_WIDTH].astype(BF16)
    h1s, kwin, vwin, s_re, s_im = _decode_mixer(
        x_sample.reshape(ROWS, D_MODEL), tables, tables_t, sink_col, to_t(cache_k_win[l]), to_t(cache_v_win[l]),
        st_in(state_ssm_re[l]), st_in(state_ssm_im[l]),
        w_kv[:, 0:KV_WIDTH].T, w_kv[:, KV_WIDTH:].T, params)
    y_prompt, ys = _mlp(h1.reshape(nb * seq, D_MODEL), h1s, *mlp_params)
    y_prompt = y_prompt.reshape(nb, seq, D_MODEL)
    y_sample = ys.reshape(n_seq, n_tok, D_MODEL)

    win = from_t
    st = lambda a, n: a.reshape(1, n, SSM_GROUPS, SSM_STATE)
    return (y_prompt, y_sample,
            win(k_last, nb), win(v_last, nb), st(p_re, nb), st(p_im, nb),
            win(kwin, n_seq), win(vwin, n_seq), st_out(s_re), st_out(s_im))
```

```python
import math

import jax
import jax.numpy as jnp
import numpy as np
from jax import lax
from jax.experimental import pallas as pl
from jax.experimental.pallas import tpu as pltpu

F32 = jnp.float32
BF16 = jnp.bfloat16

N_META = 16
HEAD_DIM = 64
N_HEADS = 8
N_KV_HEADS = 2
WINDOW = 128
BLOCK = 128
ROPE_DIM = 16
ROPE_HALF = ROPE_DIM // 2
ROPE_THETA = 500000.0
SSM_GROUP = 16
SSM_GROUPS = 32
SSM_STATE = 64
PAST_LEN = 8192
EPS = 1e-6
NEG = -1e30

D_MODEL = 1024
ATT_WIDTH = 512
KV_WIDTH = 128
SSM_WIDTH = 512
N_STATE = SSM_GROUPS * SSM_STATE
D_FF = 4096

ROWS = 512
MLP_ROWS = 1024
LANES = 128
HALF = LANES // 2
SSM_TILE = 256
STATE_TILE = (SSM_TILE // SSM_GROUP) * SSM_STATE
N_SSM_TILES = SSM_WIDTH // SSM_TILE
LB_PER_TILE = STATE_TILE // LANES
N_LB = N_STATE // LANES
FF_CHUNK = 1024
VMEM_LIMIT = 56 * 1024 * 1024


def _dot(a, b):
    return jnp.dot(a, b, preferred_element_type=F32)


def _dot_nt(a, b):
    return lax.dot_general(a, b, (((1,), (1,)), ((), ())), preferred_element_type=F32)


def _rms(x, g):
    return x * lax.rsqrt(jnp.mean(x * x, axis=-1, keepdims=True) + EPS) * g


def _rope(x, cos, sa, sb):
    return (x * cos + pltpu.roll(x, ROPE_HALF, axis=1) * sa
            + pltpu.roll(x, LANES - ROPE_HALF, axis=1) * sb)


def _lane_is_lo(shape):
    return lax.broadcasted_iota(jnp.int32, shape, 1) < HALF


def _project_qkv(hn, w_in_ref):
    q = _dot(hn, w_in_ref[:, 0:ATT_WIDTH])
    kv = _dot(hn, w_in_ref[:, ATT_WIDTH:ATT_WIDTH + 2 * KV_WIDTH])
    return q, kv[:, 0:KV_WIDTH], kv[:, KV_WIDTH:2 * KV_WIDTH]


def _project_u(hn, w_in_ref):
    return _dot(hn, w_in_ref[:, ATT_WIDTH + 2 * KV_WIDTH:])


def _ssm_input_tile(ub, c, wbre_ref, wbim_ref, s_ref):
    uc = ub[:, c * SSM_TILE:(c + 1) * SSM_TILE]
    bre, bim = _dot(uc, wbre_ref[c]), _dot(uc, wbim_ref[c])
    for l in range(LB_PER_TILE):
        s_ref[c * LB_PER_TILE + l, 0:ROWS, :] = bre[:, l * LANES:(l + 1) * LANES]
        s_ref[c * LB_PER_TILE + l, ROWS:2 * ROWS, :] = bim[:, l * LANES:(l + 1) * LANES]


def _ssm_input(u, wbre_ref, wbim_ref, s_ref):
    ub = u.astype(BF16)
    for c in range(N_SSM_TILES):
        _ssm_input_tile(ub, c, wbre_ref, wbim_ref, s_ref)


def _ssm_readout_tile(s_ref, c, wcre_ref, wcim_ref):
    blocks = range(c * LB_PER_TILE, (c + 1) * LB_PER_TILE)
    hr = jnp.concatenate([s_ref[l, 0:ROWS, :].astype(BF16) for l in blocks], axis=1)
    hi = jnp.concatenate([s_ref[l, ROWS:2 * ROWS, :].astype(BF16) for l in blocks], axis=1)
    return _dot(hr, wcre_ref[c]) + _dot(hi, wcim_ref[c])


def _ssm_output(s_ref, u, wcre_ref, wcim_ref, d_ref, wglu_ref):
    ys = [_ssm_readout_tile(s_ref, c, wcre_ref, wcim_ref) for c in range(N_SSM_TILES)]
    return _ssm_gate(ys, u, d_ref, wglu_ref)


def _ssm_gate(ys, u, d_ref, wglu_ref):
    y = jnp.concatenate(ys, axis=1) + d_ref[...] * u
    z = 0.5 * y * (1.0 + jnp.tanh(math.sqrt(2.0 / math.pi) * (y + 0.044715 * (y * y * y))))
    gate = 1.0 / (1.0 + jnp.exp(-_dot(z.astype(BF16), wglu_ref[...])))
    return z * gate


def _merge(x, att, s, gatt_ref, wout_ref, gpost_ref):
    a = _rms(att, gatt_ref[...]).astype(BF16)
    m = _dot(a, wout_ref[0:ATT_WIDTH, :]) + _dot(s, wout_ref[ATT_WIDTH:, :])
    return x + _rms(m, gpost_ref[...])


def _div(x, k):
    return lax.shift_right_logical(x, int(math.log2(k)))


def _mod(x, k):
    return lax.bitwise_and(x, k - 1)


def _sink_softmax(logits, bias, sink_col):
    lm = logits + bias
    m =jnp.maximum(jnp.max(lm, axis=-1, keepdims=True), sink_col)
    e = jnp.exp(lm - m)
    den = jnp.sum(e, axis=-1, keepdims=True) + jnp.exp(sink_col - m)
    return e, den


def _prompt_mixer_kernel(
        sink_ref, x_ref, meta_ref, tab0_ref, cos_ref, sa_ref, sb_ref, bias_ref, perm_ref, permt_ref,
        gpre_ref, win_ref,
        ar_ref, ai_ref, wbre_ref, wbim_ref, wcre_ref, wcim_ref, d_ref, wglu_ref,
        gatt_ref, gssm_ref, wout_ref, gpost_ref, wupf_ref, wdnf_ref,
        h1_ref, klast_ref, vlast_ref, sre_ref, sim_ref, wupb_ref, wdnb_ref,
        kbuf, vbuf, kcur, vcur, s_ref, hstate, att_ref):
    n = pl.program_id(0)
    wupb_ref[...] = wupf_ref[...].astype(BF16)
    wdnb_ref[...] = wdnf_ref[...].astype(BF16)
    nb = x_ref.shape[0]
    assert 2 * nb == 8
    last = pl.num_programs(0) - 1
    lo = _lane_is_lo((BLOCK, LANES))
    is_re = lax.broadcasted_iota(jnp.int32, (2 * nb, LANES), 0) < nb
    a1, a2 = [], []
    for l in range(N_LB):
        cols = slice(l * LANES, (l + 1) * LANES)
        ai = jnp.broadcast_to(ai_ref[:, cols], (2 * nb, LANES))
        a1.append(jnp.broadcast_to(ar_ref[:, cols], (2 * nb, LANES)))
        a2.append(jnp.where(is_re, -ai, ai))

    def swap(t):
        return pltpu.roll(t, nb, axis=0)

    def variants(t):
        tr = pltpu.roll(t, HALF, axis=1)
        z = jnp.zeros_like(t)
        return (jnp.where(lo, t, z), jnp.where(lo, z, tr),
                jnp.where(lo, tr, z), jnp.where(lo, z, t))

    @pl.when(n == 0)
    def _meta_block():
        kbuf[...] = jnp.zeros_like(kbuf)
        vbuf[...] = jnp.zeros_like(vbuf)
        hm = _rms(meta_ref[...], gpre_ref[...]).astype(BF16)
        _, k0, v0 = _project_qkv(hm, win_ref)
        k0 = _rope(k0, tab0_ref[0], tab0_ref[1], tab0_ref[2])
        for i, (kv_, vv_) in enumerate(zip(variants(k0), variants(v0))):
            for b in range(nb):
                kbuf[b, i, 0:BLOCK, :] = kv_.astype(BF16)
                vbuf[b, i, 0:BLOCK, :] = vv_.astype(BF16)
        um = _project_u(hm, win_ref).astype(BF16)
        for c in range(N_SSM_TILES):
            uc = um[:, c * SSM_TILE:(c + 1) * SSM_TILE]
            bre, bim = _dot(uc, wbre_ref[c]), _dot(uc, wbim_ref[c])
            for j in range(LB_PER_TILE):
                l = c * LB_PER_TILE + j
                cols = slice(j * LANES, (j + 1) * LANES)
                h = jnp.zeros((2 * nb, LANES), F32)
                for t in range(BLOCK - N_META, BLOCK):
                    x_t = jnp.where(is_re, jnp.broadcast_to(bre[t:t + 1, cols], h.shape),
                                    jnp.broadcast_to(bim[t:t + 1, cols], h.shape))
                    h = a1[l] * h + a2[l] * swap(h) + x_t
                hstate[l] = h

    x = x_ref[...].reshape(ROWS, D_MODEL)
    hn = _rms(x, gpre_ref[...]).astype(BF16)
    cos, sa, sb = cos_ref[...], sa_ref[...], sb_ref[...]

    u = _project_u(_dot(perm_ref[...], hn).astype(BF16), win_ref)

    hfin = []

    def chains(blocks):
        for l in blocks:
            h = hstate[l]
            g = swap(h)
            a2n = -a2[l]
            for i in range(BLOCK // 2):
                r_re = slice(i * 2 * nb, (i + 1) * 2 * nb)
                r_im = slice(ROWS + i * 2 * nb, ROWS + (i + 1) * 2 * nb)
                re, im_s = s_ref[l, r_re, :], swap(s_ref[l, r_im, :])
                h0 = a1[l] * h + a2[l] * g + jnp.where(is_re, re, im_s)
                g0 = swap(h0)
                g = a1[l] * g0 + a2n * h0 + jnp.where(is_re, im_s, re)
                h = swap(g)
                s_ref[l, r_re, :] = jnp.where(is_re, h0, g)
                s_ref[l, r_im, :] = jnp.where(is_re, g0, h)
            hstate[l] = h
            hfin.append(h)

    mask = bias_ref[0]
    cur = pl.ds(pl.multiple_of(_mod(n + 1, 2) * BLOCK, BLOCK), BLOCK)
    top = lax.broadcasted_iota(jnp.int32, (2 * BLOCK, 1), 0) < BLOCK

    def attend(b):
        rows = slice(b * BLOCK, (b + 1) * BLOCK)
        kb = _rope(k[rows], cos, sa, sb)
        vb = v[rows]
        for i, (kv_, vv_) in enumerate(zip(variants(kb), variants(vb))):
            kbuf[b, i, cur, :] = kv_.astype(BF16)
            vbuf[b, i, cur, :] = vv_.astype(BF16)

        kcur[b] = kb
        vcur[b] = vb

        qs = []
        for j in range(ATT_WIDTH // LANES):
            qs.append(_rope(q[rows, j * LANES:(j + 1) * LANES], cos, sa, sb).astype(BF16))
        for g in range(N_KV_HEADS):
            qst = jnp.concatenate([qs[2 * g], qs[2 * g + 1]], axis=0)
            o = None
            for half in range(2):
                var = 2 * g + half
                h_top, h_bot = 4 * g + half, 4 * g + 2 + half
                sink_col = jnp.where(top, sink_ref[h_top], sink_ref[h_bot])
                e, den = _sink_softmax(_dot_nt(qst, kbuf[b, var]), mask, sink_col)
                part = _dot(e.astype(BF16), vbuf[b, var]) / den
                o = part if o is None else o + part
            att_ref[rows, (2 * g) * LANES:(2 * g + 1) * LANES] = o[0:BLOCK]
            att_ref[rows, (2 * g + 1) * LANES:(2 * g + 2) * LANES] = o[BLOCK:2 * BLOCK]

    _ssm_input(u, wbre_ref, wbim_ref, s_ref)
    chains(range(N_LB))
    q, k, v = _project_qkv(hn, win_ref)
    ssm_o = _ssm_output(s_ref, u, wcre_ref, wcim_ref, d_ref, wglu_ref)
    ssm_n = _dot(permt_ref[...], _rms(ssm_o, gssm_ref[...]).astype(BF16)).astype(BF16)
    for b in range(nb):
        attend(b)

    h1 = _merge(x, att_ref[...], ssm_n, gatt_ref, wout_ref, gpost_ref)
    h1_ref[...] = h1.reshape(h1_ref.shape)

    @pl.when(n == last)
    def _emit_state():
        for b in range(nb):
            klast_ref[b] = kcur[b].T
            vlast_ref[b] = vcur[b].T
        for l in range(N_LB):
            sre_ref[:, l * LANES:(l + 1) * LANES] = hfin[l][0:nb]
            sim_ref[:, l * LANES:(l + 1) * LANES] = hfin[l][nb:2 * nb]


def _window_bias():
    r = (np.arange(2 * BLOCK) % BLOCK)[:, None]
    phys = np.arange(2 * BLOCK)[None, :]
    tables = []
    for c_min in (2 * BLOCK - N_META, BLOCK - N_META, 0):
        for parity in (0, 1):
            c = phys if parity == 1 else (phys + BLOCK) % (2 * BLOCK)
            ok = (c > r) & (c <= r + WINDOW) & (c >= c_min)
            tables.append(np.where(ok, 0.0, NEG))
    return jnp.asarray(np.stack(tables), dtype=F32)


def _const_spec(shape):
    zeros = (0,) * len(shape)
    return pl.BlockSpec(shape, lambda *_: zeros)


def _prompt_mixer(x_prompt, meta_blk, cos, sa, sb, sinks, p, w_up, w_down):
    nb, seq, _ = x_prompt.shape
    n_blocks = seq // BLOCK
    perm = _row_permutation(nb, BLOCK)
    xmap = lambda n: (0, n, 0)
    tmap = lambda n: (n + 1, 0)
    bmap = lambda n: (2 * jnp.minimum(n + 1, 2) + (n + 1) % 2, 0, 0)
    n_slabs = D_FF // LANES
    assert n_slabs <= n_blocks
    upmap = lambda n: (0, jnp.minimum(n, n_slabs - 1))
    dnmap = lambda n: (jnp.minimum(n, n_slabs - 1), 0)
    tab0 = jnp.stack([cos[0:BLOCK], sa[0:BLOCK], sb[0:BLOCK]])
    in_specs = [
        pl.BlockSpec(memory_space=pltpu.SMEM),
        pl.BlockSpec((nb, BLOCK, D_MODEL), xmap),
        _const_spec((BLOCK, D_MODEL)),
        _const_spec(tab0.shape),
        pl.BlockSpec((BLOCK, LANES), tmap),
        pl.BlockSpec((BLOCK, LANES), tmap),
        pl.BlockSpec((BLOCK, LANES), tmap),
        pl.BlockSpec((1, 2 * BLOCK, 2 * BLOCK), bmap),
        _const_spec((ROWS, ROWS)),
        _const_spec((ROWS, ROWS)),
    ] + [_const_spec(a.shape) for a in p] + [
        pl.BlockSpec((D_MODEL, LANES), upmap),
        pl.BlockSpec((LANES, D_MODEL), dnmap),
    ]
    out_shape = (
        jax.ShapeDtypeStruct((nb, seq, D_MODEL), F32),
        jax.ShapeDtypeStruct((nb, BLOCK, KV_WIDTH), F32),
        jax.ShapeDtypeStruct((nb, BLOCK, KV_WIDTH), F32),
        jax.ShapeDtypeStruct((nb, N_STATE), F32),
        jax.ShapeDtypeStruct((nb, N_STATE), F32),
        jax.ShapeDtypeStruct(w_up.shape, BF16),
        jax.ShapeDtypeStruct(w_down.shape, BF16),
    )
    out_specs = (
        pl.BlockSpec((nb, BLOCK, D_MODEL), xmap),
        _const_spec((nb, BLOCK, KV_WIDTH)),
        _const_spec((nb, BLOCK, KV_WIDTH)),
        _const_spec((nb, N_STATE)),
        _const_spec((nb, N_STATE)),
        pl.BlockSpec((D_MODEL, LANES), upmap),
        pl.BlockSpec((LANES, D_MODEL), dnmap),
    )
    scratch = [
        pltpu.VMEM((nb, 4, 2 * BLOCK, LANES), BF16),
        pltpu.VMEM((nb, 4, 2 * BLOCK, LANES), BF16),
        pltpu.VMEM((nb, BLOCK, KV_WIDTH), F32),
        pltpu.VMEM((nb, BLOCK, KV_WIDTH), F32),
        pltpu.VMEM((N_LB, 2 * ROWS, LANES), F32),
        pltpu.VMEM((N_LB, 2 * nb, LANES), F32),
        pltpu.VMEM((ROWS, ATT_WIDTH), F32),
    ]
    return pl.pallas_call(
        _prompt_mixer_kernel,
        grid=(n_blocks,),
        in_specs=in_specs,
        out_specs=out_specs,
        out_shape=out_shape,
        scratch_shapes=scratch,
        compiler_params=pltpu.CompilerParams(
            dimension_semantics=("arbitrary",), vmem_limit_bytes=VMEM_LIMIT),
        name="prompt_mixer",
    )(sinks, x_prompt, meta_blk, tab0, cos, sa, sb, _window_bias(), perm, perm.T, *p,
      w_up, w_down)


GROUP = 8
GROUPS_PER_STEP = 4


def _decode_mixer_kernel(
        x_ref, cos_ref, sa_ref, sb_ref, cost_ref, sat_ref, sbt_ref, sink_ref, perm_ref, permt_ref,
        ck_ref, cv_ref, stre_ref, stim_ref, wkt_ref, wvt_ref,
        gpre_ref, win_ref, ar_ref, ai_ref, wbre_ref, wbim_ref, wcre_ref, wcim_ref,
        d_ref, wglu_ref, gatt_ref, gssm_ref, wout_ref, gpost_ref,
        h1_ref, kwin_ref, vwin_ref, sre_ref, sim_ref,
        qp, knew, vnew, knew_b, vnew_b, ssm_n, att_ref, s_ref):
    step = pl.program_id(0)
    n_seq = stre_ref.shape[1]
    n_tok = ROWS // n_seq
    grp_rows = GROUP * n_tok
    last = pl.num_programs(0) - 1

    @pl.when(step == 0)
    def _project_and_ssm():
        hn = _rms(x_ref[...].reshape(ROWS, D_MODEL), gpre_ref[...]).astype(BF16)
        q = _dot(hn, win_ref[:, 0:ATT_WIDTH])
        u = _project_u(_dot(perm_ref[...], hn).astype(BF16), win_ref)
        cos, sa, sb = cos_ref[...], sa_ref[...], sb_ref[...]
        kt = _dot_nt(wkt_ref[...], hn)
        kt = (kt * cost_ref[...] + pltpu.roll(kt, ROPE_HALF, axis=0) * sat_ref[...]
              + pltpu.roll(kt, KV_WIDTH - ROPE_HALF, axis=0) * sbt_ref[...])
        vt = _dot_nt(wvt_ref[...], hn)
        knew[...] = kt
        vnew[...] = vt
        knew_b[...] = kt.astype(BF16)
        vnew_b[...] = vt.astype(BF16)
        lo = _lane_is_lo((ROWS, LANES))
        for j in range(ATT_WIDTH // LANES):
            qj = _rope(q[:, j * LANES:(j + 1) * LANES], cos, sa, sb)
            qr = pltpu.roll(qj, HALF, axis=1)
            z = jnp.zeros_like(qj)
            if j < 2:
                qp[2 * j] = jnp.where(lo, qj, z)
                qp[2 * j + 1] = jnp.where(lo, qr, z)
            else:
                qp[2 * j] = jnp.where(lo, z, qr)
                qp[2 * j + 1] = jnp.where(lo, z, qj)

        _ssm_input(u, wbre_ref, wbim_ref, s_ref)
        for l in range(N_LB):
            cols = slice(l * LANES, (l + 1) * LANES)
            ar, ai = ar_ref[:, cols], ai_ref[:, cols]
            hr, hi = stre_ref[cols, :].T, stim_ref[cols, :].T
            for t in range(n_tok):
                r_re = slice(t * n_seq, (t + 1) * n_seq)
                r_im = slice(ROWS + t * n_seq, ROWS + (t + 1) * n_seq)
                hr, hi = (ar * hr - ai * hi + s_ref[l, r_re, :],
                          ar * hi + ai * hr + s_ref[l, r_im, :])
                s_ref[l, r_re, :] = hr
                s_ref[l, r_im, :] = hi
            sre_ref[cols, :] = hr.T
            sim_ref[cols, :] = hi.T
        ssm_o = _ssm_output(s_ref, u, wcre_ref, wcim_ref, d_ref, wglu_ref)
        ssm_n[...] = _dot(permt_ref[...], _rms(ssm_o, gssm_ref[...]).astype(BF16)).astype(BF16)

    def attend_group(g, base):
        grp = pl.ds(pl.multiple_of(g * grp_rows, grp_rows), grp_rows)
        qb = jnp.concatenate([qp[h, grp, :] for h in range(N_HEADS)], axis=0).astype(BF16)
        seq_rows = [slice(base + b * KV_WIDTH, base + (b + 1) * KV_WIDTH) for b in range(GROUP)]
        kc_f = [ck_ref[r, :] for r in seq_rows]
        vc_f = [cv_ref[r, :] for r in seq_rows]
        kcat = jnp.concatenate(kc_f, axis=1).astype(BF16)
        vcat = jnp.concatenate(vc_f, axis=1).astype(BF16)
        n_q = N_HEADS * grp_rows
        n_c = GROUP * WINDOW

        r_c = lax.broadcasted_iota(jnp.int32, (n_q, n_c), 0)
        c_c = lax.broadcasted_iota(jnp.int32, (n_q, n_c), 1)
        mask_c = ((_div(c_c, WINDOW) == _mod(_div(r_c, n_tok), GROUP))
                  & (_mod(c_c, WINDOW) > _mod(r_c, n_tok)))
        lc = jnp.where(mask_c, _dot(qb, kcat), NEG)
        tile = pl.ds(pl.multiple_of(_div(g * grp_rows, LANES) * LANES, LANES), LANES)
        first = _mod(g * grp_rows, LANES)
        r_n = lax.broadcasted_iota(jnp.int32, (n_q, LANES), 0)
        c_n = lax.broadcasted_iota(jnp.int32, (n_q, LANES), 1) - first
        mask_n = ((c_n >= 0) & (c_n < grp_rows)
                  & (_div(c_n, n_tok) == _mod(_div(r_n, n_tok), GROUP))
                  & (_mod(c_n, n_tok) <= _mod(r_n, n_tok)))
        ln = jnp.where(mask_n, _dot(qb, knew_b[:, tile]), NEG)

        sink_col = sink_ref[...]
        m = jnp.maximum(jnp.maximum(jnp.max(lc, axis=-1, keepdims=True),
                                    jnp.max(ln, axis=-1, keepdims=True)), sink_col)
        ec = jnp.exp(lc - m)
        en = jnp.exp(ln - m)
        den = (jnp.sum(ec, axis=-1, keepdims=True) + jnp.sum(en, axis=-1, keepdims=True)
               + jnp.exp(sink_col - m))
        o = (_dot_nt(ec.astype(BF16), vcat) + _dot_nt(en.astype(BF16), vnew_b[:, tile])) / den

        lo = _lane_is_lo((grp_rows, LANES))
        for j in range(ATT_WIDTH // LANES):
            o_even = o[(2 * j) * grp_rows:(2 * j + 1) * grp_rows]
            o_odd = o[(2 * j + 1) * grp_rows:(2 * j + 2) * grp_rows]
            if j < 2:
                pair = jnp.where(lo, o_even, pltpu.roll(o_odd, HALF, axis=1))
            else:
                pair = jnp.where(lo, pltpu.roll(o_even, HALF, axis=1), o_odd)
            att_ref[grp, j * LANES:(j + 1) * LANES] = pair

        keep = WINDOW - n_tok
        to_tail = _mod(keep - first + LANES, LANES)
        kt_g = pltpu.roll(knew[:, tile], to_tail, axis=1)
        vt_g = pltpu.roll(vnew[:, tile], to_tail, axis=1)
        is_new = lax.broadcasted_iota(jnp.int32, (KV_WIDTH, WINDOW), 1) >= keep
        for b, rows in enumerate(seq_rows):
            k_b = kt_g if b == 0 else pltpu.roll(kt_g, LANES - n_tok * b, axis=1)
            v_b = vt_g if b == 0 else pltpu.roll(vt_g, LANES - n_tok * b, axis=1)
            kwin_ref[rows, :] = jnp.where(is_new, k_b, pltpu.roll(kc_f[b], keep, axis=1))
            vwin_ref[rows, :] = jnp.where(is_new, v_b, pltpu.roll(vc_f[b], keep, axis=1))

    for sub in range(GROUPS_PER_STEP):
        attend_group(step * GROUPS_PER_STEP + sub, sub * GROUP * KV_WIDTH)

    @pl.when(step == last)
    def _merge_out():
        h1_ref[...] = _merge(x_ref[...].reshape(ROWS, D_MODEL), att_ref[...], ssm_n[...],
                             gatt_ref, wout_ref, gpost_ref)


def _decode_mixer(x2d, tables, tables_t, sink_col, ck2d, cv2d, st_re, st_im, wkt, wvt, p):
    n_seq = st_re.shape[1]
    n_steps = n_seq // (GROUP * GROUPS_PER_STEP)
    cmap = lambda g: (g, 0)
    perm = _row_permutation(n_seq, ROWS // n_seq)
    head = (x2d, *tables, *tables_t, sink_col, perm, perm.T)
    in_specs = [_const_spec(a.shape) for a in head] + [
        pl.BlockSpec((GROUPS_PER_STEP * GROUP * KV_WIDTH, WINDOW), cmap),
        pl.BlockSpec((GROUPS_PER_STEP * GROUP * KV_WIDTH, WINDOW), cmap),
    ] + [_const_spec(a.shape) for a in (st_re, st_im, wkt, wvt, *p)]
    out_shape = (
        jax.ShapeDtypeStruct((ROWS, D_MODEL), F32),
        jax.ShapeDtypeStruct(ck2d.shape, F32),
        jax.ShapeDtypeStruct(cv2d.shape, F32),
        jax.ShapeDtypeStruct(st_re.shape, F32),
        jax.ShapeDtypeStruct(st_im.shape, F32),
    )
    out_specs = (
        _const_spec((ROWS, D_MODEL)),
        pl.BlockSpec((GROUPS_PER_STEP * GROUP * KV_WIDTH, WINDOW), cmap),
        pl.BlockSpec((GROUPS_PER_STEP * GROUP * KV_WIDTH, WINDOW), cmap),
        _const_spec(st_re.shape), _const_spec(st_im.shape),
    )
    scratch = [
        pltpu.VMEM((N_HEADS, ROWS, LANES), F32),
        pltpu.VMEM((KV_WIDTH, ROWS), F32),
        pltpu.VMEM((KV_WIDTH, ROWS), F32),
        pltpu.VMEM((KV_WIDTH, ROWS), BF16),
        pltpu.VMEM((KV_WIDTH, ROWS), BF16),
        pltpu.VMEM((ROWS, SSM_WIDTH), BF16),
        pltpu.VMEM((ROWS, ATT_WIDTH), F32),
        pltpu.VMEM((N_LB, 2 * ROWS, LANES), F32),
    ]
    return pl.pallas_call(
        _decode_mixer_kernel,
        grid=(n_steps,),
        in_specs=in_specs,
        out_specs=out_specs,
        out_shape=out_shape,
        scratch_shapes=scratch,
        compiler_params=pltpu.CompilerParams(
            dimension_semantics=("arbitrary",), vmem_limit_bytes=VMEM_LIMIT),
        name="decode_mixer",
    )(*head, ck2d, cv2d, st_re, st_im, wkt, wvt, *p)


def _mlp_kernel(xp_ref, xd_ref, gpre_ref, wup_ref, wdn_ref, gpost_ref, op_ref, od_ref):
    i = pl.program_id(0)

    def mlp(x):
        hn = _rms(x, gpre_ref[...]).astype(BF16)
        acc = None
        for c in range(D_FF // FF_CHUNK):
            cols = slice(c * FF_CHUNK, (c + 1) * FF_CHUNK)
            a = jnp.maximum(_dot(hn, wup_ref[:, cols]), 0.0)
            part = _dot((a * a).astype(BF16), wdn_ref[cols, :])
            acc = part if acc is None else acc + part
        return x + _rms(acc, gpost_ref[...])

    @pl.when(i == 0)
    def _decode_rows():
        od_ref[...] = mlp(xd_ref[...]).reshape(od_ref.shape)

    @pl.when(i > 0)
    def _prompt_tile():
        op_ref[...] = mlp(xp_ref[...])


def _mlp(xp2d, xd2d, d_shape, g_pre, w_up, w_down, g_post):
    n_tiles = xp2d.shape[0] // MLP_ROWS
    pmap = lambda i: (jnp.maximum(i - 1, 0), 0)
    return pl.pallas_call(
        _mlp_kernel,
        grid=(n_tiles + 1,),
        in_specs=[pl.BlockSpec((MLP_ROWS, D_MODEL), pmap), _const_spec(xd2d.shape),
                  _const_spec(g_pre.shape), _const_spec(w_up.shape), _const_spec(w_down.shape),
                  _const_spec(g_post.shape)],
        out_specs=(pl.BlockSpec((MLP_ROWS, D_MODEL), pmap), _const_spec(d_shape)),
        out_shape=(jax.ShapeDtypeStruct(xp2d.shape, F32), jax.ShapeDtypeStruct(d_shape, F32)),
        compiler_params=pltpu.CompilerParams(
            dimension_semantics=("arbitrary",), vmem_limit_bytes=VMEM_LIMIT),
        name="mlp",
    )(xp2d, xd2d, g_pre, w_up, w_down, g_post)


def _zoh(a_re, a_im, log_dt, b_re, b_im):
    dt = jnp.exp(log_dt)[:, None]
    mag = jnp.exp(a_re * dt)
    abar_re, abar_im = mag * jnp.cos(a_im * dt), mag * jnp.sin(a_im * dt)
    nr, ni = abar_re - 1.0, abar_im
    den = a_re * a_re + a_im * a_im
    coef_re = (nr * a_re + ni * a_im) / den
    coef_im = (ni * a_re - nr * a_im) / den
    bbar_re = coef_re[..., None] * b_re - coef_im[..., None] * b_im
    bbar_im = coef_re[..., None] * b_im + coef_im[..., None] * b_re
    return abar_re, abar_im, bbar_re, bbar_im


def _same_group_mask():
    row_g = np.arange(SSM_TILE)[:, None] // SSM_GROUP
    col_g = np.arange(STATE_TILE)[None, :] // SSM_STATE
    return row_g == col_g


def _block_diag_in(bbar):
    w = jnp.swapaxes(bbar, 1, 2).reshape(N_SSM_TILES, SSM_TILE, SSM_STATE)
    w = jnp.tile(w, (1, 1, SSM_TILE // SSM_GROUP))
    return jnp.where(_same_group_mask()[None], w, 0.0).astype(BF16)


def _block_diag_out(c):
    w = jnp.swapaxes(c, 1, 2).reshape(N_SSM_TILES, STATE_TILE, SSM_GROUP)
    w = jnp.tile(w, (1, 1, SSM_TILE // SSM_GROUP))
    return jnp.where(_same_group_mask().T[None], w, 0.0).astype(BF16)


def _rope_tables(pos):
    half = ROPE_DIM // 2
    inv = ROPE_THETA ** (-np.arange(half, dtype=np.float64) / half)
    ang = np.asarray(pos, dtype=np.float64)[:, None] * inv[None, :]
    cos, sin = np.cos(ang), np.sin(ang)
    d = np.arange(LANES) % HEAD_DIM
    f = d % half
    cos_t = np.where(d[None, :] < ROPE_DIM, cos[:, f], 1.0)
    sa_t = np.where((d[None, :] >= half) & (d[None, :] < ROPE_DIM), sin[:, f], 0.0)
    sb_t = np.where(d[None, :] < half, -sin[:, f], 0.0)
    return tuple(t.astype(np.float32) for t in (cos_t, sa_t, sb_t))


def _row_permutation(n_outer, n_inner):
    r = np.arange(n_outer * n_inner)
    perm = np.zeros((r.size, r.size), np.float32)
    perm[r, (r % n_outer) * n_inner + r // n_outer] = 1.0
    return jnp.asarray(perm, dtype=BF16)


def kernel(x_prompt, x_sample, cache_k_win, cache_v_win, state_ssm_re, state_ssm_im, meta_tokens, norm_mix_pre, w_in, attn_sinks, ssm_a_re, ssm_a_im, ssm_log_dt, ssm_b_re, ssm_b_im, ssm_c_re, ssm_c_im, ssm_d, w_glu, norm_att_out, norm_ssm_out, w_out, norm_mix_post, norm_mlp_pre, w_up, w_down, norm_mlp_post):
    depth = w_in.shape[0]
    assert depth == 1
    l = 0
    nb, seq, _ = x_prompt.shape
    n_seq, n_tok, _ = x_sample.shape
    assert n_seq * n_tok == ROWS and nb * BLOCK == ROWS and seq % BLOCK == 0

    abar_re, abar_im, bbar_re, bbar_im = _zoh(
        ssm_a_re[l], ssm_a_im[l], ssm_log_dt[l], ssm_b_re[l], ssm_b_im[l])
    row = lambda a: a.reshape(1, -1)
    q_scale = np.where(np.arange(w_in.shape[-1]) < ATT_WIDTH, HEAD_DIM ** -0.5, 1.0).astype(
        np.float32)
    params = (
        row(norm_mix_pre[l]), (w_in[l] * q_scale).astype(BF16),
        row(abar_re), row(abar_im),
        _block_diag_in(bbar_re), _block_diag_in(bbar_im),
        _block_diag_out(ssm_c_re[l]), _block_diag_out(-ssm_c_im[l]),
        row(ssm_d[l]), w_glu[l].astype(BF16),
        row(norm_att_out[l]), row(norm_ssm_out[l]), w_out[l].astype(BF16), row(norm_mix_post[l]),
    )

    front = BLOCK - N_META
    meta_blk = jnp.concatenate([jnp.zeros((front, D_MODEL), F32), meta_tokens], axis=0)
    cos, sa, sb = (jnp.asarray(t) for t in _rope_tables(np.arange(seq + BLOCK) - front))
    h1, k_last, v_last, p_re, p_im, w_up_b, w_down_b = _prompt_mixer(
        x_prompt, meta_blk, cos, sa, sb, attn_sinks[l], params, w_up[l], w_down[l])
    mlp_params = (row(norm_mlp_pre[l]), w_up_b, w_down_b, row(norm_mlp_post[l]))

    tabs = _rope_tables(PAST_LEN + np.arange(n_tok))
    tables = tuple(jnp.asarray(np.tile(t, (n_seq, 1))) for t in tabs)
    tables_t = tuple(jnp.asarray(np.tile(t.T, (1, n_seq))) for t in tabs)
    sink_col = jnp.repeat(attn_sinks[l], n_tok * GROUP).reshape(-1, 1)
    to_t = lambda a: jnp.transpose(a, (0, 2, 3, 1)).reshape(-1, WINDOW)
    from_t = lambda a, n: jnp.transpose(
        a.reshape(n, N_KV_HEADS, HEAD_DIM, WINDOW), (0, 3, 1, 2))[None]
    st_in = lambda a: jnp.transpose(a, (1, 2, 0)).reshape(N_STATE, n_seq)
    st_out = lambda a: jnp.transpose(a.reshape(SSM_GROUPS, SSM_STATE, n_seq), (2, 0, 1))[None]
    w_kv = w_in[l][:, ATT_WIDTH:ATT_WIDTH + 2 * KV_WIDTH].astype(BF16)
    h1s, kwin, vwin, s_re, s_im = _decode_mixer(
        x_sample, tables, tables_t, sink_col, to_t(cache_k_win[l]), to_t(cache_v_win[l]),
        st_in(state_ssm_re[l]), st_in(state_ssm_im[l]),
        w_kv[:, 0:KV_WIDTH].T, w_kv[:, KV_WIDTH:].T, params)
    y_prompt, y_sample = _mlp(h1.reshape(nb * seq, D_MODEL), h1s, x_sample.shape, *mlp_params)
    y_prompt = y_prompt.reshape(nb, seq, D_MODEL)

    win = from_t
    st = lambda a, n: a.reshape(1, n, SSM_GROUPS, SSM_STATE)
    return (y_prompt, y_sample,
            win(k_last, nb), win(v_last, nb), st(p_re, nb), st(p_im, nb),
            win(kwin, n_seq), win(vwin, n_seq), st_out(s_re), st_out(s_im))
```

```python
import math

import jax
import jax.numpy as jnp
import numpy as np
from jax import lax
from jax.experimental import pallas as pl
from jax.experimental.pallas import tpu as pltpu

F32 = jnp.float32
BF16 = jnp.bfloat16

N_META = 16
HEAD_DIM = 64
N_HEADS = 8
N_KV_HEADS = 2
WINDOW = 128
BLOCK = 128
ROPE_DIM = 16
ROPE_HALF = ROPE_DIM // 2
ROPE_THETA = 500000.0
SSM_GROUP = 16
SSM_GROUPS = 32
SSM_STATE = 64
PAST_LEN = 8192
EPS = 1e-6
NEG = -1e30

D_MODEL = 1024
ATT_WIDTH = 512
KV_WIDTH = 128
SSM_WIDTH = 512
N_STATE = SSM_GROUPS * SSM_STATE
D_FF = 4096

ROWS = 512
MLP_ROWS = 1024
LANES = 128
HALF = LANES // 2
SSM_TILE = 256
STATE_TILE = (SSM_TILE // SSM_GROUP) * SSM_STATE
N_SSM_TILES = SSM_WIDTH // SSM_TILE
LB_PER_TILE = STATE_TILE // LANES
N_LB = N_STATE // LANES
FF_CHUNK = 1024
VMEM_LIMIT = 56 * 1024 * 1024


def _dot(a, b):
    return jnp.dot(a, b, preferred_element_type=F32)


def _dot_nt(a, b):
    return lax.dot_general(a, b, (((1,), (1,)), ((), ())), preferred_element_type=F32)


def _rms(x, g):
    return x * lax.rsqrt(jnp.mean(x * x, axis=-1, keepdims=True) + EPS) * g


def _rope(x, cos, sa, sb):
    return (x * cos + pltpu.roll(x, ROPE_HALF, axis=1) * sa
            + pltpu.roll(x, LANES - ROPE_HALF, axis=1) * sb)


def _lane_is_lo(shape):
    return lax.broadcasted_iota(jnp.int32, shape, 1) < HALF


def _project_qkv(hn, w_in_ref):
    q = _dot(hn, w_in_ref[:, 0:ATT_WIDTH])
    kv = _dot(hn, w_in_ref[:, ATT_WIDTH:ATT_WIDTH + 2 * KV_WIDTH])
    return q, kv[:, 0:KV_WIDTH], kv[:, KV_WIDTH:2 * KV_WIDTH]


def _project_u(hn, w_in_ref):
    return _dot(hn, w_in_ref[:, ATT_WIDTH + 2 * KV_WIDTH:])


def _ssm_input_tile(ub, c, wbre_ref, wbim_ref, s_ref):
    uc = ub[:, c * SSM_TILE:(c + 1) * SSM_TILE]
    bre, bim = _dot(uc, wbre_ref[c]), _dot(uc, wbim_ref[c])
    for l in range(LB_PER_TILE):
        s_ref[c * LB_PER_TILE + l, 0:ROWS, :] = bre[:, l * LANES:(l + 1) * LANES]
        s_ref[c * LB_PER_TILE + l, ROWS:2 * ROWS, :] = bim[:, l * LANES:(l + 1) * LANES]


def _ssm_input(u, wbre_ref, wbim_ref, s_ref):
    ub = u.astype(BF16)
    for c in range(N_SSM_TILES):
        _ssm_input_tile(ub, c, wbre_ref, wbim_ref, s_ref)


def _ssm_readout_tile(s_ref, c, wcre_ref, wcim_ref):
    blocks = range(c * LB_PER_TILE, (c + 1) * LB_PER_TILE)
    hr = jnp.concatenate([s_ref[l, 0:ROWS, :].astype(BF16) for l in blocks], axis=1)
    hi = jnp.concatenate([s_ref[l, ROWS:2 * ROWS, :].astype(BF16) for l in blocks], axis=1)
    return _dot(hr, wcre_ref[c]) + _dot(hi, wcim_ref[c])


def _ssm_output(s_ref, u, wcre_ref, wcim_ref, d_ref, wglu_ref):
    ys = [_ssm_readout_tile(s_ref, c, wcre_ref, wcim_ref) for c in range(N_SSM_TILES)]
    return _ssm_gate(ys, u, d_ref, wglu_ref)


def _ssm_gate(ys, u, d_ref, wglu_ref):
    y = jnp.concatenate(ys, axis=1) + d_ref[...] * u
    z = 0.5 * y * (1.0 + jnp.tanh(math.sqrt(2.0 / math.pi) * (y + 0.044715 * (y * y * y))))
    gate = 1.0 / (1.0 + jnp.exp(-_dot(z.astype(BF16), wglu_ref[...])))
    return z * gate


def _merge(x, att, s, gatt_ref, wout_ref, gpost_ref):
    a = _rms(att, gatt_ref[...]).astype(BF16)
    m = _dot(a, wout_ref[0:ATT_WIDTH, :]) + _dot(s, wout_ref[ATT_WIDTH:, :])
    return x + _rms(m, gpost_ref[...])


def _div(x, k):
    return lax.shift_right_logical(x, int(math.log2(k)))


def _mod(x, k):
    return lax.bitwise_and(x, k - 1)


def _sink_softmax(logits, bias, sink_col):
    lm = logits + bias
    m =jnp.maximum(jnp.max(lm, axis=-1, keepdims=True), sink_col)
    e = jnp.exp(lm - m)
    den = jnp.sum(e, axis=-1, keepdims=True) + jnp.exp(sink_col - m)
    return e, den


def _prompt_mixer_kernel(
        sink_ref, x_ref, meta_ref, tab0_ref, cos_ref, sa_ref, sb_ref, bias_ref, perm_ref, permt_ref,
        gpre_ref, winf_ref,
        ar_ref, ai_ref, wbre_ref, wbim_ref, wcre_ref, wcim_ref, d_ref, wgluf_ref,
        gatt_ref, gssm_ref, woutf_ref, gpost_ref, qscale_ref, wupf_ref, wdnf_ref,
        h1_ref, klast_ref, vlast_ref, sre_ref, sim_ref, wupb_ref, wdnb_ref,
        win_ref, wout_ref, wglu_ref,
        kbuf, vbuf, kcur, vcur, s_ref, hstate, att_ref):
    n = pl.program_id(0)
    wupb_ref[...] = wupf_ref[...].astype(BF16)
    wdnb_ref[...] = wdnf_ref[...].astype(BF16)
    nb = x_ref.shape[0]
    assert 2 * nb == 8
    last = pl.num_programs(0) - 1
    lo = _lane_is_lo((BLOCK, LANES))
    is_re = lax.broadcasted_iota(jnp.int32, (2 * nb, LANES), 0) < nb
    a1, a2 = [], []
    for l in range(N_LB):
        cols = slice(l * LANES, (l + 1) * LANES)
        ai = jnp.broadcast_to(ai_ref[:, cols], (2 * nb, LANES))
        a1.append(jnp.broadcast_to(ar_ref[:, cols], (2 * nb, LANES)))
        a2.append(jnp.where(is_re, -ai, ai))

    def swap(t):
        return pltpu.roll(t, nb, axis=0)

    def variants(t):
        tr = pltpu.roll(t, HALF, axis=1)
        z = jnp.zeros_like(t)
        return (jnp.where(lo, t, z), jnp.where(lo, z, tr),
                jnp.where(lo, tr, z), jnp.where(lo, z, t))

    @pl.when(n == 0)
    def _meta_block():
        win_ref[...] = (winf_ref[...] * qscale_ref[...]).astype(BF16)
        wout_ref[...] = woutf_ref[...].astype(BF16)
        wglu_ref[...] = wgluf_ref[...].astype(BF16)
        kbuf[...] = jnp.zeros_like(kbuf)
        vbuf[...] = jnp.zeros_like(vbuf)
        hm = _rms(meta_ref[...], gpre_ref[...]).astype(BF16)
        _, k0, v0 = _project_qkv(hm, win_ref)
        k0 = _rope(k0, tab0_ref[0], tab0_ref[1], tab0_ref[2])
        for i, (kv_, vv_) in enumerate(zip(variants(k0), variants(v0))):
            for b in range(nb):
                kbuf[b, i, 0:BLOCK, :] = kv_.astype(BF16)
                vbuf[b, i, 0:BLOCK, :] = vv_.astype(BF16)
        um = _project_u(hm, win_ref).astype(BF16)
        for c in range(N_SSM_TILES):
            uc = um[:, c * SSM_TILE:(c + 1) * SSM_TILE]
            bre, bim = _dot(uc, wbre_ref[c]), _dot(uc, wbim_ref[c])
            for j in range(LB_PER_TILE):
                l = c * LB_PER_TILE + j
                cols = slice(j * LANES, (j + 1) * LANES)
                h = jnp.zeros((2 * nb, LANES), F32)
                for t in range(BLOCK - N_META, BLOCK):
                    x_t = jnp.where(is_re, jnp.broadcast_to(bre[t:t + 1, cols], h.shape),
                                    jnp.broadcast_to(bim[t:t + 1, cols], h.shape))
                    h = a1[l] * h + a2[l] * swap(h) + x_t
                hstate[l] = h

    x = x_ref[...].reshape(ROWS, D_MODEL)
    hn = _rms(x, gpre_ref[...]).astype(BF16)
    cos, sa, sb = cos_ref[...], sa_ref[...], sb_ref[...]

    u = _project_u(_dot(perm_ref[...], hn).astype(BF16), win_ref)

    hfin = []

    def chains(blocks):
        for l in blocks:
            h = hstate[l]
            g = swap(h)
            a2n = -a2[l]
            for i in range(BLOCK // 2):
                r_re = slice(i * 2 * nb, (i + 1) * 2 * nb)
                r_im = slice(ROWS + i * 2 * nb, ROWS + (i + 1) * 2 * nb)
                re, im_s = s_ref[l, r_re, :], swap(s_ref[l, r_im, :])
                h0 = a1[l] * h + a2[l] * g + jnp.where(is_re, re, im_s)
                g0 = swap(h0)
                g = a1[l] * g0 + a2n * h0 + jnp.where(is_re, im_s, re)
                h = swap(g)
                s_ref[l, r_re, :] = jnp.where(is_re, h0, g)
                s_ref[l, r_im, :] = jnp.where(is_re, g0, h)
            hstate[l] = h
            hfin.append(h)

    mask = bias_ref[0]
    cur = pl.ds(pl.multiple_of(_mod(n + 1, 2) * BLOCK, BLOCK), BLOCK)
    top = lax.broadcasted_iota(jnp.int32, (2 * BLOCK, 1), 0) < BLOCK

    def attend(b):
        rows = slice(b * BLOCK, (b + 1) * BLOCK)
        kb = _rope(k[rows], cos, sa, sb)
        vb = v[rows]
        for i, (kv_, vv_) in enumerate(zip(variants(kb), variants(vb))):
            kbuf[b, i, cur, :] = kv_.astype(BF16)
            vbuf[b, i, cur, :] = vv_.astype(BF16)

        kcur[b] = kb
        vcur[b] = vb

        qs = []
        for j in range(ATT_WIDTH // LANES):
            qs.append(_rope(q[rows, j * LANES:(j + 1) * LANES], cos, sa, sb).astype(BF16))
        for g in range(N_KV_HEADS):
            qst = jnp.concatenate([qs[2 * g], qs[2 * g + 1]], axis=0)
            o = None
            for half in range(2):
                var = 2 * g + half
                h_top, h_bot = 4 * g + half, 4 * g + 2 + half
                sink_col = jnp.where(top, sink_ref[h_top], sink_ref[h_bot])
                e, den = _sink_softmax(_dot_nt(qst, kbuf[b, var]), mask, sink_col)
                part = _dot(e.astype(BF16), vbuf[b, var]) / den
                o = part if o is None else o + part
            att_ref[rows, (2 * g) * LANES:(2 * g + 1) * LANES] = o[0:BLOCK]
            att_ref[rows, (2 * g + 1) * LANES:(2 * g + 2) * LANES] = o[BLOCK:2 * BLOCK]

    _ssm_input(u, wbre_ref, wbim_ref, s_ref)
    chains(range(N_LB))
    q, k, v = _project_qkv(hn, win_ref)
    ssm_o = _ssm_output(s_ref, u, wcre_ref, wcim_ref, d_ref, wglu_ref)
    ssm_n = _dot(permt_ref[...], _rms(ssm_o, gssm_ref[...]).astype(BF16)).astype(BF16)
    for b in range(nb):
        attend(b)

    h1 = _merge(x, att_ref[...], ssm_n, gatt_ref, wout_ref, gpost_ref)
    h1_ref[...] = h1.reshape(h1_ref.shape)

    @pl.when(n == last)
    def _emit_state():
        for b in range(nb):
            klast_ref[b] = kcur[b].T
            vlast_ref[b] = vcur[b].T
        for l in range(N_LB):
            sre_ref[:, l * LANES:(l + 1) * LANES] = hfin[l][0:nb]
            sim_ref[:, l * LANES:(l + 1) * LANES] = hfin[l][nb:2 * nb]


def _window_bias():
    r = (np.arange(2 * BLOCK) % BLOCK)[:, None]
    phys = np.arange(2 * BLOCK)[None, :]
    tables = []
    for c_min in (2 * BLOCK - N_META, BLOCK - N_META, 0):
        for parity in (0, 1):
            c = phys if parity == 1 else (phys + BLOCK) % (2 * BLOCK)
            ok = (c > r) & (c <= r + WINDOW) & (c >= c_min)
            tables.append(np.where(ok, 0.0, NEG))
    return jnp.asarray(np.stack(tables), dtype=F32)


def _const_spec(shape):
    zeros = (0,) * len(shape)
    return pl.BlockSpec(shape, lambda *_: zeros)


def _prompt_mixer(x_prompt, meta_blk, cos, sa, sb, sinks, p, q_scale, w_up, w_down):
    nb, seq, _ = x_prompt.shape
    n_blocks = seq // BLOCK
    perm = _row_permutation(nb, BLOCK)
    xmap = lambda n: (0, n, 0)
    tmap = lambda n: (n + 1, 0)
    bmap = lambda n: (2 * jnp.minimum(n + 1, 2) + (n + 1) % 2, 0, 0)
    n_slabs = D_FF // LANES
    assert n_slabs <= n_blocks
    upmap = lambda n: (0, jnp.minimum(n, n_slabs - 1))
    dnmap = lambda n: (jnp.minimum(n, n_slabs - 1), 0)
    tab0 = jnp.stack([cos[0:BLOCK], sa[0:BLOCK], sb[0:BLOCK]])
    in_specs = [
        pl.BlockSpec(memory_space=pltpu.SMEM),
        pl.BlockSpec((nb, BLOCK, D_MODEL), xmap),
        _const_spec((BLOCK, D_MODEL)),
        _const_spec(tab0.shape),
        pl.BlockSpec((BLOCK, LANES), tmap),
        pl.BlockSpec((BLOCK, LANES), tmap),
        pl.BlockSpec((BLOCK, LANES), tmap),
        pl.BlockSpec((1, 2 * BLOCK, 2 * BLOCK), bmap),
        _const_spec((ROWS, ROWS)),
        _const_spec((ROWS, ROWS)),
    ] + [_const_spec(a.shape) for a in (*p, q_scale)] + [
        pl.BlockSpec((D_MODEL, LANES), upmap),
        pl.BlockSpec((LANES, D_MODEL), dnmap),
    ]
    w_in, w_glu, w_out = p[1], p[9], p[12]
    out_shape = (
        jax.ShapeDtypeStruct((nb, seq, D_MODEL), F32),
        jax.ShapeDtypeStruct((nb, BLOCK, KV_WIDTH), F32),
        jax.ShapeDtypeStruct((nb, BLOCK, KV_WIDTH), F32),
        jax.ShapeDtypeStruct((nb, N_STATE), F32),
        jax.ShapeDtypeStruct((nb, N_STATE), F32),
        jax.ShapeDtypeStruct(w_up.shape, BF16),
        jax.ShapeDtypeStruct(w_down.shape, BF16),
        jax.ShapeDtypeStruct(w_in.shape, BF16),
        jax.ShapeDtypeStruct(w_out.shape, BF16),
        jax.ShapeDtypeStruct(w_glu.shape, BF16),
    )
    out_specs = (
        pl.BlockSpec((nb, BLOCK, D_MODEL), xmap),
        _const_spec((nb, BLOCK, KV_WIDTH)),
        _const_spec((nb, BLOCK, KV_WIDTH)),
        _const_spec((nb, N_STATE)),
        _const_spec((nb, N_STATE)),
        pl.BlockSpec((D_MODEL, LANES), upmap),
        pl.BlockSpec((LANES, D_MODEL), dnmap),
        _const_spec(w_in.shape), _const_spec(w_out.shape), _const_spec(w_glu.shape),
    )
    scratch = [
        pltpu.VMEM((nb, 4, 2 * BLOCK, LANES), BF16),
        pltpu.VMEM((nb, 4, 2 * BLOCK, LANES), BF16),
        pltpu.VMEM((nb, BLOCK, KV_WIDTH), F32),
        pltpu.VMEM((nb, BLOCK, KV_WIDTH), F32),
        pltpu.VMEM((N_LB, 2 * ROWS, LANES), F32),
        pltpu.VMEM((N_LB, 2 * nb, LANES), F32),
        pltpu.VMEM((ROWS, ATT_WIDTH), F32),
    ]
    return pl.pallas_call(
        _prompt_mixer_kernel,
        grid=(n_blocks,),
        in_specs=in_specs,
        out_specs=out_specs,
        out_shape=out_shape,
        scratch_shapes=scratch,
        compiler_params=pltpu.CompilerParams(
            dimension_semantics=("arbitrary",), vmem_limit_bytes=VMEM_LIMIT),
        name="prompt_mixer",
    )(sinks, x_prompt, meta_blk, tab0, cos, sa, sb, _window_bias(), perm, perm.T, *p, q_scale,
      w_up, w_down)


GROUP = 8
GROUPS_PER_STEP = 4


def _decode_mixer_kernel(
        x_ref, cos_ref, sa_ref, sb_ref, cost_ref, sat_ref, sbt_ref, sink_ref, perm_ref, permt_ref,
        ck_ref, cv_ref, stre_ref, stim_ref, wkt_ref, wvt_ref,
        gpre_ref, win_ref, ar_ref, ai_ref, wbre_ref, wbim_ref, wcre_ref, wcim_ref,
        d_ref, wglu_ref, gatt_ref, gssm_ref, wout_ref, gpost_ref,
        h1_ref, kwin_ref, vwin_ref, sre_ref, sim_ref,
        qp, knew, vnew, knew_b, vnew_b, ssm_n, att_ref, s_ref):
    step = pl.program_id(0)
    n_seq = stre_ref.shape[1]
    n_tok = ROWS // n_seq
    grp_rows = GROUP * n_tok
    last = pl.num_programs(0) - 1

    @pl.when(step == 0)
    def _project_and_ssm():
        hn = _rms(x_ref[...].reshape(ROWS, D_MODEL), gpre_ref[...]).astype(BF16)
        q = _dot(hn, win_ref[:, 0:ATT_WIDTH])
        u = _project_u(_dot(perm_ref[...], hn).astype(BF16), win_ref)
        cos, sa, sb = cos_ref[...], sa_ref[...], sb_ref[...]
        kt = _dot_nt(wkt_ref[...], hn)
        kt = (kt * cost_ref[...] + pltpu.roll(kt, ROPE_HALF, axis=0) * sat_ref[...]
              + pltpu.roll(kt, KV_WIDTH - ROPE_HALF, axis=0) * sbt_ref[...])
        vt = _dot_nt(wvt_ref[...], hn)
        knew[...] = kt
        vnew[...] = vt
        knew_b[...] = kt.astype(BF16)
        vnew_b[...] = vt.astype(BF16)
        lo = _lane_is_lo((ROWS, LANES))
        for j in range(ATT_WIDTH // LANES):
            qj = _rope(q[:, j * LANES:(j + 1) * LANES], cos, sa, sb)
            qr = pltpu.roll(qj, HALF, axis=1)
            z = jnp.zeros_like(qj)
            if j < 2:
                qp[2 * j] = jnp.where(lo, qj, z)
                qp[2 * j + 1] = jnp.where(lo, qr, z)
            else:
                qp[2 * j] = jnp.where(lo, z, qr)
                qp[2 * j + 1] = jnp.where(lo, z, qj)

        _ssm_input(u, wbre_ref, wbim_ref, s_ref)
        for l in range(N_LB):
            cols = slice(l * LANES, (l + 1) * LANES)
            ar, ai = ar_ref[:, cols], ai_ref[:, cols]
            hr, hi = stre_ref[cols, :].T, stim_ref[cols, :].T
            for t in range(n_tok):
                r_re = slice(t * n_seq, (t + 1) * n_seq)
                r_im = slice(ROWS + t * n_seq, ROWS + (t + 1) * n_seq)
                hr, hi = (ar * hr - ai * hi + s_ref[l, r_re, :],
                          ar * hi + ai * hr + s_ref[l, r_im, :])
                s_ref[l, r_re, :] = hr
                s_ref[l, r_im, :] = hi
            sre_ref[cols, :] = hr.T
            sim_ref[cols, :] = hi.T
        ssm_o = _ssm_output(s_ref, u, wcre_ref, wcim_ref, d_ref, wglu_ref)
        ssm_n[...] = _dot(permt_ref[...], _rms(ssm_o, gssm_ref[...]).astype(BF16)).astype(BF16)

    def attend_group(g, base):
        grp = pl.ds(pl.multiple_of(g * grp_rows, grp_rows), grp_rows)
        qb = jnp.concatenate([qp[h, grp, :] for h in range(N_HEADS)], axis=0).astype(BF16)
        seq_rows = [slice(base + b * KV_WIDTH, base + (b + 1) * KV_WIDTH) for b in range(GROUP)]
        kc_f = [ck_ref[r, :] for r in seq_rows]
        vc_f = [cv_ref[r, :] for r in seq_rows]
        kcat = jnp.concatenate(kc_f, axis=1).astype(BF16)
        vcat = jnp.concatenate(vc_f, axis=1).astype(BF16)
        n_q = N_HEADS * grp_rows
        n_c = GROUP * WINDOW

        r_c = lax.broadcasted_iota(jnp.int32, (n_q, n_c), 0)
        c_c = lax.broadcasted_iota(jnp.int32, (n_q, n_c), 1)
        mask_c = ((_div(c_c, WINDOW) == _mod(_div(r_c, n_tok), GROUP))
                  & (_mod(c_c, WINDOW) > _mod(r_c, n_tok)))
        lc = jnp.where(mask_c, _dot(qb, kcat), NEG)
        tile = pl.ds(pl.multiple_of(_div(g * grp_rows, LANES) * LANES, LANES), LANES)
        first = _mod(g * grp_rows, LANES)
        r_n = lax.broadcasted_iota(jnp.int32, (n_q, LANES), 0)
        c_n = lax.broadcasted_iota(jnp.int32, (n_q, LANES), 1) - first
        mask_n = ((c_n >= 0) & (c_n < grp_rows)
                  & (_div(c_n, n_tok) == _mod(_div(r_n, n_tok), GROUP))
                  & (_mod(c_n, n_tok) <= _mod(r_n, n_tok)))
        ln = jnp.where(mask_n, _dot(qb, knew_b[:, tile]), NEG)

        sink_col = sink_ref[...]
        m = jnp.maximum(jnp.maximum(jnp.max(lc, axis=-1, keepdims=True),
                                    jnp.max(ln, axis=-1, keepdims=True)), sink_col)
        ec = jnp.exp(lc - m)
        en = jnp.exp(ln - m)
        den = (jnp.sum(ec, axis=-1, keepdims=True) + jnp.sum(en, axis=-1, keepdims=True)
               + jnp.exp(sink_col - m))
        o = (_dot_nt(ec.astype(BF16), vcat) + _dot_nt(en.astype(BF16), vnew_b[:, tile])) / den

        lo = _lane_is_lo((grp_rows, LANES))
        for j in range(ATT_WIDTH // LANES):
            o_even = o[(2 * j) * grp_rows:(2 * j + 1) * grp_rows]
            o_odd = o[(2 * j + 1) * grp_rows:(2 * j + 2) * grp_rows]
            if j < 2:
                pair = jnp.where(lo, o_even, pltpu.roll(o_odd, HALF, axis=1))
            else:
                pair = jnp.where(lo, pltpu.roll(o_even, HALF, axis=1), o_odd)
            att_ref[grp, j * LANES:(j + 1) * LANES] = pair

        keep = WINDOW - n_tok
        to_tail = _mod(keep - first + LANES, LANES)
        kt_g = pltpu.roll(knew[:, tile], to_tail, axis=1)
        vt_g = pltpu.roll(vnew[:, tile], to_tail, axis=1)
        is_new = lax.broadcasted_iota(jnp.int32, (KV_WIDTH, WINDOW), 1) >= keep
        for b, rows in enumerate(seq_rows):
            k_b = kt_g if b == 0 else pltpu.roll(kt_g, LANES - n_tok * b, axis=1)
            v_b = vt_g if b == 0 else pltpu.roll(vt_g, LANES - n_tok * b, axis=1)
            kwin_ref[rows, :] = jnp.where(is_new, k_b, pltpu.roll(kc_f[b], keep, axis=1))
            vwin_ref[rows, :] = jnp.where(is_new, v_b, pltpu.roll(vc_f[b], keep, axis=1))

    for sub in range(GROUPS_PER_STEP):
        attend_group(step * GROUPS_PER_STEP + sub, sub * GROUP * KV_WIDTH)

    @pl.when(step == last)
    def _merge_out():
        h1_ref[...] = _merge(x_ref[...].reshape(ROWS, D_MODEL), att_ref[...], ssm_n[...],
                             gatt_ref, wout_ref, gpost_ref)


def _decode_mixer(x2d, tables, tables_t, sink_col, ck2d, cv2d, st_re, st_im, wkt, wvt, p):
    n_seq = st_re.shape[1]
    n_steps = n_seq // (GROUP * GROUPS_PER_STEP)
    cmap = lambda g: (g, 0)
    perm = _row_permutation(n_seq, ROWS // n_seq)
    head = (x2d, *tables, *tables_t, sink_col, perm, perm.T)
    in_specs = [_const_spec(a.shape) for a in head] + [
        pl.BlockSpec((GROUPS_PER_STEP * GROUP * KV_WIDTH, WINDOW), cmap),
        pl.BlockSpec((GROUPS_PER_STEP * GROUP * KV_WIDTH, WINDOW), cmap),
    ] + [_const_spec(a.shape) for a in (st_re, st_im, wkt, wvt, *p)]
    out_shape = (
        jax.ShapeDtypeStruct((ROWS, D_MODEL), F32),
        jax.ShapeDtypeStruct(ck2d.shape, F32),
        jax.ShapeDtypeStruct(cv2d.shape, F32),
        jax.ShapeDtypeStruct(st_re.shape, F32),
        jax.ShapeDtypeStruct(st_im.shape, F32),
    )
    out_specs = (
        _const_spec((ROWS, D_MODEL)),
        pl.BlockSpec((GROUPS_PER_STEP * GROUP * KV_WIDTH, WINDOW), cmap),
        pl.BlockSpec((GROUPS_PER_STEP * GROUP * KV_WIDTH, WINDOW), cmap),
        _const_spec(st_re.shape), _const_spec(st_im.shape),
    )
    scratch = [
        pltpu.VMEM((N_HEADS, ROWS, LANES), F32),
        pltpu.VMEM((KV_WIDTH, ROWS), F32),
        pltpu.VMEM((KV_WIDTH, ROWS), F32),
        pltpu.VMEM((KV_WIDTH, ROWS), BF16),
        pltpu.VMEM((KV_WIDTH, ROWS), BF16),
        pltpu.VMEM((ROWS, SSM_WIDTH), BF16),
        pltpu.VMEM((ROWS, ATT_WIDTH), F32),
        pltpu.VMEM((N_LB, 2 * ROWS, LANES), F32),
    ]
    return pl.pallas_call(
        _decode_mixer_kernel,
        grid=(n_steps,),
        in_specs=in_specs,
        out_specs=out_specs,
        out_shape=out_shape,
        scratch_shapes=scratch,
        compiler_params=pltpu.CompilerParams(
            dimension_semantics=("arbitrary",), vmem_limit_bytes=VMEM_LIMIT),
        name="decode_mixer",
    )(*head, ck2d, cv2d, st_re, st_im, wkt, wvt, *p)


def _mlp_kernel(xp_ref, xd_ref, gpre_ref, wup_ref, wdn_ref, gpost_ref, op_ref, od_ref):
    i = pl.program_id(0)

    def mlp(x):
        hn = _rms(x, gpre_ref[...]).astype(BF16)
        acc = None
        for c in range(D_FF // FF_CHUNK):
            cols = slice(c * FF_CHUNK, (c + 1) * FF_CHUNK)
            a = jnp.maximum(_dot(hn, wup_ref[:, cols]), 0.0)
            part = _dot((a * a).astype(BF16), wdn_ref[cols, :])
            acc = part if acc is None else acc + part
        return x + _rms(acc, gpost_ref[...])

    @pl.when(i == 0)
    def _decode_rows():
        od_ref[...] = mlp(xd_ref[...]).reshape(od_ref.shape)

    @pl.when(i > 0)
    def _prompt_tile():
        op_ref[...] = mlp(xp_ref[...])


def _mlp(xp2d, xd2d, d_shape, g_pre, w_up, w_down, g_post):
    n_tiles = xp2d.shape[0] // MLP_ROWS
    pmap = lambda i: (jnp.maximum(i - 1, 0), 0)
    return pl.pallas_call(
        _mlp_kernel,
        grid=(n_tiles + 1,),
        in_specs=[pl.BlockSpec((MLP_ROWS, D_MODEL), pmap), _const_spec(xd2d.shape),
                  _const_spec(g_pre.shape), _const_spec(w_up.shape), _const_spec(w_down.shape),
                  _const_spec(g_post.shape)],
        out_specs=(pl.BlockSpec((MLP_ROWS, D_MODEL), pmap), _const_spec(d_shape)),
        out_shape=(jax.ShapeDtypeStruct(xp2d.shape, F32), jax.ShapeDtypeStruct(d_shape, F32)),
        compiler_params=pltpu.CompilerParams(
            dimension_semantics=("arbitrary",), vmem_limit_bytes=VMEM_LIMIT),
        name="mlp",
    )(xp2d, xd2d, g_pre, w_up, w_down, g_post)


def _zoh(a_re, a_im, log_dt, b_re, b_im):
    dt = jnp.exp(log_dt)[:, None]
    mag = jnp.exp(a_re * dt)
    abar_re, abar_im = mag * jnp.cos(a_im * dt), mag * jnp.sin(a_im * dt)
    nr, ni = abar_re - 1.0, abar_im
    den = a_re * a_re + a_im * a_im
    coef_re = (nr * a_re + ni * a_im) / den
    coef_im = (ni * a_re - nr * a_im) / den
    bbar_re = coef_re[..., None] * b_re - coef_im[..., None] * b_im
    bbar_im = coef_re[..., None] * b_im + coef_im[..., None] * b_re
    return abar_re, abar_im, bbar_re, bbar_im


def _same_group_mask():
    row_g = np.arange(SSM_TILE)[:, None] // SSM_GROUP
    col_g = np.arange(STATE_TILE)[None, :] // SSM_STATE
    return row_g == col_g


def _block_diag_in(bbar):
    w = jnp.swapaxes(bbar, 1, 2).reshape(N_SSM_TILES, SSM_TILE, SSM_STATE)
    w = jnp.tile(w, (1, 1, SSM_TILE // SSM_GROUP))
    return jnp.where(_same_group_mask()[None], w, 0.0).astype(BF16)


def _block_diag_out(c):
    w = jnp.swapaxes(c, 1, 2).reshape(N_SSM_TILES, STATE_TILE, SSM_GROUP)
    w = jnp.tile(w, (1, 1, SSM_TILE // SSM_GROUP))
    return jnp.where(_same_group_mask().T[None], w, 0.0).astype(BF16)


def _rope_tables(pos):
    half = ROPE_DIM // 2
    inv = ROPE_THETA ** (-np.arange(half, dtype=np.float64) / half)
    ang = np.asarray(pos, dtype=np.float64)[:, None] * inv[None, :]
    cos, sin = np.cos(ang), np.sin(ang)
    d = np.arange(LANES) % HEAD_DIM
    f = d % half
    cos_t = np.where(d[None, :] < ROPE_DIM, cos[:, f], 1.0)
    sa_t = np.where((d[None, :] >= half) & (d[None, :] < ROPE_DIM), sin[:, f], 0.0)
    sb_t = np.where(d[None, :] < half, -sin[:, f], 0.0)
    return tuple(t.astype(np.float32) for t in (cos_t, sa_t, sb_t))


def _row_permutation(n_outer, n_inner):
    r = np.arange(n_outer * n_inner)
    perm = np.zeros((r.size, r.size), np.float32)
    perm[r, (r % n_outer) * n_inner + r // n_outer] = 1.0
    return jnp.asarray(perm, dtype=BF16)


def kernel(x_prompt, x_sample, cache_k_win, cache_v_win, state_ssm_re, state_ssm_im, meta_tokens, norm_mix_pre, w_in, attn_sinks, ssm_a_re, ssm_a_im, ssm_log_dt, ssm_b_re, ssm_b_im, ssm_c_re, ssm_c_im, ssm_d, w_glu, norm_att_out, norm_ssm_out, w_out, norm_mix_post, norm_mlp_pre, w_up, w_down, norm_mlp_post):
    depth = w_in.shape[0]
    assert depth == 1
    l = 0
    nb, seq, _ = x_prompt.shape
    n_seq, n_tok, _ = x_sample.shape
    assert n_seq * n_tok == ROWS and nb * BLOCK == ROWS and seq % BLOCK == 0

    abar_re, abar_im, bbar_re, bbar_im = _zoh(
        ssm_a_re[l], ssm_a_im[l], ssm_log_dt[l], ssm_b_re[l], ssm_b_im[l])
    row = lambda a: a.reshape(1, -1)
    q_scale = np.where(np.arange(w_in.shape[-1]) < ATT_WIDTH, HEAD_DIM ** -0.5, 1.0).astype(
        np.float32)
    params = (
        row(norm_mix_pre[l]), w_in[l],
        row(abar_re), row(abar_im),
        _block_diag_in(bbar_re), _block_diag_in(bbar_im),
        _block_diag_out(ssm_c_re[l]), _block_diag_out(-ssm_c_im[l]),
        row(ssm_d[l]), w_glu[l],
        row(norm_att_out[l]), row(norm_ssm_out[l]), w_out[l], row(norm_mix_post[l]),
    )

    front = BLOCK - N_META
    meta_blk = jnp.concatenate([jnp.zeros((front, D_MODEL), F32), meta_tokens], axis=0)
    cos, sa, sb = (jnp.asarray(t) for t in _rope_tables(np.arange(seq + BLOCK) - front))
    h1, k_last, v_last, p_re, p_im, w_up_b, w_down_b, w_in_b, w_out_b, w_glu_b = _prompt_mixer(
        x_prompt, meta_blk, cos, sa, sb, attn_sinks[l], params, jnp.asarray(row(q_scale)),
        w_up[l], w_down[l])
    mlp_params = (row(norm_mlp_pre[l]), w_up_b, w_down_b, row(norm_mlp_post[l]))
    params = params[:1] + (w_in_b,) + params[2:9] + (w_glu_b,) + params[10:12] + (
        w_out_b,) + params[13:]

    tabs = _rope_tables(PAST_LEN + np.arange(n_tok))
    tables = tuple(jnp.asarray(np.tile(t, (n_seq, 1))) for t in tabs)
    tables_t = tuple(jnp.asarray(np.tile(t.T, (1, n_seq))) for t in tabs)
    sink_col = jnp.repeat(attn_sinks[l], n_tok * GROUP).reshape(-1, 1)
    to_t = lambda a: jnp.transpose(a, (0, 2, 3, 1)).reshape(-1, WINDOW)
    from_t = lambda a, n: jnp.transpose(
        a.reshape(n, N_KV_HEADS, HEAD_DIM, WINDOW), (0, 3, 1, 2))[None]
    st_in = lambda a: jnp.transpose(a, (1, 2, 0)).reshape(N_STATE, n_seq)
    st_out = lambda a: jnp.transpose(a.reshape(SSM_GROUPS, SSM_STATE, n_seq), (2, 0, 1))[None]
    w_kv = w_in[l][:, ATT_WIDTH:ATT_WIDTH + 2 * KV_WIDTH].astype(BF16)
    h1s, kwin, vwin, s_re, s_im = _decode_mixer(
        x_sample, tables, tables_t, sink_col, to_t(cache_k_win[l]), to_t(cache_v_win[l]),
        st_in(state_ssm_re[l]), st_in(state_ssm_im[l]),
        w_kv[:, 0:KV_WIDTH].T, w_kv[:, KV_WIDTH:].T, params)
    y_prompt, y_sample = _mlp(h1.reshape(nb * seq, D_MODEL), h1s, x_sample.shape, *mlp_params)
    y_prompt = y_prompt.reshape(nb, seq, D_MODEL)

    win = from_t
    st = lambda a, n: a.reshape(1, n, SSM_GROUPS, SSM_STATE)
    return (y_prompt, y_sample,
            win(k_last, nb), win(v_last, nb), st(p_re, nb), st(p_im, nb),
            win(kwin, n_seq), win(vwin, n_seq), st_out(s_re), st_out(s_im))
```

```python
import math

import jax
import jax.numpy as jnp
import numpy as np
from jax import lax
from jax.experimental import pallas as pl
from jax.experimental.pallas import tpu as pltpu

F32 = jnp.float32
BF16 = jnp.bfloat16

N_META = 16
HEAD_DIM = 64
N_HEADS = 8
N_KV_HEADS = 2
WINDOW = 128
BLOCK = 128
ROPE_DIM = 16
ROPE_HALF = ROPE_DIM // 2
ROPE_THETA = 500000.0
SSM_GROUP = 16
SSM_GROUPS = 32
SSM_STATE = 64
PAST_LEN = 8192
EPS = 1e-6
NEG = -1e30

D_MODEL = 1024
ATT_WIDTH = 512
KV_WIDTH = 128
SSM_WIDTH = 512
N_STATE = SSM_GROUPS * SSM_STATE
D_FF = 4096

ROWS = 512
MLP_ROWS = 1024
LANES = 128
HALF = LANES // 2
SSM_TILE = 256
STATE_TILE = (SSM_TILE // SSM_GROUP) * SSM_STATE
N_SSM_TILES = SSM_WIDTH // SSM_TILE
LB_PER_TILE = STATE_TILE // LANES
N_LB = N_STATE // LANES
FF_CHUNK = 1024
VMEM_LIMIT = 56 * 1024 * 1024


def _dot(a, b):
    return jnp.dot(a, b, preferred_element_type=F32)


def _dot_nt(a, b):
    return lax.dot_general(a, b, (((1,), (1,)), ((), ())), preferred_element_type=F32)


def _rms(x, g):
    return x * lax.rsqrt(jnp.mean(x * x, axis=-1, keepdims=True) + EPS) * g


def _rope(x, cos, sa, sb):
    return (x * cos + pltpu.roll(x, ROPE_HALF, axis=1) * sa
            + pltpu.roll(x, LANES - ROPE_HALF, axis=1) * sb)


def _lane_is_lo(shape):
    return lax.broadcasted_iota(jnp.int32, shape, 1) < HALF


def _project_qkv(hn, w_in_ref):
    q = _dot(hn, w_in_ref[:, 0:ATT_WIDTH])
    kv = _dot(hn, w_in_ref[:, ATT_WIDTH:ATT_WIDTH + 2 * KV_WIDTH])
    return q, kv[:, 0:KV_WIDTH], kv[:, KV_WIDTH:2 * KV_WIDTH]


def _project_u(hn, w_in_ref):
    return _dot(hn, w_in_ref[:, ATT_WIDTH + 2 * KV_WIDTH:])


def _ssm_input_tile(ub, c, wbre_ref, wbim_ref, s_ref):
    uc = ub[:, c * SSM_TILE:(c + 1) * SSM_TILE]
    bre, bim = _dot(uc, wbre_ref[c]), _dot(uc, wbim_ref[c])
    for l in range(LB_PER_TILE):
        s_ref[c * LB_PER_TILE + l, 0:ROWS, :] = bre[:, l * LANES:(l + 1) * LANES]
        s_ref[c * LB_PER_TILE + l, ROWS:2 * ROWS, :] = bim[:, l * LANES:(l + 1) * LANES]


def _ssm_input(u, wbre_ref, wbim_ref, s_ref):
    ub = u.astype(BF16)
    for c in range(N_SSM_TILES):
        _ssm_input_tile(ub, c, wbre_ref, wbim_ref, s_ref)


def _ssm_readout_tile(s_ref, c, wcre_ref, wcim_ref):
    blocks = range(c * LB_PER_TILE, (c + 1) * LB_PER_TILE)
    hr = jnp.concatenate([s_ref[l, 0:ROWS, :].astype(BF16) for l in blocks], axis=1)
    hi = jnp.concatenate([s_ref[l, ROWS:2 * ROWS, :].astype(BF16) for l in blocks], axis=1)
    return _dot(hr, wcre_ref[c]) + _dot(hi, wcim_ref[c])


def _ssm_output(s_ref, u, wcre_ref, wcim_ref, d_ref, wglu_ref):
    ys = [_ssm_readout_tile(s_ref, c, wcre_ref, wcim_ref) for c in range(N_SSM_TILES)]
    return _ssm_gate(ys, u, d_ref, wglu_ref)


def _ssm_gate(ys, u, d_ref, wglu_ref):
    y = jnp.concatenate(ys, axis=1) + d_ref[...] * u
    z = 0.5 * y * (1.0 + jnp.tanh(math.sqrt(2.0 / math.pi) * (y + 0.044715 * (y * y * y))))
    gate = 1.0 / (1.0 + jnp.exp(-_dot(z.astype(BF16), wglu_ref[...])))
    return z * gate


def _merge(x, att, s, gatt_ref, wout_ref, gpost_ref):
    a = _rms(att, gatt_ref[...]).astype(BF16)
    m = _dot(a, wout_ref[0:ATT_WIDTH, :]) + _dot(s, wout_ref[ATT_WIDTH:, :])
    return x + _rms(m, gpost_ref[...])


def _div(x, k):
    return lax.shift_right_logical(x, int(math.log2(k)))


def _mod(x, k):
    return lax.bitwise_and(x, k - 1)


def _sink_softmax(logits, bias, sink_col):
    lm = logits + bias
    m =jnp.maximum(jnp.max(lm, axis=-1, keepdims=True), sink_col)
    e = jnp.exp(lm - m)
    den = jnp.sum(e, axis=-1, keepdims=True) + jnp.exp(sink_col - m)
    return e, den


def _prompt_mixer_kernel(
        sink_ref, x_ref, meta_ref, tab0_ref, cos_ref, sa_ref, sb_ref, bias_ref, perm_ref, permt_ref,
        gpre_ref, winf_ref,
        ar_ref, ai_ref, wbre_ref, wbim_ref, wcre_ref, wcim_ref, d_ref, wgluf_ref,
        gatt_ref, gssm_ref, woutf_ref, gpost_ref, qscale_ref, wupf_ref, wdnf_ref,
        h1_ref, klast_ref, vlast_ref, sre_ref, sim_ref, wupb_ref, wdnb_ref,
        win_ref, wout_ref, wglu_ref, wkt_ref, wvt_ref,
        kbuf, vbuf, kcur, vcur, s_ref, hstate, att_ref):
    n = pl.program_id(0)
    wupb_ref[...] = wupf_ref[...].astype(BF16)
    wdnb_ref[...] = wdnf_ref[...].astype(BF16)
    nb = x_ref.shape[0]
    assert 2 * nb == 8
    last = pl.num_programs(0) - 1
    lo = _lane_is_lo((BLOCK, LANES))
    is_re = lax.broadcasted_iota(jnp.int32, (2 * nb, LANES), 0) < nb
    a1, a2 = [], []
    for l in range(N_LB):
        cols = slice(l * LANES, (l + 1) * LANES)
        ai = jnp.broadcast_to(ai_ref[:, cols], (2 * nb, LANES))
        a1.append(jnp.broadcast_to(ar_ref[:, cols], (2 * nb, LANES)))
        a2.append(jnp.where(is_re, -ai, ai))

    def swap(t):
        return pltpu.roll(t, nb, axis=0)

    def variants(t):
        tr = pltpu.roll(t, HALF, axis=1)
        z = jnp.zeros_like(t)
        return (jnp.where(lo, t, z), jnp.where(lo, z, tr),
                jnp.where(lo, tr, z), jnp.where(lo, z, t))

    @pl.when(n == 0)
    def _meta_block():
        win_ref[...] = (winf_ref[...] * qscale_ref[...]).astype(BF16)
        wout_ref[...] = woutf_ref[...].astype(BF16)
        wglu_ref[...] = wgluf_ref[...].astype(BF16)
        for out_ref, c0 in ((wkt_ref, ATT_WIDTH), (wvt_ref, ATT_WIDTH + KV_WIDTH)):
            for i in range(D_MODEL // LANES):
                blk = winf_ref[i * LANES:(i + 1) * LANES, c0:c0 + KV_WIDTH]
                out_ref[:, i * LANES:(i + 1) * LANES] = blk.T.astype(BF16)
        kbuf[...] = jnp.zeros_like(kbuf)
        vbuf[...] = jnp.zeros_like(vbuf)
        hm = _rms(meta_ref[...], gpre_ref[...]).astype(BF16)
        _, k0, v0 = _project_qkv(hm, win_ref)
        k0 = _rope(k0, tab0_ref[0], tab0_ref[1], tab0_ref[2])
        for i, (kv_, vv_) in enumerate(zip(variants(k0), variants(v0))):
            for b in range(nb):
                kbuf[b, i, 0:BLOCK, :] = kv_.astype(BF16)
                vbuf[b, i, 0:BLOCK, :] = vv_.astype(BF16)
        um = _project_u(hm, win_ref).astype(BF16)
        for c in range(N_SSM_TILES):
            uc = um[:, c * SSM_TILE:(c + 1) * SSM_TILE]
            bre, bim = _dot(uc, wbre_ref[c]), _dot(uc, wbim_ref[c])
            for j in range(LB_PER_TILE):
                l = c * LB_PER_TILE + j
                cols = slice(j * LANES, (j + 1) * LANES)
                h = jnp.zeros((2 * nb, LANES), F32)
                for t in range(BLOCK - N_META, BLOCK):
                    x_t = jnp.where(is_re, jnp.broadcast_to(bre[t:t + 1, cols], h.shape),
                                    jnp.broadcast_to(bim[t:t + 1, cols], h.shape))
                    h = a1[l] * h + a2[l] * swap(h) + x_t
                hstate[l] = h

    x = x_ref[...].reshape(ROWS, D_MODEL)
    hn = _rms(x, gpre_ref[...]).astype(BF16)
    cos, sa, sb = cos_ref[...], sa_ref[...], sb_ref[...]

    u = _project_u(_dot(perm_ref[...], hn).astype(BF16), win_ref)

    hfin = []

    def chains(blocks):
        for l in blocks:
            h = hstate[l]
            g = swap(h)
            a2n = -a2[l]
            for i in range(BLOCK // 2):
                r_re = slice(i * 2 * nb, (i + 1) * 2 * nb)
                r_im = slice(ROWS + i * 2 * nb, ROWS + (i + 1) * 2 * nb)
                re, im_s = s_ref[l, r_re, :], swap(s_ref[l, r_im, :])
                h0 = a1[l] * h + a2[l] * g + jnp.where(is_re, re, im_s)
                g0 = swap(h0)
                g = a1[l] * g0 + a2n * h0 + jnp.where(is_re, im_s, re)
                h = swap(g)
                s_ref[l, r_re, :] = jnp.where(is_re, h0, g)
                s_ref[l, r_im, :] = jnp.where(is_re, g0, h)
            hstate[l] = h
            hfin.append(h)

    mask = bias_ref[0]
    cur = pl.ds(pl.multiple_of(_mod(n + 1, 2) * BLOCK, BLOCK), BLOCK)
    top = lax.broadcasted_iota(jnp.int32, (2 * BLOCK, 1), 0) < BLOCK

    def attend(b):
        rows = slice(b * BLOCK, (b + 1) * BLOCK)
        kb = _rope(k[rows], cos, sa, sb)
        vb = v[rows]
        for i, (kv_, vv_) in enumerate(zip(variants(kb), variants(vb))):
            kbuf[b, i, cur, :] = kv_.astype(BF16)
            vbuf[b, i, cur, :] = vv_.astype(BF16)

        kcur[b] = kb
        vcur[b] = vb

        qs = []
        for j in range(ATT_WIDTH // LANES):
            qs.append(_rope(q[rows, j * LANES:(j + 1) * LANES], cos, sa, sb).astype(BF16))
        for g in range(N_KV_HEADS):
            qst = jnp.concatenate([qs[2 * g], qs[2 * g + 1]], axis=0)
            o = None
            for half in range(2):
                var = 2 * g + half
                h_top, h_bot = 4 * g + half, 4 * g + 2 + half
                sink_col = jnp.where(top, sink_ref[h_top], sink_ref[h_bot])
                e, den = _sink_softmax(_dot_nt(qst, kbuf[b, var]), mask, sink_col)
                part = _dot(e.astype(BF16), vbuf[b, var]) / den
                o = part if o is None else o + part
            att_ref[rows, (2 * g) * LANES:(2 * g + 1) * LANES] = o[0:BLOCK]
            att_ref[rows, (2 * g + 1) * LANES:(2 * g + 2) * LANES] = o[BLOCK:2 * BLOCK]

    _ssm_input(u, wbre_ref, wbim_ref, s_ref)
    chains(range(N_LB))
    q, k, v = _project_qkv(hn, win_ref)
    ssm_o = _ssm_output(s_ref, u, wcre_ref, wcim_ref, d_ref, wglu_ref)
    ssm_n = _dot(permt_ref[...], _rms(ssm_o, gssm_ref[...]).astype(BF16)).astype(BF16)
    for b in range(nb):
        attend(b)

    h1 = _merge(x, att_ref[...], ssm_n, gatt_ref, wout_ref, gpost_ref)
    h1_ref[...] = h1.reshape(h1_ref.shape)

    @pl.when(n == last)
    def _emit_state():
        for b in range(nb):
            klast_ref[b] = kcur[b].T
            vlast_ref[b] = vcur[b].T
        for l in range(N_LB):
            sre_ref[:, l * LANES:(l + 1) * LANES] = hfin[l][0:nb]
            sim_ref[:, l * LANES:(l + 1) * LANES] = hfin[l][nb:2 * nb]


def _window_bias():
    r = (np.arange(2 * BLOCK) % BLOCK)[:, None]
    phys = np.arange(2 * BLOCK)[None, :]
    tables = []
    for c_min in (2 * BLOCK - N_META, BLOCK - N_META, 0):
        for parity in (0, 1):
            c = phys if parity == 1 else (phys + BLOCK) % (2 * BLOCK)
            ok = (c > r) & (c <= r + WINDOW) & (c >= c_min)
            tables.append(np.where(ok, 0.0, NEG))
    return jnp.asarray(np.stack(tables), dtype=F32)


def _const_spec(shape):
    zeros = (0,) * len(shape)
    return pl.BlockSpec(shape, lambda *_: zeros)


def _prompt_mixer(x_prompt, meta_blk, cos, sa, sb, sinks, p, q_scale, w_up, w_down):
    nb, seq, _ = x_prompt.shape
    n_blocks = seq // BLOCK
    perm = _row_permutation(nb, BLOCK)
    xmap = lambda n: (0, n, 0)
    tmap = lambda n: (n + 1, 0)
    bmap = lambda n: (2 * jnp.minimum(n + 1, 2) + (n + 1) % 2, 0, 0)
    n_slabs = D_FF // LANES
    assert n_slabs <= n_blocks
    upmap = lambda n: (0, jnp.minimum(n, n_slabs - 1))
    dnmap = lambda n: (jnp.minimum(n, n_slabs - 1), 0)
    tab0 = jnp.stack([cos[0:BLOCK], sa[0:BLOCK], sb[0:BLOCK]])
    in_specs = [
        pl.BlockSpec(memory_space=pltpu.SMEM),
        pl.BlockSpec((nb, BLOCK, D_MODEL), xmap),
        _const_spec((BLOCK, D_MODEL)),
        _const_spec(tab0.shape),
        pl.BlockSpec((BLOCK, LANES), tmap),
        pl.BlockSpec((BLOCK, LANES), tmap),
        pl.BlockSpec((BLOCK, LANES), tmap),
        pl.BlockSpec((1, 2 * BLOCK, 2 * BLOCK), bmap),
        _const_spec((ROWS, ROWS)),
        _const_spec((ROWS, ROWS)),
    ] + [_const_spec(a.shape) for a in (*p, q_scale)] + [
        pl.BlockSpec((D_MODEL, LANES), upmap),
        pl.BlockSpec((LANES, D_MODEL), dnmap),
    ]
    w_in, w_glu, w_out = p[1], p[9], p[12]
    out_shape = (
        jax.ShapeDtypeStruct((nb, seq, D_MODEL), F32),
        jax.ShapeDtypeStruct((nb, BLOCK, KV_WIDTH), F32),
        jax.ShapeDtypeStruct((nb, BLOCK, KV_WIDTH), F32),
        jax.ShapeDtypeStruct((nb, N_STATE), F32),
        jax.ShapeDtypeStruct((nb, N_STATE), F32),
        jax.ShapeDtypeStruct(w_up.shape, BF16),
        jax.ShapeDtypeStruct(w_down.shape, BF16),
        jax.ShapeDtypeStruct(w_in.shape, BF16),
        jax.ShapeDtypeStruct(w_out.shape, BF16),
        jax.ShapeDtypeStruct(w_glu.shape, BF16),
        jax.ShapeDtypeStruct((KV_WIDTH, D_MODEL), BF16),
        jax.ShapeDtypeStruct((KV_WIDTH, D_MODEL), BF16),
    )
    out_specs = (
        pl.BlockSpec((nb, BLOCK, D_MODEL), xmap),
        _const_spec((nb, BLOCK, KV_WIDTH)),
        _const_spec((nb, BLOCK, KV_WIDTH)),
        _const_spec((nb, N_STATE)),
        _const_spec((nb, N_STATE)),
        pl.BlockSpec((D_MODEL, LANES), upmap),
        pl.BlockSpec((LANES, D_MODEL), dnmap),
        _const_spec(w_in.shape), _const_spec(w_out.shape), _const_spec(w_glu.shape),
        _const_spec((KV_WIDTH, D_MODEL)), _const_spec((KV_WIDTH, D_MODEL)),
    )
    scratch = [
        pltpu.VMEM((nb, 4, 2 * BLOCK, LANES), BF16),
        pltpu.VMEM((nb, 4, 2 * BLOCK, LANES), BF16),
        pltpu.VMEM((nb, BLOCK, KV_WIDTH), F32),
        pltpu.VMEM((nb, BLOCK, KV_WIDTH), F32),
        pltpu.VMEM((N_LB, 2 * ROWS, LANES), F32),
        pltpu.VMEM((N_LB, 2 * nb, LANES), F32),
        pltpu.VMEM((ROWS, ATT_WIDTH), F32),
    ]
    return pl.pallas_call(
        _prompt_mixer_kernel,
        grid=(n_blocks,),
        in_specs=in_specs,
        out_specs=out_specs,
        out_shape=out_shape,
        scratch_shapes=scratch,
        compiler_params=pltpu.CompilerParams(
            dimension_semantics=("arbitrary",), vmem_limit_bytes=VMEM_LIMIT),
        name="prompt_mixer",
    )(sinks, x_prompt, meta_blk, tab0, cos, sa, sb, _window_bias(), perm, perm.T, *p, q_scale,
      w_up, w_down)


GROUP = 8
GROUPS_PER_STEP = 4


def _decode_mixer_kernel(
        x_ref, cos_ref, sa_ref, sb_ref, cost_ref, sat_ref, sbt_ref, sink_ref, perm_ref, permt_ref,
        ck_ref, cv_ref, stre_ref, stim_ref, wkt_ref, wvt_ref,
        gpre_ref, win_ref, ar_ref, ai_ref, wbre_ref, wbim_ref, wcre_ref, wcim_ref,
        d_ref, wglu_ref, gatt_ref, gssm_ref, wout_ref, gpost_ref,
        h1_ref, kwin_ref, vwin_ref, sre_ref, sim_ref,
        qp, knew, vnew, knew_b, vnew_b, ssm_n, att_ref, s_ref):
    step = pl.program_id(0)
    n_seq = stre_ref.shape[1]
    n_tok = ROWS // n_seq
    grp_rows = GROUP * n_tok
    last = pl.num_programs(0) - 1

    @pl.when(step == 0)
    def _project_and_ssm():
        hn = _rms(x_ref[...].reshape(ROWS, D_MODEL), gpre_ref[...]).astype(BF16)
        q = _dot(hn, win_ref[:, 0:ATT_WIDTH])
        u = _project_u(_dot(perm_ref[...], hn).astype(BF16), win_ref)
        cos, sa, sb = cos_ref[...], sa_ref[...], sb_ref[...]
        kt = _dot_nt(wkt_ref[...], hn)
        kt = (kt * cost_ref[...] + pltpu.roll(kt, ROPE_HALF, axis=0) * sat_ref[...]
              + pltpu.roll(kt, KV_WIDTH - ROPE_HALF, axis=0) * sbt_ref[...])
        vt = _dot_nt(wvt_ref[...], hn)
        knew[...] = kt
        vnew[...] = vt
        knew_b[...] = kt.astype(BF16)
        vnew_b[...] = vt.astype(BF16)
        lo = _lane_is_lo((ROWS, LANES))
        for j in range(ATT_WIDTH // LANES):
            qj = _rope(q[:, j * LANES:(j + 1) * LANES], cos, sa, sb)
            qr = pltpu.roll(qj, HALF, axis=1)
            z = jnp.zeros_like(qj)
            if j < 2:
                qp[2 * j] = jnp.where(lo, qj, z)
                qp[2 * j + 1] = jnp.where(lo, qr, z)
            else:
                qp[2 * j] = jnp.where(lo, z, qr)
                qp[2 * j + 1] = jnp.where(lo, z, qj)

        _ssm_input(u, wbre_ref, wbim_ref, s_ref)
        for l in range(N_LB):
            cols = slice(l * LANES, (l + 1) * LANES)
            ar, ai = ar_ref[:, cols], ai_ref[:, cols]
            hr, hi = stre_ref[cols, :].T, stim_ref[cols, :].T
            for t in range(n_tok):
                r_re = slice(t * n_seq, (t + 1) * n_seq)
                r_im = slice(ROWS + t * n_seq, ROWS + (t + 1) * n_seq)
                hr, hi = (ar * hr - ai * hi + s_ref[l, r_re, :],
                          ar * hi + ai * hr + s_ref[l, r_im, :])
                s_ref[l, r_re, :] = hr
                s_ref[l, r_im, :] = hi
            sre_ref[cols, :] = hr.T
            sim_ref[cols, :] = hi.T
        ssm_o = _ssm_output(s_ref, u, wcre_ref, wcim_ref, d_ref, wglu_ref)
        ssm_n[...] = _dot(permt_ref[...], _rms(ssm_o, gssm_ref[...]).astype(BF16)).astype(BF16)

    def attend_group(g, base):
        grp = pl.ds(pl.multiple_of(g * grp_rows, grp_rows), grp_rows)
        qb = jnp.concatenate([qp[h, grp, :] for h in range(N_HEADS)], axis=0).astype(BF16)
        seq_rows = [slice(base + b * KV_WIDTH, base + (b + 1) * KV_WIDTH) for b in range(GROUP)]
        kc_f = [ck_ref[r, :] for r in seq_rows]
        vc_f = [cv_ref[r, :] for r in seq_rows]
        kcat = jnp.concatenate(kc_f, axis=1).astype(BF16)
        vcat = jnp.concatenate(vc_f, axis=1).astype(BF16)
        n_q = N_HEADS * grp_rows
        n_c = GROUP * WINDOW

        r_c = lax.broadcasted_iota(jnp.int32, (n_q, n_c), 0)
        c_c = lax.broadcasted_iota(jnp.int32, (n_q, n_c), 1)
        mask_c = ((_div(c_c, WINDOW) == _mod(_div(r_c, n_tok), GROUP))
                  & (_mod(c_c, WINDOW) > _mod(r_c, n_tok)))
        lc = jnp.where(mask_c, _dot(qb, kcat), NEG)
        tile = pl.ds(pl.multiple_of(_div(g * grp_rows, LANES) * LANES, LANES), LANES)
        first = _mod(g * grp_rows, LANES)
        r_n = lax.broadcasted_iota(jnp.int32, (n_q, LANES), 0)
        c_n = lax.broadcasted_iota(jnp.int32, (n_q, LANES), 1) - first
        mask_n = ((c_n >= 0) & (c_n < grp_rows)
                  & (_div(c_n, n_tok) == _mod(_div(r_n, n_tok), GROUP))
                  & (_mod(c_n, n_tok) <= _mod(r_n, n_tok)))
        ln = jnp.where(mask_n, _dot(qb, knew_b[:, tile]), NEG)

        sink_col = sink_ref[...]
        m = jnp.maximum(jnp.maximum(jnp.max(lc, axis=-1, keepdims=True),
                                    jnp.max(ln, axis=-1, keepdims=True)), sink_col)
        ec = jnp.exp(lc - m)
        en = jnp.exp(ln - m)
        den = (jnp.sum(ec, axis=-1, keepdims=True) + jnp.sum(en, axis=-1, keepdims=True)
               + jnp.exp(sink_col - m))
        o = (_dot_nt(ec.astype(BF16), vcat) + _dot_nt(en.astype(BF16), vnew_b[:, tile])) / den

        lo = _lane_is_lo((grp_rows, LANES))
        for j in range(ATT_WIDTH // LANES):
            o_even = o[(2 * j) * grp_rows:(2 * j + 1) * grp_rows]
            o_odd = o[(2 * j + 1) * grp_rows:(2 * j + 2) * grp_rows]
            if j < 2:
                pair = jnp.where(lo, o_even, pltpu.roll(o_odd, HALF, axis=1))
            else:
                pair = jnp.where(lo, pltpu.roll(o_even, HALF, axis=1), o_odd)
            att_ref[grp, j * LANES:(j + 1) * LANES] = pair

        keep = WINDOW - n_tok
        to_tail = _mod(keep - first + LANES, LANES)
        kt_g = pltpu.roll(knew[:, tile], to_tail, axis=1)
        vt_g = pltpu.roll(vnew[:, tile], to_tail, axis=1)
        is_new = lax.broadcasted_iota(jnp.int32, (KV_WIDTH, WINDOW), 1) >= keep
        for b, rows in enumerate(seq_rows):
            k_b = kt_g if b == 0 else pltpu.roll(kt_g, LANES - n_tok * b, axis=1)
            v_b = vt_g if b == 0 else pltpu.roll(vt_g, LANES - n_tok * b, axis=1)
            kwin_ref[rows, :] = jnp.where(is_new, k_b, pltpu.roll(kc_f[b], keep, axis=1))
            vwin_ref[rows, :] = jnp.where(is_new, v_b, pltpu.roll(vc_f[b], keep, axis=1))

    for sub in range(GROUPS_PER_STEP):
        attend_group(step * GROUPS_PER_STEP + sub, sub * GROUP * KV_WIDTH)

    @pl.when(step == last)
    def _merge_out():
        h1_ref[...] = _merge(x_ref[...].reshape(ROWS, D_MODEL), att_ref[...], ssm_n[...],
                             gatt_ref, wout_ref, gpost_ref)


def _decode_mixer(x2d, tables, tables_t, sink_col, ck2d, cv2d, st_re, st_im, wkt, wvt, p):
    n_seq = st_re.shape[1]
    n_steps = n_seq // (GROUP * GROUPS_PER_STEP)
    cmap = lambda g: (g, 0)
    perm = _row_permutation(n_seq, ROWS // n_seq)
    head = (x2d, *tables, *tables_t, sink_col, perm, perm.T)
    in_specs = [_const_spec(a.shape) for a in head] + [
        pl.BlockSpec((GROUPS_PER_STEP * GROUP * KV_WIDTH, WINDOW), cmap),
        pl.BlockSpec((GROUPS_PER_STEP * GROUP * KV_WIDTH, WINDOW), cmap),
    ] + [_const_spec(a.shape) for a in (st_re, st_im, wkt, wvt, *p)]
    out_shape = (
        jax.ShapeDtypeStruct((ROWS, D_MODEL), F32),
        jax.ShapeDtypeStruct(ck2d.shape, F32),
        jax.ShapeDtypeStruct(cv2d.shape, F32),
        jax.ShapeDtypeStruct(st_re.shape, F32),
        jax.ShapeDtypeStruct(st_im.shape, F32),
    )
    out_specs = (
        _const_spec((ROWS, D_MODEL)),
        pl.BlockSpec((GROUPS_PER_STEP * GROUP * KV_WIDTH, WINDOW), cmap),
        pl.BlockSpec((GROUPS_PER_STEP * GROUP * KV_WIDTH, WINDOW), cmap),
        _const_spec(st_re.shape), _const_spec(st_im.shape),
    )
    scratch = [
        pltpu.VMEM((N_HEADS, ROWS, LANES), F32),
        pltpu.VMEM((KV_WIDTH, ROWS), F32),
        pltpu.VMEM((KV_WIDTH, ROWS), F32),
        pltpu.VMEM((KV_WIDTH, ROWS), BF16),
        pltpu.VMEM((KV_WIDTH, ROWS), BF16),
        pltpu.VMEM((ROWS, SSM_WIDTH), BF16),
        pltpu.VMEM((ROWS, ATT_WIDTH), F32),
        pltpu.VMEM((N_LB, 2 * ROWS, LANES), F32),
    ]
    return pl.pallas_call(
        _decode_mixer_kernel,
        grid=(n_steps,),
        in_specs=in_specs,
        out_specs=out_specs,
        out_shape=out_shape,
        scratch_shapes=scratch,
        compiler_params=pltpu.CompilerParams(
            dimension_semantics=("arbitrary",), vmem_limit_bytes=VMEM_LIMIT),
        name="decode_mixer",
    )(*head, ck2d, cv2d, st_re, st_im, wkt, wvt, *p)


def _mlp_kernel(xp_ref, xd_ref, gpre_ref, wup_ref, wdn_ref, gpost_ref, op_ref, od_ref):
    i = pl.program_id(0)

    def mlp(x):
        hn = _rms(x, gpre_ref[...]).astype(BF16)
        acc = None
        for c in range(D_FF // FF_CHUNK):
            cols = slice(c * FF_CHUNK, (c + 1) * FF_CHUNK)
            a = jnp.maximum(_dot(hn, wup_ref[:, cols]), 0.0)
            part = _dot((a * a).astype(BF16), wdn_ref[cols, :])
            acc = part if acc is None else acc + part
        return x + _rms(acc, gpost_ref[...])

    @pl.when(i == 0)
    def _decode_rows():
        od_ref[...] = mlp(xd_ref[...]).reshape(od_ref.shape)

    @pl.when(i > 0)
    def _prompt_tile():
        op_ref[...] = mlp(xp_ref[...])


def _mlp(xp2d, xd2d, d_shape, g_pre, w_up, w_down, g_post):
    n_tiles = xp2d.shape[0] // MLP_ROWS
    pmap = lambda i: (jnp.maximum(i - 1, 0), 0)
    return pl.pallas_call(
        _mlp_kernel,
        grid=(n_tiles + 1,),
        in_specs=[pl.BlockSpec((MLP_ROWS, D_MODEL), pmap), _const_spec(xd2d.shape),
                  _const_spec(g_pre.shape), _const_spec(w_up.shape), _const_spec(w_down.shape),
                  _const_spec(g_post.shape)],
        out_specs=(pl.BlockSpec((MLP_ROWS, D_MODEL), pmap), _const_spec(d_shape)),
        out_shape=(jax.ShapeDtypeStruct(xp2d.shape, F32), jax.ShapeDtypeStruct(d_shape, F32)),
        compiler_params=pltpu.CompilerParams(
            dimension_semantics=("arbitrary",), vmem_limit_bytes=VMEM_LIMIT),
        name="mlp",
    )(xp2d, xd2d, g_pre, w_up, w_down, g_post)


def _zoh(a_re, a_im, log_dt, b_re, b_im):
    dt = jnp.exp(log_dt)[:, None]
    mag = jnp.exp(a_re * dt)
    abar_re, abar_im = mag * jnp.cos(a_im * dt), mag * jnp.sin(a_im * dt)
    nr, ni = abar_re - 1.0, abar_im
    den = a_re * a_re + a_im * a_im
    coef_re = (nr * a_re + ni * a_im) / den
    coef_im = (ni * a_re - nr * a_im) / den
    bbar_re = coef_re[..., None] * b_re - coef_im[..., None] * b_im
    bbar_im = coef_re[..., None] * b_im + coef_im[..., None] * b_re
    return abar_re, abar_im, bbar_re, bbar_im


def _same_group_mask():
    row_g = np.arange(SSM_TILE)[:, None] // SSM_GROUP
    col_g = np.arange(STATE_TILE)[None, :] // SSM_STATE
    return row_g == col_g


def _block_diag_in(bbar):
    w = jnp.swapaxes(bbar, 1, 2).reshape(N_SSM_TILES, SSM_TILE, SSM_STATE)
    w = jnp.tile(w, (1, 1, SSM_TILE // SSM_GROUP))
    return jnp.where(_same_group_mask()[None], w, 0.0).astype(BF16)


def _block_diag_out(c):
    w = jnp.swapaxes(c, 1, 2).reshape(N_SSM_TILES, STATE_TILE, SSM_GROUP)
    w = jnp.tile(w, (1, 1, SSM_TILE // SSM_GROUP))
    return jnp.where(_same_group_mask().T[None], w, 0.0).astype(BF16)


def _rope_tables(pos):
    half = ROPE_DIM // 2
    inv = ROPE_THETA ** (-np.arange(half, dtype=np.float64) / half)
    ang = np.asarray(pos, dtype=np.float64)[:, None] * inv[None, :]
    cos, sin = np.cos(ang), np.sin(ang)
    d = np.arange(LANES) % HEAD_DIM
    f = d % half
    cos_t = np.where(d[None, :] < ROPE_DIM, cos[:, f], 1.0)
    sa_t = np.where((d[None, :] >= half) & (d[None, :] < ROPE_DIM), sin[:, f], 0.0)
    sb_t = np.where(d[None, :] < half, -sin[:, f], 0.0)
    return tuple(t.astype(np.float32) for t in (cos_t, sa_t, sb_t))


def _row_permutation(n_outer, n_inner):
    r = np.arange(n_outer * n_inner)
    perm = np.zeros((r.size, r.size), np.float32)
    perm[r, (r % n_outer) * n_inner + r // n_outer] = 1.0
    return jnp.asarray(perm, dtype=BF16)


def kernel(x_prompt, x_sample, cache_k_win, cache_v_win, state_ssm_re, state_ssm_im, meta_tokens, norm_mix_pre, w_in, attn_sinks, ssm_a_re, ssm_a_im, ssm_log_dt, ssm_b_re, ssm_b_im, ssm_c_re, ssm_c_im, ssm_d, w_glu, norm_att_out, norm_ssm_out, w_out, norm_mix_post, norm_mlp_pre, w_up, w_down, norm_mlp_post):
    depth = w_in.shape[0]
    assert depth == 1
    l = 0
    nb, seq, _ = x_prompt.shape
    n_seq, n_tok, _ = x_sample.shape
    assert n_seq * n_tok == ROWS and nb * BLOCK == ROWS and seq % BLOCK == 0

    abar_re, abar_im, bbar_re, bbar_im = _zoh(
        ssm_a_re[l], ssm_a_im[l], ssm_log_dt[l], ssm_b_re[l], ssm_b_im[l])
    row = lambda a: a.reshape(1, -1)
    q_scale = np.where(np.arange(w_in.shape[-1]) < ATT_WIDTH, HEAD_DIM ** -0.5, 1.0).astype(
        np.float32)
    params = (
        row(norm_mix_pre[l]), w_in[l],
        row(abar_re), row(abar_im),
        _block_diag_in(bbar_re), _block_diag_in(bbar_im),
        _block_diag_out(ssm_c_re[l]), _block_diag_out(-ssm_c_im[l]),
        row(ssm_d[l]), w_glu[l],
        row(norm_att_out[l]), row(norm_ssm_out[l]), w_out[l], row(norm_mix_post[l]),
    )

    front = BLOCK - N_META
    meta_blk = jnp.concatenate([jnp.zeros((front, D_MODEL), F32), meta_tokens], axis=0)
    cos, sa, sb = (jnp.asarray(t) for t in _rope_tables(np.arange(seq + BLOCK) - front))
    (h1, k_last, v_last, p_re, p_im, w_up_b, w_down_b, w_in_b, w_out_b, w_glu_b,
     w_k_t, w_v_t) = _prompt_mixer(
        x_prompt, meta_blk, cos, sa, sb, attn_sinks[l], params, jnp.asarray(row(q_scale)),
        w_up[l], w_down[l])
    mlp_params = (row(norm_mlp_pre[l]), w_up_b, w_down_b, row(norm_mlp_post[l]))
    params = params[:1] + (w_in_b,) + params[2:9] + (w_glu_b,) + params[10:12] + (
        w_out_b,) + params[13:]

    tabs = _rope_tables(PAST_LEN + np.arange(n_tok))
    tables = tuple(jnp.asarray(np.tile(t, (n_seq, 1))) for t in tabs)
    tables_t = tuple(jnp.asarray(np.tile(t.T, (1, n_seq))) for t in tabs)
    sink_col = jnp.repeat(attn_sinks[l], n_tok * GROUP).reshape(-1, 1)
    to_t = lambda a: jnp.transpose(a, (0, 2, 3, 1)).reshape(-1, WINDOW)
    from_t = lambda a, n: jnp.transpose(
        a.reshape(n, N_KV_HEADS, HEAD_DIM, WINDOW), (0, 3, 1, 2))[None]
    st_in = lambda a: jnp.transpose(a, (1, 2, 0)).reshape(N_STATE, n_seq)
    st_out = lambda a: jnp.transpose(a.reshape(SSM_GROUPS, SSM_STATE, n_seq), (2, 0, 1))[None]
    h1s, kwin, vwin, s_re, s_im = _decode_mixer(
        x_sample, tables, tables_t, sink_col, to_t(cache_k_win[l]), to_t(cache_v_win[l]),
        st_in(state_ssm_re[l]), st_in(state_ssm_im[l]), w_k_t, w_v_t, params)
    y_prompt, y_sample = _mlp(h1.reshape(nb * seq, D_MODEL), h1s, x_sample.shape, *mlp_params)
    y_prompt = y_prompt.reshape(nb, seq, D_MODEL)

    win = from_t
    st = lambda a, n: a.reshape(1, n, SSM_GROUPS, SSM_STATE)
    return (y_prompt, y_sample,
            win(k_last, nb), win(v_last, nb), st(p_re, nb), st(p_im, nb),
            win(kwin, n_seq), win(vwin, n_seq), st_out(s_re), st_out(s_im))
```

```python
import math

import jax
import jax.numpy as jnp
import numpy as np
from jax import lax
from jax.experimental import pallas as pl
from jax.experimental.pallas import tpu as pltpu

F32 = jnp.float32
BF16 = jnp.bfloat16

N_META = 16
HEAD_DIM = 64
N_HEADS = 8
N_KV_HEADS = 2
WINDOW = 128
BLOCK = 128
ROPE_DIM = 16
ROPE_HALF = ROPE_DIM // 2
ROPE_THETA = 500000.0
SSM_GROUP = 16
SSM_GROUPS = 32
SSM_STATE = 64
PAST_LEN = 8192
EPS = 1e-6
NEG = -1e30
LOG2E = math.log2(math.e)

D_MODEL = 1024
ATT_WIDTH = 512
KV_WIDTH = 128
SSM_WIDTH = 512
N_STATE = SSM_GROUPS * SSM_STATE
D_FF = 4096

ROWS = 512
MLP_ROWS = 1024
LANES = 128
HALF = LANES // 2
SSM_TILE = 256
STATE_TILE = (SSM_TILE // SSM_GROUP) * SSM_STATE
N_SSM_TILES = SSM_WIDTH // SSM_TILE
LB_PER_TILE = STATE_TILE // LANES
N_LB = N_STATE // LANES
FF_CHUNK = 1024
VMEM_LIMIT = 56 * 1024 * 1024


def _dot(a, b):
    return jnp.dot(a, b, preferred_element_type=F32)


def _dot_nt(a, b):
    return lax.dot_general(a, b, (((1,), (1,)), ((), ())), preferred_element_type=F32)


def _rms(x, g):
    return x * lax.rsqrt(jnp.mean(x * x, axis=-1, keepdims=True) + EPS) * g


def _rope(x, cos, sa, sb):
    return (x * cos + pltpu.roll(x, ROPE_HALF, axis=1) * sa
            + pltpu.roll(x, LANES - ROPE_HALF, axis=1) * sb)


def _lane_is_lo(shape):
    return lax.broadcasted_iota(jnp.int32, shape, 1) < HALF


def _project_qkv(hn, w_in_ref):
    q = _dot(hn, w_in_ref[:, 0:ATT_WIDTH])
    kv = _dot(hn, w_in_ref[:, ATT_WIDTH:ATT_WIDTH + 2 * KV_WIDTH])
    return q, kv[:, 0:KV_WIDTH], kv[:, KV_WIDTH:2 * KV_WIDTH]


def _project_u(hn, w_in_ref):
    return _dot(hn, w_in_ref[:, ATT_WIDTH + 2 * KV_WIDTH:])


def _ssm_input_tile(ub, c, wbre_ref, wbim_ref, s_ref):
    uc = ub[:, c * SSM_TILE:(c + 1) * SSM_TILE]
    bre, bim = _dot(uc, wbre_ref[c]), _dot(uc, wbim_ref[c])
    for l in range(LB_PER_TILE):
        s_ref[c * LB_PER_TILE + l, 0:ROWS, :] = bre[:, l * LANES:(l + 1) * LANES]
        s_ref[c * LB_PER_TILE + l, ROWS:2 * ROWS, :] = bim[:, l * LANES:(l + 1) * LANES]


def _ssm_input(u, wbre_ref, wbim_ref, s_ref):
    ub = u.astype(BF16)
    for c in range(N_SSM_TILES):
        _ssm_input_tile(ub, c, wbre_ref, wbim_ref, s_ref)


def _ssm_readout_tile(s_ref, c, wcre_ref, wcim_ref):
    blocks = range(c * LB_PER_TILE, (c + 1) * LB_PER_TILE)
    hr = jnp.concatenate([s_ref[l, 0:ROWS, :].astype(BF16) for l in blocks], axis=1)
    hi = jnp.concatenate([s_ref[l, ROWS:2 * ROWS, :].astype(BF16) for l in blocks], axis=1)
    return _dot(hr, wcre_ref[c]) + _dot(hi, wcim_ref[c])


def _ssm_output(s_ref, u, wcre_ref, wcim_ref, d_ref, wglu_ref):
    ys = [_ssm_readout_tile(s_ref, c, wcre_ref, wcim_ref) for c in range(N_SSM_TILES)]
    return _ssm_gate(ys, u, d_ref, wglu_ref)


def _ssm_gate(ys, u, d_ref, wglu_ref):
    y = jnp.concatenate(ys, axis=1) + d_ref[...] * u
    z = 0.5 * y * (1.0 + jnp.tanh(math.sqrt(2.0 / math.pi) * (y + 0.044715 * (y * y * y))))
    gate = 1.0 / (1.0 + jnp.exp(-_dot(z.astype(BF16), wglu_ref[...])))
    return z * gate


def _merge(x, att, s, gatt_ref, wout_ref, gpost_ref):
    a = _rms(att, gatt_ref[...]).astype(BF16)
    m = _dot(a, wout_ref[0:ATT_WIDTH, :]) + _dot(s, wout_ref[ATT_WIDTH:, :])
    return x + _rms(m, gpost_ref[...])


def _div(x, k):
    return lax.shift_right_logical(x, int(math.log2(k)))


def _mod(x, k):
    return lax.bitwise_and(x, k - 1)


def _sink_softmax(logits, bias, sink_col):
    lm = logits + bias
    m =jnp.maximum(jnp.max(lm, axis=-1, keepdims=True), sink_col)
    e = jnp.exp2(lm - m)
    den = jnp.sum(e, axis=-1, keepdims=True) + jnp.exp2(sink_col - m)
    return e, den


def _prompt_mixer_kernel(
        sink_ref, x_ref, meta_ref, tab0_ref, cos_ref, sa_ref, sb_ref, bias_ref, perm_ref, permt_ref,
        gpre_ref, winf_ref,
        ar_ref, ai_ref, wbre_ref, wbim_ref, wcre_ref, wcim_ref, d_ref, wgluf_ref,
        gatt_ref, gssm_ref, woutf_ref, gpost_ref, qscale_ref, wupf_ref, wdnf_ref,
        h1_ref, klast_ref, vlast_ref, sre_ref, sim_ref, wupb_ref, wdnb_ref,
        win_ref, wout_ref, wglu_ref, wkt_ref, wvt_ref,
        kbuf, vbuf, kcur, vcur, s_ref, hstate, att_ref):
    n = pl.program_id(0)
    wupb_ref[...] = wupf_ref[...].astype(BF16)
    wdnb_ref[...] = wdnf_ref[...].astype(BF16)
    nb = x_ref.shape[0]
    assert 2 * nb == 8
    last = pl.num_programs(0) - 1
    lo = _lane_is_lo((BLOCK, LANES))
    is_re = lax.broadcasted_iota(jnp.int32, (2 * nb, LANES), 0) < nb
    a1, a2 = [], []
    for l in range(N_LB):
        cols = slice(l * LANES, (l + 1) * LANES)
        ai = jnp.broadcast_to(ai_ref[:, cols], (2 * nb, LANES))
        a1.append(jnp.broadcast_to(ar_ref[:, cols], (2 * nb, LANES)))
        a2.append(jnp.where(is_re, -ai, ai))

    def swap(t):
        return pltpu.roll(t, nb, axis=0)

    def variants(t):
        tr = pltpu.roll(t, HALF, axis=1)
        z = jnp.zeros_like(t)
        return (jnp.where(lo, t, z), jnp.where(lo, z, tr),
                jnp.where(lo, tr, z), jnp.where(lo, z, t))

    @pl.when(n == 0)
    def _meta_block():
        win_ref[...] = (winf_ref[...] * qscale_ref[...]).astype(BF16)
        wout_ref[...] = woutf_ref[...].astype(BF16)
        wglu_ref[...] = wgluf_ref[...].astype(BF16)
        for out_ref, c0 in ((wkt_ref, ATT_WIDTH), (wvt_ref, ATT_WIDTH + KV_WIDTH)):
            for i in range(D_MODEL // LANES):
                blk = winf_ref[i * LANES:(i + 1) * LANES, c0:c0 + KV_WIDTH]
                out_ref[:, i * LANES:(i + 1) * LANES] = blk.T.astype(BF16)
        kbuf[...] = jnp.zeros_like(kbuf)
        vbuf[...] = jnp.zeros_like(vbuf)
        hm = _rms(meta_ref[...], gpre_ref[...]).astype(BF16)
        _, k0, v0 = _project_qkv(hm, win_ref)
        k0 = _rope(k0, tab0_ref[0], tab0_ref[1], tab0_ref[2])
        for i, (kv_, vv_) in enumerate(zip(variants(k0), variants(v0))):
            for b in range(nb):
                kbuf[b, i, 0:BLOCK, :] = kv_.astype(BF16)
                vbuf[b, i, 0:BLOCK, :] = vv_.astype(BF16)
        um = _project_u(hm, win_ref).astype(BF16)
        for c in range(N_SSM_TILES):
            uc = um[:, c * SSM_TILE:(c + 1) * SSM_TILE]
            bre, bim = _dot(uc, wbre_ref[c]), _dot(uc, wbim_ref[c])
            for j in range(LB_PER_TILE):
                l = c * LB_PER_TILE + j
                cols = slice(j * LANES, (j + 1) * LANES)
                h = jnp.zeros((2 * nb, LANES), F32)
                for t in range(BLOCK - N_META, BLOCK):
                    x_t = jnp.where(is_re, jnp.broadcast_to(bre[t:t + 1, cols], h.shape),
                                    jnp.broadcast_to(bim[t:t + 1, cols], h.shape))
                    h = a1[l] * h + a2[l] * swap(h) + x_t
                hstate[l] = h

    x = x_ref[...].reshape(ROWS, D_MODEL)
    hn = _rms(x, gpre_ref[...]).astype(BF16)
    cos, sa, sb = cos_ref[...], sa_ref[...], sb_ref[...]

    u = _project_u(_dot(perm_ref[...], hn).astype(BF16), win_ref)

    hfin = []

    def chains(blocks):
        for l in blocks:
            h = hstate[l]
            g = swap(h)
            a2n = -a2[l]
            for i in range(BLOCK // 2):
                r_re = slice(i * 2 * nb, (i + 1) * 2 * nb)
                r_im = slice(ROWS + i * 2 * nb, ROWS + (i + 1) * 2 * nb)
                re, im_s = s_ref[l, r_re, :], swap(s_ref[l, r_im, :])
                h0 = a1[l] * h + a2[l] * g + jnp.where(is_re, re, im_s)
                g0 = swap(h0)
                g = a1[l] * g0 + a2n * h0 + jnp.where(is_re, im_s, re)
                h = swap(g)
                s_ref[l, r_re, :] = jnp.where(is_re, h0, g)
                s_ref[l, r_im, :] = jnp.where(is_re, g0, h)
            hstate[l] = h
            hfin.append(h)

    mask = bias_ref[0]
    cur = pl.ds(pl.multiple_of(_mod(n + 1, 2) * BLOCK, BLOCK), BLOCK)
    top = lax.broadcasted_iota(jnp.int32, (2 * BLOCK, 1), 0) < BLOCK

    def attend(b):
        rows = slice(b * BLOCK, (b + 1) * BLOCK)
        kb = _rope(k[rows], cos, sa, sb)
        vb = v[rows]
        for i, (kv_, vv_) in enumerate(zip(variants(kb), variants(vb))):
            kbuf[b, i, cur, :] = kv_.astype(BF16)
            vbuf[b, i, cur, :] = vv_.astype(BF16)

        kcur[b] = kb
        vcur[b] = vb

        qs = []
        for j in range(ATT_WIDTH // LANES):
            qs.append(_rope(q[rows, j * LANES:(j + 1) * LANES], cos, sa, sb).astype(BF16))
        for g in range(N_KV_HEADS):
            qst = jnp.concatenate([qs[2 * g], qs[2 * g + 1]], axis=0)
            o = None
            for half in range(2):
                var = 2 * g + half
                h_top, h_bot = 4 * g + half, 4 * g + 2 + half
                sink_col = jnp.where(top, sink_ref[h_top], sink_ref[h_bot]) * LOG2E
                e, den = _sink_softmax(_dot_nt(qst, kbuf[b, var]), mask, sink_col)
                part = _dot(e.astype(BF16), vbuf[b, var]) / den
                o = part if o is None else o + part
            att_ref[rows, (2 * g) * LANES:(2 * g + 1) * LANES] = o[0:BLOCK]
            att_ref[rows, (2 * g + 1) * LANES:(2 * g + 2) * LANES] = o[BLOCK:2 * BLOCK]

    _ssm_input(u, wbre_ref, wbim_ref, s_ref)
    chains(range(N_LB))
    q, k, v = _project_qkv(hn, win_ref)
    ssm_o = _ssm_output(s_ref, u, wcre_ref, wcim_ref, d_ref, wglu_ref)
    ssm_n = _dot(permt_ref[...], _rms(ssm_o, gssm_ref[...]).astype(BF16)).astype(BF16)
    for b in range(nb):
        attend(b)

    h1 = _merge(x, att_ref[...], ssm_n, gatt_ref, wout_ref, gpost_ref)
    h1_ref[...] = h1.reshape(h1_ref.shape)

    @pl.when(n == last)
    def _emit_state():
        for b in range(nb):
            klast_ref[b] = kcur[b].T
            vlast_ref[b] = vcur[b].T
        for l in range(N_LB):
            sre_ref[:, l * LANES:(l + 1) * LANES] = hfin[l][0:nb]
            sim_ref[:, l * LANES:(l + 1) * LANES] = hfin[l][nb:2 * nb]


def _window_bias():
    r = (np.arange(2 * BLOCK) % BLOCK)[:, None]
    phys = np.arange(2 * BLOCK)[None, :]
    tables = []
    for c_min in (2 * BLOCK - N_META, BLOCK - N_META, 0):
        for parity in (0, 1):
            c = phys if parity == 1 else (phys + BLOCK) % (2 * BLOCK)
            ok = (c > r) & (c <= r + WINDOW) & (c >= c_min)
            tables.append(np.where(ok, 0.0, NEG))
    return jnp.asarray(np.stack(tables), dtype=F32)


def _const_spec(shape):
    zeros = (0,) * len(shape)
    return pl.BlockSpec(shape, lambda *_: zeros)


def _prompt_mixer(x_prompt, meta_blk, cos, sa, sb, sinks, p, q_scale, w_up, w_down):
    nb, seq, _ = x_prompt.shape
    n_blocks = seq // BLOCK
    perm = _row_permutation(nb, BLOCK)
    xmap = lambda n: (0, n, 0)
    tmap = lambda n: (n + 1, 0)
    bmap = lambda n: (2 * jnp.minimum(n + 1, 2) + (n + 1) % 2, 0, 0)
    n_slabs = D_FF // LANES
    assert n_slabs <= n_blocks
    upmap = lambda n: (0, jnp.minimum(n, n_slabs - 1))
    dnmap = lambda n: (jnp.minimum(n, n_slabs - 1), 0)
    tab0 = jnp.stack([cos[0:BLOCK], sa[0:BLOCK], sb[0:BLOCK]])
    in_specs = [
        pl.BlockSpec(memory_space=pltpu.SMEM),
        pl.BlockSpec((nb, BLOCK, D_MODEL), xmap),
        _const_spec((BLOCK, D_MODEL)),
        _const_spec(tab0.shape),
        pl.BlockSpec((BLOCK, LANES), tmap),
        pl.BlockSpec((BLOCK, LANES), tmap),
        pl.BlockSpec((BLOCK, LANES), tmap),
        pl.BlockSpec((1, 2 * BLOCK, 2 * BLOCK), bmap),
        _const_spec((ROWS, ROWS)),
        _const_spec((ROWS, ROWS)),
    ] + [_const_spec(a.shape) for a in (*p, q_scale)] + [
        pl.BlockSpec((D_MODEL, LANES), upmap),
        pl.BlockSpec((LANES, D_MODEL), dnmap),
    ]
    w_in, w_glu, w_out = p[1], p[9], p[12]
    out_shape = (
        jax.ShapeDtypeStruct((nb, seq, D_MODEL), F32),
        jax.ShapeDtypeStruct((nb, BLOCK, KV_WIDTH), F32),
        jax.ShapeDtypeStruct((nb, BLOCK, KV_WIDTH), F32),
        jax.ShapeDtypeStruct((nb, N_STATE), F32),
        jax.ShapeDtypeStruct((nb, N_STATE), F32),
        jax.ShapeDtypeStruct(w_up.shape, BF16),
        jax.ShapeDtypeStruct(w_down.shape, BF16),
        jax.ShapeDtypeStruct(w_in.shape, BF16),
        jax.ShapeDtypeStruct(w_out.shape, BF16),
        jax.ShapeDtypeStruct(w_glu.shape, BF16),
        jax.ShapeDtypeStruct((KV_WIDTH, D_MODEL), BF16),
        jax.ShapeDtypeStruct((KV_WIDTH, D_MODEL), BF16),
    )
    out_specs = (
        pl.BlockSpec((nb, BLOCK, D_MODEL), xmap),
        _const_spec((nb, BLOCK, KV_WIDTH)),
        _const_spec((nb, BLOCK, KV_WIDTH)),
        _const_spec((nb, N_STATE)),
        _const_spec((nb, N_STATE)),
        pl.BlockSpec((D_MODEL, LANES), upmap),
        pl.BlockSpec((LANES, D_MODEL), dnmap),
        _const_spec(w_in.shape), _const_spec(w_out.shape), _const_spec(w_glu.shape),
        _const_spec((KV_WIDTH, D_MODEL)), _const_spec((KV_WIDTH, D_MODEL)),
    )
    scratch = [
        pltpu.VMEM((nb, 4, 2 * BLOCK, LANES), BF16),
        pltpu.VMEM((nb, 4, 2 * BLOCK, LANES), BF16),
        pltpu.VMEM((nb, BLOCK, KV_WIDTH), F32),
        pltpu.VMEM((nb, BLOCK, KV_WIDTH), F32),
        pltpu.VMEM((N_LB, 2 * ROWS, LANES), F32),
        pltpu.VMEM((N_LB, 2 * nb, LANES), F32),
        pltpu.VMEM((ROWS, ATT_WIDTH), F32),
    ]
    return pl.pallas_call(
        _prompt_mixer_kernel,
        grid=(n_blocks,),
        in_specs=in_specs,
        out_specs=out_specs,
        out_shape=out_shape,
        scratch_shapes=scratch,
        compiler_params=pltpu.CompilerParams(
            dimension_semantics=("arbitrary",), vmem_limit_bytes=VMEM_LIMIT),
        name="prompt_mixer",
    )(sinks, x_prompt, meta_blk, tab0, cos, sa, sb, _window_bias(), perm, perm.T, *p, q_scale,
      w_up, w_down)


GROUP = 8
GROUPS_PER_STEP = 4


def _decode_mixer_kernel(
        x_ref, cos_ref, sa_ref, sb_ref, cost_ref, sat_ref, sbt_ref, sink_ref, perm_ref, permt_ref,
        ck_ref, cv_ref, stre_ref, stim_ref, wkt_ref, wvt_ref,
        gpre_ref, win_ref, ar_ref, ai_ref, wbre_ref, wbim_ref, wcre_ref, wcim_ref,
        d_ref, wglu_ref, gatt_ref, gssm_ref, wout_ref, gpost_ref,
        h1_ref, kwin_ref, vwin_ref, sre_ref, sim_ref,
        qp, knew, vnew, knew_b, vnew_b, ssm_n, att_ref, s_ref):
    step = pl.program_id(0)
    n_seq = stre_ref.shape[1]
    n_tok = ROWS // n_seq
    grp_rows = GROUP * n_tok
    last = pl.num_programs(0) - 1

    @pl.when(step == 0)
    def _project_and_ssm():
        hn = _rms(x_ref[...].reshape(ROWS, D_MODEL), gpre_ref[...]).astype(BF16)
        q = _dot(hn, win_ref[:, 0:ATT_WIDTH])
        u = _project_u(_dot(perm_ref[...], hn).astype(BF16), win_ref)
        cos, sa, sb = cos_ref[...], sa_ref[...], sb_ref[...]
        kt = _dot_nt(wkt_ref[...], hn)
        kt = (kt * cost_ref[...] + pltpu.roll(kt, ROPE_HALF, axis=0) * sat_ref[...]
              + pltpu.roll(kt, KV_WIDTH - ROPE_HALF, axis=0) * sbt_ref[...])
        vt = _dot_nt(wvt_ref[...], hn)
        knew[...] = kt
        vnew[...] = vt
        knew_b[...] = kt.astype(BF16)
        vnew_b[...] = vt.astype(BF16)
        lo = _lane_is_lo((ROWS, LANES))
        for j in range(ATT_WIDTH // LANES):
            qj = _rope(q[:, j * LANES:(j + 1) * LANES], cos, sa, sb)
            qr = pltpu.roll(qj, HALF, axis=1)
            z = jnp.zeros_like(qj)
            if j < 2:
                qp[2 * j] = jnp.where(lo, qj, z)
                qp[2 * j + 1] = jnp.where(lo, qr, z)
            else:
                qp[2 * j] = jnp.where(lo, z, qr)
                qp[2 * j + 1] = jnp.where(lo, z, qj)

        _ssm_input(u, wbre_ref, wbim_ref, s_ref)
        for l in range(N_LB):
            cols = slice(l * LANES, (l + 1) * LANES)
            ar, ai = ar_ref[:, cols], ai_ref[:, cols]
            hr, hi = stre_ref[cols, :].T, stim_ref[cols, :].T
            for t in range(n_tok):
                r_re = slice(t * n_seq, (t + 1) * n_seq)
                r_im = slice(ROWS + t * n_seq, ROWS + (t + 1) * n_seq)
                hr, hi = (ar * hr - ai * hi + s_ref[l, r_re, :],
                          ar * hi + ai * hr + s_ref[l, r_im, :])
                s_ref[l, r_re, :] = hr
                s_ref[l, r_im, :] = hi
            sre_ref[cols, :] = hr.T
            sim_ref[cols, :] = hi.T
        ssm_o = _ssm_output(s_ref, u, wcre_ref, wcim_ref, d_ref, wglu_ref)
        ssm_n[...] = _dot(permt_ref[...], _rms(ssm_o, gssm_ref[...]).astype(BF16)).astype(BF16)

    def attend_group(g, base):
        grp = pl.ds(pl.multiple_of(g * grp_rows, grp_rows), grp_rows)
        qb = jnp.concatenate([qp[h, grp, :] for h in range(N_HEADS)], axis=0).astype(BF16)
        seq_rows = [slice(base + b * KV_WIDTH, base + (b + 1) * KV_WIDTH) for b in range(GROUP)]
        kc_f = [ck_ref[r, :] for r in seq_rows]
        vc_f = [cv_ref[r, :] for r in seq_rows]
        kcat = jnp.concatenate(kc_f, axis=1).astype(BF16)
        vcat = jnp.concatenate(vc_f, axis=1).astype(BF16)
        n_q = N_HEADS * grp_rows
        n_c = GROUP * WINDOW

        r_c = lax.broadcasted_iota(jnp.int32, (n_q, n_c), 0)
        c_c = lax.broadcasted_iota(jnp.int32, (n_q, n_c), 1)
        mask_c = ((_div(c_c, WINDOW) == _mod(_div(r_c, n_tok), GROUP))
                  & (_mod(c_c, WINDOW) > _mod(r_c, n_tok)))
        lc = jnp.where(mask_c, _dot(qb, kcat), NEG)
        tile = pl.ds(pl.multiple_of(_div(g * grp_rows, LANES) * LANES, LANES), LANES)
        first = _mod(g * grp_rows, LANES)
        r_n = lax.broadcasted_iota(jnp.int32, (n_q, LANES), 0)
        c_n = lax.broadcasted_iota(jnp.int32, (n_q, LANES), 1) - first
        mask_n = ((c_n >= 0) & (c_n < grp_rows)
                  & (_div(c_n, n_tok) == _mod(_div(r_n, n_tok), GROUP))
                  & (_mod(c_n, n_tok) <= _mod(r_n, n_tok)))
        ln = jnp.where(mask_n, _dot(qb, knew_b[:, tile]), NEG)

        sink_col = sink_ref[...] * LOG2E
        m = jnp.maximum(jnp.maximum(jnp.max(lc, axis=-1, keepdims=True),
                                    jnp.max(ln, axis=-1, keepdims=True)), sink_col)
        ec = jnp.exp2(lc - m)
        en = jnp.exp2(ln - m)
        den = (jnp.sum(ec, axis=-1, keepdims=True) + jnp.sum(en, axis=-1, keepdims=True)
               + jnp.exp2(sink_col - m))
        o = (_dot_nt(ec.astype(BF16), vcat) + _dot_nt(en.astype(BF16), vnew_b[:, tile])) / den

        lo = _lane_is_lo((grp_rows, LANES))
        for j in range(ATT_WIDTH // LANES):
            o_even = o[(2 * j) * grp_rows:(2 * j + 1) * grp_rows]
            o_odd = o[(2 * j + 1) * grp_rows:(2 * j + 2) * grp_rows]
            if j < 2:
                pair = jnp.where(lo, o_even, pltpu.roll(o_odd, HALF, axis=1))
            else:
                pair = jnp.where(lo, pltpu.roll(o_even, HALF, axis=1), o_odd)
            att_ref[grp, j * LANES:(j + 1) * LANES] = pair

        keep = WINDOW - n_tok
        to_tail = _mod(keep - first + LANES, LANES)
        kt_g = pltpu.roll(knew[:, tile], to_tail, axis=1)
        vt_g = pltpu.roll(vnew[:, tile], to_tail, axis=1)
        is_new = lax.broadcasted_iota(jnp.int32, (KV_WIDTH, WINDOW), 1) >= keep
        for b, rows in enumerate(seq_rows):
            k_b = kt_g if b == 0 else pltpu.roll(kt_g, LANES - n_tok * b, axis=1)
            v_b = vt_g if b == 0 else pltpu.roll(vt_g, LANES - n_tok * b, axis=1)
            kwin_ref[rows, :] = jnp.where(is_new, k_b, pltpu.roll(kc_f[b], keep, axis=1))
            vwin_ref[rows, :] = jnp.where(is_new, v_b, pltpu.roll(vc_f[b], keep, axis=1))

    for sub in range(GROUPS_PER_STEP):
        attend_group(step * GROUPS_PER_STEP + sub, sub * GROUP * KV_WIDTH)

    @pl.when(step == last)
    def _merge_out():
        h1_ref[...] = _merge(x_ref[...].reshape(ROWS, D_MODEL), att_ref[...], ssm_n[...],
                             gatt_ref, wout_ref, gpost_ref)


def _decode_mixer(x2d, tables, tables_t, sink_col, ck2d, cv2d, st_re, st_im, wkt, wvt, p):
    n_seq = st_re.shape[1]
    n_steps = n_seq // (GROUP * GROUPS_PER_STEP)
    cmap = lambda g: (g, 0)
    perm = _row_permutation(n_seq, ROWS // n_seq)
    head = (x2d, *tables, *tables_t, sink_col, perm, perm.T)
    in_specs = [_const_spec(a.shape) for a in head] + [
        pl.BlockSpec((GROUPS_PER_STEP * GROUP * KV_WIDTH, WINDOW), cmap),
        pl.BlockSpec((GROUPS_PER_STEP * GROUP * KV_WIDTH, WINDOW), cmap),
    ] + [_const_spec(a.shape) for a in (st_re, st_im, wkt, wvt, *p)]
    out_shape = (
        jax.ShapeDtypeStruct((ROWS, D_MODEL), F32),
        jax.ShapeDtypeStruct(ck2d.shape, F32),
        jax.ShapeDtypeStruct(cv2d.shape, F32),
        jax.ShapeDtypeStruct(st_re.shape, F32),
        jax.ShapeDtypeStruct(st_im.shape, F32),
    )
    out_specs = (
        _const_spec((ROWS, D_MODEL)),
        pl.BlockSpec((GROUPS_PER_STEP * GROUP * KV_WIDTH, WINDOW), cmap),
        pl.BlockSpec((GROUPS_PER_STEP * GROUP * KV_WIDTH, WINDOW), cmap),
        _const_spec(st_re.shape), _const_spec(st_im.shape),
    )
    scratch = [
        pltpu.VMEM((N_HEADS, ROWS, LANES), F32),
        pltpu.VMEM((KV_WIDTH, ROWS), F32),
        pltpu.VMEM((KV_WIDTH, ROWS), F32),
        pltpu.VMEM((KV_WIDTH, ROWS), BF16),
        pltpu.VMEM((KV_WIDTH, ROWS), BF16),
        pltpu.VMEM((ROWS, SSM_WIDTH), BF16),
        pltpu.VMEM((ROWS, ATT_WIDTH), F32),
        pltpu.VMEM((N_LB, 2 * ROWS, LANES), F32),
    ]
    return pl.pallas_call(
        _decode_mixer_kernel,
        grid=(n_steps,),
        in_specs=in_specs,
        out_specs=out_specs,
        out_shape=out_shape,
        scratch_shapes=scratch,
        compiler_params=pltpu.CompilerParams(
            dimension_semantics=("arbitrary",), vmem_limit_bytes=VMEM_LIMIT),
        name="decode_mixer",
    )(*head, ck2d, cv2d, st_re, st_im, wkt, wvt, *p)


def _mlp_kernel(xp_ref, xd_ref, gpre_ref, wup_ref, wdn_ref, gpost_ref, op_ref, od_ref):
    i = pl.program_id(0)

    def mlp(x):
        hn = _rms(x, gpre_ref[...]).astype(BF16)
        acc = None
        for c in range(D_FF // FF_CHUNK):
            cols = slice(c * FF_CHUNK, (c + 1) * FF_CHUNK)
            a = jnp.maximum(_dot(hn, wup_ref[:, cols]), 0.0)
            part = _dot((a * a).astype(BF16), wdn_ref[cols, :])
            acc = part if acc is None else acc + part
        return x + _rms(acc, gpost_ref[...])

    @pl.when(i == 0)
    def _decode_rows():
        od_ref[...] = mlp(xd_ref[...]).reshape(od_ref.shape)

    @pl.when(i > 0)
    def _prompt_tile():
        op_ref[...] = mlp(xp_ref[...])


def _mlp(xp2d, xd2d, d_shape, g_pre, w_up, w_down, g_post):
    n_tiles = xp2d.shape[0] // MLP_ROWS
    pmap = lambda i: (jnp.maximum(i - 1, 0), 0)
    return pl.pallas_call(
        _mlp_kernel,
        grid=(n_tiles + 1,),
        in_specs=[pl.BlockSpec((MLP_ROWS, D_MODEL), pmap), _const_spec(xd2d.shape),
                  _const_spec(g_pre.shape), _const_spec(w_up.shape), _const_spec(w_down.shape),
                  _const_spec(g_post.shape)],
        out_specs=(pl.BlockSpec((MLP_ROWS, D_MODEL), pmap), _const_spec(d_shape)),
        out_shape=(jax.ShapeDtypeStruct(xp2d.shape, F32), jax.ShapeDtypeStruct(d_shape, F32)),
        compiler_params=pltpu.CompilerParams(
            dimension_semantics=("arbitrary",), vmem_limit_bytes=VMEM_LIMIT),
        name="mlp",
    )(xp2d, xd2d, g_pre, w_up, w_down, g_post)


def _zoh(a_re, a_im, log_dt, b_re, b_im):
    dt = jnp.exp(log_dt)[:, None]
    mag = jnp.exp(a_re * dt)
    abar_re, abar_im = mag * jnp.cos(a_im * dt), mag * jnp.sin(a_im * dt)
    nr, ni = abar_re - 1.0, abar_im
    den = a_re * a_re + a_im * a_im
    coef_re = (nr * a_re + ni * a_im) / den
    coef_im = (ni * a_re - nr * a_im) / den
    bbar_re = coef_re[..., None] * b_re - coef_im[..., None] * b_im
    bbar_im = coef_re[..., None] * b_im + coef_im[..., None] * b_re
    return abar_re, abar_im, bbar_re, bbar_im


def _same_group_mask():
    row_g = np.arange(SSM_TILE)[:, None] // SSM_GROUP
    col_g = np.arange(STATE_TILE)[None, :] // SSM_STATE
    return row_g == col_g


def _block_diag_in(bbar):
    w = jnp.swapaxes(bbar, 1, 2).reshape(N_SSM_TILES, SSM_TILE, SSM_STATE)
    w = jnp.tile(w, (1, 1, SSM_TILE // SSM_GROUP))
    return jnp.where(_same_group_mask()[None], w, 0.0).astype(BF16)


def _block_diag_out(c):
    w = jnp.swapaxes(c, 1, 2).reshape(N_SSM_TILES, STATE_TILE, SSM_GROUP)
    w = jnp.tile(w, (1, 1, SSM_TILE // SSM_GROUP))
    return jnp.where(_same_group_mask().T[None], w, 0.0).astype(BF16)


def _rope_tables(pos):
    half = ROPE_DIM // 2
    inv = ROPE_THETA ** (-np.arange(half, dtype=np.float64) / half)
    ang = np.asarray(pos, dtype=np.float64)[:, None] * inv[None, :]
    cos, sin = np.cos(ang), np.sin(ang)
    d = np.arange(LANES) % HEAD_DIM
    f = d % half
    cos_t = np.where(d[None, :] < ROPE_DIM, cos[:, f], 1.0)
    sa_t = np.where((d[None, :] >= half) & (d[None, :] < ROPE_DIM), sin[:, f], 0.0)
    sb_t = np.where(d[None, :] < half, -sin[:, f], 0.0)
    return tuple(t.astype(np.float32) for t in (cos_t, sa_t, sb_t))


def _row_permutation(n_outer, n_inner):
    r = np.arange(n_outer * n_inner)
    perm = np.zeros((r.size, r.size), np.float32)
    perm[r, (r % n_outer) * n_inner + r // n_outer] = 1.0
    return jnp.asarray(perm, dtype=BF16)


def kernel(x_prompt, x_sample, cache_k_win, cache_v_win, state_ssm_re, state_ssm_im, meta_tokens, norm_mix_pre, w_in, attn_sinks, ssm_a_re, ssm_a_im, ssm_log_dt, ssm_b_re, ssm_b_im, ssm_c_re, ssm_c_im, ssm_d, w_glu, norm_att_out, norm_ssm_out, w_out, norm_mix_post, norm_mlp_pre, w_up, w_down, norm_mlp_post):
    depth = w_in.shape[0]
    assert depth == 1
    l = 0
    nb, seq, _ = x_prompt.shape
    n_seq, n_tok, _ = x_sample.shape
    assert n_seq * n_tok == ROWS and nb * BLOCK == ROWS and seq % BLOCK == 0

    abar_re, abar_im, bbar_re, bbar_im = _zoh(
        ssm_a_re[l], ssm_a_im[l], ssm_log_dt[l], ssm_b_re[l], ssm_b_im[l])
    row = lambda a: a.reshape(1, -1)
    q_scale = np.where(np.arange(w_in.shape[-1]) < ATT_WIDTH, HEAD_DIM ** -0.5 * LOG2E,
                       1.0).astype(np.float32)
    params = (
        row(norm_mix_pre[l]), w_in[l],
        row(abar_re), row(abar_im),
        _block_diag_in(bbar_re), _block_diag_in(bbar_im),
        _block_diag_out(ssm_c_re[l]), _block_diag_out(-ssm_c_im[l]),
        row(ssm_d[l]), w_glu[l],
        row(norm_att_out[l]), row(norm_ssm_out[l]), w_out[l], row(norm_mix_post[l]),
    )

    front = BLOCK - N_META
    meta_blk = jnp.concatenate([jnp.zeros((front, D_MODEL), F32), meta_tokens], axis=0)
    cos, sa, sb = (jnp.asarray(t) for t in _rope_tables(np.arange(seq + BLOCK) - front))
    (h1, k_last, v_last, p_re, p_im, w_up_b, w_down_b, w_in_b, w_out_b, w_glu_b,
     w_k_t, w_v_t) = _prompt_mixer(
        x_prompt, meta_blk, cos, sa, sb, attn_sinks[l], params, jnp.asarray(row(q_scale)),
        w_up[l], w_down[l])
    mlp_params = (row(norm_mlp_pre[l]), w_up_b, w_down_b, row(norm_mlp_post[l]))
    params = params[:1] + (w_in_b,) + params[2:9] + (w_glu_b,) + params[10:12] + (
        w_out_b,) + params[13:]

    tabs = _rope_tables(PAST_LEN + np.arange(n_tok))
    tables = tuple(jnp.asarray(np.tile(t, (n_seq, 1))) for t in tabs)
    tables_t = tuple(jnp.asarray(np.tile(t.T, (1, n_seq))) for t in tabs)
    sink_col = jnp.repeat(attn_sinks[l], n_tok * GROUP).reshape(-1, 1)
    to_t = lambda a: jnp.transpose(a, (0, 2, 3, 1)).reshape(-1, WINDOW)
    from_t = lambda a, n: jnp.transpose(
        a.reshape(n, N_KV_HEADS, HEAD_DIM, WINDOW), (0, 3, 1, 2))[None]
    st_in = lambda a: jnp.transpose(a, (1, 2, 0)).reshape(N_STATE, n_seq)
    st_out = lambda a: jnp.transpose(a.reshape(SSM_GROUPS, SSM_STATE, n_seq), (2, 0, 1))[None]
    h1s, kwin, vwin, s_re, s_im = _decode_mixer(
        x_sample, tables, tables_t, sink_col, to_t(cache_k_win[l]), to_t(cache_v_win[l]),
        st_in(state_ssm_re[l]), st_in(state_ssm_im[l]), w_k_t, w_v_t, params)
    y_prompt, y_sample = _mlp(h1.reshape(nb * seq, D_MODEL), h1s, x_sample.shape, *mlp_params)
    y_prompt = y_prompt.reshape(nb, seq, D_MODEL)

    win = from_t
    st = lambda a, n: a.reshape(1, n, SSM_GROUPS, SSM_STATE)
    return (y_prompt, y_sample,
            win(k_last, nb), win(v_last, nb), st(p_re, nb), st(p_im, nb),
            win(kwin, n_seq), win(vwin, n_seq), st_out(s_re), st_out(s_im))
```

```python
import math

import jax
import jax.numpy as jnp
import numpy as np
from jax import lax
from jax.experimental import pallas as pl
from jax.experimental.pallas import tpu as pltpu

F32 = jnp.float32
BF16 = jnp.bfloat16

N_META = 16
HEAD_DIM = 64
N_HEADS = 8
N_KV_HEADS = 2
WINDOW = 128
BLOCK = 128
ROPE_DIM = 16
ROPE_HALF = ROPE_DIM // 2
ROPE_THETA = 500000.0
SSM_GROUP = 16
SSM_GROUPS = 32
SSM_STATE = 64
PAST_LEN = 8192
EPS = 1e-6
NEG = -1e30
LOG2E = math.log2(math.e)

D_MODEL = 1024
ATT_WIDTH = 512
KV_WIDTH = 128
SSM_WIDTH = 512
N_STATE = SSM_GROUPS * SSM_STATE
D_FF = 4096

ROWS = 512
MLP_ROWS = 1024
LANES = 128
HALF = LANES // 2
SSM_TILE = 256
STATE_TILE = (SSM_TILE // SSM_GROUP) * SSM_STATE
N_SSM_TILES = SSM_WIDTH // SSM_TILE
LB_PER_TILE = STATE_TILE // LANES
N_LB = N_STATE // LANES
FF_CHUNK = 1024
VMEM_LIMIT = 56 * 1024 * 1024


def _dot(a, b):
    return jnp.dot(a, b, preferred_element_type=F32)


def _dot_nt(a, b):
    return lax.dot_general(a, b, (((1,), (1,)), ((), ())), preferred_element_type=F32)


def _rms(x, g):
    return x * lax.rsqrt(jnp.mean(x * x, axis=-1, keepdims=True) + EPS) * g


def _rope(x, cos, sa, sb):
    return (x * cos + pltpu.roll(x, ROPE_HALF, axis=1) * sa
            + pltpu.roll(x, LANES - ROPE_HALF, axis=1) * sb)


def _lane_is_lo(shape):
    return lax.broadcasted_iota(jnp.int32, shape, 1) < HALF


def _project_qkv(hn, w_in_ref):
    q = _dot(hn, w_in_ref[:, 0:ATT_WIDTH])
    kv = _dot(hn, w_in_ref[:, ATT_WIDTH:ATT_WIDTH + 2 * KV_WIDTH])
    return q, kv[:, 0:KV_WIDTH], kv[:, KV_WIDTH:2 * KV_WIDTH]


def _project_u(hn, w_in_ref):
    return _dot(hn, w_in_ref[:, ATT_WIDTH + 2 * KV_WIDTH:])


def _ssm_input_tile(ub, c, wbre_ref, wbim_ref, s_ref):
    uc = ub[:, c * SSM_TILE:(c + 1) * SSM_TILE]
    bre, bim = _dot(uc, wbre_ref[c]), _dot(uc, wbim_ref[c])
    for l in range(LB_PER_TILE):
        s_ref[c * LB_PER_TILE + l, 0:ROWS, :] = bre[:, l * LANES:(l + 1) * LANES]
        s_ref[c * LB_PER_TILE + l, ROWS:2 * ROWS, :] = bim[:, l * LANES:(l + 1) * LANES]


def _ssm_input(u, wbre_ref, wbim_ref, s_ref):
    ub = u.astype(BF16)
    for c in range(N_SSM_TILES):
        _ssm_input_tile(ub, c, wbre_ref, wbim_ref, s_ref)


def _ssm_readout_tile(s_ref, c, wcre_ref, wcim_ref):
    blocks = range(c * LB_PER_TILE, (c + 1) * LB_PER_TILE)
    hr = jnp.concatenate([s_ref[l, 0:ROWS, :].astype(BF16) for l in blocks], axis=1)
    hi = jnp.concatenate([s_ref[l, ROWS:2 * ROWS, :].astype(BF16) for l in blocks], axis=1)
    return _dot(hr, wcre_ref[c]) + _dot(hi, wcim_ref[c])


def _ssm_output(s_ref, u, wcre_ref, wcim_ref, d_ref, wglu_ref):
    ys = [_ssm_readout_tile(s_ref, c, wcre_ref, wcim_ref) for c in range(N_SSM_TILES)]
    return _ssm_gate(ys, u, d_ref, wglu_ref)


def _ssm_gate(ys, u, d_ref, wglu_ref):
    y = jnp.concatenate(ys, axis=1) + d_ref[...] * u
    z = 0.5 * y * (1.0 + jnp.tanh(math.sqrt(2.0 / math.pi) * (y + 0.044715 * (y * y * y))))
    gate = 1.0 / (1.0 + jnp.exp(-_dot(z.astype(BF16), wglu_ref[...])))
    return z * gate


def _merge(x, att, s, gatt_ref, wout_ref, gpost_ref):
    a = _rms(att, gatt_ref[...]).astype(BF16)
    m = _dot(a, wout_ref[0:ATT_WIDTH, :]) + _dot(s, wout_ref[ATT_WIDTH:, :])
    return x + _rms(m, gpost_ref[...])


def _div(x, k):
    return lax.shift_right_logical(x, int(math.log2(k)))


def _mod(x, k):
    return lax.bitwise_and(x, k - 1)


def _sink_softmax(logits, bias, sink_col):
    lm = logits + bias
    m =jnp.maximum(jnp.max(lm, axis=-1, keepdims=True), sink_col)
    e = jnp.exp2(lm - m)
    den = jnp.sum(e, axis=-1, keepdims=True) + jnp.exp2(sink_col - m)
    return e, den


def _prompt_mixer_kernel(
        sink_ref, x_ref, meta_ref, tab0_ref, cos_ref, sa_ref, sb_ref, bias_ref, perm_ref, permt_ref,
        gpre_ref, winf_ref,
        ar_ref, ai_ref, wbre_ref, wbim_ref, wcre_ref, wcim_ref, d_ref, wgluf_ref,
        gatt_ref, gssm_ref, woutf_ref, gpost_ref, qscale_ref, wupf_ref, wdnf_ref,
        h1_ref, klast_ref, vlast_ref, sre_ref, sim_ref, wupb_ref, wdnb_ref,
        win_ref, wout_ref, wglu_ref, wkt_ref, wvt_ref,
        kbuf, vbuf, kcur, vcur, s_ref, hstate, att_ref):
    n = pl.program_id(0)
    wupb_ref[...] = wupf_ref[...].astype(BF16)
    wdnb_ref[...] = wdnf_ref[...].astype(BF16)
    nb = x_ref.shape[0]
    assert 2 * nb == 8
    last = pl.num_programs(0) - 1
    lo = _lane_is_lo((BLOCK, LANES))
    is_re = lax.broadcasted_iota(jnp.int32, (2 * nb, LANES), 0) < nb
    a1, a2 = [], []
    for l in range(N_LB):
        cols = slice(l * LANES, (l + 1) * LANES)
        ai = jnp.broadcast_to(ai_ref[:, cols], (2 * nb, LANES))
        a1.append(jnp.broadcast_to(ar_ref[:, cols], (2 * nb, LANES)))
        a2.append(jnp.where(is_re, -ai, ai))

    def swap(t):
        return pltpu.roll(t, nb, axis=0)

    def variants(t):
        tr = pltpu.roll(t, HALF, axis=1)
        z = jnp.zeros_like(t)
        return (jnp.where(lo, t, z), jnp.where(lo, z, tr),
                jnp.where(lo, tr, z), jnp.where(lo, z, t))

    @pl.when(n == 0)
    def _meta_block():
        win_ref[...] = (winf_ref[...] * qscale_ref[...]).astype(BF16)
        wout_ref[...] = woutf_ref[...].astype(BF16)
        wglu_ref[...] = wgluf_ref[...].astype(BF16)
        for out_ref, c0 in ((wkt_ref, ATT_WIDTH), (wvt_ref, ATT_WIDTH + KV_WIDTH)):
            for i in range(D_MODEL // LANES):
                blk = winf_ref[i * LANES:(i + 1) * LANES, c0:c0 + KV_WIDTH]
                out_ref[:, i * LANES:(i + 1) * LANES] = blk.T.astype(BF16)
        kbuf[...] = jnp.zeros_like(kbuf)
        vbuf[...] = jnp.zeros_like(vbuf)
        hm = _rms(meta_ref[...], gpre_ref[...]).astype(BF16)
        _, k0, v0 = _project_qkv(hm, win_ref)
        k0 = _rope(k0, tab0_ref[0], tab0_ref[1], tab0_ref[2])
        for i, (kv_, vv_) in enumerate(zip(variants(k0), variants(v0))):
            for b in range(nb):
                kbuf[b, i, 0:BLOCK, :] = kv_.astype(BF16)
                vbuf[b, i, 0:BLOCK, :] = vv_.astype(BF16)
        um = _project_u(hm, win_ref).astype(BF16)
        for c in range(N_SSM_TILES):
            uc = um[:, c * SSM_TILE:(c + 1) * SSM_TILE]
            bre, bim = _dot(uc, wbre_ref[c]), _dot(uc, wbim_ref[c])
            for j in range(LB_PER_TILE):
                l = c * LB_PER_TILE + j
                cols = slice(j * LANES, (j + 1) * LANES)
                h = jnp.zeros((2 * nb, LANES), F32)
                for t in range(BLOCK - N_META, BLOCK):
                    x_t = jnp.where(is_re, jnp.broadcast_to(bre[t:t + 1, cols], h.shape),
                                    jnp.broadcast_to(bim[t:t + 1, cols], h.shape))
                    h = a1[l] * h + a2[l] * swap(h) + x_t
                hstate[l] = h

    x = x_ref[...].reshape(ROWS, D_MODEL)
    hn = _rms(x, gpre_ref[...]).astype(BF16)
    cos, sa, sb = cos_ref[...], sa_ref[...], sb_ref[...]

    u = _project_u(_dot(perm_ref[...], hn).astype(BF16), win_ref)

    hfin = []

    def chains(blocks):
        for l in blocks:
            h = hstate[l]
            g = swap(h)
            a2n = -a2[l]
            for i in range(BLOCK // 2):
                r_re = slice(i * 2 * nb, (i + 1) * 2 * nb)
                r_im = slice(ROWS + i * 2 * nb, ROWS + (i + 1) * 2 * nb)
                re, im_s = s_ref[l, r_re, :], swap(s_ref[l, r_im, :])
                h0 = a1[l] * h + a2[l] * g + jnp.where(is_re, re, im_s)
                g0 = swap(h0)
                g = a1[l] * g0 + a2n * h0 + jnp.where(is_re, im_s, re)
                h = swap(g)
                s_ref[l, r_re, :] = jnp.where(is_re, h0, g)
                s_ref[l, r_im, :] = jnp.where(is_re, g0, h)
            hstate[l] = h
            hfin.append(h)

    mask = bias_ref[0]
    cur = pl.ds(pl.multiple_of(_mod(n + 1, 2) * BLOCK, BLOCK), BLOCK)
    top = lax.broadcasted_iota(jnp.int32, (2 * BLOCK, 1), 0) < BLOCK

    def attend(b):
        rows = slice(b * BLOCK, (b + 1) * BLOCK)
        kb = _rope(k[rows], cos, sa, sb)
        vb = v[rows]
        for i, (kv_, vv_) in enumerate(zip(variants(kb), variants(vb))):
            kbuf[b, i, cur, :] = kv_.astype(BF16)
            vbuf[b, i, cur, :] = vv_.astype(BF16)

        kcur[b] = kb
        vcur[b] = vb

        qs = []
        for j in range(ATT_WIDTH // LANES):
            qs.append(_rope(q[rows, j * LANES:(j + 1) * LANES], cos, sa, sb).astype(BF16))
        for g in range(N_KV_HEADS):
            qst = jnp.concatenate([qs[2 * g], qs[2 * g + 1]], axis=0)
            o = None
            for half in range(2):
                var = 2 * g + half
                h_top, h_bot = 4 * g + half, 4 * g + 2 + half
                sink_col = jnp.where(top, sink_ref[h_top], sink_ref[h_bot]) * LOG2E
                e, den = _sink_softmax(_dot_nt(qst, kbuf[b, var]), mask, sink_col)
                part = _dot(e.astype(BF16), vbuf[b, var]) / den
                o = part if o is None else o + part
            att_ref[rows, (2 * g) * LANES:(2 * g + 1) * LANES] = o[0:BLOCK]
            att_ref[rows, (2 * g + 1) * LANES:(2 * g + 2) * LANES] = o[BLOCK:2 * BLOCK]

    _ssm_input(u, wbre_ref, wbim_ref, s_ref)
    chains(range(N_LB))
    q, k, v = _project_qkv(hn, win_ref)
    ssm_o = _ssm_output(s_ref, u, wcre_ref, wcim_ref, d_ref, wglu_ref)
    ssm_n = _dot(permt_ref[...], _rms(ssm_o, gssm_ref[...]).astype(BF16)).astype(BF16)
    for b in range(nb):
        attend(b)

    h1 = _merge(x, att_ref[...], ssm_n, gatt_ref, wout_ref, gpost_ref)
    h1_ref[...] = h1.reshape(h1_ref.shape)

    @pl.when(n == last)
    def _emit_state():
        for b in range(nb):
            klast_ref[b] = kcur[b].T
            vlast_ref[b] = vcur[b].T
        for l in range(N_LB):
            sre_ref[:, l * LANES:(l + 1) * LANES] = hfin[l][0:nb]
            sim_ref[:, l * LANES:(l + 1) * LANES] = hfin[l][nb:2 * nb]


def _window_bias():
    r = (np.arange(2 * BLOCK) % BLOCK)[:, None]
    phys = np.arange(2 * BLOCK)[None, :]
    tables = []
    for c_min in (2 * BLOCK - N_META, BLOCK - N_META, 0):
        for parity in (0, 1):
            c = phys if parity == 1 else (phys + BLOCK) % (2 * BLOCK)
            ok = (c > r) & (c <= r + WINDOW) & (c >= c_min)
            tables.append(np.where(ok, 0.0, NEG))
    return jnp.asarray(np.stack(tables), dtype=F32)


def _const_spec(shape):
    zeros = (0,) * len(shape)
    return pl.BlockSpec(shape, lambda *_: zeros)


def _prompt_mixer(x_prompt, meta_blk, cos, sa, sb, sinks, p, q_scale, w_up, w_down):
    nb, seq, _ = x_prompt.shape
    n_blocks = seq // BLOCK
    perm = _row_permutation(nb, BLOCK)
    xmap = lambda n: (0, n, 0)
    tmap = lambda n: (n + 1, 0)
    bmap = lambda n: (2 * jnp.minimum(n + 1, 2) + (n + 1) % 2, 0, 0)
    n_slabs = D_FF // LANES
    assert n_slabs <= n_blocks
    upmap = lambda n: (0, jnp.minimum(n, n_slabs - 1))
    dnmap = lambda n: (jnp.minimum(n, n_slabs - 1), 0)
    tab0 = jnp.stack([cos[0:BLOCK], sa[0:BLOCK], sb[0:BLOCK]])
    in_specs = [
        pl.BlockSpec(memory_space=pltpu.SMEM),
        pl.BlockSpec((nb, BLOCK, D_MODEL), xmap),
        _const_spec((BLOCK, D_MODEL)),
        _const_spec(tab0.shape),
        pl.BlockSpec((BLOCK, LANES), tmap),
        pl.BlockSpec((BLOCK, LANES), tmap),
        pl.BlockSpec((BLOCK, LANES), tmap),
        pl.BlockSpec((1, 2 * BLOCK, 2 * BLOCK), bmap),
        _const_spec((ROWS, ROWS)),
        _const_spec((ROWS, ROWS)),
    ] + [_const_spec(a.shape) for a in (*p, q_scale)] + [
        pl.BlockSpec((D_MODEL, LANES), upmap),
        pl.BlockSpec((LANES, D_MODEL), dnmap),
    ]
    w_in, w_glu, w_out = p[1], p[9], p[12]
    out_shape = (
        jax.ShapeDtypeStruct((nb, seq, D_MODEL), F32),
        jax.ShapeDtypeStruct((nb, BLOCK, KV_WIDTH), F32),
        jax.ShapeDtypeStruct((nb, BLOCK, KV_WIDTH), F32),
        jax.ShapeDtypeStruct((nb, N_STATE), F32),
        jax.ShapeDtypeStruct((nb, N_STATE), F32),
        jax.ShapeDtypeStruct(w_up.shape, BF16),
        jax.ShapeDtypeStruct(w_down.shape, BF16),
        jax.ShapeDtypeStruct(w_in.shape, BF16),
        jax.ShapeDtypeStruct(w_out.shape, BF16),
        jax.ShapeDtypeStruct(w_glu.shape, BF16),
        jax.ShapeDtypeStruct((KV_WIDTH, D_MODEL), BF16),
        jax.ShapeDtypeStruct((KV_WIDTH, D_MODEL), BF16),
    )
    out_specs = (
        pl.BlockSpec((nb, BLOCK, D_MODEL), xmap),
        _const_spec((nb, BLOCK, KV_WIDTH)),
        _const_spec((nb, BLOCK, KV_WIDTH)),
        _const_spec((nb, N_STATE)),
        _const_spec((nb, N_STATE)),
        pl.BlockSpec((D_MODEL, LANES), upmap),
        pl.BlockSpec((LANES, D_MODEL), dnmap),
        _const_spec(w_in.shape), _const_spec(w_out.shape), _const_spec(w_glu.shape),
        _const_spec((KV_WIDTH, D_MODEL)), _const_spec((KV_WIDTH, D_MODEL)),
    )
    scratch = [
        pltpu.VMEM((nb, 4, 2 * BLOCK, LANES), BF16),
        pltpu.VMEM((nb, 4, 2 * BLOCK, LANES), BF16),
        pltpu.VMEM((nb, BLOCK, KV_WIDTH), F32),
        pltpu.VMEM((nb, BLOCK, KV_WIDTH), F32),
        pltpu.VMEM((N_LB, 2 * ROWS, LANES), F32),
        pltpu.VMEM((N_LB, 2 * nb, LANES), F32),
        pltpu.VMEM((ROWS, ATT_WIDTH), F32),
    ]
    return pl.pallas_call(
        _prompt_mixer_kernel,
        grid=(n_blocks,),
        in_specs=in_specs,
        out_specs=out_specs,
        out_shape=out_shape,
        scratch_shapes=scratch,
        compiler_params=pltpu.CompilerParams(
            dimension_semantics=("arbitrary",), vmem_limit_bytes=VMEM_LIMIT),
        name="prompt_mixer",
    )(sinks, x_prompt, meta_blk, tab0, cos, sa, sb, _window_bias(), perm, perm.T, *p, q_scale,
      w_up, w_down)


GROUP = 8
GROUPS_PER_STEP = 4


def _decode_mixer_kernel(
        x_ref, cos_ref, sa_ref, sb_ref, cost_ref, sat_ref, sbt_ref, sink_ref, perm_ref, permt_ref,
        ck_ref, cv_ref, stre_ref, stim_ref, wkt_ref, wvt_ref,
        gpre_ref, win_ref, ar_ref, ai_ref, wbre_ref, wbim_ref, wcre_ref, wcim_ref,
        d_ref, wglu_ref, gatt_ref, gssm_ref, wout_ref, gpost_ref,
        h1_ref, kwin_ref, vwin_ref, sre_ref, sim_ref,
        qp, knew, vnew, knew_b, vnew_b, ssm_n, att_ref, s_ref):
    step = pl.program_id(0)
    n_seq = stre_ref.shape[1]
    n_tok = ROWS // n_seq
    grp_rows = GROUP * n_tok
    last = pl.num_programs(0) - 1

    @pl.when(step == 0)
    def _project_and_ssm():
        hn = _rms(x_ref[...].reshape(ROWS, D_MODEL), gpre_ref[...]).astype(BF16)
        q = _dot(hn, win_ref[:, 0:ATT_WIDTH])
        u = _project_u(_dot(perm_ref[...], hn).astype(BF16), win_ref)
        cos, sa, sb = cos_ref[...], sa_ref[...], sb_ref[...]
        kt = _dot_nt(wkt_ref[...], hn)
        kt = (kt * cost_ref[...] + pltpu.roll(kt, ROPE_HALF, axis=0) * sat_ref[...]
              + pltpu.roll(kt, KV_WIDTH - ROPE_HALF, axis=0) * sbt_ref[...])
        vt = _dot_nt(wvt_ref[...], hn)
        knew[...] = kt
        vnew[...] = vt
        knew_b[...] = kt.astype(BF16)
        vnew_b[...] = vt.astype(BF16)
        lo = _lane_is_lo((ROWS, LANES))
        for j in range(ATT_WIDTH // LANES):
            qj = _rope(q[:, j * LANES:(j + 1) * LANES], cos, sa, sb)
            qr = pltpu.roll(qj, HALF, axis=1)
            z = jnp.zeros_like(qj)
            if j < 2:
                qp[2 * j] = jnp.where(lo, qj, z)
                qp[2 * j + 1] = jnp.where(lo, qr, z)
            else:
                qp[2 * j] = jnp.where(lo, z, qr)
                qp[2 * j + 1] = jnp.where(lo, z, qj)

        _ssm_input(u, wbre_ref, wbim_ref, s_ref)
        for l in range(N_LB):
            cols = slice(l * LANES, (l + 1) * LANES)
            ar, ai = ar_ref[:, cols], ai_ref[:, cols]
            hr, hi = stre_ref[cols, :].T, stim_ref[cols, :].T
            for t in range(n_tok):
                r_re = slice(t * n_seq, (t + 1) * n_seq)
                r_im = slice(ROWS + t * n_seq, ROWS + (t + 1) * n_seq)
                hr, hi = (ar * hr - ai * hi + s_ref[l, r_re, :],
                          ar * hi + ai * hr + s_ref[l, r_im, :])
                s_ref[l, r_re, :] = hr
                s_ref[l, r_im, :] = hi
            sre_ref[cols, :] = hr.T
            sim_ref[cols, :] = hi.T
        ssm_o = _ssm_output(s_ref, u, wcre_ref, wcim_ref, d_ref, wglu_ref)
        ssm_n[...] = _dot(permt_ref[...], _rms(ssm_o, gssm_ref[...]).astype(BF16)).astype(BF16)

    def attend_group(g, base):
        grp = pl.ds(pl.multiple_of(g * grp_rows, grp_rows), grp_rows)
        qb = jnp.concatenate([qp[h, grp, :] for h in range(N_HEADS)], axis=0).astype(BF16)
        seq_rows = [slice(base + b * KV_WIDTH, base + (b + 1) * KV_WIDTH) for b in range(GROUP)]
        kc_f = [ck_ref[r, :] for r in seq_rows]
        vc_f = [cv_ref[r, :] for r in seq_rows]
        kcat = jnp.concatenate(kc_f, axis=1).astype(BF16)
        vcat = jnp.concatenate(vc_f, axis=1).astype(BF16)
        n_q = N_HEADS * grp_rows
        n_c = GROUP * WINDOW

        r_c = lax.broadcasted_iota(jnp.int32, (n_q, n_c), 0)
        c_c = lax.broadcasted_iota(jnp.int32, (n_q, n_c), 1)
        mask_c = ((_div(c_c, WINDOW) == _mod(_div(r_c, n_tok), GROUP))
                  & (_mod(c_c, WINDOW) > _mod(r_c, n_tok)))
        lc = jnp.where(mask_c, _dot(qb, kcat), NEG)
        tile = pl.ds(pl.multiple_of(_div(g * grp_rows, LANES) * LANES, LANES), LANES)
        first = _mod(g * grp_rows, LANES)
        r_n = lax.broadcasted_iota(jnp.int32, (n_q, LANES), 0)
        c_n = lax.broadcasted_iota(jnp.int32, (n_q, LANES), 1) - first
        mask_n = ((c_n >= 0) & (c_n < grp_rows)
                  & (_div(c_n, n_tok) == _mod(_div(r_n, n_tok), GROUP))
                  & (_mod(c_n, n_tok) <= _mod(r_n, n_tok)))
        ln = jnp.where(mask_n, _dot(qb, knew_b[:, tile]), NEG)

        sink_col = sink_ref[...] * LOG2E
        m = jnp.maximum(jnp.maximum(jnp.max(lc, axis=-1, keepdims=True),
                                    jnp.max(ln, axis=-1, keepdims=True)), sink_col)
        ec = jnp.exp2(lc - m)
        en = jnp.exp2(ln - m)
        den = (jnp.sum(ec, axis=-1, keepdims=True) + jnp.sum(en, axis=-1, keepdims=True)
               + jnp.exp2(sink_col - m))
        o = (_dot_nt(ec.astype(BF16), vcat) + _dot_nt(en.astype(BF16), vnew_b[:, tile])) / den

        lo = _lane_is_lo((grp_rows, LANES))
        for j in range(ATT_WIDTH // LANES):
            o_even = o[(2 * j) * grp_rows:(2 * j + 1) * grp_rows]
            o_odd = o[(2 * j + 1) * grp_rows:(2 * j + 2) * grp_rows]
            if j < 2:
                pair = jnp.where(lo, o_even, pltpu.roll(o_odd, HALF, axis=1))
            else:
                pair = jnp.where(lo, pltpu.roll(o_even, HALF, axis=1), o_odd)
            att_ref[grp, j * LANES:(j + 1) * LANES] = pair

        keep = WINDOW - n_tok
        to_tail = _mod(keep - first + LANES, LANES)
        kt_g = pltpu.roll(knew[:, tile], to_tail, axis=1)
        vt_g = pltpu.roll(vnew[:, tile], to_tail, axis=1)
        is_new = lax.broadcasted_iota(jnp.int32, (KV_WIDTH, WINDOW), 1) >= keep
        for b, rows in enumerate(seq_rows):
            k_b = kt_g if b == 0 else pltpu.roll(kt_g, LANES - n_tok * b, axis=1)
            v_b = vt_g if b == 0 else pltpu.roll(vt_g, LANES - n_tok * b, axis=1)
            kwin_ref[rows, :] = jnp.where(is_new, k_b, pltpu.roll(kc_f[b], keep, axis=1))
            vwin_ref[rows, :] = jnp.where(is_new, v_b, pltpu.roll(vc_f[b], keep, axis=1))

    for sub in range(GROUPS_PER_STEP):
        attend_group(step * GROUPS_PER_STEP + sub, sub * GROUP * KV_WIDTH)

    @pl.when(step == last)
    def _merge_out():
        h1_ref[...] = _merge(x_ref[...].reshape(ROWS, D_MODEL), att_ref[...], ssm_n[...],
                             gatt_ref, wout_ref, gpost_ref)


def _decode_mixer(x2d, tables, tables_t, sink_col, ck2d, cv2d, st_re, st_im, wkt, wvt, p):
    n_seq = st_re.shape[1]
    n_steps = n_seq // (GROUP * GROUPS_PER_STEP)
    cmap = lambda g: (g, 0)
    perm = _row_permutation(n_seq, ROWS // n_seq)
    head = (x2d, *tables, *tables_t, sink_col, perm, perm.T)
    in_specs = [_const_spec(a.shape) for a in head] + [
        pl.BlockSpec((GROUPS_PER_STEP * GROUP * KV_WIDTH, WINDOW), cmap),
        pl.BlockSpec((GROUPS_PER_STEP * GROUP * KV_WIDTH, WINDOW), cmap),
    ] + [_const_spec(a.shape) for a in (st_re, st_im, wkt, wvt, *p)]
    out_shape = (
        jax.ShapeDtypeStruct((ROWS, D_MODEL), F32),
        jax.ShapeDtypeStruct(ck2d.shape, F32),
        jax.ShapeDtypeStruct(cv2d.shape, F32),
        jax.ShapeDtypeStruct(st_re.shape, F32),
        jax.ShapeDtypeStruct(st_im.shape, F32),
    )
    out_specs = (
        _const_spec((ROWS, D_MODEL)),
        pl.BlockSpec((GROUPS_PER_STEP * GROUP * KV_WIDTH, WINDOW), cmap),
        pl.BlockSpec((GROUPS_PER_STEP * GROUP * KV_WIDTH, WINDOW), cmap),
        _const_spec(st_re.shape), _const_spec(st_im.shape),
    )
    scratch = [
        pltpu.VMEM((N_HEADS, ROWS, LANES), F32),
        pltpu.VMEM((KV_WIDTH, ROWS), F32),
        pltpu.VMEM((KV_WIDTH, ROWS), F32),
        pltpu.VMEM((KV_WIDTH, ROWS), BF16),
        pltpu.VMEM((KV_WIDTH, ROWS), BF16),
        pltpu.VMEM((ROWS, SSM_WIDTH), BF16),
        pltpu.VMEM((ROWS, ATT_WIDTH), F32),
        pltpu.VMEM((N_LB, 2 * ROWS, LANES), F32),
    ]
    return pl.pallas_call(
        _decode_mixer_kernel,
        grid=(n_steps,),
        in_specs=in_specs,
        out_specs=out_specs,
        out_shape=out_shape,
        scratch_shapes=scratch,
        compiler_params=pltpu.CompilerParams(
            dimension_semantics=("arbitrary",), vmem_limit_bytes=VMEM_LIMIT),
        name="decode_mixer",
    )(*head, ck2d, cv2d, st_re, st_im, wkt, wvt, *p)


def _mlp_kernel(xp_ref, xd_ref, gpre_ref, wup_hbm, wdn_hbm, gpost_ref, op_ref, od_ref,
                wup_ref, wdn_ref, sem):
    i = pl.program_id(0)
    n_chunks = D_FF // FF_CHUNK

    def up_copy(c):
        cols = pl.ds(c * FF_CHUNK, FF_CHUNK)
        return pltpu.make_async_copy(wup_hbm.at[:, cols], wup_ref.at[:, cols], sem.at[0, c])

    def dn_copy(c):
        cols = pl.ds(c * FF_CHUNK, FF_CHUNK)
        return pltpu.make_async_copy(wdn_hbm.at[cols, :], wdn_ref.at[cols, :], sem.at[1, c])

    def mlp(x, first):
        if first:
            for c in range(n_chunks):
                up_copy(c).start()
                dn_copy(c).start()
        hn = _rms(x, gpre_ref[...]).astype(BF16)
        acc = None
        for c in range(n_chunks):
            cols = slice(c * FF_CHUNK, (c + 1) * FF_CHUNK)
            if first:
                up_copy(c).wait()
            a = jnp.maximum(_dot(hn, wup_ref[:, cols]), 0.0)
            if first:
                dn_copy(c).wait()
            part = _dot((a * a).astype(BF16), wdn_ref[cols, :])
            acc = part if acc is None else acc + part
        return x + _rms(acc, gpost_ref[...])

    @pl.when(i == 0)
    def _decode_rows():
        od_ref[...] = mlp(xd_ref[...], True).reshape(od_ref.shape)

    @pl.when(i > 0)
    def _prompt_tile():
        op_ref[...] = mlp(xp_ref[...], False)


def _mlp(xp2d, xd2d, d_shape, g_pre, w_up, w_down, g_post):
    n_tiles = xp2d.shape[0] // MLP_ROWS
    pmap = lambda i: (jnp.maximum(i - 1, 0), 0)
    return pl.pallas_call(
        _mlp_kernel,
        grid=(n_tiles + 1,),
        in_specs=[pl.BlockSpec((MLP_ROWS, D_MODEL), pmap), _const_spec(xd2d.shape),
                  _const_spec(g_pre.shape), pl.BlockSpec(memory_space=pl.ANY),
                  pl.BlockSpec(memory_space=pl.ANY), _const_spec(g_post.shape)],
        out_specs=(pl.BlockSpec((MLP_ROWS, D_MODEL), pmap), _const_spec(d_shape)),
        out_shape=(jax.ShapeDtypeStruct(xp2d.shape, F32), jax.ShapeDtypeStruct(d_shape, F32)),
        scratch_shapes=[pltpu.VMEM(w_up.shape, BF16), pltpu.VMEM(w_down.shape, BF16),
                        pltpu.SemaphoreType.DMA((2, D_FF // FF_CHUNK))],
        compiler_params=pltpu.CompilerParams(
            dimension_semantics=("arbitrary",), vmem_limit_bytes=VMEM_LIMIT),
        name="mlp",
    )(xp2d, xd2d, g_pre, w_up, w_down, g_post)


def _zoh(a_re, a_im, log_dt, b_re, b_im):
    dt = jnp.exp(log_dt)[:, None]
    mag = jnp.exp(a_re * dt)
    abar_re, abar_im = mag * jnp.cos(a_im * dt), mag * jnp.sin(a_im * dt)
    nr, ni = abar_re - 1.0, abar_im
    den = a_re * a_re + a_im * a_im
    coef_re = (nr * a_re + ni * a_im) / den
    coef_im = (ni * a_re - nr * a_im) / den
    bbar_re = coef_re[..., None] * b_re - coef_im[..., None] * b_im
    bbar_im = coef_re[..., None] * b_im + coef_im[..., None] * b_re
    return abar_re, abar_im, bbar_re, bbar_im


def _same_group_mask():
    row_g = np.arange(SSM_TILE)[:, None] // SSM_GROUP
    col_g = np.arange(STATE_TILE)[None, :] // SSM_STATE
    return row_g == col_g


def _block_diag_in(bbar):
    w = jnp.swapaxes(bbar, 1, 2).reshape(N_SSM_TILES, SSM_TILE, SSM_STATE)
    w = jnp.tile(w, (1, 1, SSM_TILE // SSM_GROUP))
    return jnp.where(_same_group_mask()[None], w, 0.0).astype(BF16)


def _block_diag_out(c):
    w = jnp.swapaxes(c, 1, 2).reshape(N_SSM_TILES, STATE_TILE, SSM_GROUP)
    w = jnp.tile(w, (1, 1, SSM_TILE // SSM_GROUP))
    return jnp.where(_same_group_mask().T[None], w, 0.0).astype(BF16)


def _rope_tables(pos):
    half = ROPE_DIM // 2
    inv = ROPE_THETA ** (-np.arange(half, dtype=np.float64) / half)
    ang = np.asarray(pos, dtype=np.float64)[:, None] * inv[None, :]
    cos, sin = np.cos(ang), np.sin(ang)
    d = np.arange(LANES) % HEAD_DIM
    f = d % half
    cos_t = np.where(d[None, :] < ROPE_DIM, cos[:, f], 1.0)
    sa_t = np.where((d[None, :] >= half) & (d[None, :] < ROPE_DIM), sin[:, f], 0.0)
    sb_t = np.where(d[None, :] < half, -sin[:, f], 0.0)
    return tuple(t.astype(np.float32) for t in (cos_t, sa_t, sb_t))


def _row_permutation(n_outer, n_inner):
    r = np.arange(n_outer * n_inner)
    perm = np.zeros((r.size, r.size), np.float32)
    perm[r, (r % n_outer) * n_inner + r // n_outer] = 1.0
    return jnp.asarray(perm, dtype=BF16)


def kernel(x_prompt, x_sample, cache_k_win, cache_v_win, state_ssm_re, state_ssm_im, meta_tokens, norm_mix_pre, w_in, attn_sinks, ssm_a_re, ssm_a_im, ssm_log_dt, ssm_b_re, ssm_b_im, ssm_c_re, ssm_c_im, ssm_d, w_glu, norm_att_out, norm_ssm_out, w_out, norm_mix_post, norm_mlp_pre, w_up, w_down, norm_mlp_post):
    depth = w_in.shape[0]
    assert depth == 1
    l = 0
    nb, seq, _ = x_prompt.shape
    n_seq, n_tok, _ = x_sample.shape
    assert n_seq * n_tok == ROWS and nb * BLOCK == ROWS and seq % BLOCK == 0

    abar_re, abar_im, bbar_re, bbar_im = _zoh(
        ssm_a_re[l], ssm_a_im[l], ssm_log_dt[l], ssm_b_re[l], ssm_b_im[l])
    row = lambda a: a.reshape(1, -1)
    q_scale = np.where(np.arange(w_in.shape[-1]) < ATT_WIDTH, HEAD_DIM ** -0.5 * LOG2E,
                       1.0).astype(np.float32)
    params = (
        row(norm_mix_pre[l]), w_in[l],
        row(abar_re), row(abar_im),
        _block_diag_in(bbar_re), _block_diag_in(bbar_im),
        _block_diag_out(ssm_c_re[l]), _block_diag_out(-ssm_c_im[l]),
        row(ssm_d[l]), w_glu[l],
        row(norm_att_out[l]), row(norm_ssm_out[l]), w_out[l], row(norm_mix_post[l]),
    )

    front = BLOCK - N_META
    meta_blk = jnp.concatenate([jnp.zeros((front, D_MODEL), F32), meta_tokens], axis=0)
    cos, sa, sb = (jnp.asarray(t) for t in _rope_tables(np.arange(seq + BLOCK) - front))
    (h1, k_last, v_last, p_re, p_im, w_up_b, w_down_b, w_in_b, w_out_b, w_glu_b,
     w_k_t, w_v_t) = _prompt_mixer(
        x_prompt, meta_blk, cos, sa, sb, attn_sinks[l], params, jnp.asarray(row(q_scale)),
        w_up[l], w_down[l])
    mlp_params = (row(norm_mlp_pre[l]), w_up_b, w_down_b, row(norm_mlp_post[l]))
    params = params[:1] + (w_in_b,) + params[2:9] + (w_glu_b,) + params[10:12] + (
        w_out_b,) + params[13:]

    tabs = _rope_tables(PAST_LEN + np.arange(n_tok))
    tables = tuple(jnp.asarray(np.tile(t, (n_seq, 1))) for t in tabs)
    tables_t = tuple(jnp.asarray(np.tile(t.T, (1, n_seq))) for t in tabs)
    sink_col = jnp.repeat(attn_sinks[l], n_tok * GROUP).reshape(-1, 1)
    to_t = lambda a: jnp.transpose(a, (0, 2, 3, 1)).reshape(-1, WINDOW)
    from_t = lambda a, n: jnp.transpose(
        a.reshape(n, N_KV_HEADS, HEAD_DIM, WINDOW), (0, 3, 1, 2))[None]
    st_in = lambda a: jnp.transpose(a, (1, 2, 0)).reshape(N_STATE, n_seq)
    st_out = lambda a: jnp.transpose(a.reshape(SSM_GROUPS, SSM_STATE, n_seq), (2, 0, 1))[None]
    h1s, kwin, vwin, s_re, s_im = _decode_mixer(
        x_sample, tables, tables_t, sink_col, to_t(cache_k_win[l]), to_t(cache_v_win[l]),
        st_in(state_ssm_re[l]), st_in(state_ssm_im[l]), w_k_t, w_v_t, params)
    y_prompt, y_sample = _mlp(h1.reshape(nb * seq, D_MODEL), h1s, x_sample.shape, *mlp_params)
    y_prompt = y_prompt.reshape(nb, seq, D_MODEL)

    win = from_t
    st = lambda a, n: a.reshape(1, n, SSM_GROUPS, SSM_STATE)
    return (y_prompt, y_sample,
            win(k_last, nb), win(v_last, nb), st(p_re, nb), st(p_im, nb),
            win(kwin, n_seq), win(vwin, n_seq), st_out(s_re), st_out(s_im))
```

```python
import math

import jax
import jax.numpy as jnp
import numpy as np
from jax import lax
from jax.experimental import pallas as pl
from jax.experimental.pallas import tpu as pltpu

F32 = jnp.float32
BF16 = jnp.bfloat16

N_META = 16
HEAD_DIM = 64
N_HEADS = 8
N_KV_HEADS = 2
WINDOW = 128
BLOCK = 128
ROPE_DIM = 16
ROPE_HALF = ROPE_DIM // 2
ROPE_THETA = 500000.0
SSM_GROUP = 16
SSM_GROUPS = 32
SSM_STATE = 64
PAST_LEN = 8192
EPS = 1e-6
NEG = -1e30
LOG2E = math.log2(math.e)

D_MODEL = 1024
ATT_WIDTH = 512
KV_WIDTH = 128
SSM_WIDTH = 512
N_STATE = SSM_GROUPS * SSM_STATE
D_FF = 4096

ROWS = 512
MLP_ROWS = 1024
LANES = 128
HALF = LANES // 2
SSM_TILE = 256
STATE_TILE = (SSM_TILE // SSM_GROUP) * SSM_STATE
N_SSM_TILES = SSM_WIDTH // SSM_TILE
LB_PER_TILE = STATE_TILE // LANES
N_LB = N_STATE // LANES
FF_CHUNK = 1024
VMEM_LIMIT = 56 * 1024 * 1024


def _dot(a, b):
    return jnp.dot(a, b, preferred_element_type=F32)


def _dot_nt(a, b):
    return lax.dot_general(a, b, (((1,), (1,)), ((), ())), preferred_element_type=F32)


def _rms(x, g):
    return x * lax.rsqrt(jnp.mean(x * x, axis=-1, keepdims=True) + EPS) * g


def _rope(x, cos, sa, sb):
    return (x * cos + pltpu.roll(x, ROPE_HALF, axis=1) * sa
            + pltpu.roll(x, LANES - ROPE_HALF, axis=1) * sb)


def _lane_is_lo(shape):
    return lax.broadcasted_iota(jnp.int32, shape, 1) < HALF


def _project_qkv(hn, w_in_ref):
    q = _dot(hn, w_in_ref[:, 0:ATT_WIDTH])
    kv = _dot(hn, w_in_ref[:, ATT_WIDTH:ATT_WIDTH + 2 * KV_WIDTH])
    return q, kv[:, 0:KV_WIDTH], kv[:, KV_WIDTH:2 * KV_WIDTH]


def _project_u(hn, w_in_ref):
    return _dot(hn, w_in_ref[:, ATT_WIDTH + 2 * KV_WIDTH:])


def _ssm_input_tile(ub, c, wbre_ref, wbim_ref, s_ref):
    uc = ub[:, c * SSM_TILE:(c + 1) * SSM_TILE]
    bre, bim = _dot(uc, wbre_ref[c]), _dot(uc, wbim_ref[c])
    for l in range(LB_PER_TILE):
        s_ref[c * LB_PER_TILE + l, 0:ROWS, :] = bre[:, l * LANES:(l + 1) * LANES]
        s_ref[c * LB_PER_TILE + l, ROWS:2 * ROWS, :] = bim[:, l * LANES:(l + 1) * LANES]


def _ssm_input(u, wbre_ref, wbim_ref, s_ref):
    ub = u.astype(BF16)
    for c in range(N_SSM_TILES):
        _ssm_input_tile(ub, c, wbre_ref, wbim_ref, s_ref)


def _ssm_readout_tile(s_ref, c, wcre_ref, wcim_ref):
    blocks = range(c * LB_PER_TILE, (c + 1) * LB_PER_TILE)
    hr = jnp.concatenate([s_ref[l, 0:ROWS, :].astype(BF16) for l in blocks], axis=1)
    hi = jnp.concatenate([s_ref[l, ROWS:2 * ROWS, :].astype(BF16) for l in blocks], axis=1)
    return _dot(hr, wcre_ref[c]) + _dot(hi, wcim_ref[c])


def _ssm_output(s_ref, u, wcre_ref, wcim_ref, d_ref, wglu_ref):
    ys = [_ssm_readout_tile(s_ref, c, wcre_ref, wcim_ref) for c in range(N_SSM_TILES)]
    return _ssm_gate(ys, u, d_ref, wglu_ref)


def _ssm_gate(ys, u, d_ref, wglu_ref):
    y = jnp.concatenate(ys, axis=1) + d_ref[...] * u
    z = 0.5 * y * (1.0 + jnp.tanh(math.sqrt(2.0 / math.pi) * (y + 0.044715 * (y * y * y))))
    gate = 1.0 / (1.0 + jnp.exp(-_dot(z.astype(BF16), wglu_ref[...])))
    return z * gate


def _merge(x, att, s, gatt_ref, wout_ref, gpost_ref):
    a = _rms(att, gatt_ref[...]).astype(BF16)
    m = _dot(a, wout_ref[0:ATT_WIDTH, :]) + _dot(s, wout_ref[ATT_WIDTH:, :])
    return x + _rms(m, gpost_ref[...])


def _div(x, k):
    return lax.shift_right_logical(x, int(math.log2(k)))


def _mod(x, k):
    return lax.bitwise_and(x, k - 1)


def _sink_softmax(logits, bias, sink_col):
    lm = logits + bias
    m =jnp.maximum(jnp.max(lm, axis=-1, keepdims=True), sink_col)
    e = jnp.exp2(lm - m)
    den = jnp.sum(e, axis=-1, keepdims=True) + jnp.exp2(sink_col - m)
    return e, den


def _prompt_mixer_kernel(
        sink_ref, x_ref, meta_ref, tab0_ref, cos_ref, sa_ref, sb_ref, bias_ref, perm_ref, permt_ref,
        gpre_ref, winf_ref,
        ar_ref, ai_ref, wbre_ref, wbim_ref, wcre_ref, wcim_ref, d_ref, wgluf_ref,
        gatt_ref, gssm_ref, woutf_ref, gpost_ref, qscale_ref, wupf_ref, wdnf_ref,
        h1_ref, klast_ref, vlast_ref, sre_ref, sim_ref, wupb_ref, wdnb_ref,
        win_ref, wout_ref, wglu_ref, wkt_ref, wvt_ref,
        kbuf, vbuf, kcur, vcur, s_ref, hstate, att_ref):
    n = pl.program_id(0)
    wupb_ref[...] = wupf_ref[...].astype(BF16)
    wdnb_ref[...] = wdnf_ref[...].astype(BF16)
    nb = x_ref.shape[0]
    assert 2 * nb == 8
    last = pl.num_programs(0) - 1
    lo = _lane_is_lo((BLOCK, LANES))
    is_re = lax.broadcasted_iota(jnp.int32, (2 * nb, LANES), 0) < nb
    a1, a2 = [], []
    for l in range(N_LB):
        cols = slice(l * LANES, (l + 1) * LANES)
        ai = jnp.broadcast_to(ai_ref[:, cols], (2 * nb, LANES))
        a1.append(jnp.broadcast_to(ar_ref[:, cols], (2 * nb, LANES)))
        a2.append(jnp.where(is_re, -ai, ai))

    def swap(t):
        return pltpu.roll(t, nb, axis=0)

    def variants(t):
        tr = pltpu.roll(t, HALF, axis=1)
        z = jnp.zeros_like(t)
        return (jnp.where(lo, t, z), jnp.where(lo, z, tr),
                jnp.where(lo, tr, z), jnp.where(lo, z, t))

    @pl.when(n == 0)
    def _meta_block():
        win_ref[...] = (winf_ref[...] * qscale_ref[...]).astype(BF16)
        wout_ref[...] = woutf_ref[...].astype(BF16)
        wglu_ref[...] = wgluf_ref[...].astype(BF16)
        for out_ref, c0 in ((wkt_ref, ATT_WIDTH), (wvt_ref, ATT_WIDTH + KV_WIDTH)):
            for i in range(D_MODEL // LANES):
                blk = winf_ref[i * LANES:(i + 1) * LANES, c0:c0 + KV_WIDTH]
                out_ref[:, i * LANES:(i + 1) * LANES] = blk.T.astype(BF16)
        kbuf[...] = jnp.zeros_like(kbuf)
        vbuf[...] = jnp.zeros_like(vbuf)
        hm = _rms(meta_ref[...], gpre_ref[...]).astype(BF16)
        _, k0, v0 = _project_qkv(hm, win_ref)
        k0 = _rope(k0, tab0_ref[0], tab0_ref[1], tab0_ref[2])
        for i, (kv_, vv_) in enumerate(zip(variants(k0), variants(v0))):
            for b in range(nb):
                kbuf[b, i, 0:BLOCK, :] = kv_.astype(BF16)
                vbuf[b, i, 0:BLOCK, :] = vv_.astype(BF16)
        um = _project_u(hm, win_ref).astype(BF16)
        for c in range(N_SSM_TILES):
            uc = um[:, c * SSM_TILE:(c + 1) * SSM_TILE]
            bre, bim = _dot(uc, wbre_ref[c]), _dot(uc, wbim_ref[c])
            for j in range(LB_PER_TILE):
                l = c * LB_PER_TILE + j
                cols = slice(j * LANES, (j + 1) * LANES)
                h = jnp.zeros((2 * nb, LANES), F32)
                for t in range(BLOCK - N_META, BLOCK):
                    x_t = jnp.where(is_re, jnp.broadcast_to(bre[t:t + 1, cols], h.shape),
                                    jnp.broadcast_to(bim[t:t + 1, cols], h.shape))
                    h = a1[l] * h + a2[l] * swap(h) + x_t
                hstate[l] = h

    x = x_ref[...].reshape(ROWS, D_MODEL)
    hn = _rms(x, gpre_ref[...]).astype(BF16)
    cos, sa, sb = cos_ref[...], sa_ref[...], sb_ref[...]

    u = _project_u(_dot(perm_ref[...], hn).astype(BF16), win_ref)

    hfin = []

    def chains(blocks):
        for l in blocks:
            h = hstate[l]
            g = swap(h)
            a2n = -a2[l]
            for i in range(BLOCK // 2):
                r_re = slice(i * 2 * nb, (i + 1) * 2 * nb)
                r_im = slice(ROWS + i * 2 * nb, ROWS + (i + 1) * 2 * nb)
                re, im_s = s_ref[l, r_re, :], swap(s_ref[l, r_im, :])
                h0 = a1[l] * h + a2[l] * g + jnp.where(is_re, re, im_s)
                g0 = swap(h0)
                g = a1[l] * g0 + a2n * h0 + jnp.where(is_re, im_s, re)
                h = swap(g)
                s_ref[l, r_re, :] = jnp.where(is_re, h0, g)
                s_ref[l, r_im, :] = jnp.where(is_re, g0, h)
            hstate[l] = h
            hfin.append(h)

    mask = bias_ref[0]
    cur = pl.ds(pl.multiple_of(_mod(n + 1, 2) * BLOCK, BLOCK), BLOCK)
    top = lax.broadcasted_iota(jnp.int32, (2 * BLOCK, 1), 0) < BLOCK

    def attend(b):
        rows = slice(b * BLOCK, (b + 1) * BLOCK)
        kb = _rope(k[rows], cos, sa, sb)
        vb = v[rows]
        for i, (kv_, vv_) in enumerate(zip(variants(kb), variants(vb))):
            kbuf[b, i, cur, :] = kv_.astype(BF16)
            vbuf[b, i, cur, :] = vv_.astype(BF16)

        kcur[b] = kb
        vcur[b] = vb

        qs = []
        for j in range(ATT_WIDTH // LANES):
            qs.append(_rope(q[rows, j * LANES:(j + 1) * LANES], cos, sa, sb).astype(BF16))
        for g in range(N_KV_HEADS):
            qst = jnp.concatenate([qs[2 * g], qs[2 * g + 1]], axis=0)
            o = None
            for half in range(2):
                var = 2 * g + half
                h_top, h_bot = 4 * g + half, 4 * g + 2 + half
                sink_col = jnp.where(top, sink_ref[h_top], sink_ref[h_bot]) * LOG2E
                e, den = _sink_softmax(_dot_nt(qst, kbuf[b, var]), mask, sink_col)
                part = _dot(e.astype(BF16), vbuf[b, var]) / den
                o = part if o is None else o + part
            att_ref[rows, (2 * g) * LANES:(2 * g + 1) * LANES] = o[0:BLOCK]
            att_ref[rows, (2 * g + 1) * LANES:(2 * g + 2) * LANES] = o[BLOCK:2 * BLOCK]

    _ssm_input(u, wbre_ref, wbim_ref, s_ref)
    chains(range(N_LB))
    q, k, v = _project_qkv(hn, win_ref)
    ssm_o = _ssm_output(s_ref, u, wcre_ref, wcim_ref, d_ref, wglu_ref)
    ssm_n = _dot(permt_ref[...], _rms(ssm_o, gssm_ref[...]).astype(BF16)).astype(BF16)
    for b in range(nb):
        attend(b)

    h1 = _merge(x, att_ref[...], ssm_n, gatt_ref, wout_ref, gpost_ref)
    h1_ref[...] = h1.reshape(h1_ref.shape)

    @pl.when(n == last)
    def _emit_state():
        for b in range(nb):
            klast_ref[b] = kcur[b].T
            vlast_ref[b] = vcur[b].T
        for l in range(N_LB):
            sre_ref[:, l * LANES:(l + 1) * LANES] = hfin[l][0:nb]
            sim_ref[:, l * LANES:(l + 1) * LANES] = hfin[l][nb:2 * nb]


def _window_bias():
    r = (np.arange(2 * BLOCK) % BLOCK)[:, None]
    phys = np.arange(2 * BLOCK)[None, :]
    tables = []
    for c_min in (2 * BLOCK - N_META, BLOCK - N_META, 0):
        for parity in (0, 1):
            c = phys if parity == 1 else (phys + BLOCK) % (2 * BLOCK)
            ok = (c > r) & (c <= r + WINDOW) & (c >= c_min)
            tables.append(np.where(ok, 0.0, NEG))
    return jnp.asarray(np.stack(tables), dtype=F32)


def _const_spec(shape):
    zeros = (0,) * len(shape)
    return pl.BlockSpec(shape, lambda *_: zeros)


def _prompt_mixer(x_prompt, meta_blk, cos, sa, sb, sinks, p, q_scale, w_up, w_down):
    nb, seq, _ = x_prompt.shape
    n_blocks = seq // BLOCK
    perm = _row_permutation(nb, BLOCK)
    xmap = lambda n: (0, n, 0)
    tmap = lambda n: (n + 1, 0)
    bmap = lambda n: (2 * jnp.minimum(n + 1, 2) + (n + 1) % 2, 0, 0)
    n_slabs = D_FF // LANES
    assert n_slabs <= n_blocks
    upmap = lambda n: (0, jnp.minimum(n, n_slabs - 1))
    dnmap = lambda n: (jnp.minimum(n, n_slabs - 1), 0)
    tab0 = jnp.stack([cos[0:BLOCK], sa[0:BLOCK], sb[0:BLOCK]])
    in_specs = [
        pl.BlockSpec(memory_space=pltpu.SMEM),
        pl.BlockSpec((nb, BLOCK, D_MODEL), xmap),
        _const_spec((BLOCK, D_MODEL)),
        _const_spec(tab0.shape),
        pl.BlockSpec((BLOCK, LANES), tmap),
        pl.BlockSpec((BLOCK, LANES), tmap),
        pl.BlockSpec((BLOCK, LANES), tmap),
        pl.BlockSpec((1, 2 * BLOCK, 2 * BLOCK), bmap),
        _const_spec((ROWS, ROWS)),
        _const_spec((ROWS, ROWS)),
    ] + [_const_spec(a.shape) for a in (*p, q_scale)] + [
        pl.BlockSpec((D_MODEL, LANES), upmap),
        pl.BlockSpec((LANES, D_MODEL), dnmap),
    ]
    w_in, w_glu, w_out = p[1], p[9], p[12]
    out_shape = (
        jax.ShapeDtypeStruct((nb, seq, D_MODEL), F32),
        jax.ShapeDtypeStruct((nb, BLOCK, KV_WIDTH), F32),
        jax.ShapeDtypeStruct((nb, BLOCK, KV_WIDTH), F32),
        jax.ShapeDtypeStruct((nb, N_STATE), F32),
        jax.ShapeDtypeStruct((nb, N_STATE), F32),
        jax.ShapeDtypeStruct(w_up.shape, BF16),
        jax.ShapeDtypeStruct(w_down.shape, BF16),
        jax.ShapeDtypeStruct(w_in.shape, BF16),
        jax.ShapeDtypeStruct(w_out.shape, BF16),
        jax.ShapeDtypeStruct(w_glu.shape, BF16),
        jax.ShapeDtypeStruct((KV_WIDTH, D_MODEL), BF16),
        jax.ShapeDtypeStruct((KV_WIDTH, D_MODEL), BF16),
    )
    out_specs = (
        pl.BlockSpec((nb, BLOCK, D_MODEL), xmap),
        _const_spec((nb, BLOCK, KV_WIDTH)),
        _const_spec((nb, BLOCK, KV_WIDTH)),
        _const_spec((nb, N_STATE)),
        _const_spec((nb, N_STATE)),
        pl.BlockSpec((D_MODEL, LANES), upmap),
        pl.BlockSpec((LANES, D_MODEL), dnmap),
        _const_spec(w_in.shape), _const_spec(w_out.shape), _const_spec(w_glu.shape),
        _const_spec((KV_WIDTH, D_MODEL)), _const_spec((KV_WIDTH, D_MODEL)),
    )
    scratch = [
        pltpu.VMEM((nb, 4, 2 * BLOCK, LANES), BF16),
        pltpu.VMEM((nb, 4, 2 * BLOCK, LANES), BF16),
        pltpu.VMEM((nb, BLOCK, KV_WIDTH), F32),
        pltpu.VMEM((nb, BLOCK, KV_WIDTH), F32),
        pltpu.VMEM((N_LB, 2 * ROWS, LANES), F32),
        pltpu.VMEM((N_LB, 2 * nb, LANES), F32),
        pltpu.VMEM((ROWS, ATT_WIDTH), F32),
    ]
    return pl.pallas_call(
        _prompt_mixer_kernel,
        grid=(n_blocks,),
        in_specs=in_specs,
        out_specs=out_specs,
        out_shape=out_shape,
        scratch_shapes=scratch,
        compiler_params=pltpu.CompilerParams(
            dimension_semantics=("arbitrary",), vmem_limit_bytes=VMEM_LIMIT),
        name="prompt_mixer",
    )(sinks, x_prompt, meta_blk, tab0, cos, sa, sb, _window_bias(), perm, perm.T, *p, q_scale,
      w_up, w_down)


GROUP = 8
GROUPS_PER_STEP = 4


def _decode_mixer_kernel(
        x_ref, cos_ref, sa_ref, sb_ref, cost_ref, sat_ref, sbt_ref, sink_ref, perm_ref, permt_ref,
        ck_ref, cv_ref, stre_ref, stim_ref, wkt_ref, wvt_ref,
        gpre_ref, win_ref, ar_ref, ai_ref, wbre_hbm, wbim_hbm, wcre_hbm, wcim_hbm,
        d_ref, wglu_ref, gatt_ref, gssm_ref, wout_ref, gpost_ref,
        h1_ref, kwin_ref, vwin_ref, sre_ref, sim_ref,
        qp, knew, vnew, knew_b, vnew_b, ssm_n, att_ref, s_ref,
        wbre_ref, wbim_ref, wcre_ref, wcim_ref, wsem):
    step = pl.program_id(0)
    n_seq = stre_ref.shape[1]
    n_tok = ROWS // n_seq
    grp_rows = GROUP * n_tok
    last = pl.num_programs(0) - 1

    @pl.when(step == 0)
    def _project_and_ssm():
        ssm_w = [pltpu.make_async_copy(src, dst, wsem.at[j]) for j, (src, dst) in enumerate(
            ((wbre_hbm, wbre_ref), (wbim_hbm, wbim_ref), (wcre_hbm, wcre_ref), (wcim_hbm, wcim_ref)))]
        for cp in ssm_w:
            cp.start()
        hn = _rms(x_ref[...].reshape(ROWS, D_MODEL), gpre_ref[...]).astype(BF16)
        q = _dot(hn, win_ref[:, 0:ATT_WIDTH])
        u = _project_u(_dot(perm_ref[...], hn).astype(BF16), win_ref)
        cos, sa, sb = cos_ref[...], sa_ref[...], sb_ref[...]
        kt = _dot_nt(wkt_ref[...], hn)
        kt = (kt * cost_ref[...] + pltpu.roll(kt, ROPE_HALF, axis=0) * sat_ref[...]
              + pltpu.roll(kt, KV_WIDTH - ROPE_HALF, axis=0) * sbt_ref[...])
        vt = _dot_nt(wvt_ref[...], hn)
        knew[...] = kt
        vnew[...] = vt
        knew_b[...] = kt.astype(BF16)
        vnew_b[...] = vt.astype(BF16)
        lo = _lane_is_lo((ROWS, LANES))
        for j in range(ATT_WIDTH // LANES):
            qj = _rope(q[:, j * LANES:(j + 1) * LANES], cos, sa, sb)
            qr = pltpu.roll(qj, HALF, axis=1)
            z = jnp.zeros_like(qj)
            if j < 2:
                qp[2 * j] = jnp.where(lo, qj, z)
                qp[2 * j + 1] = jnp.where(lo, qr, z)
            else:
                qp[2 * j] = jnp.where(lo, z, qr)
                qp[2 * j + 1] = jnp.where(lo, z, qj)

        for cp in ssm_w:
            cp.wait()
        _ssm_input(u, wbre_ref, wbim_ref, s_ref)
        for l in range(N_LB):
            cols = slice(l * LANES, (l + 1) * LANES)
            ar, ai = ar_ref[:, cols], ai_ref[:, cols]
            hr, hi = stre_ref[cols, :].T, stim_ref[cols, :].T
            for t in range(n_tok):
                r_re = slice(t * n_seq, (t + 1) * n_seq)
                r_im = slice(ROWS + t * n_seq, ROWS + (t + 1) * n_seq)
                hr, hi = (ar * hr - ai * hi + s_ref[l, r_re, :],
                          ar * hi + ai * hr + s_ref[l, r_im, :])
                s_ref[l, r_re, :] = hr
                s_ref[l, r_im, :] = hi
            sre_ref[cols, :] = hr.T
            sim_ref[cols, :] = hi.T
        ssm_o = _ssm_output(s_ref, u, wcre_ref, wcim_ref, d_ref, wglu_ref)
        ssm_n[...] = _dot(permt_ref[...], _rms(ssm_o, gssm_ref[...]).astype(BF16)).astype(BF16)

    def attend_group(g, base):
        grp = pl.ds(pl.multiple_of(g * grp_rows, grp_rows), grp_rows)
        qb = jnp.concatenate([qp[h, grp, :] for h in range(N_HEADS)], axis=0).astype(BF16)
        seq_rows = [slice(base + b * KV_WIDTH, base + (b + 1) * KV_WIDTH) for b in range(GROUP)]
        kc_f = [ck_ref[r, :] for r in seq_rows]
        vc_f = [cv_ref[r, :] for r in seq_rows]
        kcat = jnp.concatenate(kc_f, axis=1).astype(BF16)
        vcat = jnp.concatenate(vc_f, axis=1).astype(BF16)
        n_q = N_HEADS * grp_rows
        n_c = GROUP * WINDOW

        r_c = lax.broadcasted_iota(jnp.int32, (n_q, n_c), 0)
        c_c = lax.broadcasted_iota(jnp.int32, (n_q, n_c), 1)
        mask_c = ((_div(c_c, WINDOW) == _mod(_div(r_c, n_tok), GROUP))
                  & (_mod(c_c, WINDOW) > _mod(r_c, n_tok)))
        lc = jnp.where(mask_c, _dot(qb, kcat), NEG)
        tile = pl.ds(pl.multiple_of(_div(g * grp_rows, LANES) * LANES, LANES), LANES)
        first = _mod(g * grp_rows, LANES)
        r_n = lax.broadcasted_iota(jnp.int32, (n_q, LANES), 0)
        c_n = lax.broadcasted_iota(jnp.int32, (n_q, LANES), 1) - first
        mask_n = ((c_n >= 0) & (c_n < grp_rows)
                  & (_div(c_n, n_tok) == _mod(_div(r_n, n_tok), GROUP))
                  & (_mod(c_n, n_tok) <= _mod(r_n, n_tok)))
        ln = jnp.where(mask_n, _dot(qb, knew_b[:, tile]), NEG)

        sink_col = sink_ref[...] * LOG2E
        m = jnp.maximum(jnp.maximum(jnp.max(lc, axis=-1, keepdims=True),
                                    jnp.max(ln, axis=-1, keepdims=True)), sink_col)
        ec = jnp.exp2(lc - m)
        en = jnp.exp2(ln - m)
        den = (jnp.sum(ec, axis=-1, keepdims=True) + jnp.sum(en, axis=-1, keepdims=True)
               + jnp.exp2(sink_col - m))
        o = (_dot_nt(ec.astype(BF16), vcat) + _dot_nt(en.astype(BF16), vnew_b[:, tile])) / den

        lo = _lane_is_lo((grp_rows, LANES))
        for j in range(ATT_WIDTH // LANES):
            o_even = o[(2 * j) * grp_rows:(2 * j + 1) * grp_rows]
            o_odd = o[(2 * j + 1) * grp_rows:(2 * j + 2) * grp_rows]
            if j < 2:
                pair = jnp.where(lo, o_even, pltpu.roll(o_odd, HALF, axis=1))
            else:
                pair = jnp.where(lo, pltpu.roll(o_even, HALF, axis=1), o_odd)
            att_ref[grp, j * LANES:(j + 1) * LANES] = pair

        keep = WINDOW - n_tok
        to_tail = _mod(keep - first + LANES, LANES)
        kt_g = pltpu.roll(knew[:, tile], to_tail, axis=1)
        vt_g = pltpu.roll(vnew[:, tile], to_tail, axis=1)
        is_new = lax.broadcasted_iota(jnp.int32, (KV_WIDTH, WINDOW), 1) >= keep
        for b, rows in enumerate(seq_rows):
            k_b = kt_g if b == 0 else pltpu.roll(kt_g, LANES - n_tok * b, axis=1)
            v_b = vt_g if b == 0 else pltpu.roll(vt_g, LANES - n_tok * b, axis=1)
            kwin_ref[rows, :] = jnp.where(is_new, k_b, pltpu.roll(kc_f[b], keep, axis=1))
            vwin_ref[rows, :] = jnp.where(is_new, v_b, pltpu.roll(vc_f[b], keep, axis=1))

    for sub in range(GROUPS_PER_STEP):
        attend_group(step * GROUPS_PER_STEP + sub, sub * GROUP * KV_WIDTH)

    @pl.when(step == last)
    def _merge_out():
        h1_ref[...] = _merge(x_ref[...].reshape(ROWS, D_MODEL), att_ref[...], ssm_n[...],
                             gatt_ref, wout_ref, gpost_ref)


def _decode_mixer(x2d, tables, tables_t, sink_col, ck2d, cv2d, st_re, st_im, wkt, wvt, p):
    n_seq = st_re.shape[1]
    n_steps = n_seq // (GROUP * GROUPS_PER_STEP)
    cmap = lambda g: (g, 0)
    perm = _row_permutation(n_seq, ROWS // n_seq)
    head = (x2d, *tables, *tables_t, sink_col, perm, perm.T)
    in_specs = [_const_spec(a.shape) for a in head] + [
        pl.BlockSpec((GROUPS_PER_STEP * GROUP * KV_WIDTH, WINDOW), cmap),
        pl.BlockSpec((GROUPS_PER_STEP * GROUP * KV_WIDTH, WINDOW), cmap),
    ] + [_const_spec(a.shape) for a in (st_re, st_im, wkt, wvt)] + [
        pl.BlockSpec(memory_space=pl.ANY) if 4 <= j < 8 else _const_spec(a.shape)
        for j, a in enumerate(p)]
    out_shape = (
        jax.ShapeDtypeStruct((ROWS, D_MODEL), F32),
        jax.ShapeDtypeStruct(ck2d.shape, F32),
        jax.ShapeDtypeStruct(cv2d.shape, F32),
        jax.ShapeDtypeStruct(st_re.shape, F32),
        jax.ShapeDtypeStruct(st_im.shape, F32),
    )
    out_specs = (
        _const_spec((ROWS, D_MODEL)),
        pl.BlockSpec((GROUPS_PER_STEP * GROUP * KV_WIDTH, WINDOW), cmap),
        pl.BlockSpec((GROUPS_PER_STEP * GROUP * KV_WIDTH, WINDOW), cmap),
        _const_spec(st_re.shape), _const_spec(st_im.shape),
    )
    scratch = [
        pltpu.VMEM((N_HEADS, ROWS, LANES), F32),
        pltpu.VMEM((KV_WIDTH, ROWS), F32),
        pltpu.VMEM((KV_WIDTH, ROWS), F32),
        pltpu.VMEM((KV_WIDTH, ROWS), BF16),
        pltpu.VMEM((KV_WIDTH, ROWS), BF16),
        pltpu.VMEM((ROWS, SSM_WIDTH), BF16),
        pltpu.VMEM((ROWS, ATT_WIDTH), F32),
        pltpu.VMEM((N_LB, 2 * ROWS, LANES), F32),
    ] + [pltpu.VMEM(a.shape, a.dtype) for a in p[4:8]] + [pltpu.SemaphoreType.DMA((4,))]
    return pl.pallas_call(
        _decode_mixer_kernel,
        grid=(n_steps,),
        in_specs=in_specs,
        out_specs=out_specs,
        out_shape=out_shape,
        scratch_shapes=scratch,
        compiler_params=pltpu.CompilerParams(
            dimension_semantics=("arbitrary",), vmem_limit_bytes=VMEM_LIMIT),
        name="decode_mixer",
    )(*head, ck2d, cv2d, st_re, st_im, wkt, wvt, *p)


def _mlp_kernel(xp_ref, xd_ref, gpre_ref, wup_hbm, wdn_hbm, gpost_ref, op_ref, od_ref,
                wup_ref, wdn_ref, sem):
    i = pl.program_id(0)
    n_chunks = D_FF // FF_CHUNK

    def up_copy(c):
        cols = pl.ds(c * FF_CHUNK, FF_CHUNK)
        return pltpu.make_async_copy(wup_hbm.at[:, cols], wup_ref.at[:, cols], sem.at[0, c])

    def dn_copy(c):
        cols = pl.ds(c * FF_CHUNK, FF_CHUNK)
        return pltpu.make_async_copy(wdn_hbm.at[cols, :], wdn_ref.at[cols, :], sem.at[1, c])

    def mlp(x, first):
        if first:
            for c in range(n_chunks):
                up_copy(c).start()
                dn_copy(c).start()
        hn = _rms(x, gpre_ref[...]).astype(BF16)
        acc = None
        for c in range(n_chunks):
            cols = slice(c * FF_CHUNK, (c + 1) * FF_CHUNK)
            if first:
                up_copy(c).wait()
            a = jnp.maximum(_dot(hn, wup_ref[:, cols]), 0.0)
            if first:
                dn_copy(c).wait()
            part = _dot((a * a).astype(BF16), wdn_ref[cols, :])
            acc = part if acc is None else acc + part
        return x + _rms(acc, gpost_ref[...])

    @pl.when(i == 0)
    def _decode_rows():
        od_ref[...] = mlp(xd_ref[...], True).reshape(od_ref.shape)

    @pl.when(i > 0)
    def _prompt_tile():
        op_ref[...] = mlp(xp_ref[...], False)


def _mlp(xp2d, xd2d, d_shape, g_pre, w_up, w_down, g_post):
    n_tiles = xp2d.shape[0] // MLP_ROWS
    pmap = lambda i: (jnp.maximum(i - 1, 0), 0)
    return pl.pallas_call(
        _mlp_kernel,
        grid=(n_tiles + 1,),
        in_specs=[pl.BlockSpec((MLP_ROWS, D_MODEL), pmap), _const_spec(xd2d.shape),
                  _const_spec(g_pre.shape), pl.BlockSpec(memory_space=pl.ANY),
                  pl.BlockSpec(memory_space=pl.ANY), _const_spec(g_post.shape)],
        out_specs=(pl.BlockSpec((MLP_ROWS, D_MODEL), pmap), _const_spec(d_shape)),
        out_shape=(jax.ShapeDtypeStruct(xp2d.shape, F32), jax.ShapeDtypeStruct(d_shape, F32)),
        scratch_shapes=[pltpu.VMEM(w_up.shape, BF16), pltpu.VMEM(w_down.shape, BF16),
                        pltpu.SemaphoreType.DMA((2, D_FF // FF_CHUNK))],
        compiler_params=pltpu.CompilerParams(
            dimension_semantics=("arbitrary",), vmem_limit_bytes=VMEM_LIMIT),
        name="mlp",
    )(xp2d, xd2d, g_pre, w_up, w_down, g_post)


def _zoh(a_re, a_im, log_dt, b_re, b_im):
    dt = jnp.exp(log_dt)[:, None]
    mag = jnp.exp(a_re * dt)
    abar_re, abar_im = mag * jnp.cos(a_im * dt), mag * jnp.sin(a_im * dt)
    nr, ni = abar_re - 1.0, abar_im
    den = a_re * a_re + a_im * a_im
    coef_re = (nr * a_re + ni * a_im) / den
    coef_im = (ni * a_re - nr * a_im) / den
    bbar_re = coef_re[..., None] * b_re - coef_im[..., None] * b_im
    bbar_im = coef_re[..., None] * b_im + coef_im[..., None] * b_re
    return abar_re, abar_im, bbar_re, bbar_im


def _same_group_mask():
    row_g = np.arange(SSM_TILE)[:, None] // SSM_GROUP
    col_g = np.arange(STATE_TILE)[None, :] // SSM_STATE
    return row_g == col_g


def _block_diag_in(bbar):
    w = jnp.swapaxes(bbar, 1, 2).reshape(N_SSM_TILES, SSM_TILE, SSM_STATE)
    w = jnp.tile(w, (1, 1, SSM_TILE // SSM_GROUP))
    return jnp.where(_same_group_mask()[None], w, 0.0).astype(BF16)


def _block_diag_out(c):
    w = jnp.swapaxes(c, 1, 2).reshape(N_SSM_TILES, STATE_TILE, SSM_GROUP)
    w = jnp.tile(w, (1, 1, SSM_TILE // SSM_GROUP))
    return jnp.where(_same_group_mask().T[None], w, 0.0).astype(BF16)


def _rope_tables(pos):
    half = ROPE_DIM // 2
    inv = ROPE_THETA ** (-np.arange(half, dtype=np.float64) / half)
    ang = np.asarray(pos, dtype=np.float64)[:, None] * inv[None, :]
    cos, sin = np.cos(ang), np.sin(ang)
    d = np.arange(LANES) % HEAD_DIM
    f = d % half
    cos_t = np.where(d[None, :] < ROPE_DIM, cos[:, f], 1.0)
    sa_t = np.where((d[None, :] >= half) & (d[None, :] < ROPE_DIM), sin[:, f], 0.0)
    sb_t = np.where(d[None, :] < half, -sin[:, f], 0.0)
    return tuple(t.astype(np.float32) for t in (cos_t, sa_t, sb_t))


def _row_permutation(n_outer, n_inner):
    r = np.arange(n_outer * n_inner)
    perm = np.zeros((r.size, r.size), np.float32)
    perm[r, (r % n_outer) * n_inner + r // n_outer] = 1.0
    return jnp.asarray(perm, dtype=BF16)


def kernel(x_prompt, x_sample, cache_k_win, cache_v_win, state_ssm_re, state_ssm_im, meta_tokens, norm_mix_pre, w_in, attn_sinks, ssm_a_re, ssm_a_im, ssm_log_dt, ssm_b_re, ssm_b_im, ssm_c_re, ssm_c_im, ssm_d, w_glu, norm_att_out, norm_ssm_out, w_out, norm_mix_post, norm_mlp_pre, w_up, w_down, norm_mlp_post):
    depth = w_in.shape[0]
    assert depth == 1
    l = 0
    nb, seq, _ = x_prompt.shape
    n_seq, n_tok, _ = x_sample.shape
    assert n_seq * n_tok == ROWS and nb * BLOCK == ROWS and seq % BLOCK == 0

    abar_re, abar_im, bbar_re, bbar_im = _zoh(
        ssm_a_re[l], ssm_a_im[l], ssm_log_dt[l], ssm_b_re[l], ssm_b_im[l])
    row = lambda a: a.reshape(1, -1)
    q_scale = np.where(np.arange(w_in.shape[-1]) < ATT_WIDTH, HEAD_DIM ** -0.5 * LOG2E,
                       1.0).astype(np.float32)
    params = (
        row(norm_mix_pre[l]), w_in[l],
        row(abar_re), row(abar_im),
        _block_diag_in(bbar_re), _block_diag_in(bbar_im),
        _block_diag_out(ssm_c_re[l]), _block_diag_out(-ssm_c_im[l]),
        row(ssm_d[l]), w_glu[l],
        row(norm_att_out[l]), row(norm_ssm_out[l]), w_out[l], row(norm_mix_post[l]),
    )

    front = BLOCK - N_META
    meta_blk = jnp.concatenate([jnp.zeros((front, D_MODEL), F32), meta_tokens], axis=0)
    cos, sa, sb = (jnp.asarray(t) for t in _rope_tables(np.arange(seq + BLOCK) - front))
    (h1, k_last, v_last, p_re, p_im, w_up_b, w_down_b, w_in_b, w_out_b, w_glu_b,
     w_k_t, w_v_t) = _prompt_mixer(
        x_prompt, meta_blk, cos, sa, sb, attn_sinks[l], params, jnp.asarray(row(q_scale)),
        w_up[l], w_down[l])
    mlp_params = (row(norm_mlp_pre[l]), w_up_b, w_down_b, row(norm_mlp_post[l]))
    params = params[:1] + (w_in_b,) + params[2:9] + (w_glu_b,) + params[10:12] + (
        w_out_b,) + params[13:]

    tabs = _rope_tables(PAST_LEN + np.arange(n_tok))
    tables = tuple(jnp.asarray(np.tile(t, (n_seq, 1))) for t in tabs)
    tables_t = tuple(jnp.asarray(np.tile(t.T, (1, n_seq))) for t in tabs)
    sink_col = jnp.repeat(attn_sinks[l], n_tok * GROUP).reshape(-1, 1)
    to_t = lambda a: jnp.transpose(a, (0, 2, 3, 1)).reshape(-1, WINDOW)
    from_t = lambda a, n: jnp.transpose(
        a.reshape(n, N_KV_HEADS, HEAD_DIM, WINDOW), (0, 3, 1, 2))[None]
    st_in = lambda a: jnp.transpose(a, (1, 2, 0)).reshape(N_STATE, n_seq)
    st_out = lambda a: jnp.transpose(a.reshape(SSM_GROUPS, SSM_STATE, n_seq), (2, 0, 1))[None]
    h1s, kwin, vwin, s_re, s_im = _decode_mixer(
        x_sample, tables, tables_t, sink_col, to_t(cache_k_win[l]), to_t(cache_v_win[l]),
        st_in(state_ssm_re[l]), st_in(state_ssm_im[l]), w_k_t, w_v_t, params)
    y_prompt, y_sample = _mlp(h1.reshape(nb * seq, D_MODEL), h1s, x_sample.shape, *mlp_params)
    y_prompt = y_prompt.reshape(nb, seq, D_MODEL)

    win = from_t
    st = lambda a, n: a.reshape(1, n, SSM_GROUPS, SSM_STATE)
    return (y_prompt, y_sample,
            win(k_last, nb), win(v_last, nb), st(p_re, nb), st(p_im, nb),
            win(kwin, n_seq), win(vwin, n_seq), st_out(s_re), st_out(s_im))
```
